```python
import math
import jax, jax.numpy as jnp
from jax import lax
import numpy as np

D_MODEL = 1024
BATCH = 32
SEQ = 2048
DEPTH = 4

GRID_W = 64
CTX_LEN = 256
HEAD_DIM = 64
N_MIXERS = 4
GROUP_W = D_MODEL // N_MIXERS
HEADS = GROUP_W // HEAD_DIM
KV_HEADS_A = HEADS // 2
KV_HEADS_D = HEADS // 2
DIFF_DIM = HEAD_DIM // 2
WINDOW = 128
BLOCK_Q = 128
NA_ROWS = 8
NA_COLS = 16
ROPE_BASE = 10000.0
N_EXPERTS = 16
N_EXPERT_GROUPS = 4
EXPERTS_PER_GROUP = N_EXPERTS // N_EXPERT_GROUPS
TOP_K = 2
D_EXPERT = D_MODEL // 2
MOE_BLOCK = 256
ADA_CHUNKS = 6
EPS = 1e-6
NEG = -1e30
F32 = jnp.float32
COL_SIZES = (HEADS * HEAD_DIM, KV_HEADS_A * HEAD_DIM, KV_HEADS_A * HEAD_DIM,
             HEADS * HEAD_DIM, HEADS * HEAD_DIM, HEADS * HEAD_DIM,
             HEADS * 2 * DIFF_DIM, HEADS * 2 * DIFF_DIM, HEADS * HEAD_DIM,
             HEADS * HEAD_DIM, KV_HEADS_D * HEAD_DIM, KV_HEADS_D * HEAD_DIM)
D_IN = sum(COL_SIZES)

kernel_name = "hybrid_headgroup_diffusion_moe_trunk"


def rms(x, eps=EPS):
    x32 = x.astype(F32)
    return (x32 * lax.rsqrt(jnp.mean(x32 * x32, axis=-1, keepdims=True) + eps)).astype(x.dtype)


def rms_norm(x, g):
    return rms(x) * g.astype(x.dtype)


def modulate(h, shift, scale):
    return h * (1.0 + scale) + shift


def to_heads(t, n):
    B, T = t.shape[:2]
    return t.reshape(B, T, n, -1).transpose(0, 2, 1, 3)


def from_heads(o):
    B, H, T, d = o.shape
    return o.transpose(0, 2, 1, 3).reshape(B, T, H * d)


def to_diff_heads(t):
    B, T = t.shape[:2]
    return t.reshape(B, T, HEADS, 2, DIFF_DIM).transpose(0, 2, 3, 1, 4)


def split_cols(u):
    points, acc = [], 0
    for size in COL_SIZES[:-1]:
        acc += size
        points.append(acc)
    return jnp.split(u, points, axis=-1)


def rope_1d(x, pos):
    half = x.shape[-1] // 2
    inv = ROPE_BASE ** (-jnp.arange(half, dtype=F32) / half)
    ang = pos.astype(F32)[:, None] * inv
    cos, sin = jnp.cos(ang), jnp.sin(ang)
    x32 = x.astype(F32)
    x1, x2 = x32[..., :half], x32[..., half:]
    return jnp.concatenate([x1 * cos - x2 * sin, x1 * sin + x2 * cos], axis=-1).astype(x.dtype)


def rope_2d(x, row, col):
    h = x.shape[-1] // 2
    return jnp.concatenate([rope_1d(x[..., :h], row), rope_1d(x[..., h:], col)], axis=-1)


def attend_gqa(q, k, v, scale):
    s = jnp.einsum('bhgqd,bhkd->bhgqk', q, k).astype(F32) * scale
    p = jax.nn.softmax(s, axis=-1).astype(v.dtype)
    return jnp.einsum('bhgqk,bhkd->bhgqd', p, v)


def mixer_window(q, k, v, qc, kc, vc, g, sink, row, col, need_ctx):
    q = rope_2d(rms_norm(to_heads(q, HEADS), g[0]), row, col)
    k = rope_2d(rms_norm(to_heads(k, KV_HEADS_A), g[1]), row, col)
    v = to_heads(v, KV_HEADS_A)
    qc = rms_norm(to_heads(qc, HEADS), g[0])
    kc = rms_norm(to_heads(kc, KV_HEADS_A), g[1])
    vc = to_heads(vc, KV_HEADS_A)
    B, H, S, d = q.shape
    G = H // KV_HEADS_A
    NB, BQ, L = S // BLOCK_Q, BLOCK_Q, kc.shape[2]
    scale = d ** -0.5
    pad = ((0, 0), (0, 0), (BQ, BQ), (0, 0))
    kp, vp = jnp.pad(k, pad), jnp.pad(v, pad)

    def band(t):
        return jnp.concatenate([t[:, :, o * BQ:o * BQ + S].reshape(B, KV_HEADS_A, NB, BQ, d) for o in range(3)], axis=3)

    kb, vb = band(kp), band(vp)
    qb = q.reshape(B, KV_HEADS_A, G, NB, BQ, d)
    qpos = jnp.arange(S).reshape(NB, BQ, 1)
    kpos = (jnp.arange(NB)[:, None, None] - 1) * BQ + jnp.arange(3 * BQ)[None, None, :]
    valid = (jnp.abs(kpos - qpos) <= WINDOW) & (kpos >= 0) & (kpos < S)
    s_loc = jnp.where(valid, jnp.einsum('bhgnqd,bhnkd->bhgnqk', qb, kb).astype(F32) * scale, NEG)
    s_ctx = jnp.einsum('bhgnqd,bhkd->bhgnqk', qb, kc).astype(F32) * scale
    sink_b = sink.astype(F32).reshape(1, KV_HEADS_A, G, 1, 1, 1)
    s_sink = jnp.broadcast_to(sink_b, s_ctx.shape[:-1] + (1,))
    p = jax.nn.softmax(jnp.concatenate([s_loc, s_ctx, s_sink], axis=-1), axis=-1).astype(v.dtype)
    o = (jnp.einsum('bhgnqk,bhnkd->bhgnqd', p[..., :3 * BQ], vb)
         + jnp.einsum('bhgnqk,bhkd->bhgnqd', p[..., 3 * BQ:3 * BQ + L], vc))
    o = o.reshape(B, H, S, d)
    oc = None
    if need_ctx:
        qg = qc.reshape(B, KV_HEADS_A, G, L, d)
        s = jnp.einsum('bhgqd,bhkd->bhgqk', qg, kc).astype(F32) * scale
        s_sink_c = jnp.broadcast_to(sink_b[..., 0], s.shape[:-1] + (1,))
        pc = jax.nn.softmax(jnp.concatenate([s, s_sink_c], axis=-1), axis=-1).astype(vc.dtype)
        oc = from_heads(jnp.einsum('bhgqk,bhkd->bhgqd', pc[..., :L], vc).reshape(B, H, L, d))
    return from_heads(o), oc


def mixer_neighbourhood(q, k, v, qc, kc, vc, g, rpb, need_ctx):
    q = rms_norm(to_heads(q, HEADS), g[0])
    k = rms_norm(to_heads(k, HEADS), g[1])
    v = to_heads(v, HEADS)
    qc = rms_norm(to_heads(qc, HEADS), g[0])
    kc = rms_norm(to_heads(kc, HEADS), g[1])
    vc = to_heads(vc, HEADS)
    B, H, S, d = q.shape
    W = GRID_W
    R = S // W
    NR, NC = min(NA_ROWS, R), NA_COLS
    KN = NR * W
    scale = d ** -0.5
    qg, kg, vg = (t.reshape(B, H, R, W, d) for t in (q, k, v))
    q_col = jnp.arange(W)
    c_start = jnp.clip(q_col - NC // 2, 0, W - NC)
    k_col = jnp.tile(jnp.arange(W), NR)
    k_row = jnp.repeat(jnp.arange(NR), W)
    col_ok = (k_col[None, :] >= c_start[:, None]) & (k_col[None, :] < c_start[:, None] + NC)
    dc_idx = jnp.clip(k_col[None, :] - q_col[:, None] + NA_COLS - 1, 0, 2 * NA_COLS - 2)

    def row_block(r):
        r_start = jnp.clip(r - NR // 2, 0, R - NR)
        kr = lax.dynamic_slice_in_dim(kg, r_start, NR, axis=2).reshape(B, H, KN, d)
        vr = lax.dynamic_slice_in_dim(vg, r_start, NR, axis=2).reshape(B, H, KN, d)
        qr = lax.dynamic_index_in_dim(qg, r, axis=2, keepdims=False)
        dr_idx = r_start + k_row - r + NA_ROWS - 1
        bias = rpb[:, dr_idx[None, :], dc_idx].astype(F32)
        s_loc = jnp.einsum('bhqd,bhkd->bhqk', qr, kr).astype(F32) * scale + bias
        s_loc = jnp.where(col_ok, s_loc, NEG)
        s_ctx = jnp.einsum('bhqd,bhkd->bhqk', qr, kc).astype(F32) * scale
        p = jax.nn.softmax(jnp.concatenate([s_loc, s_ctx], axis=-1), axis=-1).astype(v.dtype)
        return jnp.einsum('bhqk,bhkd->bhqd', p[..., :KN], vr) + jnp.einsum('bhqk,bhkd->bhqd', p[..., KN:], vc)

    o = lax.map(row_block, jnp.arange(R))
    o = o.transpose(1, 2, 0, 3, 4).reshape(B, H, S, d)
    oc = from_heads(attend_gqa(qc[:, :, None], kc, vc, scale)[:, :, 0]) if need_ctx else None
    return from_heads(o), oc


def diff_attend(q, k, v, lam, scale):
    s = jnp.einsum('bhiqd,bhikd->bhiqk', q, k).astype(F32) * scale
    p = jax.nn.softmax(s, axis=-1)
    a = (p[:, :, 0] - lam * p[:, :, 1]).astype(v.dtype)
    return jnp.einsum('bhqk,bhkd->bhqd', a, v)


def mixer_diff(q, k, v, qc, kc, vc, g, lam_p, lam_init, row, col, need_ctx):
    q = rope_2d(rms_norm(to_diff_heads(q), g[0]), row, col)
    k = rope_2d(rms_norm(to_diff_heads(k), g[1]), row, col)
    v = to_heads(v, HEADS)
    qc = rms_norm(to_diff_heads(qc), g[0])
    kc = rms_norm(to_diff_heads(kc), g[1])
    vc = to_heads(vc, HEADS)
    B, H, _, S, dh = q.shape
    NB = S // BLOCK_Q
    scale = dh ** -0.5
    lp = lam_p.astype(F32)
    lam = jnp.exp(jnp.sum(lp[0] * lp[1])) - jnp.exp(jnp.sum(lp[2] * lp[3])) + lam_init
    k_all = jnp.concatenate([k, kc], axis=3)
    v_all = jnp.concatenate([v, vc], axis=2)
    qb = q.reshape(B, H, 2, NB, BLOCK_Q, dh).transpose(3, 0, 1, 2, 4, 5)
    o = lax.map(lambda blk: diff_attend(blk, k_all, v_all, lam, scale), qb)
    o = o.transpose(1, 2, 0, 3, 4).reshape(B, H, S, -1)

    def post(t):
        return from_heads(rms(t) * (1.0 - lam_init))

    oc = post(diff_attend(qc, kc, vc, lam, scale)) if need_ctx else None
    return post(o), oc


def mixer_dense(q, k, v, qc, kc, vc, g, row, col, need_ctx):
    q = rope_2d(rms_norm(to_heads(q, HEADS), g[0]), row, col)
    k = rope_2d(rms_norm(to_heads(k, KV_HEADS_D), g[1]), row, col)
    v = to_heads(v, KV_HEADS_D)
    qc = rms_norm(to_heads(qc, HEADS), g[0])
    kc = rms_norm(to_heads(kc, KV_HEADS_D), g[1])
    vc = to_heads(vc, KV_HEADS_D)
    B, H, S, d = q.shape
    G = H // KV_HEADS_D
    NB, L = S // BLOCK_Q, kc.shape[2]
    scale = d ** -0.5
    k_all = jnp.concatenate([k, kc], axis=2)
    v_all = jnp.concatenate([v, vc], axis=2)
    qb = q.reshape(B, KV_HEADS_D, G, NB, BLOCK_Q, d).transpose(3, 0, 1, 2, 4, 5)
    o = lax.map(lambda blk: attend_gqa(blk, k_all, v_all, scale), qb)
    o = o.transpose(1, 2, 3, 0, 4, 5).reshape(B, H, S, d)
    oc = from_heads(attend_gqa(qc.reshape(B, KV_HEADS_D, G, L, d), kc, vc, scale).reshape(B, H, L, d)) if need_ctx else None
    return from_heads(o), oc


def token_mix(u, uc, layer, g_win, g_na, g_diff, g_gqa, sink, rpb, lam_p, gain, row, col, need_ctx):
    pl, pc = split_cols(u), split_cols(uc)
    lam_init = 0.8 - 0.6 * math.exp(-0.3 * layer)
    y_win, c_win = mixer_window(*pl[0:3], *pc[0:3], g_win, sink, row, col, need_ctx)
    y_na, c_na = mixer_neighbourhood(*pl[3:6], *pc[3:6], g_na, rpb, need_ctx)
    y_diff, c_diff = mixer_diff(*pl[6:9], *pc[6:9], g_diff, lam_p, lam_init, row, col, need_ctx)
    y_gqa, c_gqa = mixer_dense(*pl[9:12], *pc[9:12], g_gqa, row, col, need_ctx)

    def merge(a, b, dd, e):
        return jnp.concatenate([rms(a), rms(b), dd, rms(e)], axis=-1) * gain.astype(a.dtype)

    y = merge(y_win, y_na, y_diff, y_gqa)
    yc = merge(c_win, c_na, c_diff, c_gqa) if need_ctx else None
    return y, yc


def route(h, router_w, router_b):
    N = h.shape[0]
    score = jax.nn.sigmoid((h @ router_w).astype(F32))
    sel = score + router_b.astype(F32)
    sel_g = sel.reshape(N, N_EXPERT_GROUPS, EXPERTS_PER_GROUP)
    g_score = lax.top_k(sel_g, TOP_K)[0].sum(-1)
    g_idx = jnp.argmax(g_score, axis=-1).astype(jnp.int32)
    idx = jnp.broadcast_to(g_idx[:, None, None], (N, 1, EXPERTS_PER_GROUP))
    in_group = jnp.take_along_axis(sel_g, idx, axis=1)[:, 0]
    _, local = lax.top_k(in_group, TOP_K)
    expert = (g_idx[:, None] * EXPERTS_PER_GROUP + local).astype(jnp.int32)
    w = jnp.take_along_axis(score, expert, axis=-1)
    return expert, w / jnp.sum(w, axis=-1, keepdims=True)


def moe_ffn(h, router_w, router_b, w_gate, w_up, w_down):
    shp = h.shape
    hf = h.reshape(-1, shp[-1])
    N = hf.shape[0]
    expert, w = route(hf, router_w, router_b)
    A = N * TOP_K
    e_flat = expert.reshape(A)
    t_flat = jnp.repeat(jnp.arange(N, dtype=jnp.int32), TOP_K)
    w_flat = w.reshape(A)
    order = jnp.argsort(e_flat)
    e_s, t_s, w_s = e_flat[order], t_flat[order], w_flat[order]
    counts = jnp.bincount(e_flat, length=N_EXPERTS)
    padded = (counts + MOE_BLOCK - 1) // MOE_BLOCK * MOE_BLOCK
    p_end = jnp.cumsum(padded)
    p_start = p_end - padded
    c_start = jnp.cumsum(counts) - counts
    dest = p_start[e_s] + jnp.arange(A, dtype=jnp.int32) - c_start[e_s]
    n_blocks = -(-A // MOE_BLOCK) + N_EXPERTS
    P = n_blocks * MOE_BLOCK
    buf_t = jnp.zeros((P,), jnp.int32).at[dest].set(t_s)
    buf_w = jnp.zeros((P,), F32).at[dest].set(w_s)
    blk_e = jnp.minimum(jnp.searchsorted(p_end, jnp.arange(n_blocks, dtype=jnp.int32) * MOE_BLOCK, side='right'),
                        N_EXPERTS - 1)
    xb = hf[buf_t].reshape(n_blocks, MOE_BLOCK, shp[-1])

    def expert_block(args):
        xe, e = args
        return (jax.nn.silu(xe @ w_gate[e]) * (xe @ w_up[e])) @ w_down[e]

    yb = lax.map(expert_block, (xb, blk_e)).reshape(P, shp[-1])
    y = jnp.zeros_like(hf).at[buf_t].add(yb * buf_w[:, None].astype(yb.dtype))
    return y.reshape(shp)


def setup_inputs(seed: int = 0) -> dict:
    key = jax.random.key(seed)
    ks = jax.random.split(key, 23)
    D = D_MODEL

    def nrm(k, shape, s):
        return jax.random.normal(k, shape, jnp.float32) * s

    def gain(k, shape):
        return 1.0 + 0.02 * jax.random.normal(k, shape, jnp.float32)

    return {
        "x": nrm(ks[0], (BATCH, SEQ, D), 1.0),
        "c": nrm(ks[1], (BATCH, D), 1.0),
        "ctx": nrm(ks[2], (BATCH, CTX_LEN, D), 1.0),
        "c_ctx": nrm(ks[3], (D,), 1.0),
        "ada_w": nrm(ks[4], (DEPTH, D, ADA_CHUNKS * D), 0.25 * D ** -0.5),
        "ada_b": nrm(ks[5], (DEPTH, ADA_CHUNKS * D), 0.02),
        "norm_mix_g": gain(ks[6], (DEPTH, D)),
        "norm_ffn_g": gain(ks[7], (DEPTH, D)),
        "w_in": nrm(ks[8], (DEPTH, D, D_IN), D ** -0.5),
        "qk_g_win": gain(ks[9], (DEPTH, 2, HEAD_DIM)),
        "qk_g_na": gain(ks[10], (DEPTH, 2, HEAD_DIM)),
        "qk_g_diff": gain(ks[11], (DEPTH, 2, DIFF_DIM)),
        "qk_g_gqa": gain(ks[12], (DEPTH, 2, HEAD_DIM)),
        "sink_win": nrm(ks[13], (DEPTH, HEADS), 0.5),
        "rpb_na": nrm(ks[14], (DEPTH, HEADS, 2 * NA_ROWS - 1, 2 * NA_COLS - 1), 0.1),
        "lambda_diff": nrm(ks[15], (DEPTH, 4, DIFF_DIM), 0.1),
        "out_gain": gain(ks[16], (DEPTH, D)),
        "w_out": nrm(ks[17], (DEPTH, D, D), D ** -0.5),
        "router_w": nrm(ks[18], (D, N_EXPERTS), D ** -0.5),
        "router_b": nrm(ks[19], (N_EXPERTS,), 0.01),
        "w_gate": nrm(ks[20], (DEPTH, N_EXPERTS, D, D_EXPERT), D ** -0.5),
        "w_up": nrm(ks[21], (DEPTH, N_EXPERTS, D, D_EXPERT), D ** -0.5),
        "w_down": nrm(ks[22], (DEPTH, N_EXPERTS, D_EXPERT, D), D_EXPERT ** -0.5),
    }


def reference(x, c, ctx, c_ctx, ada_w, ada_b, norm_mix_g, norm_ffn_g, w_in, qk_g_win, qk_g_na, qk_g_diff,
              qk_g_gqa, sink_win, rpb_na, lambda_diff, out_gain, w_out, router_w, router_b, w_gate, w_up, w_down):
    S = x.shape[1]
    pos = jnp.arange(S, dtype=jnp.int32)
    row, col = pos // GRID_W, pos % GRID_W
    s_lat = jax.nn.silu(c)
    s_ctx = jax.nn.silu(c_ctx)
    for layer in range(DEPTH):
        need_ctx = layer < DEPTH - 1
        m_lat = jnp.split((s_lat @ ada_w[layer] + ada_b[layer])[:, None, :], ADA_CHUNKS, axis=-1)
        m_ctx = jnp.split(s_ctx @ ada_w[layer] + ada_b[layer], ADA_CHUNKS, axis=-1)
        h = modulate(rms_norm(x, norm_mix_g[layer]), m_lat[0], m_lat[1])
        hc = modulate(rms_norm(ctx, norm_mix_g[layer]), m_ctx[0], m_ctx[1])
        y, yc = token_mix(h @ w_in[layer], hc @ w_in[layer], layer, qk_g_win[layer], qk_g_na[layer],
                          qk_g_diff[layer], qk_g_gqa[layer], sink_win[layer], rpb_na[layer], lambda_diff[layer],
                          out_gain[layer], row, col, need_ctx)
        x = x + m_lat[2] * (y @ w_out[layer])
        h2 = modulate(rms_norm(x, norm_ffn_g[layer]), m_lat[3], m_lat[4])
        x = x + m_lat[5] * moe_ffn(h2, router_w, router_b, w_gate[layer], w_up[layer], w_down[layer])
        if need_ctx:
            ctx = ctx + m_ctx[2] * (yc @ w_out[layer])
            hc2 = modulate(rms_norm(ctx, norm_ffn_g[layer]), m_ctx[3], m_ctx[4])
            ctx = ctx + m_ctx[5] * moe_ffn(hc2, router_w, router_b, w_gate[layer], w_up[layer], w_down[layer])
    return x
```

```python
import functools
import math

import numpy as np
import jax
import jax.numpy as jnp
from jax import lax
from jax.experimental import pallas as pl
from jax.experimental.pallas import tpu as pltpu

F32 = jnp.float32
BF16 = jnp.bfloat16
I32 = jnp.int32

D_MODEL = 1024
GRID_W = 64
HEAD_DIM = 64
HEADS = 4
DIFF_DIM = 32
WINDOW = 128
NA_ROWS = 8
NA_COLS = 16
ROPE_BASE = 10000.0
N_EXPERTS = 16
EXPERTS_PER_GROUP = 4
D_EXPERT = 512
ADA_CHUNKS = 6
EPS = 1e-6
NEG = -1e30
D_IN = 2560

COL_A_Q, COL_A_KV, COL_B_Q, COL_B_K, COL_B_V, COL_C_Q, COL_C_K, COL_C_V, COL_D_Q, COL_D_KV = range(10)

TOKEN_TILE = 512
ATT_TQ = 256
NA_QROWS = 4
NA_KROWS = NA_ROWS + NA_QROWS - 1
MOE_BLK = 256
ROW_DMA = 256
N_PAIRS = 6
N_CLASSES = (N_EXPERTS // EXPERTS_PER_GROUP) * N_PAIRS
PAIR_LO = (0, 0, 0, 1, 1, 2)
PAIR_HI = (1, 2, 3, 2, 3, 3)
EXT_W = D_MODEL + 128
VMEM_LIMIT = 56 * 1024 * 1024


def _cparams(*sem):
    return pltpu.CompilerParams(dimension_semantics=sem, vmem_limit_bytes=VMEM_LIMIT)


def _nt_dot(a, b):
    return lax.dot_general(a, b, (((1,), (1,)), ((), ())), preferred_element_type=F32)


def _dot(a, b):
    return jnp.dot(a, b, preferred_element_type=F32)


def _sigmoid(x):
    return 1.0 / (1.0 + jnp.exp(-x))


def _drop_arg(kern, pos):
    def wrapped(*refs):
        return kern(*refs[:pos], *refs[pos + 1:])
    return wrapped


def _ada_kernel(s_ref, w_ref, b_ref, o_ref):
    s = s_ref[...]
    act = s * _sigmoid(s)
    o_ref[0] = jnp.dot(act, w_ref[0], precision=lax.Precision.HIGHEST,
                       preferred_element_type=F32) + b_ref[0]


def _ada_call(s_all, ada_w, ada_b):
    depth, d, n = ada_w.shape
    rows = s_all.shape[0]
    tn = 1536
    return pl.pallas_call(
        _ada_kernel,
        out_shape=jax.ShapeDtypeStruct((depth, rows, n), F32),
        grid=(depth, n // tn),
        in_specs=[
            pl.BlockSpec((rows, d), lambda l, j: (0, 0)),
            pl.BlockSpec((1, d, tn), lambda l, j: (l, 0, j)),
            pl.BlockSpec((1, 1, tn), lambda l, j: (l, 0, j)),
        ],
        out_specs=pl.BlockSpec((1, rows, tn), lambda l, j: (l, 0, j)),
        compiler_params=_cparams("arbitrary", "arbitrary"),
        name="ada_mod",
    )(s_all, ada_w, ada_b.reshape(depth, 1, n))


def _inproj_kernel(*refs, has_ffn, tm, n_lat_tiles, tiles_per_batch, seq):
    if has_ffn:
        (x_ref, f_ref, modp_ref, mod_ref, g_ref, w_ref, qkg_ref, bd64_ref, bd32_ref,
         c64_ref, s64_ref, c32_ref, s32_ref, xo_ref, u_ref) = refs
    else:
        (x_ref, mod_ref, g_ref, w_ref, qkg_ref, bd64_ref, bd32_ref,
         c64_ref, s64_ref, c32_ref, s32_ref, u_ref) = refs
    i = pl.program_id(0)
    x = x_ref[...]
    if has_ffn:
        x = x + modp_ref[0, 5:6, :] * f_ref[...]
        xo_ref[...] = x
    ms = jnp.mean(x * x, axis=-1, keepdims=True)
    hn = x * lax.rsqrt(ms + EPS) * g_ref[...]
    h = hn * (1.0 + mod_ref[0, 1:2, :]) + mod_ref[0, 0:1, :]
    hb = h.astype(BF16)

    p0 = jnp.where(i < n_lat_tiles, (i % tiles_per_batch) * tm, seq)
    p0 = pl.multiple_of(p0, tm)
    lane = lax.broadcasted_iota(I32, (tm, 128), 1)

    def norm_rope(a, gidx, group, bd_ref, rope_tabs):
        w = a.shape[1]
        ssq = _dot((a * a).astype(BF16), bd_ref[0:w, 0:w])
        r = lax.rsqrt(ssq * (1.0 / group) + EPS)
        outs = []
        for s in range(w // 128):
            t = a[:, 128 * s:128 * s + 128] * r[:, 128 * s:128 * s + 128] * qkg_ref[gidx:gidx + 1, :]
            if rope_tabs is not None:
                c_ref, s_ref, half = rope_tabs
                fwd = pltpu.roll(t, 128 - half, axis=1)
                bwd = pltpu.roll(t, half, axis=1)
                sw = jnp.where((lane % (2 * half)) < half, fwd, bwd)
                t = t * c_ref[pl.ds(p0, tm), :] + sw * s_ref[pl.ds(p0, tm), :]
            outs.append(t)
        return outs[0] if len(outs) == 1 else jnp.concatenate(outs, axis=1)

    rope64 = (c64_ref, s64_ref, HEAD_DIM // 4)
    rope32 = (c32_ref, s32_ref, DIFF_DIM // 4)
    for blk in range(D_IN // 256):
        acc = _dot(hb, w_ref[:, 256 * blk:256 * blk + 256])
        if blk == COL_A_Q:
            out = norm_rope(acc, 0, HEAD_DIM, bd64_ref, rope64)
        elif blk == COL_A_KV:
            out = jnp.concatenate([norm_rope(acc[:, :128], 1, HEAD_DIM, bd64_ref, rope64), acc[:, 128:]], axis=1)
        elif blk == COL_B_Q:
            out = norm_rope(acc, 2, HEAD_DIM, bd64_ref, None)
        elif blk == COL_B_K:
            out = norm_rope(acc, 3, HEAD_DIM, bd64_ref, None)
        elif blk == COL_C_Q:
            out = norm_rope(acc, 4, DIFF_DIM, bd32_ref, rope32)
        elif blk == COL_C_K:
            out = norm_rope(acc, 5, DIFF_DIM, bd32_ref, rope32)
        elif blk == COL_D_Q:
            out = norm_rope(acc, 6, HEAD_DIM, bd64_ref, rope64)
        elif blk == COL_D_KV:
            out = jnp.concatenate([norm_rope(acc[:, :128], 7, HEAD_DIM, bd64_ref, rope64), acc[:, 128:]], axis=1)
        else:
            out = acc
        u_ref[:, 256 * blk:256 * blk + 256] = out.astype(BF16)


def _mod_spec(n_lat_tiles, tiles_per_batch, batch):
    return pl.BlockSpec((1, ADA_CHUNKS, D_MODEL),
                        lambda i: (jnp.where(i < n_lat_tiles, i // tiles_per_batch, batch), 0, 0))


def _const_spec(shape):
    return pl.BlockSpec(shape, lambda i: (0,) * len(shape))


def _inproj_call(x, ffn, mod_prev, mod, g, w_in, qkg, bd64, bd32, tabs, *, batch, seq):
    n_all = x.shape[0]
    tm = TOKEN_TILE
    n_lat_tiles = batch * seq // tm
    tpb = seq // tm
    has_ffn = ffn is not None
    tile = pl.BlockSpec((tm, D_MODEL), lambda i: (i, 0))
    modspec = _mod_spec(n_lat_tiles, tpb, batch)
    tab_rows = tabs[0].shape[0]
    in_specs = [tile]
    args = [x]
    if has_ffn:
        in_specs += [tile, modspec]
        args += [ffn, mod_prev]
    in_specs += [modspec, _const_spec((1, D_MODEL)), _const_spec((D_MODEL, D_IN)), _const_spec((8, 128)),
                 _const_spec((256, 256)), _const_spec((256, 256))] + [_const_spec((tab_rows, 128))] * 4
    args += [mod, g, w_in, qkg, bd64, bd32] + list(tabs)
    u_shape = jax.ShapeDtypeStruct((n_all, D_IN), BF16)
    u_spec = pl.BlockSpec((tm, D_IN), lambda i: (i, 0))
    if has_ffn:
        out_shape = (jax.ShapeDtypeStruct((n_all, D_MODEL), F32), u_shape)
        out_specs = (tile, u_spec)
    else:
        out_shape, out_specs = u_shape, u_spec
    kern = functools.partial(_inproj_kernel, has_ffn=has_ffn, tm=tm, n_lat_tiles=n_lat_tiles,
                             tiles_per_batch=tpb, seq=seq)
    res = pl.pallas_call(
        kern, out_shape=out_shape, grid=(n_all // tm,), in_specs=in_specs, out_specs=out_specs,
        compiler_params=_cparams("arbitrary"), name="inproj",
    )(*args)
    return res if has_ffn else (x, res)


def _softmax_parts(scores, extra=None):
    m = scores[0].max(axis=-1, keepdims=True)
    for s in scores[1:]:
        m = jnp.maximum(m, s.max(axis=-1, keepdims=True))
    if extra is not None:
        m = jnp.maximum(m, extra)
    ps = [jnp.exp(s - m) for s in scores]
    l = ps[0].sum(axis=-1, keepdims=True)
    for p in ps[1:]:
        l = l + p.sum(axis=-1, keepdims=True)
    if extra is not None:
        l = l + jnp.exp(extra - m)
    return ps, 1.0 / l


def _group_rms(y):
    return y * lax.rsqrt(jnp.mean(y * y, axis=-1, keepdims=True) + EPS)


def _gqa_pair(q_ref, j):
    return jnp.concatenate([q_ref[:, 128 * j:128 * j + 64], q_ref[:, 128 * j + 64:128 * j + 128]], axis=0)


def _dense_kernel(q_ref, kv_ref, kvc_ref, o_ref, *, tq):
    outs = [None] * HEADS
    for j in range(2):
        k = kv_ref[:, 64 * j:64 * j + 64]
        v = kv_ref[:, 128 + 64 * j:128 + 64 * j + 64]
        kc = kvc_ref[:, 64 * j:64 * j + 64]
        vc = kvc_ref[:, 128 + 64 * j:128 + 64 * j + 64]
        q2 = _gqa_pair(q_ref, j)
        (p, pc), inv = _softmax_parts([_nt_dot(q2, k), _nt_dot(q2, kc)])
        o = (_dot(p.astype(BF16), v) + _dot(pc.astype(BF16), vc)) * inv
        outs[2 * j], outs[2 * j + 1] = o[:tq], o[tq:]
    o_ref[...] = _group_rms(jnp.concatenate(outs, axis=1)).astype(BF16)


def _window_kernel(sink_ref, q_ref, kv_ref, kvc_ref, o_ref, *, tq, seq):
    kw = tq + 2 * WINDOW
    q0 = pl.program_id(1) * tq
    ks = pl.multiple_of(jnp.clip(q0 - WINDOW, 0, seq - kw), 128)
    qpos = q0 + lax.broadcasted_iota(I32, (2 * tq, kw), 0) % tq
    kpos = ks + lax.broadcasted_iota(I32, (2 * tq, kw), 1)
    valid = jnp.abs(kpos - qpos) <= WINDOW
    top = lax.broadcasted_iota(I32, (2 * tq, 1), 0) < tq
    outs = [None] * HEADS
    for j in range(2):
        k = kv_ref[pl.ds(ks, kw), 64 * j:64 * j + 64]
        v = kv_ref[pl.ds(ks, kw), 128 + 64 * j:128 + 64 * j + 64]
        kc = kvc_ref[:, 64 * j:64 * j + 64]
        vc = kvc_ref[:, 128 + 64 * j:128 + 64 * j + 64]
        q2 = _gqa_pair(q_ref, j)
        s = jnp.where(valid, _nt_dot(q2, k), NEG)
        snk = jnp.where(top, sink_ref[2 * j], sink_ref[2 * j + 1])
        (p, pc), inv = _softmax_parts([s, _nt_dot(q2, kc)], extra=snk)
        o = (_dot(p.astype(BF16), v) + _dot(pc.astype(BF16), vc)) * inv
        outs[2 * j], outs[2 * j + 1] = o[:tq], o[tq:]
    o_ref[...] = _group_rms(jnp.concatenate(outs, axis=1)).astype(BF16)


def _neigh_kernel(q_ref, k_ref, v_ref, kc_ref, vc_ref, bias_ref, o_ref, *, rows):
    rb = pl.program_id(1)
    rs = jnp.clip(NA_QROWS * rb - NA_ROWS // 2, 0, rows - NA_KROWS)
    k0 = pl.multiple_of(rs * GRID_W, GRID_W)
    nk = NA_KROWS * GRID_W
    outs = []
    for h in range(HEADS):
        sl = slice(64 * h, 64 * h + 64)
        q = q_ref[:, sl]
        s = _nt_dot(q, k_ref[pl.ds(k0, nk), sl]) + bias_ref[0, h]
        (p, pc), inv = _softmax_parts([s, _nt_dot(q, kc_ref[:, sl])])
        o = (_dot(p.astype(BF16), v_ref[pl.ds(k0, nk), sl]) + _dot(pc.astype(BF16), vc_ref[:, sl])) * inv
        outs.append(o)
    o_ref[...] = _group_rms(jnp.concatenate(outs, axis=1)).astype(BF16)


def _diff_lambda(lam_ref, lam_init):
    lp = lam_ref[...]
    a = jnp.sum(lp[0:1] * lp[1:2], axis=-1, keepdims=True)
    b = jnp.sum(lp[2:3] * lp[3:4], axis=-1, keepdims=True)
    return jnp.exp(a) - jnp.exp(b) + lam_init


def _diff_heads(q, k_parts, v_parts, lam, lam_init):
    t = q.shape[0]
    first = lax.broadcasted_iota(I32, (t, HEAD_DIM), 1) < DIFF_DIM
    zero = jnp.zeros((t, HEAD_DIM), BF16)
    outs = []
    for h in range(HEADS):
        sl = slice(64 * h, 64 * h + 64)
        qh = q[:, sl]
        ks = [kp[:, sl] for kp in k_parts]
        maps = []
        for half in range(2):
            qm = jnp.where(first, qh, zero) if half == 0 else jnp.where(first, zero, qh)
            ps, inv = _softmax_parts([_nt_dot(qm, kk) for kk in ks])
            maps.append([p * inv for p in ps])
        o = None
        for a0, a1, vp in zip(maps[0], maps[1], v_parts):
            c = _dot((a0 - lam * a1).astype(BF16), vp[:, sl])
            o = c if o is None else o + c
        outs.append(_group_rms(o) * (1.0 - lam_init))
    return jnp.concatenate(outs, axis=1)


def _diff_kernel(lam_ref, q_ref, k_ref, v_ref, kc_ref, vc_ref, o_ref, *, lam_init):
    lam = _diff_lambda(lam_ref, lam_init)
    o_ref[...] = _diff_heads(q_ref[...], [k_ref, kc_ref], [v_ref, vc_ref], lam, lam_init).astype(BF16)


class _Cols:
    def __init__(self, ref, blk):
        self.ref, self.base = ref, 256 * blk

    def __getitem__(self, idx):
        rows, sl = idx
        return self.ref[rows, self.base + sl.start:self.base + sl.stop]


def _ctx_kernel(sink_ref, lam_ref, u_ref, o_ref, *, lam_init):
    L = u_ref.shape[0]
    top = lax.broadcasted_iota(I32, (2 * L, 1), 0) < L
    for q_blk, kv_blk, out_col, use_sink in ((COL_A_Q, COL_A_KV, 0, True), (COL_D_Q, COL_D_KV, 768, False)):
        qv, kv = _Cols(u_ref, q_blk), _Cols(u_ref, kv_blk)
        outs = [None] * HEADS
        for j in range(2):
            kc = kv[:, slice(64 * j, 64 * j + 64)]
            vc = kv[:, slice(128 + 64 * j, 128 + 64 * j + 64)]
            q2 = jnp.concatenate([qv[:, slice(128 * j, 128 * j + 64)], qv[:, slice(128 * j + 64, 128 * j + 128)]], axis=0)
            snk = jnp.where(top, sink_ref[2 * j], sink_ref[2 * j + 1]) if use_sink else None
            (p,), inv = _softmax_parts([_nt_dot(q2, kc)], extra=snk)
            o = _dot(p.astype(BF16), vc) * inv
            outs[2 * j], outs[2 * j + 1] = o[:L], o[L:]
        o_ref[:, out_col:out_col + 256] = _group_rms(jnp.concatenate(outs, axis=1)).astype(BF16)
    qv, kv, vv = _Cols(u_ref, COL_B_Q), _Cols(u_ref, COL_B_K), _Cols(u_ref, COL_B_V)
    outs = []
    for h in range(HEADS):
        sl = slice(64 * h, 64 * h + 64)
        (p,), inv = _softmax_parts([_nt_dot(qv[:, sl], kv[:, sl])])
        outs.append(_dot(p.astype(BF16), vv[:, sl]) * inv)
    o_ref[:, 256:512] = _group_rms(jnp.concatenate(outs, axis=1)).astype(BF16)
    lam = _diff_lambda(lam_ref, lam_init)
    qc = u_ref[:, 256 * COL_C_Q:256 * COL_C_Q + 256]
    o_ref[:, 512:768] = _diff_heads(qc, [_Cols(u_ref, COL_C_K)], [_Cols(u_ref, COL_C_V)], lam, lam_init).astype(BF16)


def _attention_calls(u, sink, lam_p, bias, *, layer, batch, seq, ctx_len, need_ctx):
    lam_init = 0.8 - 0.6 * math.exp(-0.3 * layer)
    n_all = u.shape[0]
    tq = ATT_TQ
    nq = seq // tq
    cb0 = batch * seq // ctx_len
    y_shape = jax.ShapeDtypeStruct((n_all, D_MODEL), BF16)
    smem = pl.BlockSpec(memory_space=pltpu.SMEM)
    any_spec = pl.BlockSpec(memory_space=pl.ANY)
    lam_spec2 = pl.BlockSpec((4, DIFF_DIM), lambda b, i: (0, 0))
    cp = _cparams("arbitrary", "arbitrary")

    def qspec(col):
        return pl.BlockSpec((tq, 256), lambda b, i: (b * nq + i, col))

    def latspec(col):
        return pl.BlockSpec((seq, 256), lambda b, i: (b, col))

    def ctxspec(col):
        return pl.BlockSpec((ctx_len, 256), lambda b, i: (cb0 + b, col))

    y = pl.pallas_call(
        functools.partial(_window_kernel, tq=tq, seq=seq),
        out_shape=y_shape, grid=(batch, nq),
        in_specs=[smem, qspec(COL_A_Q), latspec(COL_A_KV), ctxspec(COL_A_KV)],
        out_specs=qspec(0), compiler_params=cp, name="mix_window",
    )(sink, u, u, u)

    rows = seq // GRID_W
    nrb = rows // NA_QROWS
    qrows = NA_QROWS * GRID_W

    def bias_case(b, i):
        return (jnp.where(i == 0, 0, jnp.where(i == nrb - 1, 2, 1)), 0, 0, 0)

    y = pl.pallas_call(
        _drop_arg(functools.partial(_neigh_kernel, rows=rows), 6),
        out_shape=y_shape, grid=(batch, nrb),
        in_specs=[pl.BlockSpec((qrows, 256), lambda b, i: (b * nrb + i, COL_B_Q)),
                  latspec(COL_B_K), latspec(COL_B_V), ctxspec(COL_B_K), ctxspec(COL_B_V),
                  pl.BlockSpec((1, HEADS, qrows, NA_KROWS * GRID_W), bias_case), any_spec],
        out_specs=pl.BlockSpec((qrows, 256), lambda b, i: (b * nrb + i, 1)),
        input_output_aliases={6: 0}, compiler_params=cp, name="mix_neigh",
    )(u, u, u, u, u, bias, y)

    y = pl.pallas_call(
        _drop_arg(functools.partial(_diff_kernel, lam_init=lam_init), 6),
        out_shape=y_shape, grid=(batch, nq),
        in_specs=[lam_spec2, qspec(COL_C_Q), latspec(COL_C_K), latspec(COL_C_V),
                  ctxspec(COL_C_K), ctxspec(COL_C_V), any_spec],
        out_specs=qspec(2), input_output_aliases={6: 0}, compiler_params=cp, name="mix_diff",
    )(lam_p, u, u, u, u, u, y)

    y = pl.pallas_call(
        _drop_arg(functools.partial(_dense_kernel, tq=tq), 3),
        out_shape=y_shape, grid=(batch, nq),
        in_specs=[qspec(COL_D_Q), latspec(COL_D_KV), ctxspec(COL_D_KV), any_spec],
        out_specs=qspec(3), input_output_aliases={3: 0}, compiler_params=cp, name="mix_dense",
    )(u, u, u, y)

    if need_ctx:
        y = pl.pallas_call(
            _drop_arg(functools.partial(_ctx_kernel, lam_init=lam_init), 3),
            out_shape=y_shape, grid=(batch,),
            in_specs=[smem, pl.BlockSpec((4, DIFF_DIM), lambda b: (0, 0)),
                      pl.BlockSpec((ctx_len, D_IN), lambda b: (cb0 + b, 0)), any_spec],
            out_specs=pl.BlockSpec((ctx_len, D_MODEL), lambda b: (cb0 + b, 0)),
            input_output_aliases={3: 0}, compiler_params=_cparams("arbitrary"), name="mix_ctx",
        )(sink, lam_p, u, y)
    return y


def _select4(idx, vals):
    return jnp.where(idx == 0, vals[0], jnp.where(idx == 1, vals[1], jnp.where(idx == 2, vals[2], vals[3])))


def _outproj_kernel(y_ref, x_ref, mod_ref, gain_ref, w_ref, g_ref, rw_ref, rb_ref, xo_ref, h_ref, cls_ref, *, tm):
    yg = (y_ref[...].astype(F32) * gain_ref[...]).astype(BF16)
    xm = x_ref[...] + mod_ref[0, 2:3, :] * _dot(yg, w_ref[...])
    xo_ref[...] = xm
    hn = xm * lax.rsqrt(jnp.mean(xm * xm, axis=-1, keepdims=True) + EPS) * g_ref[...]
    h2 = hn * (1.0 + mod_ref[0, 4:5, :]) + mod_ref[0, 3:4, :]
    h_ref[:, 0:D_MODEL] = h2

    hi = h2.astype(BF16)
    lo = (h2 - hi.astype(F32)).astype(BF16)
    lt = (_dot(hi, rw_ref[...]) + _dot(lo, rw_ref[...])).T
    score = _sigmoid(lt[0:N_EXPERTS] + lt[N_EXPERTS:2 * N_EXPERTS])
    sel = score + rb_ref[...]
    srow = [sel[e:e + 1] for e in range(N_EXPERTS)]
    crow = [score[e:e + 1] for e in range(N_EXPERTS)]

    gidx = jnp.zeros((1, tm), I32)
    gbest = None
    for g in range(N_EXPERTS // EXPERTS_PER_GROUP):
        v = srow[4 * g:4 * g + 4]
        best = None
        for a in range(4):
            for b in range(a + 1, 4):
                t = v[a] + v[b]
                best = t if best is None else jnp.maximum(best, t)
        if g == 0:
            gbest = best
        else:
            better = best > gbest
            gidx = jnp.where(better, g, gidx)
            gbest = jnp.where(better, best, gbest)

    iv = [_select4(gidx, [srow[4 * g + i] for g in range(4)]) for i in range(4)]
    sv = [_select4(gidx, [crow[4 * g + i] for g in range(4)]) for i in range(4)]
    chosen = []
    for i in range(4):
        rank = jnp.zeros((1, tm), I32)
        for j in range(4):
            if j != i:
                beats = (iv[j] >= iv[i]) if j < i else (iv[j] > iv[i])
                rank = rank + jnp.where(beats, 1, 0)
        chosen.append(rank < 2)
    lo_i = jnp.where(chosen[0], 0, jnp.where(chosen[1], 1, 2))
    hi_i = jnp.where(chosen[3], 3, jnp.where(chosen[2], 2, 1))
    pair = jnp.where(lo_i == 0, hi_i - 1, jnp.where(lo_i == 1, hi_i + 1, 5))
    cls_ref[0] = gidx * N_PAIRS + pair
    s_lo = jnp.where(lo_i == 0, sv[0], jnp.where(lo_i == 1, sv[1], sv[2]))
    s_hi = jnp.where(hi_i == 3, sv[3], jnp.where(hi_i == 2, sv[2], sv[1]))
    inv = 1.0 / (s_lo + s_hi)
    rowi = lax.broadcasted_iota(I32, (128, tm), 0)
    wts = jnp.where(rowi == 0, s_lo * inv, jnp.where(rowi == 1, s_hi * inv, 0.0))
    h_ref[:, D_MODEL:EXT_W] = wts.T


def _outproj_call(y, x, mod, gain, w_out, g, rw, rb, *, n_proc, batch, seq):
    tm = TOKEN_TILE
    n_tiles = n_proc // tm
    tile = pl.BlockSpec((tm, D_MODEL), lambda i: (i, 0))
    return pl.pallas_call(
        functools.partial(_outproj_kernel, tm=tm),
        out_shape=(jax.ShapeDtypeStruct((n_proc, D_MODEL), F32),
                   jax.ShapeDtypeStruct((n_proc, EXT_W), F32),
                   jax.ShapeDtypeStruct((n_tiles, 1, tm), I32)),
        grid=(n_tiles,),
        in_specs=[tile, tile, _mod_spec(batch * seq // tm, seq // tm, batch), _const_spec((1, D_MODEL)),
                  _const_spec((D_MODEL, D_MODEL)), _const_spec((1, D_MODEL)), _const_spec((D_MODEL, 128)),
                  _const_spec((N_EXPERTS, 1))],
        out_specs=(tile, pl.BlockSpec((tm, EXT_W), lambda i: (i, 0)), pl.BlockSpec((1, 1, tm), lambda i: (i, 0, 0))),
        compiler_params=_cparams("arbitrary"), name="outproj_router",
    )(y, x, mod, gain, w_out, g, rw, rb)


def _rank_kernel(cls_ref, pos_ref, bcls_ref, nval_ref, *, blk):
    c = cls_ref[...]
    nr = c.shape[0]
    nbp = bcls_ref.shape[1]
    upper = (lax.broadcasted_iota(I32, (128, 128), 0) <= lax.broadcasted_iota(I32, (128, 128), 1))
    upper = jnp.where(upper, 1.0, 0.0).astype(BF16)
    lower = (lax.broadcasted_iota(I32, (nr, nr), 1) < lax.broadcasted_iota(I32, (nr, nr), 0))
    lower = jnp.where(lower, 1.0, 0.0).astype(BF16)
    bidx = lax.broadcasted_iota(I32, (1, nbp), 1).astype(F32)
    pos = jnp.zeros((nr, 128), F32)
    bcls = jnp.zeros((1, nbp), F32)
    nval = jnp.zeros((1, nbp), F32)
    start = jnp.zeros((1, 1), F32)
    for k in range(N_CLASSES):
        m = c == k
        incl = _dot(jnp.where(m, 1.0, 0.0).astype(BF16), upper)
        tot = incl[:, 127:128]
        above = _dot(lower, jnp.broadcast_to(tot, (nr, 128)).astype(BF16))
        cnt = above[nr - 1:nr, 0:1] + tot[nr - 1:nr, 0:1]
        nblk = jnp.floor((cnt + (blk - 1)) * (1.0 / blk))
        pos = jnp.where(m, start * blk + above + incl - 1.0, pos)
        end = start + nblk
        bcls = bcls + jnp.where(bidx >= end, 1.0, 0.0)
        nval = nval + jnp.where((bidx >= start) & (bidx < end), jnp.clip(cnt - (bidx - start) * blk, 0.0, blk), 0.0)
        start = end
    pos_ref[...] = pos.astype(I32)
    bcls_ref[...] = jnp.minimum(bcls, N_CLASSES - 1.0).astype(I32)
    nval_ref[...] = nval.astype(I32)


def _rank_call(cls2d, n_blocks):
    nr = cls2d.shape[0]
    nbp = -(-n_blocks // 128) * 128
    return pl.pallas_call(
        functools.partial(_rank_kernel, blk=MOE_BLK),
        out_shape=(jax.ShapeDtypeStruct((nr, 128), I32), jax.ShapeDtypeStruct((1, nbp), I32),
                   jax.ShapeDtypeStruct((1, nbp), I32)),
        grid=(1,),
        in_specs=[_const_spec((nr, 128))],
        out_specs=(_const_spec((nr, 128)), _const_spec((1, nbp)), _const_spec((1, nbp))),
        compiler_params=_cparams("arbitrary"), name="class_rank",
    )(cls2d)


def _row_copy_wait_all(src_ref, dst_ref, sem, n):
    def wait(r, carry):
        pltpu.make_async_copy(src_ref.at[pl.ds(0, 1)], dst_ref.at[pl.ds(0, 1)], sem).wait()
        return carry
    lax.fori_loop(0, n, wait, 0)


def _scatter_kernel(pos_ref, h_ref, xs_ref, sem):
    n = h_ref.shape[0]

    def issue(r, carry):
        pltpu.make_async_copy(h_ref.at[pl.ds(r, 1)], xs_ref.at[pl.ds(pos_ref[0, 0, r], 1)], sem).start()
        return carry
    lax.fori_loop(0, n, issue, 0)
    _row_copy_wait_all(h_ref, xs_ref, sem, n)


def _scatter_call(h_ext, pos3, n_slots):
    n, w = h_ext.shape
    r = ROW_DMA
    return pl.pallas_call(
        _scatter_kernel,
        out_shape=jax.ShapeDtypeStruct((n_slots, w), h_ext.dtype),
        grid=(n // r,),
        in_specs=[pl.BlockSpec((1, 1, r), lambda i: (i, 0, 0), memory_space=pltpu.SMEM),
                  pl.BlockSpec((r, w), lambda i: (i, 0))],
        out_specs=pl.BlockSpec(memory_space=pl.ANY),
        scratch_shapes=[pltpu.SemaphoreType.DMA(())],
        compiler_params=_cparams("arbitrary"), name="row_scatter",
    )(pos3, h_ext)


def _gather_kernel(pos_ref, ys_ref, o_ref, sem):
    n = o_ref.shape[0]

    def issue(r, carry):
        pltpu.make_async_copy(ys_ref.at[pl.ds(pos_ref[0, 0, r], 1)], o_ref.at[pl.ds(r, 1)], sem).start()
        return carry
    lax.fori_loop(0, n, issue, 0)
    _row_copy_wait_all(ys_ref, o_ref, sem, n)


def _gather_call(ys, pos3):
    n = pos3.shape[0] * pos3.shape[2]
    w = ys.shape[1]
    r = ROW_DMA
    return pl.pallas_call(
        _gather_kernel,
        out_shape=jax.ShapeDtypeStruct((n, w), ys.dtype),
        grid=(n // r,),
        in_specs=[pl.BlockSpec((1, 1, r), lambda i: (i, 0, 0), memory_space=pltpu.SMEM),
                  pl.BlockSpec(memory_space=pl.ANY)],
        out_specs=pl.BlockSpec((r, w), lambda i: (i, 0)),
        scratch_shapes=[pltpu.SemaphoreType.DMA(())],
        compiler_params=_cparams("arbitrary"), name="row_gather",
    )(pos3, ys)


def _moe_kernel(elo_ref, ehi_ref, nval_ref, xs_ref, wg_lo, wu_lo, wd_lo, wg_hi, wu_hi, wd_hi, ys_ref):
    del elo_ref, ehi_ref
    nv = nval_ref[pl.program_id(0)]

    @pl.when(nv > 0)
    def _():
        live = lax.broadcasted_iota(I32, (MOE_BLK, 1), 0) < nv
        x = jnp.where(live, xs_ref[:, 0:D_MODEL], 0.0).astype(BF16)
        acts = []
        for wg, wu, col in ((wg_lo, wu_lo, D_MODEL), (wg_hi, wu_hi, D_MODEL + 1)):
            w = jnp.where(live, xs_ref[:, col:col + 1], 0.0)
            g = _dot(x, wg[0])
            acts.append((g * _sigmoid(g) * _dot(x, wu[0]) * w).astype(BF16))
        ys_ref[...] = _dot(acts[0], wd_lo[0]) + _dot(acts[1], wd_hi[0])

    @pl.when(nv == 0)
    def _():
        ys_ref[...] = jnp.zeros_like(ys_ref)


def _moe_call(xs, e_lo, e_hi, nval, w_gate, w_up, w_down):
    n_blocks = xs.shape[0] // MOE_BLK

    def wspec(shape, which):
        return pl.BlockSpec((1,) + shape, lambda b, elo, ehi, nv: ((elo, ehi)[which][b], 0, 0))

    gu, dn = (D_MODEL, D_EXPERT), (D_EXPERT, D_MODEL)
    grid_spec = pltpu.PrefetchScalarGridSpec(
        num_scalar_prefetch=3, grid=(n_blocks,),
        in_specs=[pl.BlockSpec((MOE_BLK, EXT_W), lambda b, elo, ehi, nv: (b, 0)),
                  wspec(gu, 0), wspec(gu, 0), wspec(dn, 0), wspec(gu, 1), wspec(gu, 1), wspec(dn, 1)],
        out_specs=pl.BlockSpec((MOE_BLK, D_MODEL), lambda b, elo, ehi, nv: (b, 0)),
    )
    return pl.pallas_call(
        _moe_kernel, out_shape=jax.ShapeDtypeStruct((xs.shape[0], D_MODEL), F32), grid_spec=grid_spec,
        compiler_params=_cparams("arbitrary"), name="moe_experts",
    )(e_lo, e_hi, nval, xs, w_gate, w_up, w_down, w_gate, w_up, w_down)


def _final_kernel(x_ref, f_ref, mod_ref, o_ref):
    o_ref[...] = x_ref[...] + mod_ref[0, 5:6, :] * f_ref[...]


def _final_call(x, ffn, mod, *, batch, seq):
    tm = TOKEN_TILE
    n = batch * seq
    tile = pl.BlockSpec((tm, D_MODEL), lambda i: (i, 0))
    return pl.pallas_call(
        _final_kernel, out_shape=jax.ShapeDtypeStruct((n, D_MODEL), F32), grid=(n // tm,),
        in_specs=[tile, tile, _mod_spec(n // tm, seq // tm, batch)], out_specs=tile,
        compiler_params=_cparams("arbitrary"), name="final_residual",
    )(x, ffn, mod)


def _rope_tables(seq, ident_rows):
    pos = jnp.arange(seq, dtype=jnp.int32)
    row = (pos // GRID_W).astype(F32)[:, None]
    col = (pos % GRID_W).astype(F32)[:, None]

    def tab(half):
        inv = ROPE_BASE ** (-jnp.arange(half, dtype=F32) / half)
        ar, ac = row * inv, col * inv
        cos = jnp.concatenate([jnp.cos(ar), jnp.cos(ar), jnp.cos(ac), jnp.cos(ac)], axis=1)
        sin = jnp.concatenate([-jnp.sin(ar), jnp.sin(ar), -jnp.sin(ac), jnp.sin(ac)], axis=1)
        reps = 128 // (4 * half)
        cos = jnp.concatenate([jnp.tile(cos, (1, reps)), jnp.ones((ident_rows, 128), F32)], axis=0)
        sin = jnp.concatenate([jnp.tile(sin, (1, reps)), jnp.zeros((ident_rows, 128), F32)], axis=0)
        return cos, sin

    c64, s64 = tab(HEAD_DIM // 4)
    c32, s32 = tab(DIFF_DIM // 4)
    return c64, s64, c32, s32


def _block_diag_ones(group):
    idx = np.arange(256) // group
    return jnp.asarray((idx[:, None] == idx[None, :]).astype(np.float32), dtype=BF16)


def _qk_gains(g_win, g_na, g_diff, g_gqa):
    s64, s32 = HEAD_DIM ** -0.5, DIFF_DIM ** -0.5
    rows = [jnp.tile(g_win[0] * s64, 2), jnp.tile(g_win[1], 2), jnp.tile(g_na[0] * s64, 2), jnp.tile(g_na[1], 2),
            jnp.tile(g_diff[0] * s32, 4), jnp.tile(g_diff[1], 4), jnp.tile(g_gqa[0] * s64, 2), jnp.tile(g_gqa[1], 2)]
    return jnp.stack(rows).astype(F32)


def _na_bias_tiles(rpb, rows):
    w = GRID_W
    qc = np.arange(w)[:, None]
    kc = np.arange(w)[None, :]
    c_start = np.clip(qc - NA_COLS // 2, 0, w - NA_COLS)
    col_ok = (kc >= c_start) & (kc < c_start + NA_COLS)
    dc = np.clip(kc - qc + NA_COLS - 1, 0, 2 * NA_COLS - 2)
    t = jnp.where(jnp.asarray(col_ok), rpb.astype(F32)[:, :, dc], NEG)
    neg = jnp.full((rpb.shape[0], w, w), NEG, F32)
    cases = []
    for r0, rs in ((0, 0), (NA_QROWS, 0), (rows - NA_QROWS, rows - NA_KROWS)):
        qtiles = []
        for i in range(NA_QROWS):
            qrow = r0 + i
            r_start = min(max(qrow - NA_ROWS // 2, 0), rows - NA_ROWS)
            blks = []
            for j in range(NA_KROWS):
                krow = rs + j
                blks.append(t[:, krow - qrow + NA_ROWS - 1] if r_start <= krow < r_start + NA_ROWS else neg)
            qtiles.append(jnp.concatenate(blks, axis=2))
        cases.append(jnp.concatenate(qtiles, axis=1))
    return jnp.stack(cases)


def _router_weights(router_w):
    hi = router_w.astype(BF16)
    lo = (router_w - hi.astype(F32)).astype(BF16)
    pad = jnp.zeros((router_w.shape[0], 128 - 2 * N_EXPERTS), BF16)
    return jnp.concatenate([hi, lo, pad], axis=1)


def kernel(x, c, ctx, c_ctx, ada_w, ada_b, norm_mix_g, norm_ffn_g, w_in, qk_g_win, qk_g_na, qk_g_diff, qk_g_gqa,
           sink_win, rpb_na, lambda_diff, out_gain, w_out, router_w, router_b, w_gate, w_up, w_down):
    batch, seq, d = x.shape
    ctx_len = ctx.shape[1]
    depth = w_in.shape[0]
    n_lat, n_ctx = batch * seq, batch * ctx_len
    n_all = n_lat + n_ctx
    rows = seq // GRID_W
    assert d == D_MODEL and seq % TOKEN_TILE == 0 and n_ctx % TOKEN_TILE == 0 and ctx_len % 128 == 0
    assert rows % NA_QROWS == 0 and rows >= NA_KROWS + 1 and seq >= ATT_TQ + 2 * WINDOW

    pad_rows = -(-(batch + 1) // 8) * 8
    s_all = jnp.concatenate([c, c_ctx[None, :], jnp.zeros((pad_rows - batch - 1, d), F32)], axis=0)
    mods = _ada_call(s_all, ada_w, ada_b).reshape(depth, pad_rows, ADA_CHUNKS, d)

    xs_all = jnp.concatenate([x.reshape(n_lat, d), ctx.reshape(n_ctx, d)], axis=0)
    tabs = _rope_tables(seq, TOKEN_TILE)
    bd64, bd32 = _block_diag_ones(HEAD_DIM), _block_diag_ones(DIFF_DIM)
    rw = _router_weights(router_w)
    rb = router_b.astype(F32).reshape(N_EXPERTS, 1)
    lo_tab = jnp.asarray([EXPERTS_PER_GROUP * (k // N_PAIRS) + PAIR_LO[k % N_PAIRS] for k in range(N_CLASSES)], I32)
    hi_tab = jnp.asarray([EXPERTS_PER_GROUP * (k // N_PAIRS) + PAIR_HI[k % N_PAIRS] for k in range(N_CLASSES)], I32)

    ffn, mod_prev = None, None
    for layer in range(depth):
        need_ctx = layer < depth - 1
        mod = mods[layer]
        qkg = _qk_gains(qk_g_win[layer], qk_g_na[layer], qk_g_diff[layer], qk_g_gqa[layer])
        xs_all, u = _inproj_call(xs_all, ffn, mod_prev, mod, norm_mix_g[layer].reshape(1, d),
                                 w_in[layer].astype(BF16), qkg, bd64, bd32, tabs, batch=batch, seq=seq)
        y = _attention_calls(u, sink_win[layer].astype(F32), lambda_diff[layer].astype(F32),
                             _na_bias_tiles(rpb_na[layer], rows), layer=layer, batch=batch, seq=seq,
                             ctx_len=ctx_len, need_ctx=need_ctx)
        n_proc = n_all if need_ctx else n_lat
        x_mid, h_ext, cls = _outproj_call(y, xs_all, mod, out_gain[layer].reshape(1, d), w_out[layer].astype(BF16),
                                          norm_ffn_g[layer].reshape(1, d), rw, rb, n_proc=n_proc, batch=batch, seq=seq)
        n_blocks = n_proc // MOE_BLK + N_CLASSES
        pos, bcls, nval = _rank_call(cls.reshape(n_proc // 128, 128), n_blocks)
        pos3 = pos.reshape(n_proc // ROW_DMA, 1, ROW_DMA)
        bcls, nval = bcls[0, :n_blocks], nval[0, :n_blocks]
        sorted_rows = _scatter_call(h_ext, pos3, n_blocks * MOE_BLK)
        ys = _moe_call(sorted_rows, lo_tab[bcls], hi_tab[bcls], nval, w_gate[layer].astype(BF16),
                       w_up[layer].astype(BF16), w_down[layer].astype(BF16))
        ffn = _gather_call(ys, pos3)
        xs_all, mod_prev = x_mid, mod
    out = _final_call(xs_all, ffn, mod_prev, batch=batch, seq=seq)
    return out.reshape(batch, seq, d)
```

```python
import functools
import math

import numpy as np
import jax
import jax.numpy as jnp
from jax import lax
from jax.experimental import pallas as pl
from jax.experimental.pallas import tpu as pltpu

F32 = jnp.float32
BF16 = jnp.bfloat16
I32 = jnp.int32

D_MODEL = 1024
GRID_W = 64
HEAD_DIM = 64
HEADS = 4
DIFF_DIM = 32
WINDOW = 128
NA_ROWS = 8
NA_COLS = 16
ROPE_BASE = 10000.0
N_EXPERTS = 16
EXPERTS_PER_GROUP = 4
D_EXPERT = 512
ADA_CHUNKS = 6
EPS = 1e-6
NEG = -1e30
D_IN = 2560

COL_A_Q, COL_A_KV, COL_B_Q, COL_B_K, COL_B_V, COL_C_Q, COL_C_K, COL_C_V, COL_D_Q, COL_D_KV = range(10)

TOKEN_TILE = 512
ATT_TQ = 256
NA_QROWS = 4
NA_KROWS = 12
KEY_TILE = 256
ATT_RB = 128
ATT_SLOTS = 4
LOG2E = math.log2(math.e)
MOE_BLK = 256
ROW_DMA = 256
N_PAIRS = 6
N_CLASSES = (N_EXPERTS // EXPERTS_PER_GROUP) * N_PAIRS
PAIR_LO = (0, 0, 0, 1, 1, 2)
PAIR_HI = (1, 2, 3, 2, 3, 3)
EXT_W = D_MODEL + 128
VMEM_LIMIT = 56 * 1024 * 1024


def _cparams(*sem):
    return pltpu.CompilerParams(dimension_semantics=sem, vmem_limit_bytes=VMEM_LIMIT)


def _nt_dot(a, b):
    return lax.dot_general(a, b, (((1,), (1,)), ((), ())), preferred_element_type=F32)


def _dot(a, b):
    return jnp.dot(a, b, preferred_element_type=F32)


def _sigmoid(x):
    return 1.0 / (1.0 + jnp.exp(-x))


def _drop_arg(kern, pos):
    def wrapped(*refs):
        return kern(*refs[:pos], *refs[pos + 1:])
    return wrapped


def _ada_kernel(s_ref, w_ref, b_ref, o_ref):
    s = s_ref[...]
    act = s * _sigmoid(s)
    o_ref[0] = jnp.dot(act, w_ref[0], precision=lax.Precision.HIGHEST,
                       preferred_element_type=F32) + b_ref[0]


def _ada_call(s_all, ada_w, ada_b):
    depth, d, n = ada_w.shape
    rows = s_all.shape[0]
    tn = 1536
    return pl.pallas_call(
        _ada_kernel,
        out_shape=jax.ShapeDtypeStruct((depth, rows, n), F32),
        grid=(depth, n // tn),
        in_specs=[
            pl.BlockSpec((rows, d), lambda l, j: (0, 0)),
            pl.BlockSpec((1, d, tn), lambda l, j: (l, 0, j)),
            pl.BlockSpec((1, 1, tn), lambda l, j: (l, 0, j)),
        ],
        out_specs=pl.BlockSpec((1, rows, tn), lambda l, j: (l, 0, j)),
        compiler_params=_cparams("arbitrary", "arbitrary"),
        name="ada_mod",
    )(s_all, ada_w, ada_b.reshape(depth, 1, n))


def _inproj_kernel(*refs, has_ffn, tm, n_lat_tiles, tiles_per_batch, seq):
    if has_ffn:
        (x_ref, f_ref, modp_ref, mod_ref, g_ref, w_ref, qkg_ref, bd64_ref, bd32_ref,
         c64_ref, s64_ref, c32_ref, s32_ref, xo_ref, u_ref) = refs
    else:
        (x_ref, mod_ref, g_ref, w_ref, qkg_ref, bd64_ref, bd32_ref,
         c64_ref, s64_ref, c32_ref, s32_ref, u_ref) = refs
    i = pl.program_id(0)
    x = x_ref[...]
    if has_ffn:
        x = x + modp_ref[0, 5:6, :] * f_ref[...]
        xo_ref[...] = x
    ms = jnp.mean(x * x, axis=-1, keepdims=True)
    hn = x * lax.rsqrt(ms + EPS) * g_ref[...]
    h = hn * (1.0 + mod_ref[0, 1:2, :]) + mod_ref[0, 0:1, :]
    hb = h.astype(BF16)

    p0 = jnp.where(i < n_lat_tiles, (i % tiles_per_batch) * tm, seq)
    p0 = pl.multiple_of(p0, tm)
    lane = lax.broadcasted_iota(I32, (tm, 128), 1)

    def norm_rope(a, gidx, group, bd_ref, rope_tabs):
        w = a.shape[1]
        ssq = _dot((a * a).astype(BF16), bd_ref[0:w, 0:w])
        r = lax.rsqrt(ssq * (1.0 / group) + EPS)
        outs = []
        for s in range(w // 128):
            t = a[:, 128 * s:128 * s + 128] * r[:, 128 * s:128 * s + 128] * qkg_ref[gidx:gidx + 1, :]
            if rope_tabs is not None:
                c_ref, s_ref, half = rope_tabs
                fwd = pltpu.roll(t, 128 - half, axis=1)
                bwd = pltpu.roll(t, half, axis=1)
                sw = jnp.where((lane % (2 * half)) < half, fwd, bwd)
                t = t * c_ref[pl.ds(p0, tm), :] + sw * s_ref[pl.ds(p0, tm), :]
            outs.append(t)
        return outs[0] if len(outs) == 1 else jnp.concatenate(outs, axis=1)

    rope64 = (c64_ref, s64_ref, HEAD_DIM // 4)
    rope32 = (c32_ref, s32_ref, DIFF_DIM // 4)
    for blk in range(D_IN // 256):
        acc = _dot(hb, w_ref[:, 256 * blk:256 * blk + 256])
        if blk == COL_A_Q:
            out = norm_rope(acc, 0, HEAD_DIM, bd64_ref, rope64)
        elif blk == COL_A_KV:
            out = jnp.concatenate([norm_rope(acc[:, :128], 1, HEAD_DIM, bd64_ref, rope64), acc[:, 128:]], axis=1)
        elif blk == COL_B_Q:
            out = norm_rope(acc, 2, HEAD_DIM, bd64_ref, None)
        elif blk == COL_B_K:
            out = norm_rope(acc, 3, HEAD_DIM, bd64_ref, None)
        elif blk == COL_C_Q:
            out = norm_rope(acc, 4, DIFF_DIM, bd32_ref, rope32)
        elif blk == COL_C_K:
            out = norm_rope(acc, 5, DIFF_DIM, bd32_ref, rope32)
        elif blk == COL_D_Q:
            out = norm_rope(acc, 6, HEAD_DIM, bd64_ref, rope64)
        elif blk == COL_D_KV:
            out = jnp.concatenate([norm_rope(acc[:, :128], 7, HEAD_DIM, bd64_ref, rope64), acc[:, 128:]], axis=1)
        else:
            out = acc
        u_ref[:, 256 * blk:256 * blk + 256] = out.astype(BF16)


def _mod_spec(n_lat_tiles, tiles_per_batch, batch):
    return pl.BlockSpec((1, ADA_CHUNKS, D_MODEL),
                        lambda i: (jnp.where(i < n_lat_tiles, i // tiles_per_batch, batch), 0, 0))


def _const_spec(shape):
    return pl.BlockSpec(shape, lambda i: (0,) * len(shape))


def _inproj_call(x, ffn, mod_prev, mod, g, w_in, qkg, bd64, bd32, tabs, *, batch, seq):
    n_all = x.shape[0]
    tm = TOKEN_TILE
    n_lat_tiles = batch * seq // tm
    tpb = seq // tm
    has_ffn = ffn is not None
    tile = pl.BlockSpec((tm, D_MODEL), lambda i: (i, 0))
    modspec = _mod_spec(n_lat_tiles, tpb, batch)
    tab_rows = tabs[0].shape[0]
    in_specs = [tile]
    args = [x]
    if has_ffn:
        in_specs += [tile, modspec]
        args += [ffn, mod_prev]
    in_specs += [modspec, _const_spec((1, D_MODEL)), _const_spec((D_MODEL, D_IN)), _const_spec((8, 128)),
                 _const_spec((256, 256)), _const_spec((256, 256))] + [_const_spec((tab_rows, 128))] * 4
    args += [mod, g, w_in, qkg, bd64, bd32] + list(tabs)
    u_shape = jax.ShapeDtypeStruct((n_all, D_IN), BF16)
    u_spec = pl.BlockSpec((tm, D_IN), lambda i: (i, 0))
    if has_ffn:
        out_shape = (jax.ShapeDtypeStruct((n_all, D_MODEL), F32), u_shape)
        out_specs = (tile, u_spec)
    else:
        out_shape, out_specs = u_shape, u_spec
    kern = functools.partial(_inproj_kernel, has_ffn=has_ffn, tm=tm, n_lat_tiles=n_lat_tiles,
                             tiles_per_batch=tpb, seq=seq)
    res = pl.pallas_call(
        kern, out_shape=out_shape, grid=(n_all // tm,), in_specs=in_specs, out_specs=out_specs,
        compiler_params=_cparams("arbitrary"), name="inproj",
    )(*args)
    return res if has_ffn else (x, res)


def _softmax_parts(scores, extra=None):
    m = scores[0].max(axis=-1, keepdims=True)
    for s in scores[1:]:
        m = jnp.maximum(m, s.max(axis=-1, keepdims=True))
    if extra is not None:
        m = jnp.maximum(m, extra)
    ps = [jnp.exp2(s - m) for s in scores]
    l = ps[0].sum(axis=-1, keepdims=True)
    for p in ps[1:]:
        l = l + p.sum(axis=-1, keepdims=True)
    if extra is not None:
        l = l + jnp.exp2(extra - m)
    return ps, 1.0 / l


class _AttItem:
    def __init__(self, q, kt_view, v_view, ranges, bias=None, extra=None):
        self.q, self.kt_view, self.v_view, self.ranges, self.bias, self.extra = q, kt_view, v_view, ranges, bias, extra
        self.n_keys = sum(n for _, n in ranges)
        self.m = None

    def scores(self, s_scr, rows):
        q_blk = self.q()
        m, off, ti = None, 0, 0
        for start, n in self.ranges:
            for o in range(0, n, KEY_TILE):
                w = min(KEY_TILE, n - o)
                s = _dot(q_blk, self.kt_view[:, pl.ds(start + o, w)])
                b = self.bias(ti) if self.bias is not None else None
                if b is not None:
                    s = s + b
                ti += 1
                s_scr[rows, off + o:off + o + w] = s
                tile_max = s.max(axis=-1, keepdims=True)
                m = tile_max if m is None else jnp.maximum(m, tile_max)
            off += n
        self.m = m if self.extra is None else jnp.maximum(m, self.extra)

    def probs(self, s_scr, p_scr, rows):
        for o in range(0, self.n_keys, KEY_TILE):
            w = min(KEY_TILE, self.n_keys - o)
            p_scr[rows, o:o + w] = jnp.exp2(s_scr[rows, o:o + w] - self.m).astype(BF16)

    def values(self, p_scr, rows):
        pv, off = None, 0
        for start, n in self.ranges:
            c = _dot(p_scr[rows, off:off + n], self.v_view[pl.ds(start, n), :])
            pv = c if pv is None else pv + c
            off += n
        denom = pv[:, 64:65]
        if self.extra is not None:
            denom = denom + jnp.exp2(self.extra - self.m)
        return pv[:, 0:64] / denom


def _run_att_items(items, s_scr, p_scr):
    def rows(t):
        r0 = (t % ATT_SLOTS) * ATT_RB
        return slice(r0, r0 + ATT_RB)

    outs = []
    for t in range(len(items) + 2):
        if t < len(items):
            items[t].scores(s_scr, rows(t))
        if 0 <= t - 1 < len(items):
            items[t - 1].probs(s_scr, p_scr, rows(t - 1))
        if 0 <= t - 2 < len(items):
            outs.append(items[t - 2].values(p_scr, rows(t - 2)))
    return outs


def _stage_heads(k_parts, v_parts, kt_scr, v_scr):
    eye = jnp.where(lax.broadcasted_iota(I32, (HEAD_DIM, HEAD_DIM), 0)
                    == lax.broadcasted_iota(I32, (HEAD_DIM, HEAD_DIM), 1), 1.0, 0.0).astype(BF16)
    for h in range(kt_scr.shape[0]):
        r0 = 0
        for (kr, kc0), (vr, vc0) in zip(k_parts, v_parts):
            n = kr.shape[0]
            kt_scr[h, :, r0:r0 + n] = _nt_dot(eye, kr[:, kc0 + 64 * h:kc0 + 64 * h + 64]).astype(BF16)
            v_scr[h, r0:r0 + n, 0:64] = vr[:, vc0 + 64 * h:vc0 + 64 * h + 64]
            r0 += n
        v_scr[h, :, 64:128] = jnp.ones((v_scr.shape[1], 64), BF16)


def _group_rms(y):
    return y * lax.rsqrt(jnp.mean(y * y, axis=-1, keepdims=True) + EPS)


def _gqa_pair(q_ref, j):
    return jnp.concatenate([q_ref[:, 128 * j:128 * j + 64], q_ref[:, 128 * j + 64:128 * j + 128]], axis=0)


def _stage_once_per_batch_row(k_parts, v_parts, k_scr, v_scr):
    @pl.when(pl.program_id(1) == 0)
    def _():
        _stage_heads(k_parts, v_parts, k_scr, v_scr)


def _row_blocks(n):
    return range(0, n, ATT_RB)


def _q_thunk(q_ref, r0, head):
    return lambda: q_ref[r0:r0 + ATT_RB, 64 * head:64 * head + 64]


def _heads_to_rows(outs, n_heads):
    per = len(outs) // n_heads
    return jnp.concatenate([jnp.concatenate(outs[h * per:(h + 1) * per], axis=0) for h in range(n_heads)], axis=1)


def _dense_kernel(q_ref, kv_ref, kvc_ref, o_ref, kt_scr, v_scr, s_scr, p_scr, *, tq):
    _stage_once_per_batch_row([(kv_ref, 0), (kvc_ref, 0)], [(kv_ref, 128), (kvc_ref, 128)], kt_scr, v_scr)
    nk = v_scr.shape[1]
    items = [_AttItem(_q_thunk(q_ref, r0, hq), kt_scr.at[hq // 2], v_scr.at[hq // 2], [(0, nk)])
             for hq in range(HEADS) for r0 in _row_blocks(tq)]
    y = _heads_to_rows(_run_att_items(items, s_scr, p_scr), HEADS)
    o_ref[...] = _group_rms(y).astype(BF16)


def _window_kernel(sink_ref, q_ref, kv_ref, kvc_ref, mask_ref, o_ref, kt_scr, v_scr, s_scr, p_scr, *, tq, seq):
    _stage_once_per_batch_row([(kv_ref, 0), (kvc_ref, 0)], [(kv_ref, 128), (kvc_ref, 128)], kt_scr, v_scr)
    kw = tq + 2 * WINDOW
    nc = v_scr.shape[1] - seq
    ks = pl.multiple_of(jnp.clip(pl.program_id(1) * tq - WINDOW, 0, seq - kw), 128)

    def mask_thunk(r0):
        return lambda ti: (mask_ref[0, r0:r0 + ATT_RB, KEY_TILE * ti:KEY_TILE * ti + KEY_TILE]
                           if KEY_TILE * ti < kw else None)

    items = [_AttItem(_q_thunk(q_ref, r0, hq), kt_scr.at[hq // 2], v_scr.at[hq // 2], [(ks, kw), (seq, nc)],
                      bias=mask_thunk(r0), extra=sink_ref[hq] * LOG2E)
             for hq in range(HEADS) for r0 in _row_blocks(tq)]
    y = _heads_to_rows(_run_att_items(items, s_scr, p_scr), HEADS)
    o_ref[...] = _group_rms(y).astype(BF16)


def _neigh_kernel(q_ref, k_ref, v_ref, kc_ref, vc_ref, bias_ref, o_ref, kt_scr, v_scr, s_scr, p_scr, *, rows, seq):
    _stage_once_per_batch_row([(k_ref, 0), (kc_ref, 0)], [(v_ref, 0), (vc_ref, 0)], kt_scr, v_scr)
    rs = jnp.clip(NA_QROWS * pl.program_id(1) - NA_ROWS // 2, 0, rows - NA_KROWS)
    k0 = pl.multiple_of(rs * GRID_W, KEY_TILE)
    nk = NA_KROWS * GRID_W
    nc = v_scr.shape[1] - seq

    def bias_thunk(h, r0):
        return lambda ti: (bias_ref[0, h, r0:r0 + ATT_RB, KEY_TILE * ti:KEY_TILE * ti + KEY_TILE]
                           if KEY_TILE * ti < nk else None)

    items = [_AttItem(_q_thunk(q_ref, r0, h), kt_scr.at[h], v_scr.at[h], [(k0, nk), (seq, nc)], bias=bias_thunk(h, r0))
             for h in range(HEADS) for r0 in _row_blocks(q_ref.shape[0])]
    y = _heads_to_rows(_run_att_items(items, s_scr, p_scr), HEADS)
    o_ref[...] = _group_rms(y).astype(BF16)


def _diff_lambda(lam_ref, lam_init):
    lp = lam_ref[...]
    a = jnp.sum(lp[0:1] * lp[1:2], axis=-1, keepdims=True)
    b = jnp.sum(lp[2:3] * lp[3:4], axis=-1, keepdims=True)
    return jnp.exp(a) - jnp.exp(b) + lam_init


def _diff_heads(q, k_parts, v_parts, lam, lam_init):
    t = q.shape[0]
    first = lax.broadcasted_iota(I32, (t, HEAD_DIM), 1) < DIFF_DIM
    zero = jnp.zeros((t, HEAD_DIM), BF16)
    outs = []
    for h in range(HEADS):
        sl = slice(64 * h, 64 * h + 64)
        qh = q[:, sl]
        ks = [kp[:, sl] for kp in k_parts]
        maps = []
        for half in range(2):
            qm = jnp.where(first, qh, zero) if half == 0 else jnp.where(first, zero, qh)
            ps, inv = _softmax_parts([_nt_dot(qm, kk) for kk in ks])
            maps.append([p * inv for p in ps])
        o = None
        for a0, a1, vp in zip(maps[0], maps[1], v_parts):
            c = _dot((a0 - lam * a1).astype(BF16), vp[:, sl])
            o = c if o is None else o + c
        outs.append(_group_rms(o) * (1.0 - lam_init))
    return jnp.concatenate(outs, axis=1)


def _diff_kernel(lam_ref, q_ref, k_ref, v_ref, kc_ref, vc_ref, o_ref, kt_scr, v_scr, s_scr, p_scr, *, lam_init, tq):
    _stage_once_per_batch_row([(k_ref, 0), (kc_ref, 0)], [(v_ref, 0), (vc_ref, 0)], kt_scr, v_scr)
    lam = _diff_lambda(lam_ref, lam_init)
    nk = v_scr.shape[1]
    first = lax.broadcasted_iota(I32, (ATT_RB, HEAD_DIM), 1) < DIFF_DIM
    zero = jnp.zeros((ATT_RB, HEAD_DIM), BF16)

    def q_map(r0, h, half):
        def thunk():
            qh = q_ref[r0:r0 + ATT_RB, 64 * h:64 * h + 64]
            return jnp.where(first, qh, zero) if half == 0 else jnp.where(first, zero, qh)
        return thunk

    items = [_AttItem(q_map(r0, h, half), kt_scr.at[h], v_scr.at[h], [(0, nk)])
             for h in range(HEADS) for r0 in _row_blocks(tq) for half in range(2)]
    outs = _run_att_items(items, s_scr, p_scr)
    diffs = [outs[i] - lam * outs[i + 1] for i in range(0, len(outs), 2)]
    per = len(diffs) // HEADS
    heads = [_group_rms(jnp.concatenate(diffs[h * per:(h + 1) * per], axis=0)) * (1.0 - lam_init)
             for h in range(HEADS)]
    o_ref[...] = jnp.concatenate(heads, axis=1).astype(BF16)


class _Cols:
    def __init__(self, ref, blk):
        self.ref, self.base = ref, 256 * blk

    def __getitem__(self, idx):
        rows, sl = idx
        return self.ref[rows, self.base + sl.start:self.base + sl.stop]


def _ctx_kernel(sink_ref, lam_ref, u_ref, o_ref, *, lam_init):
    L = u_ref.shape[0]
    top = lax.broadcasted_iota(I32, (2 * L, 1), 0) < L
    for q_blk, kv_blk, out_col, use_sink in ((COL_A_Q, COL_A_KV, 0, True), (COL_D_Q, COL_D_KV, 768, False)):
        qv, kv = _Cols(u_ref, q_blk), _Cols(u_ref, kv_blk)
        outs = [None] * HEADS
        for j in range(2):
            kc = kv[:, slice(64 * j, 64 * j + 64)]
            vc = kv[:, slice(128 + 64 * j, 128 + 64 * j + 64)]
            q2 = jnp.concatenate([qv[:, slice(128 * j, 128 * j + 64)], qv[:, slice(128 * j + 64, 128 * j + 128)]], axis=0)
            snk = jnp.where(top, sink_ref[2 * j], sink_ref[2 * j + 1]) * LOG2E if use_sink else None
            (p,), inv = _softmax_parts([_nt_dot(q2, kc)], extra=snk)
            o = _dot(p.astype(BF16), vc) * inv
            outs[2 * j], outs[2 * j + 1] = o[:L], o[L:]
        o_ref[:, out_col:out_col + 256] = _group_rms(jnp.concatenate(outs, axis=1)).astype(BF16)
    qv, kv, vv = _Cols(u_ref, COL_B_Q), _Cols(u_ref, COL_B_K), _Cols(u_ref, COL_B_V)
    outs = []
    for h in range(HEADS):
        sl = slice(64 * h, 64 * h + 64)
        (p,), inv = _softmax_parts([_nt_dot(qv[:, sl], kv[:, sl])])
        outs.append(_dot(p.astype(BF16), vv[:, sl]) * inv)
    o_ref[:, 256:512] = _group_rms(jnp.concatenate(outs, axis=1)).astype(BF16)
    lam = _diff_lambda(lam_ref, lam_init)
    qc = u_ref[:, 256 * COL_C_Q:256 * COL_C_Q + 256]
    o_ref[:, 512:768] = _diff_heads(qc, [_Cols(u_ref, COL_C_K)], [_Cols(u_ref, COL_C_V)], lam, lam_init).astype(BF16)


def _attention_calls(u, sink, lam_p, bias, wmask, *, layer, batch, seq, ctx_len, need_ctx):
    lam_init = 0.8 - 0.6 * math.exp(-0.3 * layer)
    n_all = u.shape[0]
    tq = ATT_TQ
    nq = seq // tq
    nk_all = seq + ctx_len
    cb0 = batch * seq // ctx_len
    y_shape = jax.ShapeDtypeStruct((n_all, D_MODEL), BF16)
    smem = pl.BlockSpec(memory_space=pltpu.SMEM)
    any_spec = pl.BlockSpec(memory_space=pl.ANY)
    lam_spec2 = pl.BlockSpec((4, DIFF_DIM), lambda b, i: (0, 0))
    cp = _cparams("arbitrary", "arbitrary")

    def scratch(n_heads, n_keys):
        return [pltpu.VMEM((n_heads, HEAD_DIM, nk_all), BF16), pltpu.VMEM((n_heads, nk_all, 128), BF16),
                pltpu.VMEM((ATT_SLOTS * ATT_RB, n_keys), F32), pltpu.VMEM((ATT_SLOTS * ATT_RB, n_keys), BF16)]

    def qspec(col):
        return pl.BlockSpec((tq, 256), lambda b, i: (b * nq + i, col))

    def latspec(col):
        return pl.BlockSpec((seq, 256), lambda b, i: (b, col))

    def ctxspec(col):
        return pl.BlockSpec((ctx_len, 256), lambda b, i: (cb0 + b, col))

    kw = tq + 2 * WINDOW

    def first_mid_last(n):
        return lambda b, i: (jnp.where(i == 0, 0, jnp.where(i == n - 1, 2, 1)),)

    wcase = first_mid_last(nq)
    y = pl.pallas_call(
        functools.partial(_window_kernel, tq=tq, seq=seq),
        out_shape=y_shape, grid=(batch, nq),
        in_specs=[smem, qspec(COL_A_Q), latspec(COL_A_KV), ctxspec(COL_A_KV),
                  pl.BlockSpec((1, tq, kw), lambda b, i: wcase(b, i) + (0, 0))],
        out_specs=qspec(0), scratch_shapes=scratch(2, kw + ctx_len),
        compiler_params=cp, name="mix_window",
    )(sink, u, u, u, wmask)

    rows = seq // GRID_W
    nrb = rows // NA_QROWS
    qrows = NA_QROWS * GRID_W
    nkb = NA_KROWS * GRID_W
    bcase = first_mid_last(nrb)
    y = pl.pallas_call(
        _drop_arg(functools.partial(_neigh_kernel, rows=rows, seq=seq), 6),
        out_shape=y_shape, grid=(batch, nrb),
        in_specs=[pl.BlockSpec((qrows, 256), lambda b, i: (b * nrb + i, COL_B_Q)),
                  latspec(COL_B_K), latspec(COL_B_V), ctxspec(COL_B_K), ctxspec(COL_B_V),
                  pl.BlockSpec((1, HEADS, qrows, nkb), lambda b, i: bcase(b, i) + (0, 0, 0)), any_spec],
        out_specs=pl.BlockSpec((qrows, 256), lambda b, i: (b * nrb + i, 1)),
        scratch_shapes=scratch(HEADS, nkb + ctx_len),
        input_output_aliases={6: 0}, compiler_params=cp, name="mix_neigh",
    )(u, u, u, u, u, bias, y)

    y = pl.pallas_call(
        _drop_arg(functools.partial(_diff_kernel, lam_init=lam_init, tq=tq), 6),
        out_shape=y_shape, grid=(batch, nq),
        in_specs=[lam_spec2, qspec(COL_C_Q), latspec(COL_C_K), latspec(COL_C_V),
                  ctxspec(COL_C_K), ctxspec(COL_C_V), any_spec],
        out_specs=qspec(2), scratch_shapes=scratch(HEADS, nk_all),
        input_output_aliases={6: 0}, compiler_params=cp, name="mix_diff",
    )(lam_p, u, u, u, u, u, y)

    y = pl.pallas_call(
        _drop_arg(functools.partial(_dense_kernel, tq=tq), 3),
        out_shape=y_shape, grid=(batch, nq),
        in_specs=[qspec(COL_D_Q), latspec(COL_D_KV), ctxspec(COL_D_KV), any_spec],
        out_specs=qspec(3), scratch_shapes=scratch(2, nk_all),
        input_output_aliases={3: 0}, compiler_params=cp, name="mix_dense",
    )(u, u, u, y)

    if need_ctx:
        y = pl.pallas_call(
            _drop_arg(functools.partial(_ctx_kernel, lam_init=lam_init), 3),
            out_shape=y_shape, grid=(batch,),
            in_specs=[smem, pl.BlockSpec((4, DIFF_DIM), lambda b: (0, 0)),
                      pl.BlockSpec((ctx_len, D_IN), lambda b: (cb0 + b, 0)), any_spec],
            out_specs=pl.BlockSpec((ctx_len, D_MODEL), lambda b: (cb0 + b, 0)),
            input_output_aliases={3: 0}, compiler_params=_cparams("arbitrary"), name="mix_ctx",
        )(sink, lam_p, u, y)
    return y


def _select4(idx, vals):
    return jnp.where(idx == 0, vals[0], jnp.where(idx == 1, vals[1], jnp.where(idx == 2, vals[2], vals[3])))


def _outproj_kernel(y_ref, x_ref, mod_ref, gain_ref, w_ref, g_ref, rw_ref, rb_ref, xo_ref, h_ref, cls_ref, *, tm):
    yg = (y_ref[...].astype(F32) * gain_ref[...]).astype(BF16)
    xm = x_ref[...] + mod_ref[0, 2:3, :] * _dot(yg, w_ref[...])
    xo_ref[...] = xm
    hn = xm * lax.rsqrt(jnp.mean(xm * xm, axis=-1, keepdims=True) + EPS) * g_ref[...]
    h2 = hn * (1.0 + mod_ref[0, 4:5, :]) + mod_ref[0, 3:4, :]
    h_ref[:, 0:D_MODEL] = h2

    hi = h2.astype(BF16)
    lo = (h2 - hi.astype(F32)).astype(BF16)
    lt = (_dot(hi, rw_ref[...]) + _dot(lo, rw_ref[...])).T
    score = _sigmoid(lt[0:N_EXPERTS] + lt[N_EXPERTS:2 * N_EXPERTS])
    sel = score + rb_ref[...]
    srow = [sel[e:e + 1] for e in range(N_EXPERTS)]
    crow = [score[e:e + 1] for e in range(N_EXPERTS)]

    gidx = jnp.zeros((1, tm), I32)
    gbest = None
    for g in range(N_EXPERTS // EXPERTS_PER_GROUP):
        v = srow[4 * g:4 * g + 4]
        best = None
        for a in range(4):
            for b in range(a + 1, 4):
                t = v[a] + v[b]
                best = t if best is None else jnp.maximum(best, t)
        if g == 0:
            gbest = best
        else:
            better = best > gbest
            gidx = jnp.where(better, g, gidx)
            gbest = jnp.where(better, best, gbest)

    iv = [_select4(gidx, [srow[4 * g + i] for g in range(4)]) for i in range(4)]
    sv = [_select4(gidx, [crow[4 * g + i] for g in range(4)]) for i in range(4)]
    chosen = []
    for i in range(4):
        rank = jnp.zeros((1, tm), I32)
        for j in range(4):
            if j != i:
                beats = (iv[j] >= iv[i]) if j < i else (iv[j] > iv[i])
                rank = rank + jnp.where(beats, 1, 0)
        chosen.append(rank < 2)
    lo_i = jnp.where(chosen[0], 0, jnp.where(chosen[1], 1, 2))
    hi_i = jnp.where(chosen[3], 3, jnp.where(chosen[2], 2, 1))
    pair = jnp.where(lo_i == 0, hi_i - 1, jnp.where(lo_i == 1, hi_i + 1, 5))
    cls_ref[0] = gidx * N_PAIRS + pair
    s_lo = jnp.where(lo_i == 0, sv[0], jnp.where(lo_i == 1, sv[1], sv[2]))
    s_hi = jnp.where(hi_i == 3, sv[3], jnp.where(hi_i == 2, sv[2], sv[1]))
    inv = 1.0 / (s_lo + s_hi)
    rowi = lax.broadcasted_iota(I32, (128, tm), 0)
    wts = jnp.where(rowi == 0, s_lo * inv, jnp.where(rowi == 1, s_hi * inv, 0.0))
    h_ref[:, D_MODEL:EXT_W] = wts.T


def _outproj_call(y, x, mod, gain, w_out, g, rw, rb, *, n_proc, batch, seq):
    tm = TOKEN_TILE
    n_tiles = n_proc // tm
    tile = pl.BlockSpec((tm, D_MODEL), lambda i: (i, 0))
    return pl.pallas_call(
        functools.partial(_outproj_kernel, tm=tm),
        out_shape=(jax.ShapeDtypeStruct((n_proc, D_MODEL), F32),
                   jax.ShapeDtypeStruct((n_proc, EXT_W), F32),
                   jax.ShapeDtypeStruct((n_tiles, 1, tm), I32)),
        grid=(n_tiles,),
        in_specs=[tile, tile, _mod_spec(batch * seq // tm, seq // tm, batch), _const_spec((1, D_MODEL)),
                  _const_spec((D_MODEL, D_MODEL)), _const_spec((1, D_MODEL)), _const_spec((D_MODEL, 128)),
                  _const_spec((N_EXPERTS, 1))],
        out_specs=(tile, pl.BlockSpec((tm, EXT_W), lambda i: (i, 0)), pl.BlockSpec((1, 1, tm), lambda i: (i, 0, 0))),
        compiler_params=_cparams("arbitrary"), name="outproj_router",
    )(y, x, mod, gain, w_out, g, rw, rb)


def _rank_kernel(cls_ref, pos_ref, bcls_ref, nval_ref, *, blk):
    c = cls_ref[...]
    nr = c.shape[0]
    nbp = bcls_ref.shape[1]
    upper = (lax.broadcasted_iota(I32, (128, 128), 0) <= lax.broadcasted_iota(I32, (128, 128), 1))
    upper = jnp.where(upper, 1.0, 0.0).astype(BF16)
    lower = (lax.broadcasted_iota(I32, (nr, nr), 1) < lax.broadcasted_iota(I32, (nr, nr), 0))
    lower = jnp.where(lower, 1.0, 0.0).astype(BF16)
    bidx = lax.broadcasted_iota(I32, (1, nbp), 1).astype(F32)
    pos = jnp.zeros((nr, 128), F32)
    bcls = jnp.zeros((1, nbp), F32)
    nval = jnp.zeros((1, nbp), F32)
    start = jnp.zeros((1, 1), F32)
    for k in range(N_CLASSES):
        m = c == k
        incl = _dot(jnp.where(m, 1.0, 0.0).astype(BF16), upper)
        tot = incl[:, 127:128]
        above = _dot(lower, jnp.broadcast_to(tot, (nr, 128)).astype(BF16))
        cnt = above[nr - 1:nr, 0:1] + tot[nr - 1:nr, 0:1]
        nblk = jnp.floor((cnt + (blk - 1)) * (1.0 / blk))
        pos = jnp.where(m, start * blk + above + incl - 1.0, pos)
        end = start + nblk
        bcls = bcls + jnp.where(bidx >= end, 1.0, 0.0)
        nval = nval + jnp.where((bidx >= start) & (bidx < end), jnp.clip(cnt - (bidx - start) * blk, 0.0, blk), 0.0)
        start = end
    pos_ref[...] = pos.astype(I32)
    bcls_ref[...] = jnp.minimum(bcls, N_CLASSES - 1.0).astype(I32)
    nval_ref[...] = nval.astype(I32)


def _rank_call(cls2d, n_blocks):
    nr = cls2d.shape[0]
    nbp = -(-n_blocks // 128) * 128
    return pl.pallas_call(
        functools.partial(_rank_kernel, blk=MOE_BLK),
        out_shape=(jax.ShapeDtypeStruct((nr, 128), I32), jax.ShapeDtypeStruct((1, nbp), I32),
                   jax.ShapeDtypeStruct((1, nbp), I32)),
        grid=(1,),
        in_specs=[_const_spec((nr, 128))],
        out_specs=(_const_spec((nr, 128)), _const_spec((1, nbp)), _const_spec((1, nbp))),
        compiler_params=_cparams("arbitrary"), name="class_rank",
    )(cls2d)


ROW_DMA_UNROLL = 8


def _scatter_kernel(pos_ref, h_ref, xs_ref, sem):
    n = h_ref.shape[0]

    def issue(r, carry):
        pltpu.make_async_copy(h_ref.at[pl.ds(r, 1)], xs_ref.at[pl.ds(pos_ref[0, 0, r], 1)], sem).start()
        return carry
    lax.fori_loop(0, n, issue, 0, unroll=ROW_DMA_UNROLL)
    pltpu.make_async_copy(h_ref, xs_ref.at[pl.ds(0, n)], sem).wait()


def _scatter_call(h_ext, pos3, n_slots):
    n, w = h_ext.shape
    r = ROW_DMA
    return pl.pallas_call(
        _scatter_kernel,
        out_shape=jax.ShapeDtypeStruct((n_slots, w), h_ext.dtype),
        grid=(n // r,),
        in_specs=[pl.BlockSpec((1, 1, r), lambda i: (i, 0, 0), memory_space=pltpu.SMEM),
                  pl.BlockSpec((r, w), lambda i: (i, 0))],
        out_specs=pl.BlockSpec(memory_space=pl.ANY),
        scratch_shapes=[pltpu.SemaphoreType.DMA(())],
        compiler_params=_cparams("arbitrary"), name="row_scatter",
    )(pos3, h_ext)


def _gather_kernel(pos_ref, ys_ref, o_ref, sem):
    n = o_ref.shape[0]

    def issue(r, carry):
        pltpu.make_async_copy(ys_ref.at[pl.ds(pos_ref[0, 0, r], 1)], o_ref.at[pl.ds(r, 1)], sem).start()
        return carry
    lax.fori_loop(0, n, issue, 0, unroll=ROW_DMA_UNROLL)
    pltpu.make_async_copy(ys_ref.at[pl.ds(0, n)], o_ref, sem).wait()


def _gather_call(ys, pos3):
    n = pos3.shape[0] * pos3.shape[2]
    w = ys.shape[1]
    r = ROW_DMA
    return pl.pallas_call(
        _gather_kernel,
        out_shape=jax.ShapeDtypeStruct((n, w), ys.dtype),
        grid=(n // r,),
        in_specs=[pl.BlockSpec((1, 1, r), lambda i: (i, 0, 0), memory_space=pltpu.SMEM),
                  pl.BlockSpec(memory_space=pl.ANY)],
        out_specs=pl.BlockSpec((r, w), lambda i: (i, 0)),
        scratch_shapes=[pltpu.SemaphoreType.DMA(())],
        compiler_params=_cparams("arbitrary"), name="row_gather",
    )(pos3, ys)


def _moe_kernel(elo_ref, ehi_ref, nval_ref, xs_ref, wg_lo, wu_lo, wd_lo, wg_hi, wu_hi, wd_hi, ys_ref):
    del elo_ref, ehi_ref
    nv = nval_ref[pl.program_id(0)]

    @pl.when(nv > 0)
    def _():
        live = lax.broadcasted_iota(I32, (MOE_BLK, 1), 0) < nv
        x = jnp.where(live, xs_ref[:, 0:D_MODEL], 0.0).astype(BF16)
        acts = []
        for wg, wu, col in ((wg_lo, wu_lo, D_MODEL), (wg_hi, wu_hi, D_MODEL + 1)):
            w = jnp.where(live, xs_ref[:, col:col + 1], 0.0)
            g = _dot(x, wg[0])
            acts.append((g * _sigmoid(g) * _dot(x, wu[0]) * w).astype(BF16))
        ys_ref[...] = _dot(acts[0], wd_lo[0]) + _dot(acts[1], wd_hi[0])

    @pl.when(nv == 0)
    def _():
        ys_ref[...] = jnp.zeros_like(ys_ref)


def _moe_call(xs, e_lo, e_hi, nval, w_gate, w_up, w_down):
    n_blocks = xs.shape[0] // MOE_BLK

    def wspec(shape, which):
        return pl.BlockSpec((1,) + shape, lambda b, elo, ehi, nv: ((elo, ehi)[which][b], 0, 0))

    gu, dn = (D_MODEL, D_EXPERT), (D_EXPERT, D_MODEL)
    grid_spec = pltpu.PrefetchScalarGridSpec(
        num_scalar_prefetch=3, grid=(n_blocks,),
        in_specs=[pl.BlockSpec((MOE_BLK, EXT_W), lambda b, elo, ehi, nv: (b, 0)),
                  wspec(gu, 0), wspec(gu, 0), wspec(dn, 0), wspec(gu, 1), wspec(gu, 1), wspec(dn, 1)],
        out_specs=pl.BlockSpec((MOE_BLK, D_MODEL), lambda b, elo, ehi, nv: (b, 0)),
    )
    return pl.pallas_call(
        _moe_kernel, out_shape=jax.ShapeDtypeStruct((xs.shape[0], D_MODEL), F32), grid_spec=grid_spec,
        compiler_params=_cparams("arbitrary"), name="moe_experts",
    )(e_lo, e_hi, nval, xs, w_gate, w_up, w_down, w_gate, w_up, w_down)


def _final_kernel(x_ref, f_ref, mod_ref, o_ref):
    o_ref[...] = x_ref[...] + mod_ref[0, 5:6, :] * f_ref[...]


def _final_call(x, ffn, mod, *, batch, seq):
    tm = TOKEN_TILE
    n = batch * seq
    tile = pl.BlockSpec((tm, D_MODEL), lambda i: (i, 0))
    return pl.pallas_call(
        _final_kernel, out_shape=jax.ShapeDtypeStruct((n, D_MODEL), F32), grid=(n // tm,),
        in_specs=[tile, tile, _mod_spec(n // tm, seq // tm, batch)], out_specs=tile,
        compiler_params=_cparams("arbitrary"), name="final_residual",
    )(x, ffn, mod)


def _rope_tables(seq, ident_rows):
    pos = jnp.arange(seq, dtype=jnp.int32)
    row = (pos // GRID_W).astype(F32)[:, None]
    col = (pos % GRID_W).astype(F32)[:, None]

    def tab(half):
        inv = ROPE_BASE ** (-jnp.arange(half, dtype=F32) / half)
        ar, ac = row * inv, col * inv
        cos = jnp.concatenate([jnp.cos(ar), jnp.cos(ar), jnp.cos(ac), jnp.cos(ac)], axis=1)
        sin = jnp.concatenate([-jnp.sin(ar), jnp.sin(ar), -jnp.sin(ac), jnp.sin(ac)], axis=1)
        reps = 128 // (4 * half)
        cos = jnp.concatenate([jnp.tile(cos, (1, reps)), jnp.ones((ident_rows, 128), F32)], axis=0)
        sin = jnp.concatenate([jnp.tile(sin, (1, reps)), jnp.zeros((ident_rows, 128), F32)], axis=0)
        return cos, sin

    c64, s64 = tab(HEAD_DIM // 4)
    c32, s32 = tab(DIFF_DIM // 4)
    return c64, s64, c32, s32


def _block_diag_ones(group):
    idx = np.arange(256) // group
    return jnp.asarray((idx[:, None] == idx[None, :]).astype(np.float32), dtype=BF16)


def _qk_gains(g_win, g_na, g_diff, g_gqa):
    s64, s32 = HEAD_DIM ** -0.5 * LOG2E, DIFF_DIM ** -0.5 * LOG2E
    rows = [jnp.tile(g_win[0] * s64, 2), jnp.tile(g_win[1], 2), jnp.tile(g_na[0] * s64, 2), jnp.tile(g_na[1], 2),
            jnp.tile(g_diff[0] * s32, 4), jnp.tile(g_diff[1], 4), jnp.tile(g_gqa[0] * s64, 2), jnp.tile(g_gqa[1], 2)]
    return jnp.stack(rows).astype(F32)


def _na_bias_tiles(rpb, rows):
    w = GRID_W
    qc = np.arange(w)[:, None]
    kc = np.arange(w)[None, :]
    c_start = np.clip(qc - NA_COLS // 2, 0, w - NA_COLS)
    col_ok = (kc >= c_start) & (kc < c_start + NA_COLS)
    dc = np.clip(kc - qc + NA_COLS - 1, 0, 2 * NA_COLS - 2)
    t = jnp.where(jnp.asarray(col_ok), rpb.astype(F32)[:, :, dc] * LOG2E, NEG)
    neg = jnp.full((rpb.shape[0], w, w), NEG, F32)
    cases = []
    for r0, rs in ((0, 0), (NA_QROWS, 0), (rows - NA_QROWS, rows - NA_KROWS)):
        qtiles = []
        for i in range(NA_QROWS):
            qrow = r0 + i
            r_start = min(max(qrow - NA_ROWS // 2, 0), rows - NA_ROWS)
            blks = []
            for j in range(NA_KROWS):
                krow = rs + j
                blks.append(t[:, krow - qrow + NA_ROWS - 1] if r_start <= krow < r_start + NA_ROWS else neg)
            qtiles.append(jnp.concatenate(blks, axis=2))
        cases.append(jnp.concatenate(qtiles, axis=1))
    return jnp.stack(cases)


def _window_mask_tiles(tq):
    kw = tq + 2 * WINDOW
    r = lax.broadcasted_iota(I32, (3, tq, kw), 1)
    c = lax.broadcasted_iota(I32, (3, tq, kw), 2)
    shift = lax.broadcasted_iota(I32, (3, tq, kw), 0) * WINDOW
    return jnp.where(jnp.abs(c - r - shift) <= WINDOW, 0.0, NEG).astype(F32)


def _router_weights(router_w):
    hi = router_w.astype(BF16)
    lo = (router_w - hi.astype(F32)).astype(BF16)
    pad = jnp.zeros((router_w.shape[0], 128 - 2 * N_EXPERTS), BF16)
    return jnp.concatenate([hi, lo, pad], axis=1)


def kernel(x, c, ctx, c_ctx, ada_w, ada_b, norm_mix_g, norm_ffn_g, w_in, qk_g_win, qk_g_na, qk_g_diff, qk_g_gqa,
           sink_win, rpb_na, lambda_diff, out_gain, w_out, router_w, router_b, w_gate, w_up, w_down):
    batch, seq, d = x.shape
    ctx_len = ctx.shape[1]
    depth = w_in.shape[0]
    n_lat, n_ctx = batch * seq, batch * ctx_len
    n_all = n_lat + n_ctx
    rows = seq // GRID_W
    assert d == D_MODEL and seq % TOKEN_TILE == 0 and n_ctx % TOKEN_TILE == 0 and ctx_len % 128 == 0
    assert rows % NA_QROWS == 0 and rows >= NA_KROWS + NA_QROWS and seq >= ATT_TQ + 2 * WINDOW
    assert ctx_len % 128 == 0 and (NA_KROWS * GRID_W) % KEY_TILE == 0

    pad_rows = -(-(batch + 1) // 8) * 8
    s_all = jnp.concatenate([c, c_ctx[None, :], jnp.zeros((pad_rows - batch - 1, d), F32)], axis=0)
    mods = _ada_call(s_all, ada_w, ada_b).reshape(depth, pad_rows, ADA_CHUNKS, d)

    xs_all = jnp.concatenate([x.reshape(n_lat, d), ctx.reshape(n_ctx, d)], axis=0)
    tabs = _rope_tables(seq, TOKEN_TILE)
    wmask = _window_mask_tiles(ATT_TQ)
    bd64, bd32 = _block_diag_ones(HEAD_DIM), _block_diag_ones(DIFF_DIM)
    rw = _router_weights(router_w)
    rb = router_b.astype(F32).reshape(N_EXPERTS, 1)
    lo_tab = jnp.asarray([EXPERTS_PER_GROUP * (k // N_PAIRS) + PAIR_LO[k % N_PAIRS] for k in range(N_CLASSES)], I32)
    hi_tab = jnp.asarray([EXPERTS_PER_GROUP * (k // N_PAIRS) + PAIR_HI[k % N_PAIRS] for k in range(N_CLASSES)], I32)

    ffn, mod_prev = None, None
    for layer in range(depth):
        need_ctx = layer < depth - 1
        mod = mods[layer]
        qkg = _qk_gains(qk_g_win[layer], qk_g_na[layer], qk_g_diff[layer], qk_g_gqa[layer])
        xs_all, u = _inproj_call(xs_all, ffn, mod_prev, mod, norm_mix_g[layer].reshape(1, d),
                                 w_in[layer].astype(BF16), qkg, bd64, bd32, tabs, batch=batch, seq=seq)
        y = _attention_calls(u, sink_win[layer].astype(F32), lambda_diff[layer].astype(F32),
                             _na_bias_tiles(rpb_na[layer], rows), wmask, layer=layer, batch=batch, seq=seq,
                             ctx_len=ctx_len, need_ctx=need_ctx)
        n_proc = n_all if need_ctx else n_lat
        x_mid, h_ext, cls = _outproj_call(y, xs_all, mod, out_gain[layer].reshape(1, d), w_out[layer].astype(BF16),
                                          norm_ffn_g[layer].reshape(1, d), rw, rb, n_proc=n_proc, batch=batch, seq=seq)
        n_blocks = n_proc // MOE_BLK + N_CLASSES
        pos, bcls, nval = _rank_call(cls.reshape(n_proc // 128, 128), n_blocks)
        pos3 = pos.reshape(n_proc // ROW_DMA, 1, ROW_DMA)
        bcls, nval = bcls[0, :n_blocks], nval[0, :n_blocks]
        sorted_rows = _scatter_call(h_ext, pos3, n_blocks * MOE_BLK)
        ys = _moe_call(sorted_rows, lo_tab[bcls], hi_tab[bcls], nval, w_gate[layer].astype(BF16),
                       w_up[layer].astype(BF16), w_down[layer].astype(BF16))
        ffn = _gather_call(ys, pos3)
        xs_all, mod_prev = x_mid, mod
    out = _final_call(xs_all, ffn, mod_prev, batch=batch, seq=seq)
    return out.reshape(batch, seq, d)
```

```python
import functools
import math

import numpy as np
import jax
import jax.numpy as jnp
from jax import lax
from jax.experimental import pallas as pl
from jax.experimental.pallas import tpu as pltpu

F32 = jnp.float32
BF16 = jnp.bfloat16
I32 = jnp.int32

D_MODEL = 1024
GRID_W = 64
HEAD_DIM = 64
HEADS = 4
DIFF_DIM = 32
WINDOW = 128
NA_ROWS = 8
NA_COLS = 16
ROPE_BASE = 10000.0
N_EXPERTS = 16
EXPERTS_PER_GROUP = 4
D_EXPERT = 512
ADA_CHUNKS = 6
EPS = 1e-6
NEG = -1e30
D_IN = 2560

COL_A_Q, COL_A_KV, COL_B_Q, COL_B_K, COL_B_V, COL_C_Q, COL_C_K, COL_C_V, COL_D_Q, COL_D_KV = range(10)

TOKEN_TILE = 512
ATT_TQ = 256
NA_QROWS = 4
NA_KROWS = 12
KEY_TILE = 256
ATT_RB = 128
ATT_SLOTS = 4
LOG2E = math.log2(math.e)
MOE_BLK = 256
ROW_DMA = 256
N_PAIRS = 6
N_CLASSES = (N_EXPERTS // EXPERTS_PER_GROUP) * N_PAIRS
PAIR_LO = (0, 0, 0, 1, 1, 2)
PAIR_HI = (1, 2, 3, 2, 3, 3)
EXT_W = D_MODEL + 128
VMEM_LIMIT = 56 * 1024 * 1024


def _cparams(*sem):
    return pltpu.CompilerParams(dimension_semantics=sem, vmem_limit_bytes=VMEM_LIMIT)


def _nt_dot(a, b):
    return lax.dot_general(a, b, (((1,), (1,)), ((), ())), preferred_element_type=F32)


def _dot(a, b):
    return jnp.dot(a, b, preferred_element_type=F32)


def _sigmoid(x):
    return 1.0 / (1.0 + jnp.exp(-x))


def _drop_arg(kern, pos):
    def wrapped(*refs):
        return kern(*refs[:pos], *refs[pos + 1:])
    return wrapped


def _ada_kernel(s_ref, w_ref, b_ref, o_ref):
    s = s_ref[...]
    act = s * _sigmoid(s)
    o_ref[0] = jnp.dot(act, w_ref[0], precision=lax.Precision.HIGHEST,
                       preferred_element_type=F32) + b_ref[0]


def _ada_call(s_all, ada_w, ada_b):
    depth, d, n = ada_w.shape
    rows = s_all.shape[0]
    tn = 1536
    return pl.pallas_call(
        _ada_kernel,
        out_shape=jax.ShapeDtypeStruct((depth, rows, n), F32),
        grid=(depth, n // tn),
        in_specs=[
            pl.BlockSpec((rows, d), lambda l, j: (0, 0)),
            pl.BlockSpec((1, d, tn), lambda l, j: (l, 0, j)),
            pl.BlockSpec((1, 1, tn), lambda l, j: (l, 0, j)),
        ],
        out_specs=pl.BlockSpec((1, rows, tn), lambda l, j: (l, 0, j)),
        compiler_params=_cparams("arbitrary", "arbitrary"),
        name="ada_mod",
    )(s_all, ada_w, ada_b.reshape(depth, 1, n))


def _inproj_kernel(*refs, has_ffn, tm, n_lat_tiles, tiles_per_batch, seq):
    if has_ffn:
        (x_ref, f_ref, modp_ref, mod_ref, g_ref, w_ref, qkg_ref, bd64_ref, bd32_ref,
         c64_ref, s64_ref, c32_ref, s32_ref, xo_ref, u_ref) = refs
    else:
        (x_ref, mod_ref, g_ref, w_ref, qkg_ref, bd64_ref, bd32_ref,
         c64_ref, s64_ref, c32_ref, s32_ref, u_ref) = refs
    i = pl.program_id(0)
    hm = tm // 2

    def prologue(r0):
        x = x_ref[r0:r0 + hm, :]
        if has_ffn:
            x = x + modp_ref[0, 5:6, :] * f_ref[r0:r0 + hm, :]
            xo_ref[r0:r0 + hm, :] = x
        ms = jnp.mean(x * x, axis=-1, keepdims=True)
        hn = x * lax.rsqrt(ms + EPS) * g_ref[...]
        return (hn * (1.0 + mod_ref[0, 1:2, :]) + mod_ref[0, 0:1, :]).astype(BF16)

    p0 = jnp.where(i < n_lat_tiles, (i % tiles_per_batch) * tm, seq)
    p0 = pl.multiple_of(p0, tm)
    lane = lax.broadcasted_iota(I32, (hm, 128), 1)

    def norm_rope(a, r0, gidx, group, bd_ref, rope_tabs):
        w = a.shape[1]
        ssq = _dot((a * a).astype(BF16), bd_ref[0:w, 0:w])
        r = lax.rsqrt(ssq * (1.0 / group) + EPS)
        outs = []
        for s in range(w // 128):
            t = a[:, 128 * s:128 * s + 128] * r[:, 128 * s:128 * s + 128] * qkg_ref[gidx:gidx + 1, :]
            if rope_tabs is not None:
                c_ref, s_ref, half = rope_tabs
                fwd = pltpu.roll(t, 128 - half, axis=1)
                bwd = pltpu.roll(t, half, axis=1)
                sw = jnp.where((lane % (2 * half)) < half, fwd, bwd)
                t = t * c_ref[pl.ds(p0 + r0, hm), :] + sw * s_ref[pl.ds(p0 + r0, hm), :]
            outs.append(t)
        return outs[0] if len(outs) == 1 else jnp.concatenate(outs, axis=1)

    rope64 = (c64_ref, s64_ref, HEAD_DIM // 4)
    rope32 = (c32_ref, s32_ref, DIFF_DIM // 4)

    def epilogue(acc, r0, blk):
        if blk == COL_A_Q:
            out = norm_rope(acc, r0, 0, HEAD_DIM, bd64_ref, rope64)
        elif blk == COL_A_KV:
            out = jnp.concatenate([norm_rope(acc[:, :128], r0, 1, HEAD_DIM, bd64_ref, rope64), acc[:, 128:]], axis=1)
        elif blk == COL_B_Q:
            out = norm_rope(acc, r0, 2, HEAD_DIM, bd64_ref, None)
        elif blk == COL_B_K:
            out = norm_rope(acc, r0, 3, HEAD_DIM, bd64_ref, None)
        elif blk == COL_C_Q:
            out = norm_rope(acc, r0, 4, DIFF_DIM, bd32_ref, rope32)
        elif blk == COL_C_K:
            out = norm_rope(acc, r0, 5, DIFF_DIM, bd32_ref, rope32)
        elif blk == COL_D_Q:
            out = norm_rope(acc, r0, 6, HEAD_DIM, bd64_ref, rope64)
        elif blk == COL_D_KV:
            out = jnp.concatenate([norm_rope(acc[:, :128], r0, 7, HEAD_DIM, bd64_ref, rope64), acc[:, 128:]], axis=1)
        else:
            out = acc
        u_ref[r0:r0 + hm, 256 * blk:256 * blk + 256] = out.astype(BF16)

    n_blk = D_IN // 256
    units = [(r0, blk) for r0 in (0, hm) for blk in range(n_blk)]
    hbs = {0: prologue(0)}
    acc = _dot(hbs[0], w_ref[:, 0:256])
    hbs[hm] = prologue(hm)
    for t, (r0, blk) in enumerate(units):
        nxt = None
        if t + 1 < len(units):
            r1, b1 = units[t + 1]
            nxt = _dot(hbs[r1], w_ref[:, 256 * b1:256 * b1 + 256])
        epilogue(acc, r0, blk)
        acc = nxt


def _mod_spec(n_lat_tiles, tiles_per_batch, batch):
    return pl.BlockSpec((1, ADA_CHUNKS, D_MODEL),
                        lambda i: (jnp.where(i < n_lat_tiles, i // tiles_per_batch, batch), 0, 0))


def _const_spec(shape):
    return pl.BlockSpec(shape, lambda i: (0,) * len(shape))


def _inproj_call(x, ffn, mod_prev, mod, g, w_in, qkg, bd64, bd32, tabs, *, batch, seq):
    n_all = x.shape[0]
    tm = TOKEN_TILE
    n_lat_tiles = batch * seq // tm
    tpb = seq // tm
    has_ffn = ffn is not None
    tile = pl.BlockSpec((tm, D_MODEL), lambda i: (i, 0))
    modspec = _mod_spec(n_lat_tiles, tpb, batch)
    tab_rows = tabs[0].shape[0]
    in_specs = [tile]
    args = [x]
    if has_ffn:
        in_specs += [tile, modspec]
        args += [ffn, mod_prev]
    in_specs += [modspec, _const_spec((1, D_MODEL)), _const_spec((D_MODEL, D_IN)), _const_spec((8, 128)),
                 _const_spec((256, 256)), _const_spec((256, 256))] + [_const_spec((tab_rows, 128))] * 4
    args += [mod, g, w_in, qkg, bd64, bd32] + list(tabs)
    u_shape = jax.ShapeDtypeStruct((n_all, D_IN), BF16)
    u_spec = pl.BlockSpec((tm, D_IN), lambda i: (i, 0))
    if has_ffn:
        out_shape = (jax.ShapeDtypeStruct((n_all, D_MODEL), F32), u_shape)
        out_specs = (tile, u_spec)
    else:
        out_shape, out_specs = u_shape, u_spec
    kern = functools.partial(_inproj_kernel, has_ffn=has_ffn, tm=tm, n_lat_tiles=n_lat_tiles,
                             tiles_per_batch=tpb, seq=seq)
    res = pl.pallas_call(
        kern, out_shape=out_shape, grid=(n_all // tm,), in_specs=in_specs, out_specs=out_specs,
        compiler_params=_cparams("arbitrary"), name="inproj",
    )(*args)
    return res if has_ffn else (x, res)


def _softmax_parts(scores, extra=None):
    m = scores[0].max(axis=-1, keepdims=True)
    for s in scores[1:]:
        m = jnp.maximum(m, s.max(axis=-1, keepdims=True))
    if extra is not None:
        m = jnp.maximum(m, extra)
    ps = [jnp.exp2(s - m) for s in scores]
    l = ps[0].sum(axis=-1, keepdims=True)
    for p in ps[1:]:
        l = l + p.sum(axis=-1, keepdims=True)
    if extra is not None:
        l = l + jnp.exp2(extra - m)
    return ps, 1.0 / l


class _AttItem:
    def __init__(self, q, kt_view, v_view, ranges, bias=None, extra=None):
        self.q, self.kt_view, self.v_view, self.ranges, self.bias, self.extra = q, kt_view, v_view, ranges, bias, extra
        self.n_keys = sum(n for _, n in ranges)
        self.m = None

    def scores(self, s_scr, rows):
        q_blk = self.q()
        m, off, ti = None, 0, 0
        for start, n in self.ranges:
            for o in range(0, n, KEY_TILE):
                w = min(KEY_TILE, n - o)
                s = _dot(q_blk, self.kt_view[:, pl.ds(start + o, w)])
                b = self.bias(ti) if self.bias is not None else None
                if b is not None:
                    s = s + b
                ti += 1
                s_scr[rows, off + o:off + o + w] = s
                tile_max = s.max(axis=-1, keepdims=True)
                m = tile_max if m is None else jnp.maximum(m, tile_max)
            off += n
        self.m = m if self.extra is None else jnp.maximum(m, self.extra)

    def probs(self, s_scr, p_scr, rows):
        for o in range(0, self.n_keys, KEY_TILE):
            w = min(KEY_TILE, self.n_keys - o)
            p_scr[rows, o:o + w] = jnp.exp2(s_scr[rows, o:o + w] - self.m).astype(BF16)

    def values(self, p_scr, rows):
        pv, off = None, 0
        for start, n in self.ranges:
            c = _dot(p_scr[rows, off:off + n], self.v_view[pl.ds(start, n), :])
            pv = c if pv is None else pv + c
            off += n
        denom = pv[:, 64:65]
        if self.extra is not None:
            denom = denom + jnp.exp2(self.extra - self.m)
        return pv[:, 0:64] / denom


def _run_att_items(items, s_scr, p_scr):
    def rows(t):
        r0 = (t % ATT_SLOTS) * ATT_RB
        return slice(r0, r0 + ATT_RB)

    outs = []
    for t in range(len(items) + 2):
        if t < len(items):
            items[t].scores(s_scr, rows(t))
        if 0 <= t - 1 < len(items):
            items[t - 1].probs(s_scr, p_scr, rows(t - 1))
        if 0 <= t - 2 < len(items):
            outs.append(items[t - 2].values(p_scr, rows(t - 2)))
    return outs


def _stage_heads(k_parts, v_parts, kt_scr, v_scr):
    eye = jnp.where(lax.broadcasted_iota(I32, (HEAD_DIM, HEAD_DIM), 0)
                    == lax.broadcasted_iota(I32, (HEAD_DIM, HEAD_DIM), 1), 1.0, 0.0).astype(BF16)
    for h in range(kt_scr.shape[0]):
        r0 = 0
        for (kr, kc0), (vr, vc0) in zip(k_parts, v_parts):
            n = kr.shape[0]
            kt_scr[h, :, r0:r0 + n] = _nt_dot(eye, kr[:, kc0 + 64 * h:kc0 + 64 * h + 64]).astype(BF16)
            v_scr[h, r0:r0 + n, 0:64] = vr[:, vc0 + 64 * h:vc0 + 64 * h + 64]
            r0 += n
        v_scr[h, :, 64:128] = jnp.ones((v_scr.shape[1], 64), BF16)


def _group_rms(y):
    return y * lax.rsqrt(jnp.mean(y * y, axis=-1, keepdims=True) + EPS)


def _gqa_pair(q_ref, j):
    return jnp.concatenate([q_ref[:, 128 * j:128 * j + 64], q_ref[:, 128 * j + 64:128 * j + 128]], axis=0)


def _stage_once_per_batch_row(k_parts, v_parts, k_scr, v_scr):
    @pl.when(pl.program_id(1) == 0)
    def _():
        _stage_heads(k_parts, v_parts, k_scr, v_scr)


def _row_blocks(n):
    return range(0, n, ATT_RB)


def _q_thunk(q_ref, r0, head):
    return lambda: q_ref[r0:r0 + ATT_RB, 64 * head:64 * head + 64]


def _heads_to_rows(outs, n_heads):
    per = len(outs) // n_heads
    return jnp.concatenate([jnp.concatenate(outs[h * per:(h + 1) * per], axis=0) for h in range(n_heads)], axis=1)


def _dense_kernel(q_ref, kv_ref, kvc_ref, o_ref, kt_scr, v_scr, s_scr, p_scr, *, tq):
    _stage_once_per_batch_row([(kv_ref, 0), (kvc_ref, 0)], [(kv_ref, 128), (kvc_ref, 128)], kt_scr, v_scr)
    nk = v_scr.shape[1]
    items = [_AttItem(_q_thunk(q_ref, r0, hq), kt_scr.at[hq // 2], v_scr.at[hq // 2], [(0, nk)])
             for hq in range(HEADS) for r0 in _row_blocks(tq)]
    y = _heads_to_rows(_run_att_items(items, s_scr, p_scr), HEADS)
    o_ref[...] = _group_rms(y).astype(BF16)


def _window_kernel(sink_ref, q_ref, kv_ref, kvc_ref, mask_ref, o_ref, kt_scr, v_scr, s_scr, p_scr, *, tq, seq):
    _stage_once_per_batch_row([(kv_ref, 0), (kvc_ref, 0)], [(kv_ref, 128), (kvc_ref, 128)], kt_scr, v_scr)
    kw = tq + 2 * WINDOW
    nc = v_scr.shape[1] - seq
    ks = pl.multiple_of(jnp.clip(pl.program_id(1) * tq - WINDOW, 0, seq - kw), 128)

    def mask_thunk(r0):
        return lambda ti: (mask_ref[0, r0:r0 + ATT_RB, KEY_TILE * ti:KEY_TILE * ti + KEY_TILE]
                           if KEY_TILE * ti < kw else None)

    items = [_AttItem(_q_thunk(q_ref, r0, hq), kt_scr.at[hq // 2], v_scr.at[hq // 2], [(ks, kw), (seq, nc)],
                      bias=mask_thunk(r0), extra=sink_ref[hq] * LOG2E)
             for hq in range(HEADS) for r0 in _row_blocks(tq)]
    y = _heads_to_rows(_run_att_items(items, s_scr, p_scr), HEADS)
    o_ref[...] = _group_rms(y).astype(BF16)


def _neigh_kernel(q_ref, k_ref, v_ref, kc_ref, vc_ref, bias_ref, o_ref, kt_scr, v_scr, s_scr, p_scr, *, rows, seq):
    _stage_once_per_batch_row([(k_ref, 0), (kc_ref, 0)], [(v_ref, 0), (vc_ref, 0)], kt_scr, v_scr)
    rs = jnp.clip(NA_QROWS * pl.program_id(1) - NA_ROWS // 2, 0, rows - NA_KROWS)
    k0 = pl.multiple_of(rs * GRID_W, KEY_TILE)
    nk = NA_KROWS * GRID_W
    nc = v_scr.shape[1] - seq

    def bias_thunk(h, r0):
        return lambda ti: (bias_ref[0, h, r0:r0 + ATT_RB, KEY_TILE * ti:KEY_TILE * ti + KEY_TILE]
                           if KEY_TILE * ti < nk else None)

    items = [_AttItem(_q_thunk(q_ref, r0, h), kt_scr.at[h], v_scr.at[h], [(k0, nk), (seq, nc)], bias=bias_thunk(h, r0))
             for h in range(HEADS) for r0 in _row_blocks(q_ref.shape[0])]
    y = _heads_to_rows(_run_att_items(items, s_scr, p_scr), HEADS)
    o_ref[...] = _group_rms(y).astype(BF16)


def _diff_lambda(lam_ref, lam_init):
    lp = lam_ref[...]
    a = jnp.sum(lp[0:1] * lp[1:2], axis=-1, keepdims=True)
    b = jnp.sum(lp[2:3] * lp[3:4], axis=-1, keepdims=True)
    return jnp.exp(a) - jnp.exp(b) + lam_init


def _diff_heads(q, k_parts, v_parts, lam, lam_init):
    t = q.shape[0]
    first = lax.broadcasted_iota(I32, (t, HEAD_DIM), 1) < DIFF_DIM
    zero = jnp.zeros((t, HEAD_DIM), BF16)
    outs = []
    for h in range(HEADS):
        sl = slice(64 * h, 64 * h + 64)
        qh = q[:, sl]
        ks = [kp[:, sl] for kp in k_parts]
        maps = []
        for half in range(2):
            qm = jnp.where(first, qh, zero) if half == 0 else jnp.where(first, zero, qh)
            ps, inv = _softmax_parts([_nt_dot(qm, kk) for kk in ks])
            maps.append([p * inv for p in ps])
        o = None
        for a0, a1, vp in zip(maps[0], maps[1], v_parts):
            c = _dot((a0 - lam * a1).astype(BF16), vp[:, sl])
            o = c if o is None else o + c
        outs.append(_group_rms(o) * (1.0 - lam_init))
    return jnp.concatenate(outs, axis=1)


def _diff_kernel(lam_ref, q_ref, k_ref, v_ref, kc_ref, vc_ref, o_ref, kt_scr, v_scr, s_scr, p_scr, *, lam_init, tq):
    _stage_once_per_batch_row([(k_ref, 0), (kc_ref, 0)], [(v_ref, 0), (vc_ref, 0)], kt_scr, v_scr)
    lam = _diff_lambda(lam_ref, lam_init)
    nk = v_scr.shape[1]
    first = lax.broadcasted_iota(I32, (ATT_RB, HEAD_DIM), 1) < DIFF_DIM
    zero = jnp.zeros((ATT_RB, HEAD_DIM), BF16)

    def q_map(r0, h, half):
        def thunk():
            qh = q_ref[r0:r0 + ATT_RB, 64 * h:64 * h + 64]
            return jnp.where(first, qh, zero) if half == 0 else jnp.where(first, zero, qh)
        return thunk

    items = [_AttItem(q_map(r0, h, half), kt_scr.at[h], v_scr.at[h], [(0, nk)])
             for h in range(HEADS) for r0 in _row_blocks(tq) for half in range(2)]
    outs = _run_att_items(items, s_scr, p_scr)
    diffs = [outs[i] - lam * outs[i + 1] for i in range(0, len(outs), 2)]
    per = len(diffs) // HEADS
    heads = [_group_rms(jnp.concatenate(diffs[h * per:(h + 1) * per], axis=0)) * (1.0 - lam_init)
             for h in range(HEADS)]
    o_ref[...] = jnp.concatenate(heads, axis=1).astype(BF16)


class _Cols:
    def __init__(self, ref, blk):
        self.ref, self.base = ref, 256 * blk

    def __getitem__(self, idx):
        rows, sl = idx
        return self.ref[rows, self.base + sl.start:self.base + sl.stop]


def _ctx_kernel(sink_ref, lam_ref, u_ref, o_ref, *, lam_init):
    L = u_ref.shape[0]
    top = lax.broadcasted_iota(I32, (2 * L, 1), 0) < L
    for q_blk, kv_blk, out_col, use_sink in ((COL_A_Q, COL_A_KV, 0, True), (COL_D_Q, COL_D_KV, 768, False)):
        qv, kv = _Cols(u_ref, q_blk), _Cols(u_ref, kv_blk)
        outs = [None] * HEADS
        for j in range(2):
            kc = kv[:, slice(64 * j, 64 * j + 64)]
            vc = kv[:, slice(128 + 64 * j, 128 + 64 * j + 64)]
            q2 = jnp.concatenate([qv[:, slice(128 * j, 128 * j + 64)], qv[:, slice(128 * j + 64, 128 * j + 128)]], axis=0)
            snk = jnp.where(top, sink_ref[2 * j], sink_ref[2 * j + 1]) * LOG2E if use_sink else None
            (p,), inv = _softmax_parts([_nt_dot(q2, kc)], extra=snk)
            o = _dot(p.astype(BF16), vc) * inv
            outs[2 * j], outs[2 * j + 1] = o[:L], o[L:]
        o_ref[:, out_col:out_col + 256] = _group_rms(jnp.concatenate(outs, axis=1)).astype(BF16)
    qv, kv, vv = _Cols(u_ref, COL_B_Q), _Cols(u_ref, COL_B_K), _Cols(u_ref, COL_B_V)
    outs = []
    for h in range(HEADS):
        sl = slice(64 * h, 64 * h + 64)
        (p,), inv = _softmax_parts([_nt_dot(qv[:, sl], kv[:, sl])])
        outs.append(_dot(p.astype(BF16), vv[:, sl]) * inv)
    o_ref[:, 256:512] = _group_rms(jnp.concatenate(outs, axis=1)).astype(BF16)
    lam = _diff_lambda(lam_ref, lam_init)
    qc = u_ref[:, 256 * COL_C_Q:256 * COL_C_Q + 256]
    o_ref[:, 512:768] = _diff_heads(qc, [_Cols(u_ref, COL_C_K)], [_Cols(u_ref, COL_C_V)], lam, lam_init).astype(BF16)


def _attention_calls(u, sink, lam_p, bias, wmask, *, layer, batch, seq, ctx_len, need_ctx):
    lam_init = 0.8 - 0.6 * math.exp(-0.3 * layer)
    n_all = u.shape[0]
    tq = ATT_TQ
    nq = seq // tq
    nk_all = seq + ctx_len
    cb0 = batch * seq // ctx_len
    y_shape = jax.ShapeDtypeStruct((n_all, D_MODEL), BF16)
    smem = pl.BlockSpec(memory_space=pltpu.SMEM)
    any_spec = pl.BlockSpec(memory_space=pl.ANY)
    lam_spec2 = pl.BlockSpec((4, DIFF_DIM), lambda b, i: (0, 0))
    cp = _cparams("arbitrary", "arbitrary")

    def scratch(n_heads, n_keys):
        return [pltpu.VMEM((n_heads, HEAD_DIM, nk_all), BF16), pltpu.VMEM((n_heads, nk_all, 128), BF16),
                pltpu.VMEM((ATT_SLOTS * ATT_RB, n_keys), F32), pltpu.VMEM((ATT_SLOTS * ATT_RB, n_keys), BF16)]

    def qspec(col):
        return pl.BlockSpec((tq, 256), lambda b, i: (b * nq + i, col))

    def latspec(col):
        return pl.BlockSpec((seq, 256), lambda b, i: (b, col))

    def ctxspec(col):
        return pl.BlockSpec((ctx_len, 256), lambda b, i: (cb0 + b, col))

    kw = tq + 2 * WINDOW

    def first_mid_last(n):
        return lambda b, i: (jnp.where(i == 0, 0, jnp.where(i == n - 1, 2, 1)),)

    wcase = first_mid_last(nq)
    y = pl.pallas_call(
        functools.partial(_window_kernel, tq=tq, seq=seq),
        out_shape=y_shape, grid=(batch, nq),
        in_specs=[smem, qspec(COL_A_Q), latspec(COL_A_KV), ctxspec(COL_A_KV),
                  pl.BlockSpec((1, tq, kw), lambda b, i: wcase(b, i) + (0, 0))],
        out_specs=qspec(0), scratch_shapes=scratch(2, kw + ctx_len),
        compiler_params=cp, name="mix_window",
    )(sink, u, u, u, wmask)

    rows = seq // GRID_W
    nrb = rows // NA_QROWS
    qrows = NA_QROWS * GRID_W
    nkb = NA_KROWS * GRID_W
    bcase = first_mid_last(nrb)
    y = pl.pallas_call(
        _drop_arg(functools.partial(_neigh_kernel, rows=rows, seq=seq), 6),
        out_shape=y_shape, grid=(batch, nrb),
        in_specs=[pl.BlockSpec((qrows, 256), lambda b, i: (b * nrb + i, COL_B_Q)),
                  latspec(COL_B_K), latspec(COL_B_V), ctxspec(COL_B_K), ctxspec(COL_B_V),
                  pl.BlockSpec((1, HEADS, qrows, nkb), lambda b, i: bcase(b, i) + (0, 0, 0)), any_spec],
        out_specs=pl.BlockSpec((qrows, 256), lambda b, i: (b * nrb + i, 1)),
        scratch_shapes=scratch(HEADS, nkb + ctx_len),
        input_output_aliases={6: 0}, compiler_params=cp, name="mix_neigh",
    )(u, u, u, u, u, bias, y)

    y = pl.pallas_call(
        _drop_arg(functools.partial(_diff_kernel, lam_init=lam_init, tq=tq), 6),
        out_shape=y_shape, grid=(batch, nq),
        in_specs=[lam_spec2, qspec(COL_C_Q), latspec(COL_C_K), latspec(COL_C_V),
                  ctxspec(COL_C_K), ctxspec(COL_C_V), any_spec],
        out_specs=qspec(2), scratch_shapes=scratch(HEADS, nk_all),
        input_output_aliases={6: 0}, compiler_params=cp, name="mix_diff",
    )(lam_p, u, u, u, u, u, y)

    y = pl.pallas_call(
        _drop_arg(functools.partial(_dense_kernel, tq=tq), 3),
        out_shape=y_shape, grid=(batch, nq),
        in_specs=[qspec(COL_D_Q), latspec(COL_D_KV), ctxspec(COL_D_KV), any_spec],
        out_specs=qspec(3), scratch_shapes=scratch(2, nk_all),
        input_output_aliases={3: 0}, compiler_params=cp, name="mix_dense",
    )(u, u, u, y)

    if need_ctx:
        y = pl.pallas_call(
            _drop_arg(functools.partial(_ctx_kernel, lam_init=lam_init), 3),
            out_shape=y_shape, grid=(batch,),
            in_specs=[smem, pl.BlockSpec((4, DIFF_DIM), lambda b: (0, 0)),
                      pl.BlockSpec((ctx_len, D_IN), lambda b: (cb0 + b, 0)), any_spec],
            out_specs=pl.BlockSpec((ctx_len, D_MODEL), lambda b: (cb0 + b, 0)),
            input_output_aliases={3: 0}, compiler_params=_cparams("arbitrary"), name="mix_ctx",
        )(sink, lam_p, u, y)
    return y


def _select4(idx, vals):
    return jnp.where(idx == 0, vals[0], jnp.where(idx == 1, vals[1], jnp.where(idx == 2, vals[2], vals[3])))


def _outproj_kernel(y_ref, x_ref, mod_ref, gain_ref, w_ref, g_ref, rw_ref, rb_ref, xo_ref, h_ref, cls_ref, *, tm):
    hm = tm // 2

    def project(r0):
        yg = (y_ref[r0:r0 + hm, :].astype(F32) * gain_ref[...]).astype(BF16)
        return _dot(yg, w_ref[...])

    def normalise(r0, proj):
        xm = x_ref[r0:r0 + hm, :] + mod_ref[0, 2:3, :] * proj
        xo_ref[r0:r0 + hm, :] = xm
        hn = xm * lax.rsqrt(jnp.mean(xm * xm, axis=-1, keepdims=True) + EPS) * g_ref[...]
        h2 = hn * (1.0 + mod_ref[0, 4:5, :]) + mod_ref[0, 3:4, :]
        h_ref[r0:r0 + hm, 0:D_MODEL] = h2
        hi = h2.astype(BF16)
        lo = (h2 - hi.astype(F32)).astype(BF16)
        return (_dot(hi, rw_ref[...]) + _dot(lo, rw_ref[...])).T

    def route(r0, lt):
        score = _sigmoid(lt[0:N_EXPERTS] + lt[N_EXPERTS:2 * N_EXPERTS])
        sel = score + rb_ref[...]
        srow = [sel[e:e + 1] for e in range(N_EXPERTS)]
        crow = [score[e:e + 1] for e in range(N_EXPERTS)]

        gidx = jnp.zeros((1, hm), I32)
        gbest = None
        for g in range(N_EXPERTS // EXPERTS_PER_GROUP):
            v = srow[4 * g:4 * g + 4]
            best = None
            for a in range(4):
                for b in range(a + 1, 4):
                    t = v[a] + v[b]
                    best = t if best is None else jnp.maximum(best, t)
            if g == 0:
                gbest = best
            else:
                better = best > gbest
                gidx = jnp.where(better, g, gidx)
                gbest = jnp.where(better, best, gbest)

        iv = [_select4(gidx, [srow[4 * g + i] for g in range(4)]) for i in range(4)]
        sv = [_select4(gidx, [crow[4 * g + i] for g in range(4)]) for i in range(4)]
        chosen = []
        for i in range(4):
            rank = jnp.zeros((1, hm), I32)
            for j in range(4):
                if j != i:
                    beats = (iv[j] >= iv[i]) if j < i else (iv[j] > iv[i])
                    rank = rank + jnp.where(beats, 1, 0)
            chosen.append(rank < 2)
        lo_i = jnp.where(chosen[0], 0, jnp.where(chosen[1], 1, 2))
        hi_i = jnp.where(chosen[3], 3, jnp.where(chosen[2], 2, 1))
        pair = jnp.where(lo_i == 0, hi_i - 1, jnp.where(lo_i == 1, hi_i + 1, 5))
        cls_ref[0, :, r0:r0 + hm] = gidx * N_PAIRS + pair
        s_lo = jnp.where(lo_i == 0, sv[0], jnp.where(lo_i == 1, sv[1], sv[2]))
        s_hi = jnp.where(hi_i == 3, sv[3], jnp.where(hi_i == 2, sv[2], sv[1]))
        inv = 1.0 / (s_lo + s_hi)
        rowi = lax.broadcasted_iota(I32, (128, hm), 0)
        wts = jnp.where(rowi == 0, s_lo * inv, jnp.where(rowi == 1, s_hi * inv, 0.0))
        h_ref[r0:r0 + hm, D_MODEL:EXT_W] = wts.T

    proj0 = project(0)
    proj1 = project(hm)
    lt0 = normalise(0, proj0)
    lt1 = normalise(hm, proj1)
    route(0, lt0)
    route(hm, lt1)


def _outproj_call(y, x, mod, gain, w_out, g, rw, rb, *, n_proc, batch, seq):
    tm = TOKEN_TILE
    n_tiles = n_proc // tm
    tile = pl.BlockSpec((tm, D_MODEL), lambda i: (i, 0))
    return pl.pallas_call(
        functools.partial(_outproj_kernel, tm=tm),
        out_shape=(jax.ShapeDtypeStruct((n_proc, D_MODEL), F32),
                   jax.ShapeDtypeStruct((n_proc, EXT_W), F32),
                   jax.ShapeDtypeStruct((n_tiles, 1, tm), I32)),
        grid=(n_tiles,),
        in_specs=[tile, tile, _mod_spec(batch * seq // tm, seq // tm, batch), _const_spec((1, D_MODEL)),
                  _const_spec((D_MODEL, D_MODEL)), _const_spec((1, D_MODEL)), _const_spec((D_MODEL, 128)),
                  _const_spec((N_EXPERTS, 1))],
        out_specs=(tile, pl.BlockSpec((tm, EXT_W), lambda i: (i, 0)), pl.BlockSpec((1, 1, tm), lambda i: (i, 0, 0))),
        compiler_params=_cparams("arbitrary"), name="outproj_router",
    )(y, x, mod, gain, w_out, g, rw, rb)


def _rank_kernel(cls_ref, pos_ref, bcls_ref, nval_ref, *, blk):
    c = cls_ref[...]
    nr = c.shape[0]
    nbp = bcls_ref.shape[1]
    upper = (lax.broadcasted_iota(I32, (128, 128), 0) <= lax.broadcasted_iota(I32, (128, 128), 1))
    upper = jnp.where(upper, 1.0, 0.0).astype(BF16)
    lower = (lax.broadcasted_iota(I32, (nr, nr), 1) < lax.broadcasted_iota(I32, (nr, nr), 0))
    lower = jnp.where(lower, 1.0, 0.0).astype(BF16)
    bidx = lax.broadcasted_iota(I32, (1, nbp), 1).astype(F32)
    pos = jnp.zeros((nr, 128), F32)
    bcls = jnp.zeros((1, nbp), F32)
    nval = jnp.zeros((1, nbp), F32)
    start = jnp.zeros((1, 1), F32)
    for k in range(N_CLASSES):
        m = c == k
        incl = _dot(jnp.where(m, 1.0, 0.0).astype(BF16), upper)
        tot = incl[:, 127:128]
        above = _dot(lower, jnp.broadcast_to(tot, (nr, 128)).astype(BF16))
        cnt = above[nr - 1:nr, 0:1] + tot[nr - 1:nr, 0:1]
        nblk = jnp.floor((cnt + (blk - 1)) * (1.0 / blk))
        pos = jnp.where(m, start * blk + above + incl - 1.0, pos)
        end = start + nblk
        bcls = bcls + jnp.where(bidx >= end, 1.0, 0.0)
        nval = nval + jnp.where((bidx >= start) & (bidx < end), jnp.clip(cnt - (bidx - start) * blk, 0.0, blk), 0.0)
        start = end
    pos_ref[...] = pos.astype(I32)
    bcls_ref[...] = jnp.minimum(bcls, N_CLASSES - 1.0).astype(I32)
    nval_ref[...] = nval.astype(I32)


def _rank_call(cls2d, n_blocks):
    nr = cls2d.shape[0]
    nbp = -(-n_blocks // 128) * 128
    return pl.pallas_call(
        functools.partial(_rank_kernel, blk=MOE_BLK),
        out_shape=(jax.ShapeDtypeStruct((nr, 128), I32), jax.ShapeDtypeStruct((1, nbp), I32),
                   jax.ShapeDtypeStruct((1, nbp), I32)),
        grid=(1,),
        in_specs=[_const_spec((nr, 128))],
        out_specs=(_const_spec((nr, 128)), _const_spec((1, nbp)), _const_spec((1, nbp))),
        compiler_params=_cparams("arbitrary"), name="class_rank",
    )(cls2d)


ROW_DMA_UNROLL = 8


def _scatter_kernel(pos_ref, h_ref, xs_ref, sem):
    n = h_ref.shape[0]

    def issue(r, carry):
        pltpu.make_async_copy(h_ref.at[pl.ds(r, 1)], xs_ref.at[pl.ds(pos_ref[0, 0, r], 1)], sem).start()
        return carry
    lax.fori_loop(0, n, issue, 0, unroll=ROW_DMA_UNROLL)
    pltpu.make_async_copy(h_ref, xs_ref.at[pl.ds(0, n)], sem).wait()


def _scatter_call(h_ext, pos3, n_slots):
    n, w = h_ext.shape
    r = ROW_DMA
    return pl.pallas_call(
        _scatter_kernel,
        out_shape=jax.ShapeDtypeStruct((n_slots, w), h_ext.dtype),
        grid=(n // r,),
        in_specs=[pl.BlockSpec((1, 1, r), lambda i: (i, 0, 0), memory_space=pltpu.SMEM),
                  pl.BlockSpec((r, w), lambda i: (i, 0))],
        out_specs=pl.BlockSpec(memory_space=pl.ANY),
        scratch_shapes=[pltpu.SemaphoreType.DMA(())],
        compiler_params=_cparams("arbitrary"), name="row_scatter",
    )(pos3, h_ext)


def _gather_kernel(pos_ref, ys_ref, o_ref, sem):
    n = o_ref.shape[0]

    def issue(r, carry):
        pltpu.make_async_copy(ys_ref.at[pl.ds(pos_ref[0, 0, r], 1)], o_ref.at[pl.ds(r, 1)], sem).start()
        return carry
    lax.fori_loop(0, n, issue, 0, unroll=ROW_DMA_UNROLL)
    pltpu.make_async_copy(ys_ref.at[pl.ds(0, n)], o_ref, sem).wait()


def _gather_call(ys, pos3):
    n = pos3.shape[0] * pos3.shape[2]
    w = ys.shape[1]
    r = ROW_DMA
    return pl.pallas_call(
        _gather_kernel,
        out_shape=jax.ShapeDtypeStruct((n, w), ys.dtype),
        grid=(n // r,),
        in_specs=[pl.BlockSpec((1, 1, r), lambda i: (i, 0, 0), memory_space=pltpu.SMEM),
                  pl.BlockSpec(memory_space=pl.ANY)],
        out_specs=pl.BlockSpec((r, w), lambda i: (i, 0)),
        scratch_shapes=[pltpu.SemaphoreType.DMA(())],
        compiler_params=_cparams("arbitrary"), name="row_gather",
    )(pos3, ys)


def _moe_kernel(elo_ref, ehi_ref, nval_ref, xs_ref, wg_lo, wu_lo, wd_lo, wg_hi, wu_hi, wd_hi, ys_ref):
    del elo_ref, ehi_ref
    nv = nval_ref[pl.program_id(0)]

    @pl.when(nv > 0)
    def _():
        live = lax.broadcasted_iota(I32, (MOE_BLK, 1), 0) < nv
        x = jnp.where(live, xs_ref[:, 0:D_MODEL], 0.0).astype(BF16)
        acts = []
        for wg, wu, col in ((wg_lo, wu_lo, D_MODEL), (wg_hi, wu_hi, D_MODEL + 1)):
            w = jnp.where(live, xs_ref[:, col:col + 1], 0.0)
            g = _dot(x, wg[0])
            acts.append((g * _sigmoid(g) * _dot(x, wu[0]) * w).astype(BF16))
        ys_ref[...] = _dot(acts[0], wd_lo[0]) + _dot(acts[1], wd_hi[0])

    @pl.when(nv == 0)
    def _():
        ys_ref[...] = jnp.zeros_like(ys_ref)


def _moe_call(xs, e_lo, e_hi, nval, w_gate, w_up, w_down):
    n_blocks = xs.shape[0] // MOE_BLK

    def wspec(shape, which):
        return pl.BlockSpec((1,) + shape, lambda b, elo, ehi, nv: ((elo, ehi)[which][b], 0, 0))

    gu, dn = (D_MODEL, D_EXPERT), (D_EXPERT, D_MODEL)
    grid_spec = pltpu.PrefetchScalarGridSpec(
        num_scalar_prefetch=3, grid=(n_blocks,),
        in_specs=[pl.BlockSpec((MOE_BLK, EXT_W), lambda b, elo, ehi, nv: (b, 0)),
                  wspec(gu, 0), wspec(gu, 0), wspec(dn, 0), wspec(gu, 1), wspec(gu, 1), wspec(dn, 1)],
        out_specs=pl.BlockSpec((MOE_BLK, D_MODEL), lambda b, elo, ehi, nv: (b, 0)),
    )
    return pl.pallas_call(
        _moe_kernel, out_shape=jax.ShapeDtypeStruct((xs.shape[0], D_MODEL), F32), grid_spec=grid_spec,
        compiler_params=_cparams("arbitrary"), name="moe_experts",
    )(e_lo, e_hi, nval, xs, w_gate, w_up, w_down, w_gate, w_up, w_down)


def _final_kernel(x_ref, f_ref, mod_ref, o_ref):
    o_ref[...] = x_ref[...] + mod_ref[0, 5:6, :] * f_ref[...]


def _final_call(x, ffn, mod, *, batch, seq):
    tm = TOKEN_TILE
    n = batch * seq
    tile = pl.BlockSpec((tm, D_MODEL), lambda i: (i, 0))
    return pl.pallas_call(
        _final_kernel, out_shape=jax.ShapeDtypeStruct((n, D_MODEL), F32), grid=(n // tm,),
        in_specs=[tile, tile, _mod_spec(n // tm, seq // tm, batch)], out_specs=tile,
        compiler_params=_cparams("arbitrary"), name="final_residual",
    )(x, ffn, mod)


def _rope_tables(seq, ident_rows):
    pos = jnp.arange(seq, dtype=jnp.int32)
    row = (pos // GRID_W).astype(F32)[:, None]
    col = (pos % GRID_W).astype(F32)[:, None]

    def tab(half):
        inv = ROPE_BASE ** (-jnp.arange(half, dtype=F32) / half)
        ar, ac = row * inv, col * inv
        cos = jnp.concatenate([jnp.cos(ar), jnp.cos(ar), jnp.cos(ac), jnp.cos(ac)], axis=1)
        sin = jnp.concatenate([-jnp.sin(ar), jnp.sin(ar), -jnp.sin(ac), jnp.sin(ac)], axis=1)
        reps = 128 // (4 * half)
        cos = jnp.concatenate([jnp.tile(cos, (1, reps)), jnp.ones((ident_rows, 128), F32)], axis=0)
        sin = jnp.concatenate([jnp.tile(sin, (1, reps)), jnp.zeros((ident_rows, 128), F32)], axis=0)
        return cos, sin

    c64, s64 = tab(HEAD_DIM // 4)
    c32, s32 = tab(DIFF_DIM // 4)
    return c64, s64, c32, s32


def _block_diag_ones(group):
    idx = np.arange(256) // group
    return jnp.asarray((idx[:, None] == idx[None, :]).astype(np.float32), dtype=BF16)


def _qk_gains(g_win, g_na, g_diff, g_gqa):
    s64, s32 = HEAD_DIM ** -0.5 * LOG2E, DIFF_DIM ** -0.5 * LOG2E
    rows = [jnp.tile(g_win[0] * s64, 2), jnp.tile(g_win[1], 2), jnp.tile(g_na[0] * s64, 2), jnp.tile(g_na[1], 2),
            jnp.tile(g_diff[0] * s32, 4), jnp.tile(g_diff[1], 4), jnp.tile(g_gqa[0] * s64, 2), jnp.tile(g_gqa[1], 2)]
    return jnp.stack(rows).astype(F32)


def _na_bias_tiles(rpb, rows):
    w = GRID_W
    qc = np.arange(w)[:, None]
    kc = np.arange(w)[None, :]
    c_start = np.clip(qc - NA_COLS // 2, 0, w - NA_COLS)
    col_ok = (kc >= c_start) & (kc < c_start + NA_COLS)
    dc = np.clip(kc - qc + NA_COLS - 1, 0, 2 * NA_COLS - 2)
    t = jnp.where(jnp.asarray(col_ok), rpb.astype(F32)[:, :, dc] * LOG2E, NEG)
    neg = jnp.full((rpb.shape[0], w, w), NEG, F32)
    cases = []
    for r0, rs in ((0, 0), (NA_QROWS, 0), (rows - NA_QROWS, rows - NA_KROWS)):
        qtiles = []
        for i in range(NA_QROWS):
            qrow = r0 + i
            r_start = min(max(qrow - NA_ROWS // 2, 0), rows - NA_ROWS)
            blks = []
            for j in range(NA_KROWS):
                krow = rs + j
                blks.append(t[:, krow - qrow + NA_ROWS - 1] if r_start <= krow < r_start + NA_ROWS else neg)
            qtiles.append(jnp.concatenate(blks, axis=2))
        cases.append(jnp.concatenate(qtiles, axis=1))
    return jnp.stack(cases)


def _window_mask_tiles(tq):
    kw = tq + 2 * WINDOW
    r = lax.broadcasted_iota(I32, (3, tq, kw), 1)
    c = lax.broadcasted_iota(I32, (3, tq, kw), 2)
    shift = lax.broadcasted_iota(I32, (3, tq, kw), 0) * WINDOW
    return jnp.where(jnp.abs(c - r - shift) <= WINDOW, 0.0, NEG).astype(F32)


def _router_weights(router_w):
    hi = router_w.astype(BF16)
    lo = (router_w - hi.astype(F32)).astype(BF16)
    pad = jnp.zeros((router_w.shape[0], 128 - 2 * N_EXPERTS), BF16)
    return jnp.concatenate([hi, lo, pad], axis=1)


def kernel(x, c, ctx, c_ctx, ada_w, ada_b, norm_mix_g, norm_ffn_g, w_in, qk_g_win, qk_g_na, qk_g_diff, qk_g_gqa,
           sink_win, rpb_na, lambda_diff, out_gain, w_out, router_w, router_b, w_gate, w_up, w_down):
    batch, seq, d = x.shape
    ctx_len = ctx.shape[1]
    depth = w_in.shape[0]
    n_lat, n_ctx = batch * seq, batch * ctx_len
    n_all = n_lat + n_ctx
    rows = seq // GRID_W
    assert d == D_MODEL and seq % TOKEN_TILE == 0 and n_ctx % TOKEN_TILE == 0 and ctx_len % 128 == 0
    assert rows % NA_QROWS == 0 and rows >= NA_KROWS + NA_QROWS and seq >= ATT_TQ + 2 * WINDOW
    assert ctx_len % 128 == 0 and (NA_KROWS * GRID_W) % KEY_TILE == 0

    pad_rows = -(-(batch + 1) // 8) * 8
    s_all = jnp.concatenate([c, c_ctx[None, :], jnp.zeros((pad_rows - batch - 1, d), F32)], axis=0)
    mods = _ada_call(s_all, ada_w, ada_b).reshape(depth, pad_rows, ADA_CHUNKS, d)

    xs_all = jnp.concatenate([x.reshape(n_lat, d), ctx.reshape(n_ctx, d)], axis=0)
    tabs = _rope_tables(seq, TOKEN_TILE)
    wmask = _window_mask_tiles(ATT_TQ)
    bd64, bd32 = _block_diag_ones(HEAD_DIM), _block_diag_ones(DIFF_DIM)
    rw = _router_weights(router_w)
    rb = router_b.astype(F32).reshape(N_EXPERTS, 1)
    lo_tab = jnp.asarray([EXPERTS_PER_GROUP * (k // N_PAIRS) + PAIR_LO[k % N_PAIRS] for k in range(N_CLASSES)], I32)
    hi_tab = jnp.asarray([EXPERTS_PER_GROUP * (k // N_PAIRS) + PAIR_HI[k % N_PAIRS] for k in range(N_CLASSES)], I32)

    ffn, mod_prev = None, None
    for layer in range(depth):
        need_ctx = layer < depth - 1
        mod = mods[layer]
        qkg = _qk_gains(qk_g_win[layer], qk_g_na[layer], qk_g_diff[layer], qk_g_gqa[layer])
        xs_all, u = _inproj_call(xs_all, ffn, mod_prev, mod, norm_mix_g[layer].reshape(1, d),
                                 w_in[layer].astype(BF16), qkg, bd64, bd32, tabs, batch=batch, seq=seq)
        y = _attention_calls(u, sink_win[layer].astype(F32), lambda_diff[layer].astype(F32),
                             _na_bias_tiles(rpb_na[layer], rows), wmask, layer=layer, batch=batch, seq=seq,
                             ctx_len=ctx_len, need_ctx=need_ctx)
        n_proc = n_all if need_ctx else n_lat
        x_mid, h_ext, cls = _outproj_call(y, xs_all, mod, out_gain[layer].reshape(1, d), w_out[layer].astype(BF16),
                                          norm_ffn_g[layer].reshape(1, d), rw, rb, n_proc=n_proc, batch=batch, seq=seq)
        n_blocks = n_proc // MOE_BLK + N_CLASSES
        pos, bcls, nval = _rank_call(cls.reshape(n_proc // 128, 128), n_blocks)
        pos3 = pos.reshape(n_proc // ROW_DMA, 1, ROW_DMA)
        bcls, nval = bcls[0, :n_blocks], nval[0, :n_blocks]
        sorted_rows = _scatter_call(h_ext, pos3, n_blocks * MOE_BLK)
        ys = _moe_call(sorted_rows, lo_tab[bcls], hi_tab[bcls], nval, w_gate[layer].astype(BF16),
                       w_up[layer].astype(BF16), w_down[layer].astype(BF16))
        ffn = _gather_call(ys, pos3)
        xs_all, mod_prev = x_mid, mod
    out = _final_call(xs_all, ffn, mod_prev, batch=batch, seq=seq)
    return out.reshape(batch, seq, d)
```

```python
import functools
import math

import numpy as np
import jax
import jax.numpy as jnp
from jax import lax
from jax.experimental import pallas as pl
from jax.experimental.pallas import tpu as pltpu

F32 = jnp.float32
BF16 = jnp.bfloat16
I32 = jnp.int32

D_MODEL = 1024
GRID_W = 64
HEAD_DIM = 64
HEADS = 4
DIFF_DIM = 32
WINDOW = 128
NA_ROWS = 8
NA_COLS = 16
ROPE_BASE = 10000.0
N_EXPERTS = 16
EXPERTS_PER_GROUP = 4
D_EXPERT = 512
ADA_CHUNKS = 6
EPS = 1e-6
NEG = -1e30
D_IN = 2560

COL_A_Q, COL_A_KV, COL_B_Q, COL_B_K, COL_B_V, COL_C_Q, COL_C_K, COL_C_V, COL_D_Q, COL_D_KV = range(10)

TOKEN_TILE = 512
ATT_TQ = 256
NA_QROWS = 4
NA_KROWS = 12
KEY_TILE = 256
ATT_RB = 128
ATT_SLOTS = 4
LOG2E = math.log2(math.e)
MOE_BLK = 256
ROW_DMA = 256
N_PAIRS = 6
N_CLASSES = (N_EXPERTS // EXPERTS_PER_GROUP) * N_PAIRS
PAIR_LO = (0, 0, 0, 1, 1, 2)
PAIR_HI = (1, 2, 3, 2, 3, 3)
EXT_W = D_MODEL + 128
VMEM_LIMIT = 56 * 1024 * 1024


def _cparams(*sem):
    return pltpu.CompilerParams(dimension_semantics=sem, vmem_limit_bytes=VMEM_LIMIT)


def _nt_dot(a, b):
    return lax.dot_general(a, b, (((1,), (1,)), ((), ())), preferred_element_type=F32)


def _dot(a, b):
    return jnp.dot(a, b, preferred_element_type=F32)


def _sigmoid(x):
    return 1.0 / (1.0 + jnp.exp(-x))


def _drop_arg(kern, pos):
    def wrapped(*refs):
        return kern(*refs[:pos], *refs[pos + 1:])
    return wrapped


def _ada_kernel(s_ref, w_ref, b_ref, o_ref):
    s = s_ref[...]
    act = s * _sigmoid(s)
    o_ref[0] = jnp.dot(act, w_ref[0], precision=lax.Precision.HIGHEST,
                       preferred_element_type=F32) + b_ref[0]


def _ada_call(s_all, ada_w, ada_b):
    depth, d, n = ada_w.shape
    rows = s_all.shape[0]
    tn = 1536
    return pl.pallas_call(
        _ada_kernel,
        out_shape=jax.ShapeDtypeStruct((depth, rows, n), F32),
        grid=(depth, n // tn),
        in_specs=[
            pl.BlockSpec((rows, d), lambda l, j: (0, 0)),
            pl.BlockSpec((1, d, tn), lambda l, j: (l, 0, j)),
            pl.BlockSpec((1, 1, tn), lambda l, j: (l, 0, j)),
        ],
        out_specs=pl.BlockSpec((1, rows, tn), lambda l, j: (l, 0, j)),
        compiler_params=_cparams("arbitrary", "arbitrary"),
        name="ada_mod",
    )(s_all, ada_w, ada_b.reshape(depth, 1, n))


def _start_row_gather(pos_ref, src_ref, dst_ref, sem):
    for r in range(dst_ref.shape[0]):
        pltpu.make_async_copy(src_ref.at[pl.ds(pos_ref[0, 0, r], 1)], dst_ref.at[pl.ds(r, 1)], sem).start()


def _wait_row_gather(src_ref, dst_ref, sem):
    pltpu.make_async_copy(src_ref.at[pl.ds(0, dst_ref.shape[0])], dst_ref, sem).wait()


def _prefetched_rows(pos_ref, pos_next_ref, src_ref, buf, sems):
    i = pl.program_id(0)
    slot = i % 2

    @pl.when(i == 0)
    def _():
        _start_row_gather(pos_ref, src_ref, buf.at[0], sems.at[0])

    @pl.when(i + 1 < pl.num_programs(0))
    def _():
        _start_row_gather(pos_next_ref, src_ref, buf.at[1 - slot], sems.at[1 - slot])

    _wait_row_gather(src_ref, buf.at[slot], sems.at[slot])
    return buf.at[slot]


def _inproj_kernel(*refs, has_ffn, tm, n_lat_tiles, tiles_per_batch, seq):
    if has_ffn:
        (x_ref, pos_ref, pos_next_ref, ys_ref, modp_ref, mod_ref, g_ref, w_ref, qkg_ref, bd64_ref, bd32_ref,
         c64_ref, s64_ref, c32_ref, s32_ref, xo_ref, u_ref, fbuf, fsems) = refs
        f_ref = _prefetched_rows(pos_ref, pos_next_ref, ys_ref, fbuf, fsems)
    else:
        (x_ref, mod_ref, g_ref, w_ref, qkg_ref, bd64_ref, bd32_ref,
         c64_ref, s64_ref, c32_ref, s32_ref, u_ref) = refs
    i = pl.program_id(0)
    hm = tm // 2

    def prologue(r0):
        x = x_ref[r0:r0 + hm, :]
        if has_ffn:
            x = x + modp_ref[0, 5:6, :] * f_ref[r0:r0 + hm, :]
            xo_ref[r0:r0 + hm, :] = x
        ms = jnp.mean(x * x, axis=-1, keepdims=True)
        hn = x * lax.rsqrt(ms + EPS) * g_ref[...]
        return (hn * (1.0 + mod_ref[0, 1:2, :]) + mod_ref[0, 0:1, :]).astype(BF16)

    p0 = jnp.where(i < n_lat_tiles, (i % tiles_per_batch) * tm, seq)
    p0 = pl.multiple_of(p0, tm)
    lane = lax.broadcasted_iota(I32, (hm, 128), 1)

    def norm_rope(a, r0, gidx, group, bd_ref, rope_tabs):
        w = a.shape[1]
        ssq = _dot((a * a).astype(BF16), bd_ref[0:w, 0:w])
        r = lax.rsqrt(ssq * (1.0 / group) + EPS)
        outs = []
        for s in range(w // 128):
            t = a[:, 128 * s:128 * s + 128] * r[:, 128 * s:128 * s + 128] * qkg_ref[gidx:gidx + 1, :]
            if rope_tabs is not None:
                c_ref, s_ref, half = rope_tabs
                fwd = pltpu.roll(t, 128 - half, axis=1)
                bwd = pltpu.roll(t, half, axis=1)
                sw = jnp.where((lane % (2 * half)) < half, fwd, bwd)
                t = t * c_ref[pl.ds(p0 + r0, hm), :] + sw * s_ref[pl.ds(p0 + r0, hm), :]
            outs.append(t)
        return outs[0] if len(outs) == 1 else jnp.concatenate(outs, axis=1)

    rope64 = (c64_ref, s64_ref, HEAD_DIM // 4)
    rope32 = (c32_ref, s32_ref, DIFF_DIM // 4)

    def epilogue(acc, r0, blk):
        if blk == COL_A_Q:
            out = norm_rope(acc, r0, 0, HEAD_DIM, bd64_ref, rope64)
        elif blk == COL_A_KV:
            out = jnp.concatenate([norm_rope(acc[:, :128], r0, 1, HEAD_DIM, bd64_ref, rope64), acc[:, 128:]], axis=1)
        elif blk == COL_B_Q:
            out = norm_rope(acc, r0, 2, HEAD_DIM, bd64_ref, None)
        elif blk == COL_B_K:
            out = norm_rope(acc, r0, 3, HEAD_DIM, bd64_ref, None)
        elif blk == COL_C_Q:
            out = norm_rope(acc, r0, 4, DIFF_DIM, bd32_ref, rope32)
        elif blk == COL_C_K:
            out = norm_rope(acc, r0, 5, DIFF_DIM, bd32_ref, rope32)
        elif blk == COL_D_Q:
            out = norm_rope(acc, r0, 6, HEAD_DIM, bd64_ref, rope64)
        elif blk == COL_D_KV:
            out = jnp.concatenate([norm_rope(acc[:, :128], r0, 7, HEAD_DIM, bd64_ref, rope64), acc[:, 128:]], axis=1)
        else:
            out = acc
        u_ref[r0:r0 + hm, 256 * blk:256 * blk + 256] = out.astype(BF16)

    n_blk = D_IN // 256
    units = [(r0, blk) for r0 in (0, hm) for blk in range(n_blk)]
    hbs = {0: prologue(0)}
    acc = _dot(hbs[0], w_ref[:, 0:256])
    hbs[hm] = prologue(hm)
    for t, (r0, blk) in enumerate(units):
        nxt = None
        if t + 1 < len(units):
            r1, b1 = units[t + 1]
            nxt = _dot(hbs[r1], w_ref[:, 256 * b1:256 * b1 + 256])
        epilogue(acc, r0, blk)
        acc = nxt


def _mod_spec(n_lat_tiles, tiles_per_batch, batch):
    return pl.BlockSpec((1, ADA_CHUNKS, D_MODEL),
                        lambda i: (jnp.where(i < n_lat_tiles, i // tiles_per_batch, batch), 0, 0))


def _const_spec(shape):
    return pl.BlockSpec(shape, lambda i: (0,) * len(shape))


def _pos_tile_specs(n_tiles, tm):
    return [pl.BlockSpec((1, 1, tm), lambda i: (i, 0, 0), memory_space=pltpu.SMEM),
            pl.BlockSpec((1, 1, tm), lambda i: (jnp.minimum(i + 1, n_tiles - 1), 0, 0), memory_space=pltpu.SMEM)]


def _inproj_call(x, ffn_src, mod_prev, mod, g, w_in, qkg, bd64, bd32, tabs, *, batch, seq):
    n_all = x.shape[0]
    tm = TOKEN_TILE
    n_lat_tiles = batch * seq // tm
    tpb = seq // tm
    has_ffn = ffn_src is not None
    tile = pl.BlockSpec((tm, D_MODEL), lambda i: (i, 0))
    modspec = _mod_spec(n_lat_tiles, tpb, batch)
    tab_rows = tabs[0].shape[0]
    in_specs = [tile]
    args = [x]
    scratch = []
    if has_ffn:
        ys, pos = ffn_src
        pos_tiles = pos.reshape(n_all // tm, 1, tm)
        in_specs += _pos_tile_specs(n_all // tm, tm) + [pl.BlockSpec(memory_space=pl.ANY), modspec]
        args += [pos_tiles, pos_tiles, ys, mod_prev]
        scratch = [pltpu.VMEM((2, tm, D_MODEL), F32), pltpu.SemaphoreType.DMA((2,))]
    in_specs += [modspec, _const_spec((1, D_MODEL)), _const_spec((D_MODEL, D_IN)), _const_spec((8, 128)),
                 _const_spec((256, 256)), _const_spec((256, 256))] + [_const_spec((tab_rows, 128))] * 4
    args += [mod, g, w_in, qkg, bd64, bd32] + list(tabs)
    u_shape = jax.ShapeDtypeStruct((n_all, D_IN), BF16)
    u_spec = pl.BlockSpec((tm, D_IN), lambda i: (i, 0))
    if has_ffn:
        out_shape = (jax.ShapeDtypeStruct((n_all, D_MODEL), F32), u_shape)
        out_specs = (tile, u_spec)
    else:
        out_shape, out_specs = u_shape, u_spec
    kern = functools.partial(_inproj_kernel, has_ffn=has_ffn, tm=tm, n_lat_tiles=n_lat_tiles,
                             tiles_per_batch=tpb, seq=seq)
    res = pl.pallas_call(
        kern, out_shape=out_shape, grid=(n_all // tm,), in_specs=in_specs, out_specs=out_specs,
        scratch_shapes=scratch, compiler_params=_cparams("arbitrary"), name="inproj",
    )(*args)
    return res if has_ffn else (x, res)


def _softmax_parts(scores, extra=None):
    m = scores[0].max(axis=-1, keepdims=True)
    for s in scores[1:]:
        m = jnp.maximum(m, s.max(axis=-1, keepdims=True))
    if extra is not None:
        m = jnp.maximum(m, extra)
    ps = [jnp.exp2(s - m) for s in scores]
    l = ps[0].sum(axis=-1, keepdims=True)
    for p in ps[1:]:
        l = l + p.sum(axis=-1, keepdims=True)
    if extra is not None:
        l = l + jnp.exp2(extra - m)
    return ps, 1.0 / l


class _AttItem:
    def __init__(self, q, kt_view, v_view, ranges, bias=None, extra=None):
        self.q, self.kt_view, self.v_view, self.ranges, self.bias, self.extra = q, kt_view, v_view, ranges, bias, extra
        self.n_keys = sum(n for _, n in ranges)
        self.m = None

    def scores(self, s_scr, rows):
        q_blk = self.q()
        m, off, ti = None, 0, 0
        for start, n in self.ranges:
            for o in range(0, n, KEY_TILE):
                w = min(KEY_TILE, n - o)
                s = _dot(q_blk, self.kt_view[:, pl.ds(start + o, w)])
                b = self.bias(ti) if self.bias is not None else None
                if b is not None:
                    s = s + b
                ti += 1
                s_scr[rows, off + o:off + o + w] = s
                tile_max = s.max(axis=-1, keepdims=True)
                m = tile_max if m is None else jnp.maximum(m, tile_max)
            off += n
        self.m = m if self.extra is None else jnp.maximum(m, self.extra)

    def probs(self, s_scr, p_scr, rows):
        for o in range(0, self.n_keys, KEY_TILE):
            w = min(KEY_TILE, self.n_keys - o)
            p_scr[rows, o:o + w] = jnp.exp2(s_scr[rows, o:o + w] - self.m).astype(BF16)

    def values(self, p_scr, rows):
        pv, off = None, 0
        for start, n in self.ranges:
            c = _dot(p_scr[rows, off:off + n], self.v_view[pl.ds(start, n), :])
            pv = c if pv is None else pv + c
            off += n
        denom = pv[:, 64:65]
        if self.extra is not None:
            denom = denom + jnp.exp2(self.extra - self.m)
        return pv[:, 0:64] / denom


def _run_att_items(items, s_scr, p_scr):
    def rows(t):
        r0 = (t % ATT_SLOTS) * ATT_RB
        return slice(r0, r0 + ATT_RB)

    outs = []
    for t in range(len(items) + 2):
        if t < len(items):
            items[t].scores(s_scr, rows(t))
        if 0 <= t - 1 < len(items):
            items[t - 1].probs(s_scr, p_scr, rows(t - 1))
        if 0 <= t - 2 < len(items):
            outs.append(items[t - 2].values(p_scr, rows(t - 2)))
    return outs


def _stage_heads(k_parts, v_parts, kt_scr, v_scr):
    eye = jnp.where(lax.broadcasted_iota(I32, (HEAD_DIM, HEAD_DIM), 0)
                    == lax.broadcasted_iota(I32, (HEAD_DIM, HEAD_DIM), 1), 1.0, 0.0).astype(BF16)
    for h in range(kt_scr.shape[0]):
        r0 = 0
        for (kr, kc0), (vr, vc0) in zip(k_parts, v_parts):
            n = kr.shape[0]
            kt_scr[h, :, r0:r0 + n] = _nt_dot(eye, kr[:, kc0 + 64 * h:kc0 + 64 * h + 64]).astype(BF16)
            v_scr[h, r0:r0 + n, 0:64] = vr[:, vc0 + 64 * h:vc0 + 64 * h + 64]
            r0 += n
        v_scr[h, :, 64:128] = jnp.ones((v_scr.shape[1], 64), BF16)


def _group_rms(y):
    return y * lax.rsqrt(jnp.mean(y * y, axis=-1, keepdims=True) + EPS)


def _gqa_pair(q_ref, j):
    return jnp.concatenate([q_ref[:, 128 * j:128 * j + 64], q_ref[:, 128 * j + 64:128 * j + 128]], axis=0)


def _stage_once_per_batch_row(k_parts, v_parts, k_scr, v_scr):
    @pl.when(pl.program_id(1) == 0)
    def _():
        _stage_heads(k_parts, v_parts, k_scr, v_scr)


def _row_blocks(n):
    return range(0, n, ATT_RB)


def _q_thunk(q_ref, r0, head):
    return lambda: q_ref[r0:r0 + ATT_RB, 64 * head:64 * head + 64]


def _heads_to_rows(outs, n_heads):
    per = len(outs) // n_heads
    return jnp.concatenate([jnp.concatenate(outs[h * per:(h + 1) * per], axis=0) for h in range(n_heads)], axis=1)


def _dense_kernel(q_ref, kv_ref, kvc_ref, o_ref, kt_scr, v_scr, s_scr, p_scr, *, tq):
    _stage_once_per_batch_row([(kv_ref, 0), (kvc_ref, 0)], [(kv_ref, 128), (kvc_ref, 128)], kt_scr, v_scr)
    nk = v_scr.shape[1]
    items = [_AttItem(_q_thunk(q_ref, r0, hq), kt_scr.at[hq // 2], v_scr.at[hq // 2], [(0, nk)])
             for hq in range(HEADS) for r0 in _row_blocks(tq)]
    y = _heads_to_rows(_run_att_items(items, s_scr, p_scr), HEADS)
    o_ref[...] = _group_rms(y).astype(BF16)


def _window_kernel(sink_ref, q_ref, kv_ref, kvc_ref, mask_ref, o_ref, kt_scr, v_scr, s_scr, p_scr, *, tq, seq):
    _stage_once_per_batch_row([(kv_ref, 0), (kvc_ref, 0)], [(kv_ref, 128), (kvc_ref, 128)], kt_scr, v_scr)
    kw = tq + 2 * WINDOW
    nc = v_scr.shape[1] - seq
    ks = pl.multiple_of(jnp.clip(pl.program_id(1) * tq - WINDOW, 0, seq - kw), 128)

    def mask_thunk(r0):
        return lambda ti: (mask_ref[0, r0:r0 + ATT_RB, KEY_TILE * ti:KEY_TILE * ti + KEY_TILE]
                           if KEY_TILE * ti < kw else None)

    items = [_AttItem(_q_thunk(q_ref, r0, hq), kt_scr.at[hq // 2], v_scr.at[hq // 2], [(ks, kw), (seq, nc)],
                      bias=mask_thunk(r0), extra=sink_ref[hq] * LOG2E)
             for hq in range(HEADS) for r0 in _row_blocks(tq)]
    y = _heads_to_rows(_run_att_items(items, s_scr, p_scr), HEADS)
    o_ref[...] = _group_rms(y).astype(BF16)


def _neigh_kernel(q_ref, k_ref, v_ref, kc_ref, vc_ref, bias_ref, o_ref, kt_scr, v_scr, s_scr, p_scr, *, rows, seq):
    _stage_once_per_batch_row([(k_ref, 0), (kc_ref, 0)], [(v_ref, 0), (vc_ref, 0)], kt_scr, v_scr)
    rs = jnp.clip(NA_QROWS * pl.program_id(1) - NA_ROWS // 2, 0, rows - NA_KROWS)
    k0 = pl.multiple_of(rs * GRID_W, KEY_TILE)
    nk = NA_KROWS * GRID_W
    nc = v_scr.shape[1] - seq

    def bias_thunk(h, r0):
        return lambda ti: (bias_ref[0, h, r0:r0 + ATT_RB, KEY_TILE * ti:KEY_TILE * ti + KEY_TILE]
                           if KEY_TILE * ti < nk else None)

    items = [_AttItem(_q_thunk(q_ref, r0, h), kt_scr.at[h], v_scr.at[h], [(k0, nk), (seq, nc)], bias=bias_thunk(h, r0))
             for h in range(HEADS) for r0 in _row_blocks(q_ref.shape[0])]
    y = _heads_to_rows(_run_att_items(items, s_scr, p_scr), HEADS)
    o_ref[...] = _group_rms(y).astype(BF16)


def _diff_lambda(lam_ref, lam_init):
    lp = lam_ref[...]
    a = jnp.sum(lp[0:1] * lp[1:2], axis=-1, keepdims=True)
    b = jnp.sum(lp[2:3] * lp[3:4], axis=-1, keepdims=True)
    return jnp.exp(a) - jnp.exp(b) + lam_init


def _diff_heads(q, k_parts, v_parts, lam, lam_init):
    t = q.shape[0]
    first = lax.broadcasted_iota(I32, (t, HEAD_DIM), 1) < DIFF_DIM
    zero = jnp.zeros((t, HEAD_DIM), BF16)
    outs = []
    for h in range(HEADS):
        sl = slice(64 * h, 64 * h + 64)
        qh = q[:, sl]
        ks = [kp[:, sl] for kp in k_parts]
        maps = []
        for half in range(2):
            qm = jnp.where(first, qh, zero) if half == 0 else jnp.where(first, zero, qh)
            ps, inv = _softmax_parts([_nt_dot(qm, kk) for kk in ks])
            maps.append([p * inv for p in ps])
        o = None
        for a0, a1, vp in zip(maps[0], maps[1], v_parts):
            c = _dot((a0 - lam * a1).astype(BF16), vp[:, sl])
            o = c if o is None else o + c
        outs.append(_group_rms(o) * (1.0 - lam_init))
    return jnp.concatenate(outs, axis=1)


def _diff_kernel(lam_ref, q_ref, k_ref, v_ref, kc_ref, vc_ref, o_ref, kt_scr, v_scr, s_scr, p_scr, *, lam_init, tq):
    _stage_once_per_batch_row([(k_ref, 0), (kc_ref, 0)], [(v_ref, 0), (vc_ref, 0)], kt_scr, v_scr)
    lam = _diff_lambda(lam_ref, lam_init)
    nk = v_scr.shape[1]
    first = lax.broadcasted_iota(I32, (ATT_RB, HEAD_DIM), 1) < DIFF_DIM
    zero = jnp.zeros((ATT_RB, HEAD_DIM), BF16)

    def q_map(r0, h, half):
        def thunk():
            qh = q_ref[r0:r0 + ATT_RB, 64 * h:64 * h + 64]
            return jnp.where(first, qh, zero) if half == 0 else jnp.where(first, zero, qh)
        return thunk

    items = [_AttItem(q_map(r0, h, half), kt_scr.at[h], v_scr.at[h], [(0, nk)])
             for h in range(HEADS) for r0 in _row_blocks(tq) for half in range(2)]
    outs = _run_att_items(items, s_scr, p_scr)
    diffs = [outs[i] - lam * outs[i + 1] for i in range(0, len(outs), 2)]
    per = len(diffs) // HEADS
    heads = [_group_rms(jnp.concatenate(diffs[h * per:(h + 1) * per], axis=0)) * (1.0 - lam_init)
             for h in range(HEADS)]
    o_ref[...] = jnp.concatenate(heads, axis=1).astype(BF16)


class _Cols:
    def __init__(self, ref, blk):
        self.ref, self.base = ref, 256 * blk

    def __getitem__(self, idx):
        rows, sl = idx
        return self.ref[rows, self.base + sl.start:self.base + sl.stop]


def _ctx_kernel(sink_ref, lam_ref, u_ref, o_ref, *, lam_init):
    L = u_ref.shape[0]
    top = lax.broadcasted_iota(I32, (2 * L, 1), 0) < L
    for q_blk, kv_blk, out_col, use_sink in ((COL_A_Q, COL_A_KV, 0, True), (COL_D_Q, COL_D_KV, 768, False)):
        qv, kv = _Cols(u_ref, q_blk), _Cols(u_ref, kv_blk)
        outs = [None] * HEADS
        for j in range(2):
            kc = kv[:, slice(64 * j, 64 * j + 64)]
            vc = kv[:, slice(128 + 64 * j, 128 + 64 * j + 64)]
            q2 = jnp.concatenate([qv[:, slice(128 * j, 128 * j + 64)], qv[:, slice(128 * j + 64, 128 * j + 128)]], axis=0)
            snk = jnp.where(top, sink_ref[2 * j], sink_ref[2 * j + 1]) * LOG2E if use_sink else None
            (p,), inv = _softmax_parts([_nt_dot(q2, kc)], extra=snk)
            o = _dot(p.astype(BF16), vc) * inv
            outs[2 * j], outs[2 * j + 1] = o[:L], o[L:]
        o_ref[:, out_col:out_col + 256] = _group_rms(jnp.concatenate(outs, axis=1)).astype(BF16)
    qv, kv, vv = _Cols(u_ref, COL_B_Q), _Cols(u_ref, COL_B_K), _Cols(u_ref, COL_B_V)
    outs = []
    for h in range(HEADS):
        sl = slice(64 * h, 64 * h + 64)
        (p,), inv = _softmax_parts([_nt_dot(qv[:, sl], kv[:, sl])])
        outs.append(_dot(p.astype(BF16), vv[:, sl]) * inv)
    o_ref[:, 256:512] = _group_rms(jnp.concatenate(outs, axis=1)).astype(BF16)
    lam = _diff_lambda(lam_ref, lam_init)
    qc = u_ref[:, 256 * COL_C_Q:256 * COL_C_Q + 256]
    o_ref[:, 512:768] = _diff_heads(qc, [_Cols(u_ref, COL_C_K)], [_Cols(u_ref, COL_C_V)], lam, lam_init).astype(BF16)


def _attention_calls(u, sink, lam_p, bias, wmask, *, layer, batch, seq, ctx_len, need_ctx):
    lam_init = 0.8 - 0.6 * math.exp(-0.3 * layer)
    n_all = u.shape[0]
    tq = ATT_TQ
    nq = seq // tq
    nk_all = seq + ctx_len
    cb0 = batch * seq // ctx_len
    y_shape = jax.ShapeDtypeStruct((n_all, D_MODEL), BF16)
    smem = pl.BlockSpec(memory_space=pltpu.SMEM)
    any_spec = pl.BlockSpec(memory_space=pl.ANY)
    lam_spec2 = pl.BlockSpec((4, DIFF_DIM), lambda b, i: (0, 0))
    cp = _cparams("arbitrary", "arbitrary")

    def scratch(n_heads, n_keys):
        return [pltpu.VMEM((n_heads, HEAD_DIM, nk_all), BF16), pltpu.VMEM((n_heads, nk_all, 128), BF16),
                pltpu.VMEM((ATT_SLOTS * ATT_RB, n_keys), F32), pltpu.VMEM((ATT_SLOTS * ATT_RB, n_keys), BF16)]

    def qspec(col):
        return pl.BlockSpec((tq, 256), lambda b, i: (b * nq + i, col))

    def latspec(col):
        return pl.BlockSpec((seq, 256), lambda b, i: (b, col))

    def ctxspec(col):
        return pl.BlockSpec((ctx_len, 256), lambda b, i: (cb0 + b, col))

    kw = tq + 2 * WINDOW

    def first_mid_last(n):
        return lambda b, i: (jnp.where(i == 0, 0, jnp.where(i == n - 1, 2, 1)),)

    wcase = first_mid_last(nq)
    y = pl.pallas_call(
        functools.partial(_window_kernel, tq=tq, seq=seq),
        out_shape=y_shape, grid=(batch, nq),
        in_specs=[smem, qspec(COL_A_Q), latspec(COL_A_KV), ctxspec(COL_A_KV),
                  pl.BlockSpec((1, tq, kw), lambda b, i: wcase(b, i) + (0, 0))],
        out_specs=qspec(0), scratch_shapes=scratch(2, kw + ctx_len),
        compiler_params=cp, name="mix_window",
    )(sink, u, u, u, wmask)

    rows = seq // GRID_W
    nrb = rows // NA_QROWS
    qrows = NA_QROWS * GRID_W
    nkb = NA_KROWS * GRID_W
    bcase = first_mid_last(nrb)
    y = pl.pallas_call(
        _drop_arg(functools.partial(_neigh_kernel, rows=rows, seq=seq), 6),
        out_shape=y_shape, grid=(batch, nrb),
        in_specs=[pl.BlockSpec((qrows, 256), lambda b, i: (b * nrb + i, COL_B_Q)),
                  latspec(COL_B_K), latspec(COL_B_V), ctxspec(COL_B_K), ctxspec(COL_B_V),
                  pl.BlockSpec((1, HEADS, qrows, nkb), lambda b, i: bcase(b, i) + (0, 0, 0)), any_spec],
        out_specs=pl.BlockSpec((qrows, 256), lambda b, i: (b * nrb + i, 1)),
        scratch_shapes=scratch(HEADS, nkb + ctx_len),
        input_output_aliases={6: 0}, compiler_params=cp, name="mix_neigh",
    )(u, u, u, u, u, bias, y)

    y = pl.pallas_call(
        _drop_arg(functools.partial(_diff_kernel, lam_init=lam_init, tq=tq), 6),
        out_shape=y_shape, grid=(batch, nq),
        in_specs=[lam_spec2, qspec(COL_C_Q), latspec(COL_C_K), latspec(COL_C_V),
                  ctxspec(COL_C_K), ctxspec(COL_C_V), any_spec],
        out_specs=qspec(2), scratch_shapes=scratch(HEADS, nk_all),
        input_output_aliases={6: 0}, compiler_params=cp, name="mix_diff",
    )(lam_p, u, u, u, u, u, y)

    y = pl.pallas_call(
        _drop_arg(functools.partial(_dense_kernel, tq=tq), 3),
        out_shape=y_shape, grid=(batch, nq),
        in_specs=[qspec(COL_D_Q), latspec(COL_D_KV), ctxspec(COL_D_KV), any_spec],
        out_specs=qspec(3), scratch_shapes=scratch(2, nk_all),
        input_output_aliases={3: 0}, compiler_params=cp, name="mix_dense",
    )(u, u, u, y)

    if need_ctx:
        y = pl.pallas_call(
            _drop_arg(functools.partial(_ctx_kernel, lam_init=lam_init), 3),
            out_shape=y_shape, grid=(batch,),
            in_specs=[smem, pl.BlockSpec((4, DIFF_DIM), lambda b: (0, 0)),
                      pl.BlockSpec((ctx_len, D_IN), lambda b: (cb0 + b, 0)), any_spec],
            out_specs=pl.BlockSpec((ctx_len, D_MODEL), lambda b: (cb0 + b, 0)),
            input_output_aliases={3: 0}, compiler_params=_cparams("arbitrary"), name="mix_ctx",
        )(sink, lam_p, u, y)
    return y


def _select4(idx, vals):
    return jnp.where(idx == 0, vals[0], jnp.where(idx == 1, vals[1], jnp.where(idx == 2, vals[2], vals[3])))


def _outproj_kernel(y_ref, x_ref, mod_ref, gain_ref, w_ref, g_ref, rw_ref, rb_ref, xo_ref, h_ref, cls_ref, *, tm):
    hm = tm // 2

    def project(r0):
        yg = (y_ref[r0:r0 + hm, :].astype(F32) * gain_ref[...]).astype(BF16)
        return _dot(yg, w_ref[...])

    def normalise(r0, proj):
        xm = x_ref[r0:r0 + hm, :] + mod_ref[0, 2:3, :] * proj
        xo_ref[r0:r0 + hm, :] = xm
        hn = xm * lax.rsqrt(jnp.mean(xm * xm, axis=-1, keepdims=True) + EPS) * g_ref[...]
        h2 = hn * (1.0 + mod_ref[0, 4:5, :]) + mod_ref[0, 3:4, :]
        h_ref[r0:r0 + hm, 0:D_MODEL] = h2
        hi = h2.astype(BF16)
        lo = (h2 - hi.astype(F32)).astype(BF16)
        return (_dot(hi, rw_ref[...]) + _dot(lo, rw_ref[...])).T

    def route(r0, lt):
        score = _sigmoid(lt[0:N_EXPERTS] + lt[N_EXPERTS:2 * N_EXPERTS])
        sel = score + rb_ref[...]
        srow = [sel[e:e + 1] for e in range(N_EXPERTS)]
        crow = [score[e:e + 1] for e in range(N_EXPERTS)]

        gidx = jnp.zeros((1, hm), I32)
        gbest = None
        for g in range(N_EXPERTS // EXPERTS_PER_GROUP):
            v = srow[4 * g:4 * g + 4]
            best = None
            for a in range(4):
                for b in range(a + 1, 4):
                    t = v[a] + v[b]
                    best = t if best is None else jnp.maximum(best, t)
            if g == 0:
                gbest = best
            else:
                better = best > gbest
                gidx = jnp.where(better, g, gidx)
                gbest = jnp.where(better, best, gbest)

        iv = [_select4(gidx, [srow[4 * g + i] for g in range(4)]) for i in range(4)]
        sv = [_select4(gidx, [crow[4 * g + i] for g in range(4)]) for i in range(4)]
        chosen = []
        for i in range(4):
            rank = jnp.zeros((1, hm), I32)
            for j in range(4):
                if j != i:
                    beats = (iv[j] >= iv[i]) if j < i else (iv[j] > iv[i])
                    rank = rank + jnp.where(beats, 1, 0)
            chosen.append(rank < 2)
        lo_i = jnp.where(chosen[0], 0, jnp.where(chosen[1], 1, 2))
        hi_i = jnp.where(chosen[3], 3, jnp.where(chosen[2], 2, 1))
        pair = jnp.where(lo_i == 0, hi_i - 1, jnp.where(lo_i == 1, hi_i + 1, 5))
        cls_ref[0, :, r0:r0 + hm] = gidx * N_PAIRS + pair
        s_lo = jnp.where(lo_i == 0, sv[0], jnp.where(lo_i == 1, sv[1], sv[2]))
        s_hi = jnp.where(hi_i == 3, sv[3], jnp.where(hi_i == 2, sv[2], sv[1]))
        inv = 1.0 / (s_lo + s_hi)
        rowi = lax.broadcasted_iota(I32, (128, hm), 0)
        wts = jnp.where(rowi == 0, s_lo * inv, jnp.where(rowi == 1, s_hi * inv, 0.0))
        h_ref[r0:r0 + hm, D_MODEL:EXT_W] = wts.T

    proj0 = project(0)
    proj1 = project(hm)
    lt0 = normalise(0, proj0)
    lt1 = normalise(hm, proj1)
    route(0, lt0)
    route(hm, lt1)


def _outproj_call(y, x, mod, gain, w_out, g, rw, rb, *, n_proc, batch, seq):
    tm = TOKEN_TILE
    n_tiles = n_proc // tm
    tile = pl.BlockSpec((tm, D_MODEL), lambda i: (i, 0))
    return pl.pallas_call(
        functools.partial(_outproj_kernel, tm=tm),
        out_shape=(jax.ShapeDtypeStruct((n_proc, D_MODEL), F32),
                   jax.ShapeDtypeStruct((n_proc, EXT_W), F32),
                   jax.ShapeDtypeStruct((n_tiles, 1, tm), I32)),
        grid=(n_tiles,),
        in_specs=[tile, tile, _mod_spec(batch * seq // tm, seq // tm, batch), _const_spec((1, D_MODEL)),
                  _const_spec((D_MODEL, D_MODEL)), _const_spec((1, D_MODEL)), _const_spec((D_MODEL, 128)),
                  _const_spec((N_EXPERTS, 1))],
        out_specs=(tile, pl.BlockSpec((tm, EXT_W), lambda i: (i, 0)), pl.BlockSpec((1, 1, tm), lambda i: (i, 0, 0))),
        compiler_params=_cparams("arbitrary"), name="outproj_router",
    )(y, x, mod, gain, w_out, g, rw, rb)


def _rank_kernel(cls_ref, pos_ref, bcls_ref, nval_ref, *, blk):
    c = cls_ref[...]
    nr = c.shape[0]
    nbp = bcls_ref.shape[1]
    upper = (lax.broadcasted_iota(I32, (128, 128), 0) <= lax.broadcasted_iota(I32, (128, 128), 1))
    upper = jnp.where(upper, 1.0, 0.0).astype(BF16)
    lower = (lax.broadcasted_iota(I32, (nr, nr), 1) < lax.broadcasted_iota(I32, (nr, nr), 0))
    lower = jnp.where(lower, 1.0, 0.0).astype(BF16)
    bidx = lax.broadcasted_iota(I32, (1, nbp), 1).astype(F32)
    pos = jnp.zeros((nr, 128), F32)
    bcls = jnp.zeros((1, nbp), F32)
    nval = jnp.zeros((1, nbp), F32)
    start = jnp.zeros((1, 1), F32)
    for k in range(N_CLASSES):
        m = c == k
        incl = _dot(jnp.where(m, 1.0, 0.0).astype(BF16), upper)
        tot = incl[:, 127:128]
        above = _dot(lower, jnp.broadcast_to(tot, (nr, 128)).astype(BF16))
        cnt = above[nr - 1:nr, 0:1] + tot[nr - 1:nr, 0:1]
        nblk = jnp.floor((cnt + (blk - 1)) * (1.0 / blk))
        pos = jnp.where(m, start * blk + above + incl - 1.0, pos)
        end = start + nblk
        bcls = bcls + jnp.where(bidx >= end, 1.0, 0.0)
        nval = nval + jnp.where((bidx >= start) & (bidx < end), jnp.clip(cnt - (bidx - start) * blk, 0.0, blk), 0.0)
        start = end
    pos_ref[...] = pos.astype(I32)
    bcls_ref[...] = jnp.minimum(bcls, N_CLASSES - 1.0).astype(I32)
    nval_ref[...] = nval.astype(I32)


def _rank_call(cls2d, n_blocks):
    nr = cls2d.shape[0]
    nbp = -(-n_blocks // 128) * 128
    return pl.pallas_call(
        functools.partial(_rank_kernel, blk=MOE_BLK),
        out_shape=(jax.ShapeDtypeStruct((nr, 128), I32), jax.ShapeDtypeStruct((1, nbp), I32),
                   jax.ShapeDtypeStruct((1, nbp), I32)),
        grid=(1,),
        in_specs=[_const_spec((nr, 128))],
        out_specs=(_const_spec((nr, 128)), _const_spec((1, nbp)), _const_spec((1, nbp))),
        compiler_params=_cparams("arbitrary"), name="class_rank",
    )(cls2d)


def _scatter_kernel(pos_ref, h_ref, xs_ref, sem):
    n = h_ref.shape[0]
    for r in range(n):
        pltpu.make_async_copy(h_ref.at[pl.ds(r, 1)], xs_ref.at[pl.ds(pos_ref[0, 0, r], 1)], sem).start()
    pltpu.make_async_copy(h_ref, xs_ref.at[pl.ds(0, n)], sem).wait()


def _scatter_call(h_ext, pos3, n_slots):
    n, w = h_ext.shape
    r = ROW_DMA
    return pl.pallas_call(
        _scatter_kernel,
        out_shape=jax.ShapeDtypeStruct((n_slots, w), h_ext.dtype),
        grid=(n // r,),
        in_specs=[pl.BlockSpec((1, 1, r), lambda i: (i, 0, 0), memory_space=pltpu.SMEM),
                  pl.BlockSpec((r, w), lambda i: (i, 0))],
        out_specs=pl.BlockSpec(memory_space=pl.ANY),
        scratch_shapes=[pltpu.SemaphoreType.DMA(())],
        compiler_params=_cparams("arbitrary"), name="row_scatter",
    )(pos3, h_ext)


def _moe_kernel(elo_ref, ehi_ref, nval_ref, xs_ref, wg_lo, wu_lo, wd_lo, wg_hi, wu_hi, wd_hi, ys_ref):
    del elo_ref, ehi_ref
    nv = nval_ref[pl.program_id(0)]

    @pl.when(nv > 0)
    def _():
        live = lax.broadcasted_iota(I32, (MOE_BLK, 1), 0) < nv
        x = jnp.where(live, xs_ref[:, 0:D_MODEL], 0.0).astype(BF16)
        acts = []
        for wg, wu, col in ((wg_lo, wu_lo, D_MODEL), (wg_hi, wu_hi, D_MODEL + 1)):
            w = jnp.where(live, xs_ref[:, col:col + 1], 0.0)
            g = _dot(x, wg[0])
            acts.append((g * _sigmoid(g) * _dot(x, wu[0]) * w).astype(BF16))
        ys_ref[...] = _dot(acts[0], wd_lo[0]) + _dot(acts[1], wd_hi[0])

    @pl.when(nv == 0)
    def _():
        ys_ref[...] = jnp.zeros_like(ys_ref)


def _moe_call(xs, e_lo, e_hi, nval, w_gate, w_up, w_down):
    n_blocks = xs.shape[0] // MOE_BLK

    def wspec(shape, which):
        return pl.BlockSpec((1,) + shape, lambda b, elo, ehi, nv: ((elo, ehi)[which][b], 0, 0))

    gu, dn = (D_MODEL, D_EXPERT), (D_EXPERT, D_MODEL)
    grid_spec = pltpu.PrefetchScalarGridSpec(
        num_scalar_prefetch=3, grid=(n_blocks,),
        in_specs=[pl.BlockSpec((MOE_BLK, EXT_W), lambda b, elo, ehi, nv: (b, 0)),
                  wspec(gu, 0), wspec(gu, 0), wspec(dn, 0), wspec(gu, 1), wspec(gu, 1), wspec(dn, 1)],
        out_specs=pl.BlockSpec((MOE_BLK, D_MODEL), lambda b, elo, ehi, nv: (b, 0)),
    )
    return pl.pallas_call(
        _moe_kernel, out_shape=jax.ShapeDtypeStruct((xs.shape[0], D_MODEL), F32), grid_spec=grid_spec,
        compiler_params=_cparams("arbitrary"), name="moe_experts",
    )(e_lo, e_hi, nval, xs, w_gate, w_up, w_down, w_gate, w_up, w_down)


def _final_kernel(x_ref, pos_ref, pos_next_ref, ys_ref, mod_ref, o_ref, fbuf, fsems):
    f_ref = _prefetched_rows(pos_ref, pos_next_ref, ys_ref, fbuf, fsems)
    o_ref[...] = x_ref[...] + mod_ref[0, 5:6, :] * f_ref[...]


def _final_call(x, ys, pos, mod, *, batch, seq):
    tm = TOKEN_TILE
    n = batch * seq
    tile = pl.BlockSpec((tm, D_MODEL), lambda i: (i, 0))
    pos_tiles = pos.reshape(n // tm, 1, tm)
    return pl.pallas_call(
        _final_kernel, out_shape=jax.ShapeDtypeStruct((n, D_MODEL), F32), grid=(n // tm,),
        in_specs=[tile] + _pos_tile_specs(n // tm, tm) + [pl.BlockSpec(memory_space=pl.ANY),
                                                          _mod_spec(n // tm, seq // tm, batch)],
        out_specs=tile, scratch_shapes=[pltpu.VMEM((2, tm, D_MODEL), F32), pltpu.SemaphoreType.DMA((2,))],
        compiler_params=_cparams("arbitrary"), name="final_residual",
    )(x, pos_tiles, pos_tiles, ys, mod)


def _rope_tables(seq, ident_rows):
    pos = jnp.arange(seq, dtype=jnp.int32)
    row = (pos // GRID_W).astype(F32)[:, None]
    col = (pos % GRID_W).astype(F32)[:, None]

    def tab(half):
        inv = ROPE_BASE ** (-jnp.arange(half, dtype=F32) / half)
        ar, ac = row * inv, col * inv
        cos = jnp.concatenate([jnp.cos(ar), jnp.cos(ar), jnp.cos(ac), jnp.cos(ac)], axis=1)
        sin = jnp.concatenate([-jnp.sin(ar), jnp.sin(ar), -jnp.sin(ac), jnp.sin(ac)], axis=1)
        reps = 128 // (4 * half)
        cos = jnp.concatenate([jnp.tile(cos, (1, reps)), jnp.ones((ident_rows, 128), F32)], axis=0)
        sin = jnp.concatenate([jnp.tile(sin, (1, reps)), jnp.zeros((ident_rows, 128), F32)], axis=0)
        return cos, sin

    c64, s64 = tab(HEAD_DIM // 4)
    c32, s32 = tab(DIFF_DIM // 4)
    return c64, s64, c32, s32


def _block_diag_ones(group):
    idx = np.arange(256) // group
    return jnp.asarray((idx[:, None] == idx[None, :]).astype(np.float32), dtype=BF16)


def _qk_gains(g_win, g_na, g_diff, g_gqa):
    s64, s32 = HEAD_DIM ** -0.5 * LOG2E, DIFF_DIM ** -0.5 * LOG2E
    rows = [jnp.tile(g_win[0] * s64, 2), jnp.tile(g_win[1], 2), jnp.tile(g_na[0] * s64, 2), jnp.tile(g_na[1], 2),
            jnp.tile(g_diff[0] * s32, 4), jnp.tile(g_diff[1], 4), jnp.tile(g_gqa[0] * s64, 2), jnp.tile(g_gqa[1], 2)]
    return jnp.stack(rows).astype(F32)


def _na_bias_tiles(rpb, rows):
    w = GRID_W
    qc = np.arange(w)[:, None]
    kc = np.arange(w)[None, :]
    c_start = np.clip(qc - NA_COLS // 2, 0, w - NA_COLS)
    col_ok = (kc >= c_start) & (kc < c_start + NA_COLS)
    dc = np.clip(kc - qc + NA_COLS - 1, 0, 2 * NA_COLS - 2)
    t = jnp.where(jnp.asarray(col_ok), rpb.astype(F32)[:, :, dc] * LOG2E, NEG)
    neg = jnp.full((rpb.shape[0], w, w), NEG, F32)
    cases = []
    for r0, rs in ((0, 0), (NA_QROWS, 0), (rows - NA_QROWS, rows - NA_KROWS)):
        qtiles = []
        for i in range(NA_QROWS):
            qrow = r0 + i
            r_start = min(max(qrow - NA_ROWS // 2, 0), rows - NA_ROWS)
            blks = []
            for j in range(NA_KROWS):
                krow = rs + j
                blks.append(t[:, krow - qrow + NA_ROWS - 1] if r_start <= krow < r_start + NA_ROWS else neg)
            qtiles.append(jnp.concatenate(blks, axis=2))
        cases.append(jnp.concatenate(qtiles, axis=1))
    return jnp.stack(cases)


def _window_mask_tiles(tq):
    kw = tq + 2 * WINDOW
    r = lax.broadcasted_iota(I32, (3, tq, kw), 1)
    c = lax.broadcasted_iota(I32, (3, tq, kw), 2)
    shift = lax.broadcasted_iota(I32, (3, tq, kw), 0) * WINDOW
    return jnp.where(jnp.abs(c - r - shift) <= WINDOW, 0.0, NEG).astype(F32)


def _router_weights(router_w):
    hi = router_w.astype(BF16)
    lo = (router_w - hi.astype(F32)).astype(BF16)
    pad = jnp.zeros((router_w.shape[0], 128 - 2 * N_EXPERTS), BF16)
    return jnp.concatenate([hi, lo, pad], axis=1)


def kernel(x, c, ctx, c_ctx, ada_w, ada_b, norm_mix_g, norm_ffn_g, w_in, qk_g_win, qk_g_na, qk_g_diff, qk_g_gqa,
           sink_win, rpb_na, lambda_diff, out_gain, w_out, router_w, router_b, w_gate, w_up, w_down):
    batch, seq, d = x.shape
    ctx_len = ctx.shape[1]
    depth = w_in.shape[0]
    n_lat, n_ctx = batch * seq, batch * ctx_len
    n_all = n_lat + n_ctx
    rows = seq // GRID_W
    assert d == D_MODEL and seq % TOKEN_TILE == 0 and n_ctx % TOKEN_TILE == 0 and ctx_len % 128 == 0
    assert rows % NA_QROWS == 0 and rows >= NA_KROWS + NA_QROWS and seq >= ATT_TQ + 2 * WINDOW
    assert ctx_len % 128 == 0 and (NA_KROWS * GRID_W) % KEY_TILE == 0

    pad_rows = -(-(batch + 1) // 8) * 8
    s_all = jnp.concatenate([c, c_ctx[None, :], jnp.zeros((pad_rows - batch - 1, d), F32)], axis=0)
    mods = _ada_call(s_all, ada_w, ada_b).reshape(depth, pad_rows, ADA_CHUNKS, d)

    xs_all = jnp.concatenate([x.reshape(n_lat, d), ctx.reshape(n_ctx, d)], axis=0)
    tabs = _rope_tables(seq, TOKEN_TILE)
    wmask = _window_mask_tiles(ATT_TQ)
    bd64, bd32 = _block_diag_ones(HEAD_DIM), _block_diag_ones(DIFF_DIM)
    rw = _router_weights(router_w)
    rb = router_b.astype(F32).reshape(N_EXPERTS, 1)
    lo_tab = jnp.asarray([EXPERTS_PER_GROUP * (k // N_PAIRS) + PAIR_LO[k % N_PAIRS] for k in range(N_CLASSES)], I32)
    hi_tab = jnp.asarray([EXPERTS_PER_GROUP * (k // N_PAIRS) + PAIR_HI[k % N_PAIRS] for k in range(N_CLASSES)], I32)

    ffn, mod_prev = None, None
    for layer in range(depth):
        need_ctx = layer < depth - 1
        mod = mods[layer]
        qkg = _qk_gains(qk_g_win[layer], qk_g_na[layer], qk_g_diff[layer], qk_g_gqa[layer])
        xs_all, u = _inproj_call(xs_all, ffn, mod_prev, mod, norm_mix_g[layer].reshape(1, d),
                                 w_in[layer].astype(BF16), qkg, bd64, bd32, tabs, batch=batch, seq=seq)
        y = _attention_calls(u, sink_win[layer].astype(F32), lambda_diff[layer].astype(F32),
                             _na_bias_tiles(rpb_na[layer], rows), wmask, layer=layer, batch=batch, seq=seq,
                             ctx_len=ctx_len, need_ctx=need_ctx)
        n_proc = n_all if need_ctx else n_lat
        x_mid, h_ext, cls = _outproj_call(y, xs_all, mod, out_gain[layer].reshape(1, d), w_out[layer].astype(BF16),
                                          norm_ffn_g[layer].reshape(1, d), rw, rb, n_proc=n_proc, batch=batch, seq=seq)
        n_blocks = n_proc // MOE_BLK + N_CLASSES
        pos, bcls, nval = _rank_call(cls.reshape(n_proc // 128, 128), n_blocks)
        pos3 = pos.reshape(n_proc // ROW_DMA, 1, ROW_DMA)
        bcls, nval = bcls[0, :n_blocks], nval[0, :n_blocks]
        sorted_rows = _scatter_call(h_ext, pos3, n_blocks * MOE_BLK)
        ys = _moe_call(sorted_rows, lo_tab[bcls], hi_tab[bcls], nval, w_gate[layer].astype(BF16),
                       w_up[layer].astype(BF16), w_down[layer].astype(BF16))
        ffn = (ys, pos)
        xs_all, mod_prev = x_mid, mod
    out = _final_call(xs_all, ffn[0], ffn[1], mod_prev, batch=batch, seq=seq)
    return out.reshape(batch, seq, d)
```

```python
import functools
import math

import numpy as np
import jax
import jax.numpy as jnp
from jax import lax
from jax.experimental import pallas as pl
from jax.experimental.pallas import tpu as pltpu

F32 = jnp.float32
BF16 = jnp.bfloat16
I32 = jnp.int32

D_MODEL = 1024
GRID_W = 64
HEAD_DIM = 64
HEADS = 4
DIFF_DIM = 32
WINDOW = 128
NA_ROWS = 8
NA_COLS = 16
ROPE_BASE = 10000.0
N_EXPERTS = 16
EXPERTS_PER_GROUP = 4
D_EXPERT = 512
ADA_CHUNKS = 6
EPS = 1e-6
NEG = -1e30
D_IN = 2560

COL_A_Q, COL_A_KV, COL_B_Q, COL_B_K, COL_B_V, COL_C_Q, COL_C_K, COL_C_V, COL_D_Q, COL_D_KV = range(10)

TOKEN_TILE = 512
ATT_TQ = 256
ATT_TQ_FULL = 512
NA_QROWS = 4
NA_KROWS = 12
KEY_TILE = 256
ATT_RB = 128
ATT_SLOTS = 4
LOG2E = math.log2(math.e)
MOE_BLK = 256
ROW_DMA = 256
N_PAIRS = 6
N_CLASSES = (N_EXPERTS // EXPERTS_PER_GROUP) * N_PAIRS
PAIR_LO = (0, 0, 0, 1, 1, 2)
PAIR_HI = (1, 2, 3, 2, 3, 3)
EXT_W = D_MODEL + 128
VMEM_LIMIT = 56 * 1024 * 1024


def _cparams(*sem):
    return pltpu.CompilerParams(dimension_semantics=sem, vmem_limit_bytes=VMEM_LIMIT)


def _nt_dot(a, b):
    return lax.dot_general(a, b, (((1,), (1,)), ((), ())), preferred_element_type=F32)


def _dot(a, b):
    return jnp.dot(a, b, preferred_element_type=F32)


def _sigmoid(x):
    return 1.0 / (1.0 + jnp.exp(-x))


def _drop_arg(kern, pos):
    def wrapped(*refs):
        return kern(*refs[:pos], *refs[pos + 1:])
    return wrapped


def _ada_kernel(s_ref, w_ref, b_ref, o_ref):
    s = s_ref[...]
    act = s * _sigmoid(s)
    o_ref[0] = jnp.dot(act, w_ref[0], precision=lax.Precision.HIGHEST,
                       preferred_element_type=F32) + b_ref[0]


def _ada_call(s_all, ada_w, ada_b):
    depth, d, n = ada_w.shape
    rows = s_all.shape[0]
    tn = 1536
    return pl.pallas_call(
        _ada_kernel,
        out_shape=jax.ShapeDtypeStruct((depth, rows, n), F32),
        grid=(depth, n // tn),
        in_specs=[
            pl.BlockSpec((rows, d), lambda l, j: (0, 0)),
            pl.BlockSpec((1, d, tn), lambda l, j: (l, 0, j)),
            pl.BlockSpec((1, 1, tn), lambda l, j: (l, 0, j)),
        ],
        out_specs=pl.BlockSpec((1, rows, tn), lambda l, j: (l, 0, j)),
        compiler_params=_cparams("arbitrary", "arbitrary"),
        name="ada_mod",
    )(s_all, ada_w, ada_b.reshape(depth, 1, n))


def _start_row_gather(pos_ref, src_ref, dst_ref, sem, lo=0, hi=None):
    for r in range(lo, dst_ref.shape[0] if hi is None else hi):
        pltpu.make_async_copy(src_ref.at[pl.ds(pos_ref[0, 0, r], 1)], dst_ref.at[pl.ds(r, 1)], sem).start()


def _wait_row_gather(src_ref, dst_ref, sem):
    pltpu.make_async_copy(src_ref.at[pl.ds(0, dst_ref.shape[0])], dst_ref, sem).wait()


class _RowPrefetch:
    def __init__(self, pos_ref, pos_next_ref, src_ref, buf, sems):
        self.i = pl.program_id(0)
        slot = self.i % 2
        self.pos_next_ref, self.src_ref = pos_next_ref, src_ref
        self.nxt, self.nxt_sem = buf.at[1 - slot], sems.at[1 - slot]

        @pl.when(self.i == 0)
        def _():
            _start_row_gather(pos_ref, src_ref, buf.at[0], sems.at[0])

        _wait_row_gather(src_ref, buf.at[slot], sems.at[slot])
        self.cur = buf.at[slot]

    def request(self, lo, hi):
        _start_row_gather(self.pos_next_ref, self.src_ref, self.nxt, self.nxt_sem, lo, hi)

    def finish(self):
        @pl.when(self.i == pl.num_programs(0) - 1)
        def _():
            _wait_row_gather(self.src_ref, self.nxt, self.nxt_sem)


def _inproj_kernel(*refs, has_ffn, tm, n_lat_tiles, tiles_per_batch, seq):
    if has_ffn:
        (x_ref, pos_ref, pos_next_ref, ys_ref, modp_ref, mod_ref, g_ref, w_ref, qkg_ref, bd64_ref, bd32_ref,
         c64_ref, s64_ref, c32_ref, s32_ref, xo_ref, u_ref, fbuf, fsems) = refs
        prefetch = _RowPrefetch(pos_ref, pos_next_ref, ys_ref, fbuf, fsems)
        f_ref = prefetch.cur
    else:
        (x_ref, mod_ref, g_ref, w_ref, qkg_ref, bd64_ref, bd32_ref,
         c64_ref, s64_ref, c32_ref, s32_ref, u_ref) = refs
    i = pl.program_id(0)
    hm = tm // 2

    def prologue(r0):
        x = x_ref[r0:r0 + hm, :]
        if has_ffn:
            x = x + modp_ref[0, 5:6, :] * f_ref[r0:r0 + hm, :]
            xo_ref[r0:r0 + hm, :] = x
        ms = jnp.mean(x * x, axis=-1, keepdims=True)
        hn = x * lax.rsqrt(ms + EPS) * g_ref[...]
        return (hn * (1.0 + mod_ref[0, 1:2, :]) + mod_ref[0, 0:1, :]).astype(BF16)

    p0 = jnp.where(i < n_lat_tiles, (i % tiles_per_batch) * tm, seq)
    p0 = pl.multiple_of(p0, tm)
    lane = lax.broadcasted_iota(I32, (hm, 128), 1)

    def norm_rope(a, r0, gidx, group, bd_ref, rope_tabs):
        w = a.shape[1]
        ssq = _dot((a * a).astype(BF16), bd_ref[0:w, 0:w])
        r = lax.rsqrt(ssq * (1.0 / group) + EPS)
        outs = []
        for s in range(w // 128):
            t = a[:, 128 * s:128 * s + 128] * r[:, 128 * s:128 * s + 128] * qkg_ref[gidx:gidx + 1, :]
            if rope_tabs is not None:
                c_ref, s_ref, half = rope_tabs
                fwd = pltpu.roll(t, 128 - half, axis=1)
                bwd = pltpu.roll(t, half, axis=1)
                sw = jnp.where((lane % (2 * half)) < half, fwd, bwd)
                t = t * c_ref[pl.ds(p0 + r0, hm), :] + sw * s_ref[pl.ds(p0 + r0, hm), :]
            outs.append(t)
        return outs[0] if len(outs) == 1 else jnp.concatenate(outs, axis=1)

    rope64 = (c64_ref, s64_ref, HEAD_DIM // 4)
    rope32 = (c32_ref, s32_ref, DIFF_DIM // 4)

    def epilogue(acc, r0, blk):
        if blk == COL_A_Q:
            out = norm_rope(acc, r0, 0, HEAD_DIM, bd64_ref, rope64)
        elif blk == COL_A_KV:
            out = jnp.concatenate([norm_rope(acc[:, :128], r0, 1, HEAD_DIM, bd64_ref, rope64), acc[:, 128:]], axis=1)
        elif blk == COL_B_Q:
            out = norm_rope(acc, r0, 2, HEAD_DIM, bd64_ref, None)
        elif blk == COL_B_K:
            out = norm_rope(acc, r0, 3, HEAD_DIM, bd64_ref, None)
        elif blk == COL_C_Q:
            out = norm_rope(acc, r0, 4, DIFF_DIM, bd32_ref, rope32)
        elif blk == COL_C_K:
            out = norm_rope(acc, r0, 5, DIFF_DIM, bd32_ref, rope32)
        elif blk == COL_D_Q:
            out = norm_rope(acc, r0, 6, HEAD_DIM, bd64_ref, rope64)
        elif blk == COL_D_KV:
            out = jnp.concatenate([norm_rope(acc[:, :128], r0, 7, HEAD_DIM, bd64_ref, rope64), acc[:, 128:]], axis=1)
        else:
            out = acc
        u_ref[r0:r0 + hm, 256 * blk:256 * blk + 256] = out.astype(BF16)

    n_blk = D_IN // 256
    units = [(r0, blk) for r0 in (0, hm) for blk in range(n_blk)]
    hbs = {0: prologue(0)}
    acc = _dot(hbs[0], w_ref[:, 0:256])
    hbs[hm] = prologue(hm)
    for t, (r0, blk) in enumerate(units):
        nxt = None
        if t + 1 < len(units):
            r1, b1 = units[t + 1]
            nxt = _dot(hbs[r1], w_ref[:, 256 * b1:256 * b1 + 256])
        if has_ffn:
            prefetch.request(tm * t // len(units), tm * (t + 1) // len(units))
        epilogue(acc, r0, blk)
        acc = nxt
    if has_ffn:
        prefetch.finish()


def _mod_spec(n_lat_tiles, tiles_per_batch, batch):
    return pl.BlockSpec((1, ADA_CHUNKS, D_MODEL),
                        lambda i: (jnp.where(i < n_lat_tiles, i // tiles_per_batch, batch), 0, 0))


def _const_spec(shape):
    return pl.BlockSpec(shape, lambda i: (0,) * len(shape))


def _pos_tile_specs(n_tiles, tm):
    return [pl.BlockSpec((1, 1, tm), lambda i: (i, 0, 0), memory_space=pltpu.SMEM),
            pl.BlockSpec((1, 1, tm), lambda i: (jnp.minimum(i + 1, n_tiles - 1), 0, 0), memory_space=pltpu.SMEM)]


def _inproj_call(x, ffn_src, mod_prev, mod, g, w_in, qkg, bd64, bd32, tabs, *, batch, seq):
    n_all = x.shape[0]
    tm = TOKEN_TILE
    n_lat_tiles = batch * seq // tm
    tpb = seq // tm
    has_ffn = ffn_src is not None
    tile = pl.BlockSpec((tm, D_MODEL), lambda i: (i, 0))
    modspec = _mod_spec(n_lat_tiles, tpb, batch)
    tab_rows = tabs[0].shape[0]
    in_specs = [tile]
    args = [x]
    scratch = []
    if has_ffn:
        ys, pos = ffn_src
        pos_tiles = pos.reshape(n_all // tm, 1, tm)
        in_specs += _pos_tile_specs(n_all // tm, tm) + [pl.BlockSpec(memory_space=pl.ANY), modspec]
        args += [pos_tiles, pos_tiles, ys, mod_prev]
        scratch = [pltpu.VMEM((2, tm, D_MODEL), F32), pltpu.SemaphoreType.DMA((2,))]
    in_specs += [modspec, _const_spec((1, D_MODEL)), _const_spec((D_MODEL, D_IN)), _const_spec((8, 128)),
                 _const_spec((256, 256)), _const_spec((256, 256))] + [_const_spec((tab_rows, 128))] * 4
    args += [mod, g, w_in, qkg, bd64, bd32] + list(tabs)
    u_shape = jax.ShapeDtypeStruct((n_all, D_IN), BF16)
    u_spec = pl.BlockSpec((tm, D_IN), lambda i: (i, 0))
    if has_ffn:
        out_shape = (jax.ShapeDtypeStruct((n_all, D_MODEL), F32), u_shape)
        out_specs = (tile, u_spec)
    else:
        out_shape, out_specs = u_shape, u_spec
    kern = functools.partial(_inproj_kernel, has_ffn=has_ffn, tm=tm, n_lat_tiles=n_lat_tiles,
                             tiles_per_batch=tpb, seq=seq)
    res = pl.pallas_call(
        kern, out_shape=out_shape, grid=(n_all // tm,), in_specs=in_specs, out_specs=out_specs,
        scratch_shapes=scratch, compiler_params=_cparams("arbitrary"), name="inproj",
    )(*args)
    return res if has_ffn else (x, res)


def _softmax_parts(scores, extra=None):
    m = scores[0].max(axis=-1, keepdims=True)
    for s in scores[1:]:
        m = jnp.maximum(m, s.max(axis=-1, keepdims=True))
    if extra is not None:
        m = jnp.maximum(m, extra)
    ps = [jnp.exp2(s - m) for s in scores]
    l = ps[0].sum(axis=-1, keepdims=True)
    for p in ps[1:]:
        l = l + p.sum(axis=-1, keepdims=True)
    if extra is not None:
        l = l + jnp.exp2(extra - m)
    return ps, 1.0 / l


class _AttItem:
    def __init__(self, q, kt_view, v_view, ranges, bias=None, extra=None):
        self.q, self.kt_view, self.v_view, self.ranges, self.bias, self.extra = q, kt_view, v_view, ranges, bias, extra
        self.n_keys = sum(n for _, n in ranges)
        self.m = None

    def scores(self, s_scr, rows):
        q_blk = self.q()
        m, off, ti = None, 0, 0
        for start, n in self.ranges:
            for o in range(0, n, KEY_TILE):
                w = min(KEY_TILE, n - o)
                s = _dot(q_blk, self.kt_view[:, pl.ds(start + o, w)])
                b = self.bias(ti) if self.bias is not None else None
                if b is not None:
                    s = s + b
                ti += 1
                s_scr[rows, off + o:off + o + w] = s
                tile_max = s.max(axis=-1, keepdims=True)
                m = tile_max if m is None else jnp.maximum(m, tile_max)
            off += n
        self.m = m if self.extra is None else jnp.maximum(m, self.extra)

    def probs(self, s_scr, p_scr, rows):
        for o in range(0, self.n_keys, KEY_TILE):
            w = min(KEY_TILE, self.n_keys - o)
            p_scr[rows, o:o + w] = jnp.exp2(s_scr[rows, o:o + w] - self.m).astype(BF16)

    def values(self, p_scr, rows):
        pv, off = None, 0
        for start, n in self.ranges:
            c = _dot(p_scr[rows, off:off + n], self.v_view[pl.ds(start, n), :])
            pv = c if pv is None else pv + c
            off += n
        denom = pv[:, 64:65]
        if self.extra is not None:
            denom = denom + jnp.exp2(self.extra - self.m)
        return pv[:, 0:64] / denom


def _run_att_items(items, s_scr, p_scr):
    def rows(t):
        r0 = (t % ATT_SLOTS) * ATT_RB
        return slice(r0, r0 + ATT_RB)

    outs = []
    for t in range(len(items) + 2):
        if t < len(items):
            items[t].scores(s_scr, rows(t))
        if 0 <= t - 1 < len(items):
            items[t - 1].probs(s_scr, p_scr, rows(t - 1))
        if 0 <= t - 2 < len(items):
            outs.append(items[t - 2].values(p_scr, rows(t - 2)))
    return outs


def _stage_heads(k_parts, v_parts, kt_scr, v_scr):
    eye = jnp.where(lax.broadcasted_iota(I32, (HEAD_DIM, HEAD_DIM), 0)
                    == lax.broadcasted_iota(I32, (HEAD_DIM, HEAD_DIM), 1), 1.0, 0.0).astype(BF16)
    for h in range(kt_scr.shape[0]):
        r0 = 0
        for (kr, kc0), (vr, vc0) in zip(k_parts, v_parts):
            n = kr.shape[0]
            kt_scr[h, :, r0:r0 + n] = _nt_dot(eye, kr[:, kc0 + 64 * h:kc0 + 64 * h + 64]).astype(BF16)
            v_scr[h, r0:r0 + n, 0:64] = vr[:, vc0 + 64 * h:vc0 + 64 * h + 64]
            r0 += n
        v_scr[h, :, 64:128] = jnp.ones((v_scr.shape[1], 64), BF16)


def _group_rms(y):
    return y * lax.rsqrt(jnp.mean(y * y, axis=-1, keepdims=True) + EPS)


def _gqa_pair(q_ref, j):
    return jnp.concatenate([q_ref[:, 128 * j:128 * j + 64], q_ref[:, 128 * j + 64:128 * j + 128]], axis=0)


def _stage_once_per_batch_row(k_parts, v_parts, k_scr, v_scr):
    @pl.when(pl.program_id(1) == 0)
    def _():
        _stage_heads(k_parts, v_parts, k_scr, v_scr)


def _row_blocks(n):
    return range(0, n, ATT_RB)


def _q_thunk(q_ref, r0, head):
    return lambda: q_ref[r0:r0 + ATT_RB, 64 * head:64 * head + 64]


def _heads_to_rows(outs, n_heads):
    per = len(outs) // n_heads
    return jnp.concatenate([jnp.concatenate(outs[h * per:(h + 1) * per], axis=0) for h in range(n_heads)], axis=1)


def _dense_kernel(q_ref, kv_ref, kvc_ref, o_ref, kt_scr, v_scr, s_scr, p_scr, *, tq):
    _stage_once_per_batch_row([(kv_ref, 0), (kvc_ref, 0)], [(kv_ref, 128), (kvc_ref, 128)], kt_scr, v_scr)
    nk = v_scr.shape[1]
    items = [_AttItem(_q_thunk(q_ref, r0, hq), kt_scr.at[hq // 2], v_scr.at[hq // 2], [(0, nk)])
             for hq in range(HEADS) for r0 in _row_blocks(tq)]
    y = _heads_to_rows(_run_att_items(items, s_scr, p_scr), HEADS)
    o_ref[...] = _group_rms(y).astype(BF16)


def _window_kernel(sink_ref, q_ref, kv_ref, kvc_ref, mask_ref, o_ref, kt_scr, v_scr, s_scr, p_scr, *, tq, seq):
    _stage_once_per_batch_row([(kv_ref, 0), (kvc_ref, 0)], [(kv_ref, 128), (kvc_ref, 128)], kt_scr, v_scr)
    kw = tq + 2 * WINDOW
    nc = v_scr.shape[1] - seq
    ks = pl.multiple_of(jnp.clip(pl.program_id(1) * tq - WINDOW, 0, seq - kw), 128)

    def mask_thunk(r0):
        return lambda ti: (mask_ref[0, r0:r0 + ATT_RB, KEY_TILE * ti:KEY_TILE * ti + KEY_TILE]
                           if KEY_TILE * ti < kw else None)

    items = [_AttItem(_q_thunk(q_ref, r0, hq), kt_scr.at[hq // 2], v_scr.at[hq // 2], [(ks, kw), (seq, nc)],
                      bias=mask_thunk(r0), extra=sink_ref[hq] * LOG2E)
             for hq in range(HEADS) for r0 in _row_blocks(tq)]
    y = _heads_to_rows(_run_att_items(items, s_scr, p_scr), HEADS)
    o_ref[...] = _group_rms(y).astype(BF16)


def _neigh_kernel(q_ref, k_ref, v_ref, kc_ref, vc_ref, bias_ref, o_ref, kt_scr, v_scr, s_scr, p_scr, *, rows, seq):
    _stage_once_per_batch_row([(k_ref, 0), (kc_ref, 0)], [(v_ref, 0), (vc_ref, 0)], kt_scr, v_scr)
    rs = jnp.clip(NA_QROWS * pl.program_id(1) - NA_ROWS // 2, 0, rows - NA_KROWS)
    k0 = pl.multiple_of(rs * GRID_W, KEY_TILE)
    nk = NA_KROWS * GRID_W
    nc = v_scr.shape[1] - seq

    def bias_thunk(h, r0):
        return lambda ti: (bias_ref[0, h, r0:r0 + ATT_RB, KEY_TILE * ti:KEY_TILE * ti + KEY_TILE]
                           if KEY_TILE * ti < nk else None)

    items = [_AttItem(_q_thunk(q_ref, r0, h), kt_scr.at[h], v_scr.at[h], [(k0, nk), (seq, nc)], bias=bias_thunk(h, r0))
             for h in range(HEADS) for r0 in _row_blocks(q_ref.shape[0])]
    y = _heads_to_rows(_run_att_items(items, s_scr, p_scr), HEADS)
    o_ref[...] = _group_rms(y).astype(BF16)


def _diff_lambda(lam_ref, lam_init):
    lp = lam_ref[...]
    a = jnp.sum(lp[0:1] * lp[1:2], axis=-1, keepdims=True)
    b = jnp.sum(lp[2:3] * lp[3:4], axis=-1, keepdims=True)
    return jnp.exp(a) - jnp.exp(b) + lam_init


def _diff_heads(q, k_parts, v_parts, lam, lam_init):
    t = q.shape[0]
    first = lax.broadcasted_iota(I32, (t, HEAD_DIM), 1) < DIFF_DIM
    zero = jnp.zeros((t, HEAD_DIM), BF16)
    outs = []
    for h in range(HEADS):
        sl = slice(64 * h, 64 * h + 64)
        qh = q[:, sl]
        ks = [kp[:, sl] for kp in k_parts]
        maps = []
        for half in range(2):
            qm = jnp.where(first, qh, zero) if half == 0 else jnp.where(first, zero, qh)
            ps, inv = _softmax_parts([_nt_dot(qm, kk) for kk in ks])
            maps.append([p * inv for p in ps])
        o = None
        for a0, a1, vp in zip(maps[0], maps[1], v_parts):
            c = _dot((a0 - lam * a1).astype(BF16), vp[:, sl])
            o = c if o is None else o + c
        outs.append(_group_rms(o) * (1.0 - lam_init))
    return jnp.concatenate(outs, axis=1)


def _diff_kernel(lam_ref, q_ref, k_ref, v_ref, kc_ref, vc_ref, o_ref, kt_scr, v_scr, s_scr, p_scr, *, lam_init, tq):
    _stage_once_per_batch_row([(k_ref, 0), (kc_ref, 0)], [(v_ref, 0), (vc_ref, 0)], kt_scr, v_scr)
    lam = _diff_lambda(lam_ref, lam_init)
    nk = v_scr.shape[1]
    first = lax.broadcasted_iota(I32, (ATT_RB, HEAD_DIM), 1) < DIFF_DIM
    zero = jnp.zeros((ATT_RB, HEAD_DIM), BF16)

    def q_map(r0, h, half):
        def thunk():
            qh = q_ref[r0:r0 + ATT_RB, 64 * h:64 * h + 64]
            return jnp.where(first, qh, zero) if half == 0 else jnp.where(first, zero, qh)
        return thunk

    items = [_AttItem(q_map(r0, h, half), kt_scr.at[h], v_scr.at[h], [(0, nk)])
             for h in range(HEADS) for r0 in _row_blocks(tq) for half in range(2)]
    outs = _run_att_items(items, s_scr, p_scr)
    diffs = [outs[i] - lam * outs[i + 1] for i in range(0, len(outs), 2)]
    per = len(diffs) // HEADS
    heads = [_group_rms(jnp.concatenate(diffs[h * per:(h + 1) * per], axis=0)) * (1.0 - lam_init)
             for h in range(HEADS)]
    o_ref[...] = jnp.concatenate(heads, axis=1).astype(BF16)


class _Cols:
    def __init__(self, ref, blk):
        self.ref, self.base = ref, 256 * blk

    def __getitem__(self, idx):
        rows, sl = idx
        return self.ref[rows, self.base + sl.start:self.base + sl.stop]


def _ctx_kernel(sink_ref, lam_ref, u_ref, o_ref, *, lam_init):
    L = u_ref.shape[0]
    top = lax.broadcasted_iota(I32, (2 * L, 1), 0) < L
    for q_blk, kv_blk, out_col, use_sink in ((COL_A_Q, COL_A_KV, 0, True), (COL_D_Q, COL_D_KV, 768, False)):
        qv, kv = _Cols(u_ref, q_blk), _Cols(u_ref, kv_blk)
        outs = [None] * HEADS
        for j in range(2):
            kc = kv[:, slice(64 * j, 64 * j + 64)]
            vc = kv[:, slice(128 + 64 * j, 128 + 64 * j + 64)]
            q2 = jnp.concatenate([qv[:, slice(128 * j, 128 * j + 64)], qv[:, slice(128 * j + 64, 128 * j + 128)]], axis=0)
            snk = jnp.where(top, sink_ref[2 * j], sink_ref[2 * j + 1]) * LOG2E if use_sink else None
            (p,), inv = _softmax_parts([_nt_dot(q2, kc)], extra=snk)
            o = _dot(p.astype(BF16), vc) * inv
            outs[2 * j], outs[2 * j + 1] = o[:L], o[L:]
        o_ref[:, out_col:out_col + 256] = _group_rms(jnp.concatenate(outs, axis=1)).astype(BF16)
    qv, kv, vv = _Cols(u_ref, COL_B_Q), _Cols(u_ref, COL_B_K), _Cols(u_ref, COL_B_V)
    outs = []
    for h in range(HEADS):
        sl = slice(64 * h, 64 * h + 64)
        (p,), inv = _softmax_parts([_nt_dot(qv[:, sl], kv[:, sl])])
        outs.append(_dot(p.astype(BF16), vv[:, sl]) * inv)
    o_ref[:, 256:512] = _group_rms(jnp.concatenate(outs, axis=1)).astype(BF16)
    lam = _diff_lambda(lam_ref, lam_init)
    qc = u_ref[:, 256 * COL_C_Q:256 * COL_C_Q + 256]
    o_ref[:, 512:768] = _diff_heads(qc, [_Cols(u_ref, COL_C_K)], [_Cols(u_ref, COL_C_V)], lam, lam_init).astype(BF16)


def _attention_calls(u, sink, lam_p, bias, wmask, *, layer, batch, seq, ctx_len, need_ctx):
    lam_init = 0.8 - 0.6 * math.exp(-0.3 * layer)
    n_all = u.shape[0]
    tq = ATT_TQ
    nq = seq // tq
    nk_all = seq + ctx_len
    cb0 = batch * seq // ctx_len
    y_shape = jax.ShapeDtypeStruct((n_all, D_MODEL), BF16)
    smem = pl.BlockSpec(memory_space=pltpu.SMEM)
    any_spec = pl.BlockSpec(memory_space=pl.ANY)
    lam_spec2 = pl.BlockSpec((4, DIFF_DIM), lambda b, i: (0, 0))
    cp = _cparams("arbitrary", "arbitrary")

    def scratch(n_heads, n_keys):
        return [pltpu.VMEM((n_heads, HEAD_DIM, nk_all), BF16), pltpu.VMEM((n_heads, nk_all, 128), BF16),
                pltpu.VMEM((ATT_SLOTS * ATT_RB, n_keys), F32), pltpu.VMEM((ATT_SLOTS * ATT_RB, n_keys), BF16)]

    def qspec(col, t=tq):
        return pl.BlockSpec((t, 256), lambda b, i: (b * (seq // t) + i, col))

    tqf = ATT_TQ_FULL
    nqf = seq // tqf

    def latspec(col):
        return pl.BlockSpec((seq, 256), lambda b, i: (b, col))

    def ctxspec(col):
        return pl.BlockSpec((ctx_len, 256), lambda b, i: (cb0 + b, col))

    kw = tq + 2 * WINDOW

    def first_mid_last(n):
        return lambda b, i: (jnp.where(i == 0, 0, jnp.where(i == n - 1, 2, 1)),)

    wcase = first_mid_last(nq)
    y = pl.pallas_call(
        functools.partial(_window_kernel, tq=tq, seq=seq),
        out_shape=y_shape, grid=(batch, nq),
        in_specs=[smem, qspec(COL_A_Q), latspec(COL_A_KV), ctxspec(COL_A_KV),
                  pl.BlockSpec((1, tq, kw), lambda b, i: wcase(b, i) + (0, 0))],
        out_specs=qspec(0), scratch_shapes=scratch(2, kw + ctx_len),
        compiler_params=cp, name="mix_window",
    )(sink, u, u, u, wmask)

    rows = seq // GRID_W
    nrb = rows // NA_QROWS
    qrows = NA_QROWS * GRID_W
    nkb = NA_KROWS * GRID_W
    bcase = first_mid_last(nrb)
    y = pl.pallas_call(
        _drop_arg(functools.partial(_neigh_kernel, rows=rows, seq=seq), 6),
        out_shape=y_shape, grid=(batch, nrb),
        in_specs=[pl.BlockSpec((qrows, 256), lambda b, i: (b * nrb + i, COL_B_Q)),
                  latspec(COL_B_K), latspec(COL_B_V), ctxspec(COL_B_K), ctxspec(COL_B_V),
                  pl.BlockSpec((1, HEADS, qrows, nkb), lambda b, i: bcase(b, i) + (0, 0, 0)), any_spec],
        out_specs=pl.BlockSpec((qrows, 256), lambda b, i: (b * nrb + i, 1)),
        scratch_shapes=scratch(HEADS, nkb + ctx_len),
        input_output_aliases={6: 0}, compiler_params=cp, name="mix_neigh",
    )(u, u, u, u, u, bias, y)

    y = pl.pallas_call(
        _drop_arg(functools.partial(_diff_kernel, lam_init=lam_init, tq=tqf), 6),
        out_shape=y_shape, grid=(batch, nqf),
        in_specs=[lam_spec2, qspec(COL_C_Q, tqf), latspec(COL_C_K), latspec(COL_C_V),
                  ctxspec(COL_C_K), ctxspec(COL_C_V), any_spec],
        out_specs=qspec(2, tqf), scratch_shapes=scratch(HEADS, nk_all),
        input_output_aliases={6: 0}, compiler_params=cp, name="mix_diff",
    )(lam_p, u, u, u, u, u, y)

    y = pl.pallas_call(
        _drop_arg(functools.partial(_dense_kernel, tq=tqf), 3),
        out_shape=y_shape, grid=(batch, nqf),
        in_specs=[qspec(COL_D_Q, tqf), latspec(COL_D_KV), ctxspec(COL_D_KV), any_spec],
        out_specs=qspec(3, tqf), scratch_shapes=scratch(2, nk_all),
        input_output_aliases={3: 0}, compiler_params=cp, name="mix_dense",
    )(u, u, u, y)

    if need_ctx:
        y = pl.pallas_call(
            _drop_arg(functools.partial(_ctx_kernel, lam_init=lam_init), 3),
            out_shape=y_shape, grid=(batch,),
            in_specs=[smem, pl.BlockSpec((4, DIFF_DIM), lambda b: (0, 0)),
                      pl.BlockSpec((ctx_len, D_IN), lambda b: (cb0 + b, 0)), any_spec],
            out_specs=pl.BlockSpec((ctx_len, D_MODEL), lambda b: (cb0 + b, 0)),
            input_output_aliases={3: 0}, compiler_params=_cparams("arbitrary"), name="mix_ctx",
        )(sink, lam_p, u, y)
    return y


def _select4(idx, vals):
    return jnp.where(idx == 0, vals[0], jnp.where(idx == 1, vals[1], jnp.where(idx == 2, vals[2], vals[3])))


def _outproj_kernel(y_ref, x_ref, mod_ref, gain_ref, w_ref, g_ref, rw_ref, rb_ref, xo_ref, h_ref, cls_ref, *, tm):
    hm = tm // 2

    def project(r0):
        yg = (y_ref[r0:r0 + hm, :].astype(F32) * gain_ref[...]).astype(BF16)
        return _dot(yg, w_ref[...])

    def normalise(r0, proj):
        xm = x_ref[r0:r0 + hm, :] + mod_ref[0, 2:3, :] * proj
        xo_ref[r0:r0 + hm, :] = xm
        hn = xm * lax.rsqrt(jnp.mean(xm * xm, axis=-1, keepdims=True) + EPS) * g_ref[...]
        h2 = hn * (1.0 + mod_ref[0, 4:5, :]) + mod_ref[0, 3:4, :]
        h_ref[r0:r0 + hm, 0:D_MODEL] = h2
        hi = h2.astype(BF16)
        lo = (h2 - hi.astype(F32)).astype(BF16)
        return (_dot(hi, rw_ref[...]) + _dot(lo, rw_ref[...])).T

    def route(r0, lt):
        score = _sigmoid(lt[0:N_EXPERTS] + lt[N_EXPERTS:2 * N_EXPERTS])
        sel = score + rb_ref[...]
        srow = [sel[e:e + 1] for e in range(N_EXPERTS)]
        crow = [score[e:e + 1] for e in range(N_EXPERTS)]

        gidx = jnp.zeros((1, hm), I32)
        gbest = None
        for g in range(N_EXPERTS // EXPERTS_PER_GROUP):
            v = srow[4 * g:4 * g + 4]
            best = None
            for a in range(4):
                for b in range(a + 1, 4):
                    t = v[a] + v[b]
                    best = t if best is None else jnp.maximum(best, t)
            if g == 0:
                gbest = best
            else:
                better = best > gbest
                gidx = jnp.where(better, g, gidx)
                gbest = jnp.where(better, best, gbest)

        iv = [_select4(gidx, [srow[4 * g + i] for g in range(4)]) for i in range(4)]
        sv = [_select4(gidx, [crow[4 * g + i] for g in range(4)]) for i in range(4)]
        chosen = []
        for i in range(4):
            rank = jnp.zeros((1, hm), I32)
            for j in range(4):
                if j != i:
                    beats = (iv[j] >= iv[i]) if j < i else (iv[j] > iv[i])
                    rank = rank + jnp.where(beats, 1, 0)
            chosen.append(rank < 2)
        lo_i = jnp.where(chosen[0], 0, jnp.where(chosen[1], 1, 2))
        hi_i = jnp.where(chosen[3], 3, jnp.where(chosen[2], 2, 1))
        pair = jnp.where(lo_i == 0, hi_i - 1, jnp.where(lo_i == 1, hi_i + 1, 5))
        cls_ref[0, :, r0:r0 + hm] = gidx * N_PAIRS + pair
        s_lo = jnp.where(lo_i == 0, sv[0], jnp.where(lo_i == 1, sv[1], sv[2]))
        s_hi = jnp.where(hi_i == 3, sv[3], jnp.where(hi_i == 2, sv[2], sv[1]))
        inv = 1.0 / (s_lo + s_hi)
        rowi = lax.broadcasted_iota(I32, (128, hm), 0)
        wts = jnp.where(rowi == 0, s_lo * inv, jnp.where(rowi == 1, s_hi * inv, 0.0))
        h_ref[r0:r0 + hm, D_MODEL:EXT_W] = wts.T

    proj0 = project(0)
    proj1 = project(hm)
    lt0 = normalise(0, proj0)
    lt1 = normalise(hm, proj1)
    route(0, lt0)
    route(hm, lt1)


def _outproj_call(y, x, mod, gain, w_out, g, rw, rb, *, n_proc, batch, seq):
    tm = TOKEN_TILE
    n_tiles = n_proc // tm
    tile = pl.BlockSpec((tm, D_MODEL), lambda i: (i, 0))
    return pl.pallas_call(
        functools.partial(_outproj_kernel, tm=tm),
        out_shape=(jax.ShapeDtypeStruct((n_proc, D_MODEL), F32),
                   jax.ShapeDtypeStruct((n_proc, EXT_W), F32),
                   jax.ShapeDtypeStruct((n_tiles, 1, tm), I32)),
        grid=(n_tiles,),
        in_specs=[tile, tile, _mod_spec(batch * seq // tm, seq // tm, batch), _const_spec((1, D_MODEL)),
                  _const_spec((D_MODEL, D_MODEL)), _const_spec((1, D_MODEL)), _const_spec((D_MODEL, 128)),
                  _const_spec((N_EXPERTS, 1))],
        out_specs=(tile, pl.BlockSpec((tm, EXT_W), lambda i: (i, 0)), pl.BlockSpec((1, 1, tm), lambda i: (i, 0, 0))),
        compiler_params=_cparams("arbitrary"), name="outproj_router",
    )(y, x, mod, gain, w_out, g, rw, rb)


def _rank_kernel(cls_ref, pos_ref, bcls_ref, nval_ref, *, blk):
    c = cls_ref[...]
    nr = c.shape[0]
    nbp = bcls_ref.shape[1]
    upper = (lax.broadcasted_iota(I32, (128, 128), 0) <= lax.broadcasted_iota(I32, (128, 128), 1))
    upper = jnp.where(upper, 1.0, 0.0).astype(BF16)
    lower = (lax.broadcasted_iota(I32, (nr, nr), 1) < lax.broadcasted_iota(I32, (nr, nr), 0))
    lower = jnp.where(lower, 1.0, 0.0).astype(BF16)
    bidx = lax.broadcasted_iota(I32, (1, nbp), 1).astype(F32)
    pos = jnp.zeros((nr, 128), F32)
    bcls = jnp.zeros((1, nbp), F32)
    nval = jnp.zeros((1, nbp), F32)
    start = jnp.zeros((1, 1), F32)
    for k in range(N_CLASSES):
        m = c == k
        incl = _dot(jnp.where(m, 1.0, 0.0).astype(BF16), upper)
        tot = incl[:, 127:128]
        above = _dot(lower, jnp.broadcast_to(tot, (nr, 128)).astype(BF16))
        cnt = above[nr - 1:nr, 0:1] + tot[nr - 1:nr, 0:1]
        nblk = jnp.floor((cnt + (blk - 1)) * (1.0 / blk))
        pos = jnp.where(m, start * blk + above + incl - 1.0, pos)
        end = start + nblk
        bcls = bcls + jnp.where(bidx >= end, 1.0, 0.0)
        nval = nval + jnp.where((bidx >= start) & (bidx < end), jnp.clip(cnt - (bidx - start) * blk, 0.0, blk), 0.0)
        start = end
    pos_ref[...] = pos.astype(I32)
    bcls_ref[...] = jnp.minimum(bcls, N_CLASSES - 1.0).astype(I32)
    nval_ref[...] = nval.astype(I32)


def _rank_call(cls2d, n_blocks):
    nr = cls2d.shape[0]
    nbp = -(-n_blocks // 128) * 128
    return pl.pallas_call(
        functools.partial(_rank_kernel, blk=MOE_BLK),
        out_shape=(jax.ShapeDtypeStruct((nr, 128), I32), jax.ShapeDtypeStruct((1, nbp), I32),
                   jax.ShapeDtypeStruct((1, nbp), I32)),
        grid=(1,),
        in_specs=[_const_spec((nr, 128))],
        out_specs=(_const_spec((nr, 128)), _const_spec((1, nbp)), _const_spec((1, nbp))),
        compiler_params=_cparams("arbitrary"), name="class_rank",
    )(cls2d)


def _scatter_kernel(pos_ref, h_ref, xs_ref, sem):
    n = h_ref.shape[0]
    for r in range(n):
        pltpu.make_async_copy(h_ref.at[pl.ds(r, 1)], xs_ref.at[pl.ds(pos_ref[0, 0, r], 1)], sem).start()
    pltpu.make_async_copy(h_ref, xs_ref.at[pl.ds(0, n)], sem).wait()


def _scatter_call(h_ext, pos3, n_slots):
    n, w = h_ext.shape
    r = ROW_DMA
    return pl.pallas_call(
        _scatter_kernel,
        out_shape=jax.ShapeDtypeStruct((n_slots, w), h_ext.dtype),
        grid=(n // r,),
        in_specs=[pl.BlockSpec((1, 1, r), lambda i: (i, 0, 0), memory_space=pltpu.SMEM),
                  pl.BlockSpec((r, w), lambda i: (i, 0))],
        out_specs=pl.BlockSpec(memory_space=pl.ANY),
        scratch_shapes=[pltpu.SemaphoreType.DMA(())],
        compiler_params=_cparams("arbitrary"), name="row_scatter",
    )(pos3, h_ext)


def _moe_kernel(elo_ref, ehi_ref, nval_ref, xs_ref, wg_lo, wu_lo, wd_lo, wg_hi, wu_hi, wd_hi, ys_ref):
    del elo_ref, ehi_ref
    nv = nval_ref[pl.program_id(0)]

    @pl.when(nv > 0)
    def _():
        live = lax.broadcasted_iota(I32, (MOE_BLK, 1), 0) < nv
        x = jnp.where(live, xs_ref[:, 0:D_MODEL], 0.0).astype(BF16)
        acts = []
        for wg, wu, col in ((wg_lo, wu_lo, D_MODEL), (wg_hi, wu_hi, D_MODEL + 1)):
            w = jnp.where(live, xs_ref[:, col:col + 1], 0.0)
            g = _dot(x, wg[0])
            acts.append((g * _sigmoid(g) * _dot(x, wu[0]) * w).astype(BF16))
        ys_ref[...] = _dot(acts[0], wd_lo[0]) + _dot(acts[1], wd_hi[0])

    @pl.when(nv == 0)
    def _():
        ys_ref[...] = jnp.zeros_like(ys_ref)


def _moe_call(xs, e_lo, e_hi, nval, w_gate, w_up, w_down):
    n_blocks = xs.shape[0] // MOE_BLK

    def wspec(shape, which):
        return pl.BlockSpec((1,) + shape, lambda b, elo, ehi, nv: ((elo, ehi)[which][b], 0, 0))

    gu, dn = (D_MODEL, D_EXPERT), (D_EXPERT, D_MODEL)
    grid_spec = pltpu.PrefetchScalarGridSpec(
        num_scalar_prefetch=3, grid=(n_blocks,),
        in_specs=[pl.BlockSpec((MOE_BLK, EXT_W), lambda b, elo, ehi, nv: (b, 0)),
                  wspec(gu, 0), wspec(gu, 0), wspec(dn, 0), wspec(gu, 1), wspec(gu, 1), wspec(dn, 1)],
        out_specs=pl.BlockSpec((MOE_BLK, D_MODEL), lambda b, elo, ehi, nv: (b, 0)),
    )
    return pl.pallas_call(
        _moe_kernel, out_shape=jax.ShapeDtypeStruct((xs.shape[0], D_MODEL), F32), grid_spec=grid_spec,
        compiler_params=_cparams("arbitrary"), name="moe_experts",
    )(e_lo, e_hi, nval, xs, w_gate, w_up, w_down, w_gate, w_up, w_down)


def _final_kernel(x_ref, pos_ref, pos_next_ref, ys_ref, mod_ref, o_ref, fbuf, fsems):
    prefetch = _RowPrefetch(pos_ref, pos_next_ref, ys_ref, fbuf, fsems)
    prefetch.request(0, fbuf.shape[1])
    o_ref[...] = x_ref[...] + mod_ref[0, 5:6, :] * prefetch.cur[...]
    prefetch.finish()


def _final_call(x, ys, pos, mod, *, batch, seq):
    tm = TOKEN_TILE
    n = batch * seq
    tile = pl.BlockSpec((tm, D_MODEL), lambda i: (i, 0))
    pos_tiles = pos.reshape(n // tm, 1, tm)
    return pl.pallas_call(
        _final_kernel, out_shape=jax.ShapeDtypeStruct((n, D_MODEL), F32), grid=(n // tm,),
        in_specs=[tile] + _pos_tile_specs(n // tm, tm) + [pl.BlockSpec(memory_space=pl.ANY),
                                                          _mod_spec(n // tm, seq // tm, batch)],
        out_specs=tile, scratch_shapes=[pltpu.VMEM((2, tm, D_MODEL), F32), pltpu.SemaphoreType.DMA((2,))],
        compiler_params=_cparams("arbitrary"), name="final_residual",
    )(x, pos_tiles, pos_tiles, ys, mod)


def _rope_tables(seq, ident_rows):
    pos = jnp.arange(seq, dtype=jnp.int32)
    row = (pos // GRID_W).astype(F32)[:, None]
    col = (pos % GRID_W).astype(F32)[:, None]

    def tab(half):
        inv = ROPE_BASE ** (-jnp.arange(half, dtype=F32) / half)
        ar, ac = row * inv, col * inv
        cos = jnp.concatenate([jnp.cos(ar), jnp.cos(ar), jnp.cos(ac), jnp.cos(ac)], axis=1)
        sin = jnp.concatenate([-jnp.sin(ar), jnp.sin(ar), -jnp.sin(ac), jnp.sin(ac)], axis=1)
        reps = 128 // (4 * half)
        cos = jnp.concatenate([jnp.tile(cos, (1, reps)), jnp.ones((ident_rows, 128), F32)], axis=0)
        sin = jnp.concatenate([jnp.tile(sin, (1, reps)), jnp.zeros((ident_rows, 128), F32)], axis=0)
        return cos, sin

    c64, s64 = tab(HEAD_DIM // 4)
    c32, s32 = tab(DIFF_DIM // 4)
    return c64, s64, c32, s32


def _block_diag_ones(group):
    idx = np.arange(256) // group
    return jnp.asarray((idx[:, None] == idx[None, :]).astype(np.float32), dtype=BF16)


def _qk_gains(g_win, g_na, g_diff, g_gqa):
    s64, s32 = HEAD_DIM ** -0.5 * LOG2E, DIFF_DIM ** -0.5 * LOG2E
    rows = [jnp.tile(g_win[0] * s64, 2), jnp.tile(g_win[1], 2), jnp.tile(g_na[0] * s64, 2), jnp.tile(g_na[1], 2),
            jnp.tile(g_diff[0] * s32, 4), jnp.tile(g_diff[1], 4), jnp.tile(g_gqa[0] * s64, 2), jnp.tile(g_gqa[1], 2)]
    return jnp.stack(rows).astype(F32)


def _na_bias_tiles(rpb, rows):
    w = GRID_W
    qc = np.arange(w)[:, None]
    kc = np.arange(w)[None, :]
    c_start = np.clip(qc - NA_COLS // 2, 0, w - NA_COLS)
    col_ok = (kc >= c_start) & (kc < c_start + NA_COLS)
    dc = np.clip(kc - qc + NA_COLS - 1, 0, 2 * NA_COLS - 2)
    t = jnp.where(jnp.asarray(col_ok), rpb.astype(F32)[:, :, dc] * LOG2E, NEG)
    neg = jnp.full((rpb.shape[0], w, w), NEG, F32)
    cases = []
    for r0, rs in ((0, 0), (NA_QROWS, 0), (rows - NA_QROWS, rows - NA_KROWS)):
        qtiles = []
        for i in range(NA_QROWS):
            qrow = r0 + i
            r_start = min(max(qrow - NA_ROWS // 2, 0), rows - NA_ROWS)
            blks = []
            for j in range(NA_KROWS):
                krow = rs + j
                blks.append(t[:, krow - qrow + NA_ROWS - 1] if r_start <= krow < r_start + NA_ROWS else neg)
            qtiles.append(jnp.concatenate(blks, axis=2))
        cases.append(jnp.concatenate(qtiles, axis=1))
    return jnp.stack(cases)


def _window_mask_tiles(tq):
    kw = tq + 2 * WINDOW
    r = lax.broadcasted_iota(I32, (3, tq, kw), 1)
    c = lax.broadcasted_iota(I32, (3, tq, kw), 2)
    shift = lax.broadcasted_iota(I32, (3, tq, kw), 0) * WINDOW
    return jnp.where(jnp.abs(c - r - shift) <= WINDOW, 0.0, NEG).astype(F32)


def _router_weights(router_w):
    hi = router_w.astype(BF16)
    lo = (router_w - hi.astype(F32)).astype(BF16)
    pad = jnp.zeros((router_w.shape[0], 128 - 2 * N_EXPERTS), BF16)
    return jnp.concatenate([hi, lo, pad], axis=1)


def kernel(x, c, ctx, c_ctx, ada_w, ada_b, norm_mix_g, norm_ffn_g, w_in, qk_g_win, qk_g_na, qk_g_diff, qk_g_gqa,
           sink_win, rpb_na, lambda_diff, out_gain, w_out, router_w, router_b, w_gate, w_up, w_down):
    batch, seq, d = x.shape
    ctx_len = ctx.shape[1]
    depth = w_in.shape[0]
    n_lat, n_ctx = batch * seq, batch * ctx_len
    n_all = n_lat + n_ctx
    rows = seq // GRID_W
    assert d == D_MODEL and seq % TOKEN_TILE == 0 and n_ctx % TOKEN_TILE == 0 and ctx_len % 128 == 0
    assert rows % NA_QROWS == 0 and rows >= NA_KROWS + NA_QROWS and seq >= ATT_TQ + 2 * WINDOW
    assert ctx_len % 128 == 0 and (NA_KROWS * GRID_W) % KEY_TILE == 0

    pad_rows = -(-(batch + 1) // 8) * 8
    s_all = jnp.concatenate([c, c_ctx[None, :], jnp.zeros((pad_rows - batch - 1, d), F32)], axis=0)
    mods = _ada_call(s_all, ada_w, ada_b).reshape(depth, pad_rows, ADA_CHUNKS, d)

    xs_all = jnp.concatenate([x.reshape(n_lat, d), ctx.reshape(n_ctx, d)], axis=0)
    tabs = _rope_tables(seq, TOKEN_TILE)
    wmask = _window_mask_tiles(ATT_TQ)
    bd64, bd32 = _block_diag_ones(HEAD_DIM), _block_diag_ones(DIFF_DIM)
    rw = _router_weights(router_w)
    rb = router_b.astype(F32).reshape(N_EXPERTS, 1)
    lo_tab = jnp.asarray([EXPERTS_PER_GROUP * (k // N_PAIRS) + PAIR_LO[k % N_PAIRS] for k in range(N_CLASSES)], I32)
    hi_tab = jnp.asarray([EXPERTS_PER_GROUP * (k // N_PAIRS) + PAIR_HI[k % N_PAIRS] for k in range(N_CLASSES)], I32)

    ffn, mod_prev = None, None
    for layer in range(depth):
        need_ctx = layer < depth - 1
        mod = mods[layer]
        qkg = _qk_gains(qk_g_win[layer], qk_g_na[layer], qk_g_diff[layer], qk_g_gqa[layer])
        xs_all, u = _inproj_call(xs_all, ffn, mod_prev, mod, norm_mix_g[layer].reshape(1, d),
                                 w_in[layer].astype(BF16), qkg, bd64, bd32, tabs, batch=batch, seq=seq)
        y = _attention_calls(u, sink_win[layer].astype(F32), lambda_diff[layer].astype(F32),
                             _na_bias_tiles(rpb_na[layer], rows), wmask, layer=layer, batch=batch, seq=seq,
                             ctx_len=ctx_len, need_ctx=need_ctx)
        n_proc = n_all if need_ctx else n_lat
        x_mid, h_ext, cls = _outproj_call(y, xs_all, mod, out_gain[layer].reshape(1, d), w_out[layer].astype(BF16),
                                          norm_ffn_g[layer].reshape(1, d), rw, rb, n_proc=n_proc, batch=batch, seq=seq)
        n_blocks = n_proc // MOE_BLK + N_CLASSES
        pos, bcls, nval = _rank_call(cls.reshape(n_proc // 128, 128), n_blocks)
        pos3 = pos.reshape(n_proc // ROW_DMA, 1, ROW_DMA)
        bcls, nval = bcls[0, :n_blocks], nval[0, :n_blocks]
        sorted_rows = _scatter_call(h_ext, pos3, n_blocks * MOE_BLK)
        ys = _moe_call(sorted_rows, lo_tab[bcls], hi_tab[bcls], nval, w_gate[layer].astype(BF16),
                       w_up[layer].astype(BF16), w_down[layer].astype(BF16))
        ffn = (ys, pos)
        xs_all, mod_prev = x_mid, mod
    out = _final_call(xs_all, ffn[0], ffn[1], mod_prev, batch=batch, seq=seq)
    return out.reshape(batch, seq, d)
```

```python
import functools
import math

import numpy as np
import jax
import jax.numpy as jnp
from jax import lax
from jax.experimental import pallas as pl
from jax.experimental.pallas import tpu as pltpu

F32 = jnp.float32
BF16 = jnp.bfloat16
I32 = jnp.int32
U32 = jnp.uint32

D_MODEL = 1024
GRID_W = 64
HEAD_DIM = 64
HEADS = 4
DIFF_DIM = 32
WINDOW = 128
NA_ROWS = 8
NA_COLS = 16
ROPE_BASE = 10000.0
N_EXPERTS = 16
EXPERTS_PER_GROUP = 4
D_EXPERT = 512
ADA_CHUNKS = 6
EPS = 1e-6
NEG = -1e30
D_IN = 2560

COL_A_Q, COL_A_KV, COL_B_Q, COL_B_K, COL_B_V, COL_C_Q, COL_C_K, COL_C_V, COL_D_Q, COL_D_KV = range(10)

TOKEN_TILE = 512
ATT_TQ = 256
ATT_TQ_FULL = 512
NA_QROWS = 4
NA_KROWS = 12
KEY_TILE = 256
ATT_RB = 128
ATT_SLOTS = 4
LOG2E = math.log2(math.e)
MOE_BLK = 256
ROW_DMA = 512
N_PAIRS = 6
N_CLASSES = (N_EXPERTS // EXPERTS_PER_GROUP) * N_PAIRS
PAIR_LO = (0, 0, 0, 1, 1, 2)
PAIR_HI = (1, 2, 3, 2, 3, 3)
EXT_W = D_MODEL // 2 + 128
VMEM_LIMIT = 56 * 1024 * 1024


def _cparams(*sem):
    return pltpu.CompilerParams(dimension_semantics=sem, vmem_limit_bytes=VMEM_LIMIT)


def _nt_dot(a, b):
    return lax.dot_general(a, b, (((1,), (1,)), ((), ())), preferred_element_type=F32)


def _dot(a, b):
    return jnp.dot(a, b, preferred_element_type=F32)


def _sigmoid(x):
    return 1.0 / (1.0 + jnp.exp(-x))


def _drop_arg(kern, pos):
    def wrapped(*refs):
        return kern(*refs[:pos], *refs[pos + 1:])
    return wrapped


def _ada_kernel(s_ref, w_ref, b_ref, o_ref):
    s = s_ref[...]
    act = s * _sigmoid(s)
    o_ref[0] = jnp.dot(act, w_ref[0], precision=lax.Precision.HIGHEST,
                       preferred_element_type=F32) + b_ref[0]


def _ada_call(s_all, ada_w, ada_b):
    depth, d, n = ada_w.shape
    rows = s_all.shape[0]
    tn = 1536
    return pl.pallas_call(
        _ada_kernel,
        out_shape=jax.ShapeDtypeStruct((depth, rows, n), F32),
        grid=(depth, n // tn),
        in_specs=[
            pl.BlockSpec((rows, d), lambda l, j: (0, 0)),
            pl.BlockSpec((1, d, tn), lambda l, j: (l, 0, j)),
            pl.BlockSpec((1, 1, tn), lambda l, j: (l, 0, j)),
        ],
        out_specs=pl.BlockSpec((1, rows, tn), lambda l, j: (l, 0, j)),
        compiler_params=_cparams("arbitrary", "arbitrary"),
        name="ada_mod",
    )(s_all, ada_w, ada_b.reshape(depth, 1, n))


def _start_row_gather(pos_ref, src_ref, dst_ref, sem):
    for r in range(dst_ref.shape[0]):
        pltpu.make_async_copy(src_ref.at[pl.ds(pos_ref[0, 0, r], 1)], dst_ref.at[pl.ds(r, 1)], sem).start()


def _wait_row_gather(src_ref, dst_ref, sem):
    pltpu.make_async_copy(src_ref.at[pl.ds(0, dst_ref.shape[0])], dst_ref, sem).wait()


def _prefetched_rows(pos_ref, pos_next_ref, src_ref, buf, sems):
    i = pl.program_id(0)
    slot = i % 2

    @pl.when(i == 0)
    def _():
        _start_row_gather(pos_ref, src_ref, buf.at[0], sems.at[0])

    @pl.when(i + 1 < pl.num_programs(0))
    def _():
        _start_row_gather(pos_next_ref, src_ref, buf.at[1 - slot], sems.at[1 - slot])

    _wait_row_gather(src_ref, buf.at[slot], sems.at[slot])
    return buf.at[slot]


def _inproj_kernel(*refs, has_ffn, tm, n_lat_tiles, tiles_per_batch, seq):
    if has_ffn:
        (x_ref, pos_ref, pos_next_ref, ys_ref, modp_ref, mod_ref, g_ref, w_ref, qkg_ref, bd64_ref, bd32_ref,
         c64_ref, s64_ref, c32_ref, s32_ref, xo_ref, u_ref, fbuf, fsems) = refs
        f_ref = _prefetched_rows(pos_ref, pos_next_ref, ys_ref, fbuf, fsems)
    else:
        (x_ref, mod_ref, g_ref, w_ref, qkg_ref, bd64_ref, bd32_ref,
         c64_ref, s64_ref, c32_ref, s32_ref, u_ref) = refs
    i = pl.program_id(0)
    hm = tm // 2

    def prologue(r0):
        x = x_ref[r0:r0 + hm, :]
        if has_ffn:
            x = x + modp_ref[0, 5:6, :] * f_ref[r0:r0 + hm, :]
            xo_ref[r0:r0 + hm, :] = x
        ms = jnp.mean(x * x, axis=-1, keepdims=True)
        hn = x * lax.rsqrt(ms + EPS) * g_ref[...]
        return (hn * (1.0 + mod_ref[0, 1:2, :]) + mod_ref[0, 0:1, :]).astype(BF16)

    p0 = jnp.where(i < n_lat_tiles, (i % tiles_per_batch) * tm, seq)
    p0 = pl.multiple_of(p0, tm)
    lane = lax.broadcasted_iota(I32, (hm, 128), 1)

    def norm_rope(a, r0, gidx, group, bd_ref, rope_tabs):
        w = a.shape[1]
        ssq = _dot((a * a).astype(BF16), bd_ref[0:w, 0:w])
        r = lax.rsqrt(ssq * (1.0 / group) + EPS)
        outs = []
        for s in range(w // 128):
            t = a[:, 128 * s:128 * s + 128] * r[:, 128 * s:128 * s + 128] * qkg_ref[gidx:gidx + 1, :]
            if rope_tabs is not None:
                c_ref, s_ref, half = rope_tabs
                fwd = pltpu.roll(t, 128 - half, axis=1)
                bwd = pltpu.roll(t, half, axis=1)
                sw = jnp.where((lane % (2 * half)) < half, fwd, bwd)
                t = t * c_ref[pl.ds(p0 + r0, hm), :] + sw * s_ref[pl.ds(p0 + r0, hm), :]
            outs.append(t)
        return outs[0] if len(outs) == 1 else jnp.concatenate(outs, axis=1)

    rope64 = (c64_ref, s64_ref, HEAD_DIM // 4)
    rope32 = (c32_ref, s32_ref, DIFF_DIM // 4)

    def epilogue(acc, r0, blk):
        if blk == COL_A_Q:
            out = norm_rope(acc, r0, 0, HEAD_DIM, bd64_ref, rope64)
        elif blk == COL_A_KV:
            out = jnp.concatenate([norm_rope(acc[:, :128], r0, 1, HEAD_DIM, bd64_ref, rope64), acc[:, 128:]], axis=1)
        elif blk == COL_B_Q:
            out = norm_rope(acc, r0, 2, HEAD_DIM, bd64_ref, None)
        elif blk == COL_B_K:
            out = norm_rope(acc, r0, 3, HEAD_DIM, bd64_ref, None)
        elif blk == COL_C_Q:
            out = norm_rope(acc, r0, 4, DIFF_DIM, bd32_ref, rope32)
        elif blk == COL_C_K:
            out = norm_rope(acc, r0, 5, DIFF_DIM, bd32_ref, rope32)
        elif blk == COL_D_Q:
            out = norm_rope(acc, r0, 6, HEAD_DIM, bd64_ref, rope64)
        elif blk == COL_D_KV:
            out = jnp.concatenate([norm_rope(acc[:, :128], r0, 7, HEAD_DIM, bd64_ref, rope64), acc[:, 128:]], axis=1)
        else:
            out = acc
        u_ref[r0:r0 + hm, 256 * blk:256 * blk + 256] = out.astype(BF16)

    n_blk = D_IN // 256
    units = [(r0, blk) for r0 in (0, hm) for blk in range(n_blk)]
    hbs = {0: prologue(0)}
    acc = _dot(hbs[0], w_ref[:, 0:256])
    hbs[hm] = prologue(hm)
    for t, (r0, blk) in enumerate(units):
        nxt = None
        if t + 1 < len(units):
            r1, b1 = units[t + 1]
            nxt = _dot(hbs[r1], w_ref[:, 256 * b1:256 * b1 + 256])
        epilogue(acc, r0, blk)
        acc = nxt


def _mod_spec(n_lat_tiles, tiles_per_batch, batch):
    return pl.BlockSpec((1, ADA_CHUNKS, D_MODEL),
                        lambda i: (jnp.where(i < n_lat_tiles, i // tiles_per_batch, batch), 0, 0))


def _const_spec(shape):
    return pl.BlockSpec(shape, lambda i: (0,) * len(shape))


def _pos_tile_specs(n_tiles, tm):
    return [pl.BlockSpec((1, 1, tm), lambda i: (i, 0, 0), memory_space=pltpu.SMEM),
            pl.BlockSpec((1, 1, tm), lambda i: (jnp.minimum(i + 1, n_tiles - 1), 0, 0), memory_space=pltpu.SMEM)]


def _inproj_call(x, ffn_src, mod_prev, mod, g, w_in, qkg, bd64, bd32, tabs, *, batch, seq):
    n_all = x.shape[0]
    tm = TOKEN_TILE
    n_lat_tiles = batch * seq // tm
    tpb = seq // tm
    has_ffn = ffn_src is not None
    tile = pl.BlockSpec((tm, D_MODEL), lambda i: (i, 0))
    modspec = _mod_spec(n_lat_tiles, tpb, batch)
    tab_rows = tabs[0].shape[0]
    in_specs = [tile]
    args = [x]
    scratch = []
    if has_ffn:
        ys, pos = ffn_src
        pos_tiles = pos.reshape(n_all // tm, 1, tm)
        in_specs += _pos_tile_specs(n_all // tm, tm) + [pl.BlockSpec(memory_space=pl.ANY), modspec]
        args += [pos_tiles, pos_tiles, ys, mod_prev]
        scratch = [pltpu.VMEM((2, tm, D_MODEL), F32), pltpu.SemaphoreType.DMA((2,))]
    in_specs += [modspec, _const_spec((1, D_MODEL)), _const_spec((D_MODEL, D_IN)), _const_spec((8, 128)),
                 _const_spec((256, 256)), _const_spec((256, 256))] + [_const_spec((tab_rows, 128))] * 4
    args += [mod, g, w_in, qkg, bd64, bd32] + list(tabs)
    u_shape = jax.ShapeDtypeStruct((n_all, D_IN), BF16)
    u_spec = pl.BlockSpec((tm, D_IN), lambda i: (i, 0))
    if has_ffn:
        out_shape = (jax.ShapeDtypeStruct((n_all, D_MODEL), F32), u_shape)
        out_specs = (tile, u_spec)
    else:
        out_shape, out_specs = u_shape, u_spec
    kern = functools.partial(_inproj_kernel, has_ffn=has_ffn, tm=tm, n_lat_tiles=n_lat_tiles,
                             tiles_per_batch=tpb, seq=seq)
    res = pl.pallas_call(
        kern, out_shape=out_shape, grid=(n_all // tm,), in_specs=in_specs, out_specs=out_specs,
        scratch_shapes=scratch, compiler_params=_cparams("arbitrary"), name="inproj",
    )(*args)
    return res if has_ffn else (x, res)


def _softmax_parts(scores, extra=None):
    m = scores[0].max(axis=-1, keepdims=True)
    for s in scores[1:]:
        m = jnp.maximum(m, s.max(axis=-1, keepdims=True))
    if extra is not None:
        m = jnp.maximum(m, extra)
    ps = [jnp.exp2(s - m) for s in scores]
    l = ps[0].sum(axis=-1, keepdims=True)
    for p in ps[1:]:
        l = l + p.sum(axis=-1, keepdims=True)
    if extra is not None:
        l = l + jnp.exp2(extra - m)
    return ps, 1.0 / l


class _AttItem:
    def __init__(self, q, kt_view, v_view, ranges, bias=None, extra=None):
        self.q, self.kt_view, self.v_view, self.ranges, self.bias, self.extra = q, kt_view, v_view, ranges, bias, extra
        self.n_keys = sum(n for _, n in ranges)
        self.m = None

    def scores(self, s_scr, rows):
        q_blk = self.q()
        m, off, ti = None, 0, 0
        for start, n in self.ranges:
            for o in range(0, n, KEY_TILE):
                w = min(KEY_TILE, n - o)
                s = _dot(q_blk, self.kt_view[:, pl.ds(start + o, w)])
                b = self.bias(ti) if self.bias is not None else None
                if b is not None:
                    s = s + b
                ti += 1
                s_scr[rows, off + o:off + o + w] = s
                tile_max = s.max(axis=-1, keepdims=True)
                m = tile_max if m is None else jnp.maximum(m, tile_max)
            off += n
        self.m = m if self.extra is None else jnp.maximum(m, self.extra)

    def probs(self, s_scr, p_scr, rows):
        for o in range(0, self.n_keys, KEY_TILE):
            w = min(KEY_TILE, self.n_keys - o)
            p_scr[rows, o:o + w] = jnp.exp2(s_scr[rows, o:o + w] - self.m).astype(BF16)

    def values(self, p_scr, rows):
        pv, off = None, 0
        for start, n in self.ranges:
            c = _dot(p_scr[rows, off:off + n], self.v_view[pl.ds(start, n), :])
            pv = c if pv is None else pv + c
            off += n
        denom = pv[:, 64:65]
        if self.extra is not None:
            denom = denom + jnp.exp2(self.extra - self.m)
        return pv[:, 0:64] / denom


def _run_att_items(items, s_scr, p_scr):
    def rows(t):
        r0 = (t % ATT_SLOTS) * ATT_RB
        return slice(r0, r0 + ATT_RB)

    outs = []
    for t in range(len(items) + 2):
        if t < len(items):
            items[t].scores(s_scr, rows(t))
        if 0 <= t - 1 < len(items):
            items[t - 1].probs(s_scr, p_scr, rows(t - 1))
        if 0 <= t - 2 < len(items):
            outs.append(items[t - 2].values(p_scr, rows(t - 2)))
    return outs


def _stage_heads(k_parts, v_parts, kt_scr, v_scr):
    eye = jnp.where(lax.broadcasted_iota(I32, (HEAD_DIM, HEAD_DIM), 0)
                    == lax.broadcasted_iota(I32, (HEAD_DIM, HEAD_DIM), 1), 1.0, 0.0).astype(BF16)
    for h in range(kt_scr.shape[0]):
        r0 = 0
        for (kr, kc0), (vr, vc0) in zip(k_parts, v_parts):
            n = kr.shape[0]
            kt_scr[h, :, r0:r0 + n] = _nt_dot(eye, kr[:, kc0 + 64 * h:kc0 + 64 * h + 64]).astype(BF16)
            v_scr[h, r0:r0 + n, 0:64] = vr[:, vc0 + 64 * h:vc0 + 64 * h + 64]
            r0 += n
        v_scr[h, :, 64:128] = jnp.ones((v_scr.shape[1], 64), BF16)


def _group_rms(y):
    return y * lax.rsqrt(jnp.mean(y * y, axis=-1, keepdims=True) + EPS)


def _gqa_pair(q_ref, j):
    return jnp.concatenate([q_ref[:, 128 * j:128 * j + 64], q_ref[:, 128 * j + 64:128 * j + 128]], axis=0)


def _stage_once_per_batch_row(k_parts, v_parts, k_scr, v_scr):
    @pl.when(pl.program_id(1) == 0)
    def _():
        _stage_heads(k_parts, v_parts, k_scr, v_scr)


def _row_blocks(n):
    return range(0, n, ATT_RB)


def _q_thunk(q_ref, r0, head):
    return lambda: q_ref[r0:r0 + ATT_RB, 64 * head:64 * head + 64]


def _heads_to_rows(outs, n_heads):
    per = len(outs) // n_heads
    return jnp.concatenate([jnp.concatenate(outs[h * per:(h + 1) * per], axis=0) for h in range(n_heads)], axis=1)


def _dense_kernel(q_ref, kv_ref, kvc_ref, o_ref, kt_scr, v_scr, s_scr, p_scr, *, tq):
    _stage_once_per_batch_row([(kv_ref, 0), (kvc_ref, 0)], [(kv_ref, 128), (kvc_ref, 128)], kt_scr, v_scr)
    nk = v_scr.shape[1]
    items = [_AttItem(_q_thunk(q_ref, r0, hq), kt_scr.at[hq // 2], v_scr.at[hq // 2], [(0, nk)])
             for hq in range(HEADS) for r0 in _row_blocks(tq)]
    y = _heads_to_rows(_run_att_items(items, s_scr, p_scr), HEADS)
    o_ref[...] = _group_rms(y).astype(BF16)


def _window_kernel(sink_ref, q_ref, kv_ref, kvc_ref, mask_ref, o_ref, kt_scr, v_scr, s_scr, p_scr, *, tq, seq):
    _stage_once_per_batch_row([(kv_ref, 0), (kvc_ref, 0)], [(kv_ref, 128), (kvc_ref, 128)], kt_scr, v_scr)
    kw = tq + 2 * WINDOW
    nc = v_scr.shape[1] - seq
    ks = pl.multiple_of(jnp.clip(pl.program_id(1) * tq - WINDOW, 0, seq - kw), 128)

    def mask_thunk(r0):
        return lambda ti: (mask_ref[0, r0:r0 + ATT_RB, KEY_TILE * ti:KEY_TILE * ti + KEY_TILE]
                           if KEY_TILE * ti < kw else None)

    items = [_AttItem(_q_thunk(q_ref, r0, hq), kt_scr.at[hq // 2], v_scr.at[hq // 2], [(ks, kw), (seq, nc)],
                      bias=mask_thunk(r0), extra=sink_ref[hq] * LOG2E)
             for hq in range(HEADS) for r0 in _row_blocks(tq)]
    y = _heads_to_rows(_run_att_items(items, s_scr, p_scr), HEADS)
    o_ref[...] = _group_rms(y).astype(BF16)


def _neigh_kernel(q_ref, k_ref, v_ref, kc_ref, vc_ref, bias_ref, o_ref, kt_scr, v_scr, s_scr, p_scr, *, rows, seq):
    _stage_once_per_batch_row([(k_ref, 0), (kc_ref, 0)], [(v_ref, 0), (vc_ref, 0)], kt_scr, v_scr)
    rs = jnp.clip(NA_QROWS * pl.program_id(1) - NA_ROWS // 2, 0, rows - NA_KROWS)
    k0 = pl.multiple_of(rs * GRID_W, KEY_TILE)
    nk = NA_KROWS * GRID_W
    nc = v_scr.shape[1] - seq

    def bias_thunk(h, r0):
        return lambda ti: (bias_ref[0, h, r0:r0 + ATT_RB, KEY_TILE * ti:KEY_TILE * ti + KEY_TILE]
                           if KEY_TILE * ti < nk else None)

    items = [_AttItem(_q_thunk(q_ref, r0, h), kt_scr.at[h], v_scr.at[h], [(k0, nk), (seq, nc)], bias=bias_thunk(h, r0))
             for h in range(HEADS) for r0 in _row_blocks(q_ref.shape[0])]
    y = _heads_to_rows(_run_att_items(items, s_scr, p_scr), HEADS)
    o_ref[...] = _group_rms(y).astype(BF16)


def _diff_lambda(lam_ref, lam_init):
    lp = lam_ref[...]
    a = jnp.sum(lp[0:1] * lp[1:2], axis=-1, keepdims=True)
    b = jnp.sum(lp[2:3] * lp[3:4], axis=-1, keepdims=True)
    return jnp.exp(a) - jnp.exp(b) + lam_init


def _diff_heads(q, k_parts, v_parts, lam, lam_init):
    t = q.shape[0]
    first = lax.broadcasted_iota(I32, (t, HEAD_DIM), 1) < DIFF_DIM
    zero = jnp.zeros((t, HEAD_DIM), BF16)
    outs = []
    for h in range(HEADS):
        sl = slice(64 * h, 64 * h + 64)
        qh = q[:, sl]
        ks = [kp[:, sl] for kp in k_parts]
        maps = []
        for half in range(2):
            qm = jnp.where(first, qh, zero) if half == 0 else jnp.where(first, zero, qh)
            ps, inv = _softmax_parts([_nt_dot(qm, kk) for kk in ks])
            maps.append([p * inv for p in ps])
        o = None
        for a0, a1, vp in zip(maps[0], maps[1], v_parts):
            c = _dot((a0 - lam * a1).astype(BF16), vp[:, sl])
            o = c if o is None else o + c
        outs.append(_group_rms(o) * (1.0 - lam_init))
    return jnp.concatenate(outs, axis=1)


def _diff_kernel(lam_ref, q_ref, k_ref, v_ref, kc_ref, vc_ref, o_ref, kt_scr, v_scr, s_scr, p_scr, *, lam_init, tq):
    _stage_once_per_batch_row([(k_ref, 0), (kc_ref, 0)], [(v_ref, 0), (vc_ref, 0)], kt_scr, v_scr)
    lam = _diff_lambda(lam_ref, lam_init)
    nk = v_scr.shape[1]
    first = lax.broadcasted_iota(I32, (ATT_RB, HEAD_DIM), 1) < DIFF_DIM
    zero = jnp.zeros((ATT_RB, HEAD_DIM), BF16)

    def q_map(r0, h, half):
        def thunk():
            qh = q_ref[r0:r0 + ATT_RB, 64 * h:64 * h + 64]
            return jnp.where(first, qh, zero) if half == 0 else jnp.where(first, zero, qh)
        return thunk

    items = [_AttItem(q_map(r0, h, half), kt_scr.at[h], v_scr.at[h], [(0, nk)])
             for h in range(HEADS) for r0 in _row_blocks(tq) for half in range(2)]
    outs = _run_att_items(items, s_scr, p_scr)
    diffs = [outs[i] - lam * outs[i + 1] for i in range(0, len(outs), 2)]
    per = len(diffs) // HEADS
    heads = [_group_rms(jnp.concatenate(diffs[h * per:(h + 1) * per], axis=0)) * (1.0 - lam_init)
             for h in range(HEADS)]
    o_ref[...] = jnp.concatenate(heads, axis=1).astype(BF16)


class _Cols:
    def __init__(self, ref, blk):
        self.ref, self.base = ref, 256 * blk

    def __getitem__(self, idx):
        rows, sl = idx
        return self.ref[rows, self.base + sl.start:self.base + sl.stop]


def _ctx_kernel(sink_ref, lam_ref, u_ref, o_ref, *, lam_init):
    L = u_ref.shape[0]
    top = lax.broadcasted_iota(I32, (2 * L, 1), 0) < L
    for q_blk, kv_blk, out_col, use_sink in ((COL_A_Q, COL_A_KV, 0, True), (COL_D_Q, COL_D_KV, 768, False)):
        qv, kv = _Cols(u_ref, q_blk), _Cols(u_ref, kv_blk)
        outs = [None] * HEADS
        for j in range(2):
            kc = kv[:, slice(64 * j, 64 * j + 64)]
            vc = kv[:, slice(128 + 64 * j, 128 + 64 * j + 64)]
            q2 = jnp.concatenate([qv[:, slice(128 * j, 128 * j + 64)], qv[:, slice(128 * j + 64, 128 * j + 128)]], axis=0)
            snk = jnp.where(top, sink_ref[2 * j], sink_ref[2 * j + 1]) * LOG2E if use_sink else None
            (p,), inv = _softmax_parts([_nt_dot(q2, kc)], extra=snk)
            o = _dot(p.astype(BF16), vc) * inv
            outs[2 * j], outs[2 * j + 1] = o[:L], o[L:]
        o_ref[:, out_col:out_col + 256] = _group_rms(jnp.concatenate(outs, axis=1)).astype(BF16)
    qv, kv, vv = _Cols(u_ref, COL_B_Q), _Cols(u_ref, COL_B_K), _Cols(u_ref, COL_B_V)
    outs = []
    for h in range(HEADS):
        sl = slice(64 * h, 64 * h + 64)
        (p,), inv = _softmax_parts([_nt_dot(qv[:, sl], kv[:, sl])])
        outs.append(_dot(p.astype(BF16), vv[:, sl]) * inv)
    o_ref[:, 256:512] = _group_rms(jnp.concatenate(outs, axis=1)).astype(BF16)
    lam = _diff_lambda(lam_ref, lam_init)
    qc = u_ref[:, 256 * COL_C_Q:256 * COL_C_Q + 256]
    o_ref[:, 512:768] = _diff_heads(qc, [_Cols(u_ref, COL_C_K)], [_Cols(u_ref, COL_C_V)], lam, lam_init).astype(BF16)


def _attention_calls(u, sink, lam_p, bias, wmask, *, layer, batch, seq, ctx_len, need_ctx):
    lam_init = 0.8 - 0.6 * math.exp(-0.3 * layer)
    n_all = u.shape[0]
    tq = ATT_TQ
    nq = seq // tq
    nk_all = seq + ctx_len
    cb0 = batch * seq // ctx_len
    y_shape = jax.ShapeDtypeStruct((n_all, D_MODEL), BF16)
    smem = pl.BlockSpec(memory_space=pltpu.SMEM)
    any_spec = pl.BlockSpec(memory_space=pl.ANY)
    lam_spec2 = pl.BlockSpec((4, DIFF_DIM), lambda b, i: (0, 0))
    cp = _cparams("arbitrary", "arbitrary")

    def scratch(n_heads, n_keys):
        return [pltpu.VMEM((n_heads, HEAD_DIM, nk_all), BF16), pltpu.VMEM((n_heads, nk_all, 128), BF16),
                pltpu.VMEM((ATT_SLOTS * ATT_RB, n_keys), F32), pltpu.VMEM((ATT_SLOTS * ATT_RB, n_keys), BF16)]

    def qspec(col, t=tq):
        return pl.BlockSpec((t, 256), lambda b, i: (b * (seq // t) + i, col))

    tqf = ATT_TQ_FULL
    nqf = seq // tqf

    def latspec(col):
        return pl.BlockSpec((seq, 256), lambda b, i: (b, col))

    def ctxspec(col):
        return pl.BlockSpec((ctx_len, 256), lambda b, i: (cb0 + b, col))

    kw = tq + 2 * WINDOW

    def first_mid_last(n):
        return lambda b, i: (jnp.where(i == 0, 0, jnp.where(i == n - 1, 2, 1)),)

    wcase = first_mid_last(nq)
    y = pl.pallas_call(
        functools.partial(_window_kernel, tq=tq, seq=seq),
        out_shape=y_shape, grid=(batch, nq),
        in_specs=[smem, qspec(COL_A_Q), latspec(COL_A_KV), ctxspec(COL_A_KV),
                  pl.BlockSpec((1, tq, kw), lambda b, i: wcase(b, i) + (0, 0))],
        out_specs=qspec(0), scratch_shapes=scratch(2, kw + ctx_len),
        compiler_params=cp, name="mix_window",
    )(sink, u, u, u, wmask)

    rows = seq // GRID_W
    nrb = rows // NA_QROWS
    qrows = NA_QROWS * GRID_W
    nkb = NA_KROWS * GRID_W
    bcase = first_mid_last(nrb)
    y = pl.pallas_call(
        _drop_arg(functools.partial(_neigh_kernel, rows=rows, seq=seq), 6),
        out_shape=y_shape, grid=(batch, nrb),
        in_specs=[pl.BlockSpec((qrows, 256), lambda b, i: (b * nrb + i, COL_B_Q)),
                  latspec(COL_B_K), latspec(COL_B_V), ctxspec(COL_B_K), ctxspec(COL_B_V),
                  pl.BlockSpec((1, HEADS, qrows, nkb), lambda b, i: bcase(b, i) + (0, 0, 0)), any_spec],
        out_specs=pl.BlockSpec((qrows, 256), lambda b, i: (b * nrb + i, 1)),
        scratch_shapes=scratch(HEADS, nkb + ctx_len),
        input_output_aliases={6: 0}, compiler_params=cp, name="mix_neigh",
    )(u, u, u, u, u, bias, y)

    y = pl.pallas_call(
        _drop_arg(functools.partial(_diff_kernel, lam_init=lam_init, tq=tqf), 6),
        out_shape=y_shape, grid=(batch, nqf),
        in_specs=[lam_spec2, qspec(COL_C_Q, tqf), latspec(COL_C_K), latspec(COL_C_V),
                  ctxspec(COL_C_K), ctxspec(COL_C_V), any_spec],
        out_specs=qspec(2, tqf), scratch_shapes=scratch(HEADS, nk_all),
        input_output_aliases={6: 0}, compiler_params=cp, name="mix_diff",
    )(lam_p, u, u, u, u, u, y)

    y = pl.pallas_call(
        _drop_arg(functools.partial(_dense_kernel, tq=tqf), 3),
        out_shape=y_shape, grid=(batch, nqf),
        in_specs=[qspec(COL_D_Q, tqf), latspec(COL_D_KV), ctxspec(COL_D_KV), any_spec],
        out_specs=qspec(3, tqf), scratch_shapes=scratch(2, nk_all),
        input_output_aliases={3: 0}, compiler_params=cp, name="mix_dense",
    )(u, u, u, y)

    if need_ctx:
        y = pl.pallas_call(
            _drop_arg(functools.partial(_ctx_kernel, lam_init=lam_init), 3),
            out_shape=y_shape, grid=(batch,),
            in_specs=[smem, pl.BlockSpec((4, DIFF_DIM), lambda b: (0, 0)),
                      pl.BlockSpec((ctx_len, D_IN), lambda b: (cb0 + b, 0)), any_spec],
            out_specs=pl.BlockSpec((ctx_len, D_MODEL), lambda b: (cb0 + b, 0)),
            input_output_aliases={3: 0}, compiler_params=_cparams("arbitrary"), name="mix_ctx",
        )(sink, lam_p, u, y)
    return y


def _select4(idx, vals):
    return jnp.where(idx == 0, vals[0], jnp.where(idx == 1, vals[1], jnp.where(idx == 2, vals[2], vals[3])))


def _outproj_kernel(y_ref, x_ref, mod_ref, gain_ref, w_ref, g_ref, rw_ref, rb_ref, xo_ref, h_ref, cls_ref, *, tm):
    hm = tm // 2

    def project(r0):
        yg = (y_ref[r0:r0 + hm, :].astype(F32) * gain_ref[...]).astype(BF16)
        return _dot(yg, w_ref[...])

    def normalise(r0, proj):
        xm = x_ref[r0:r0 + hm, :] + mod_ref[0, 2:3, :] * proj
        xo_ref[r0:r0 + hm, :] = xm
        hn = xm * lax.rsqrt(jnp.mean(xm * xm, axis=-1, keepdims=True) + EPS) * g_ref[...]
        h2 = hn * (1.0 + mod_ref[0, 4:5, :]) + mod_ref[0, 3:4, :]
        hi = h2.astype(BF16)
        hi32 = hi.astype(F32)
        lo = (h2 - hi32).astype(BF16)
        bits = lax.bitcast_convert_type(hi32, U32)
        half = D_MODEL // 2
        h_ref[r0:r0 + hm, 0:half] = (bits[:, 0:half] >> 16) | (bits[:, half:D_MODEL] & jnp.uint32(0xFFFF0000))
        return (_dot(hi, rw_ref[...]) + _dot(lo, rw_ref[...])).T

    def route(r0, lt):
        score = _sigmoid(lt[0:N_EXPERTS] + lt[N_EXPERTS:2 * N_EXPERTS])
        sel = score + rb_ref[...]
        srow = [sel[e:e + 1] for e in range(N_EXPERTS)]
        crow = [score[e:e + 1] for e in range(N_EXPERTS)]

        gidx = jnp.zeros((1, hm), I32)
        gbest = None
        for g in range(N_EXPERTS // EXPERTS_PER_GROUP):
            v = srow[4 * g:4 * g + 4]
            best = None
            for a in range(4):
                for b in range(a + 1, 4):
                    t = v[a] + v[b]
                    best = t if best is None else jnp.maximum(best, t)
            if g == 0:
                gbest = best
            else:
                better = best > gbest
                gidx = jnp.where(better, g, gidx)
                gbest = jnp.where(better, best, gbest)

        iv = [_select4(gidx, [srow[4 * g + i] for g in range(4)]) for i in range(4)]
        sv = [_select4(gidx, [crow[4 * g + i] for g in range(4)]) for i in range(4)]
        chosen = []
        for i in range(4):
            rank = jnp.zeros((1, hm), I32)
            for j in range(4):
                if j != i:
                    beats = (iv[j] >= iv[i]) if j < i else (iv[j] > iv[i])
                    rank = rank + jnp.where(beats, 1, 0)
            chosen.append(rank < 2)
        lo_i = jnp.where(chosen[0], 0, jnp.where(chosen[1], 1, 2))
        hi_i = jnp.where(chosen[3], 3, jnp.where(chosen[2], 2, 1))
        pair = jnp.where(lo_i == 0, hi_i - 1, jnp.where(lo_i == 1, hi_i + 1, 5))
        cls_ref[0, :, r0:r0 + hm] = gidx * N_PAIRS + pair
        s_lo = jnp.where(lo_i == 0, sv[0], jnp.where(lo_i == 1, sv[1], sv[2]))
        s_hi = jnp.where(hi_i == 3, sv[3], jnp.where(hi_i == 2, sv[2], sv[1]))
        inv = 1.0 / (s_lo + s_hi)
        rowi = lax.broadcasted_iota(I32, (128, hm), 0)
        wts = jnp.where(rowi == 0, s_lo * inv, jnp.where(rowi == 1, s_hi * inv, 0.0))
        h_ref[r0:r0 + hm, D_MODEL // 2:EXT_W] = lax.bitcast_convert_type(wts.T, U32)

    proj0 = project(0)
    proj1 = project(hm)
    lt0 = normalise(0, proj0)
    lt1 = normalise(hm, proj1)
    route(0, lt0)
    route(hm, lt1)


def _outproj_call(y, x, mod, gain, w_out, g, rw, rb, *, n_proc, batch, seq):
    tm = TOKEN_TILE
    n_tiles = n_proc // tm
    tile = pl.BlockSpec((tm, D_MODEL), lambda i: (i, 0))
    return pl.pallas_call(
        functools.partial(_outproj_kernel, tm=tm),
        out_shape=(jax.ShapeDtypeStruct((n_proc, D_MODEL), F32),
                   jax.ShapeDtypeStruct((n_proc, EXT_W), U32),
                   jax.ShapeDtypeStruct((n_tiles, 1, tm), I32)),
        grid=(n_tiles,),
        in_specs=[tile, tile, _mod_spec(batch * seq // tm, seq // tm, batch), _const_spec((1, D_MODEL)),
                  _const_spec((D_MODEL, D_MODEL)), _const_spec((1, D_MODEL)), _const_spec((D_MODEL, 128)),
                  _const_spec((N_EXPERTS, 1))],
        out_specs=(tile, pl.BlockSpec((tm, EXT_W), lambda i: (i, 0)), pl.BlockSpec((1, 1, tm), lambda i: (i, 0, 0))),
        compiler_params=_cparams("arbitrary"), name="outproj_router",
    )(y, x, mod, gain, w_out, g, rw, rb)


def _rank_kernel(cls_ref, pos_ref, bcls_ref, nval_ref, *, blk):
    c = cls_ref[...]
    nr = c.shape[0]
    nbp = bcls_ref.shape[1]
    upper = (lax.broadcasted_iota(I32, (128, 128), 0) <= lax.broadcasted_iota(I32, (128, 128), 1))
    upper = jnp.where(upper, 1.0, 0.0).astype(BF16)
    lower = (lax.broadcasted_iota(I32, (nr, nr), 1) < lax.broadcasted_iota(I32, (nr, nr), 0))
    lower = jnp.where(lower, 1.0, 0.0).astype(BF16)
    bidx = lax.broadcasted_iota(I32, (1, nbp), 1).astype(F32)
    pos = jnp.zeros((nr, 128), F32)
    bcls = jnp.zeros((1, nbp), F32)
    nval = jnp.zeros((1, nbp), F32)
    start = jnp.zeros((1, 1), F32)
    for k in range(N_CLASSES):
        m = c == k
        incl = _dot(jnp.where(m, 1.0, 0.0).astype(BF16), upper)
        tot = incl[:, 127:128]
        above = _dot(lower, jnp.broadcast_to(tot, (nr, 128)).astype(BF16))
        cnt = above[nr - 1:nr, 0:1] + tot[nr - 1:nr, 0:1]
        nblk = jnp.floor((cnt + (blk - 1)) * (1.0 / blk))
        pos = jnp.where(m, start * blk + above + incl - 1.0, pos)
        end = start + nblk
        bcls = bcls + jnp.where(bidx >= end, 1.0, 0.0)
        nval = nval + jnp.where((bidx >= start) & (bidx < end), jnp.clip(cnt - (bidx - start) * blk, 0.0, blk), 0.0)
        start = end
    pos_ref[...] = pos.astype(I32)
    bcls_ref[...] = jnp.minimum(bcls, N_CLASSES - 1.0).astype(I32)
    nval_ref[...] = nval.astype(I32)


def _rank_call(cls2d, n_blocks):
    nr = cls2d.shape[0]
    nbp = -(-n_blocks // 128) * 128
    return pl.pallas_call(
        functools.partial(_rank_kernel, blk=MOE_BLK),
        out_shape=(jax.ShapeDtypeStruct((nr, 128), I32), jax.ShapeDtypeStruct((1, nbp), I32),
                   jax.ShapeDtypeStruct((1, nbp), I32)),
        grid=(1,),
        in_specs=[_const_spec((nr, 128))],
        out_specs=(_const_spec((nr, 128)), _const_spec((1, nbp)), _const_spec((1, nbp))),
        compiler_params=_cparams("arbitrary"), name="class_rank",
    )(cls2d)


def _scatter_kernel(pos_ref, h_ref, xs_ref, sem):
    n = h_ref.shape[0]
    for r in range(n):
        pltpu.make_async_copy(h_ref.at[pl.ds(r, 1)], xs_ref.at[pl.ds(pos_ref[0, 0, r], 1)], sem).start()
    pltpu.make_async_copy(h_ref, xs_ref.at[pl.ds(0, n)], sem).wait()


def _scatter_call(h_ext, pos3, n_slots):
    n, w = h_ext.shape
    r = ROW_DMA
    return pl.pallas_call(
        _scatter_kernel,
        out_shape=jax.ShapeDtypeStruct((n_slots, w), h_ext.dtype),
        grid=(n // r,),
        in_specs=[pl.BlockSpec((1, 1, r), lambda i: (i, 0, 0), memory_space=pltpu.SMEM),
                  pl.BlockSpec((r, w), lambda i: (i, 0))],
        out_specs=pl.BlockSpec(memory_space=pl.ANY),
        scratch_shapes=[pltpu.SemaphoreType.DMA(())],
        compiler_params=_cparams("arbitrary"), name="row_scatter",
    )(pos3, h_ext)


def _moe_kernel(elo_ref, ehi_ref, nval_ref, xs_ref, wg_lo, wu_lo, wd_lo, wg_hi, wu_hi, wd_hi, ys_ref):
    del elo_ref, ehi_ref
    nv = nval_ref[pl.program_id(0)]

    @pl.when(nv > 0)
    def _():
        live = lax.broadcasted_iota(I32, (MOE_BLK, 1), 0) < nv
        half = D_MODEL // 2
        words = jnp.where(live, xs_ref[:, 0:half], jnp.uint32(0))
        x = jnp.concatenate([lax.bitcast_convert_type(words << 16, F32).astype(BF16),
                             lax.bitcast_convert_type(words & jnp.uint32(0xFFFF0000), F32).astype(BF16)], axis=1)
        acts = []
        for wg, wu, col in ((wg_lo, wu_lo, half), (wg_hi, wu_hi, half + 1)):
            w = jnp.where(live, lax.bitcast_convert_type(xs_ref[:, col:col + 1], F32), 0.0)
            g = _dot(x, wg[0])
            acts.append((g * _sigmoid(g) * _dot(x, wu[0]) * w).astype(BF16))
        ys_ref[...] = _dot(acts[0], wd_lo[0]) + _dot(acts[1], wd_hi[0])

    @pl.when(nv == 0)
    def _():
        ys_ref[...] = jnp.zeros_like(ys_ref)


def _moe_call(xs, e_lo, e_hi, nval, w_gate, w_up, w_down):
    n_blocks = xs.shape[0] // MOE_BLK

    def wspec(shape, which):
        return pl.BlockSpec((1,) + shape, lambda b, elo, ehi, nv: ((elo, ehi)[which][b], 0, 0))

    gu, dn = (D_MODEL, D_EXPERT), (D_EXPERT, D_MODEL)
    grid_spec = pltpu.PrefetchScalarGridSpec(
        num_scalar_prefetch=3, grid=(n_blocks,),
        in_specs=[pl.BlockSpec((MOE_BLK, EXT_W), lambda b, elo, ehi, nv: (b, 0)),
                  wspec(gu, 0), wspec(gu, 0), wspec(dn, 0), wspec(gu, 1), wspec(gu, 1), wspec(dn, 1)],
        out_specs=pl.BlockSpec((MOE_BLK, D_MODEL), lambda b, elo, ehi, nv: (b, 0)),
    )
    return pl.pallas_call(
        _moe_kernel, out_shape=jax.ShapeDtypeStruct((xs.shape[0], D_MODEL), F32), grid_spec=grid_spec,
        compiler_params=_cparams("arbitrary"), name="moe_experts",
    )(e_lo, e_hi, nval, xs, w_gate, w_up, w_down, w_gate, w_up, w_down)


def _final_kernel(x_ref, pos_ref, pos_next_ref, ys_ref, mod_ref, o_ref, fbuf, fsems):
    f_ref = _prefetched_rows(pos_ref, pos_next_ref, ys_ref, fbuf, fsems)
    o_ref[...] = x_ref[...] + mod_ref[0, 5:6, :] * f_ref[...]


def _final_call(x, ys, pos, mod, *, batch, seq):
    tm = TOKEN_TILE
    n = batch * seq
    tile = pl.BlockSpec((tm, D_MODEL), lambda i: (i, 0))
    pos_tiles = pos.reshape(n // tm, 1, tm)
    return pl.pallas_call(
        _final_kernel, out_shape=jax.ShapeDtypeStruct((n, D_MODEL), F32), grid=(n // tm,),
        in_specs=[tile] + _pos_tile_specs(n // tm, tm) + [pl.BlockSpec(memory_space=pl.ANY),
                                                          _mod_spec(n // tm, seq // tm, batch)],
        out_specs=tile, scratch_shapes=[pltpu.VMEM((2, tm, D_MODEL), F32), pltpu.SemaphoreType.DMA((2,))],
        compiler_params=_cparams("arbitrary"), name="final_residual",
    )(x, pos_tiles, pos_tiles, ys, mod)


def _rope_tables(seq, ident_rows):
    pos = jnp.arange(seq, dtype=jnp.int32)
    row = (pos // GRID_W).astype(F32)[:, None]
    col = (pos % GRID_W).astype(F32)[:, None]

    def tab(half):
        inv = ROPE_BASE ** (-jnp.arange(half, dtype=F32) / half)
        ar, ac = row * inv, col * inv
        cos = jnp.concatenate([jnp.cos(ar), jnp.cos(ar), jnp.cos(ac), jnp.cos(ac)], axis=1)
        sin = jnp.concatenate([-jnp.sin(ar), jnp.sin(ar), -jnp.sin(ac), jnp.sin(ac)], axis=1)
        reps = 128 // (4 * half)
        cos = jnp.concatenate([jnp.tile(cos, (1, reps)), jnp.ones((ident_rows, 128), F32)], axis=0)
        sin = jnp.concatenate([jnp.tile(sin, (1, reps)), jnp.zeros((ident_rows, 128), F32)], axis=0)
        return cos, sin

    c64, s64 = tab(HEAD_DIM // 4)
    c32, s32 = tab(DIFF_DIM // 4)
    return c64, s64, c32, s32


def _block_diag_ones(group):
    idx = np.arange(256) // group
    return jnp.asarray((idx[:, None] == idx[None, :]).astype(np.float32), dtype=BF16)


def _qk_gains(g_win, g_na, g_diff, g_gqa):
    s64, s32 = HEAD_DIM ** -0.5 * LOG2E, DIFF_DIM ** -0.5 * LOG2E
    rows = [jnp.tile(g_win[0] * s64, 2), jnp.tile(g_win[1], 2), jnp.tile(g_na[0] * s64, 2), jnp.tile(g_na[1], 2),
            jnp.tile(g_diff[0] * s32, 4), jnp.tile(g_diff[1], 4), jnp.tile(g_gqa[0] * s64, 2), jnp.tile(g_gqa[1], 2)]
    return jnp.stack(rows).astype(F32)


def _na_bias_tiles(rpb, rows):
    w = GRID_W
    qc = np.arange(w)[:, None]
    kc = np.arange(w)[None, :]
    c_start = np.clip(qc - NA_COLS // 2, 0, w - NA_COLS)
    col_ok = (kc >= c_start) & (kc < c_start + NA_COLS)
    dc = np.clip(kc - qc + NA_COLS - 1, 0, 2 * NA_COLS - 2)
    t = jnp.where(jnp.asarray(col_ok), rpb.astype(F32)[:, :, dc] * LOG2E, NEG)
    neg = jnp.full((rpb.shape[0], w, w), NEG, F32)
    cases = []
    for r0, rs in ((0, 0), (NA_QROWS, 0), (rows - NA_QROWS, rows - NA_KROWS)):
        qtiles = []
        for i in range(NA_QROWS):
            qrow = r0 + i
            r_start = min(max(qrow - NA_ROWS // 2, 0), rows - NA_ROWS)
            blks = []
            for j in range(NA_KROWS):
                krow = rs + j
                blks.append(t[:, krow - qrow + NA_ROWS - 1] if r_start <= krow < r_start + NA_ROWS else neg)
            qtiles.append(jnp.concatenate(blks, axis=2))
        cases.append(jnp.concatenate(qtiles, axis=1))
    return jnp.stack(cases)


def _window_mask_tiles(tq):
    kw = tq + 2 * WINDOW
    r = lax.broadcasted_iota(I32, (3, tq, kw), 1)
    c = lax.broadcasted_iota(I32, (3, tq, kw), 2)
    shift = lax.broadcasted_iota(I32, (3, tq, kw), 0) * WINDOW
    return jnp.where(jnp.abs(c - r - shift) <= WINDOW, 0.0, NEG).astype(F32)


def _router_weights(router_w):
    hi = router_w.astype(BF16)
    lo = (router_w - hi.astype(F32)).astype(BF16)
    pad = jnp.zeros((router_w.shape[0], 128 - 2 * N_EXPERTS), BF16)
    return jnp.concatenate([hi, lo, pad], axis=1)


def kernel(x, c, ctx, c_ctx, ada_w, ada_b, norm_mix_g, norm_ffn_g, w_in, qk_g_win, qk_g_na, qk_g_diff, qk_g_gqa,
           sink_win, rpb_na, lambda_diff, out_gain, w_out, router_w, router_b, w_gate, w_up, w_down):
    batch, seq, d = x.shape
    ctx_len = ctx.shape[1]
    depth = w_in.shape[0]
    n_lat, n_ctx = batch * seq, batch * ctx_len
    n_all = n_lat + n_ctx
    rows = seq // GRID_W
    assert d == D_MODEL and seq % TOKEN_TILE == 0 and n_ctx % TOKEN_TILE == 0 and ctx_len % 128 == 0
    assert rows % NA_QROWS == 0 and rows >= NA_KROWS + NA_QROWS and seq >= ATT_TQ + 2 * WINDOW
    assert ctx_len % 128 == 0 and (NA_KROWS * GRID_W) % KEY_TILE == 0

    pad_rows = -(-(batch + 1) // 8) * 8
    s_all = jnp.concatenate([c, c_ctx[None, :], jnp.zeros((pad_rows - batch - 1, d), F32)], axis=0)
    mods = _ada_call(s_all, ada_w, ada_b).reshape(depth, pad_rows, ADA_CHUNKS, d)

    xs_all = jnp.concatenate([x.reshape(n_lat, d), ctx.reshape(n_ctx, d)], axis=0)
    tabs = _rope_tables(seq, TOKEN_TILE)
    wmask = _window_mask_tiles(ATT_TQ)
    bd64, bd32 = _block_diag_ones(HEAD_DIM), _block_diag_ones(DIFF_DIM)
    rw = _router_weights(router_w)
    rb = router_b.astype(F32).reshape(N_EXPERTS, 1)
    lo_tab = jnp.asarray([EXPERTS_PER_GROUP * (k // N_PAIRS) + PAIR_LO[k % N_PAIRS] for k in range(N_CLASSES)], I32)
    hi_tab = jnp.asarray([EXPERTS_PER_GROUP * (k // N_PAIRS) + PAIR_HI[k % N_PAIRS] for k in range(N_CLASSES)], I32)

    ffn, mod_prev = None, None
    for layer in range(depth):
        need_ctx = layer < depth - 1
        mod = mods[layer]
        qkg = _qk_gains(qk_g_win[layer], qk_g_na[layer], qk_g_diff[layer], qk_g_gqa[layer])
        xs_all, u = _inproj_call(xs_all, ffn, mod_prev, mod, norm_mix_g[layer].reshape(1, d),
                                 w_in[layer].astype(BF16), qkg, bd64, bd32, tabs, batch=batch, seq=seq)
        y = _attention_calls(u, sink_win[layer].astype(F32), lambda_diff[layer].astype(F32),
                             _na_bias_tiles(rpb_na[layer], rows), wmask, layer=layer, batch=batch, seq=seq,
                             ctx_len=ctx_len, need_ctx=need_ctx)
        n_proc = n_all if need_ctx else n_lat
        x_mid, h_ext, cls = _outproj_call(y, xs_all, mod, out_gain[layer].reshape(1, d), w_out[layer].astype(BF16),
                                          norm_ffn_g[layer].reshape(1, d), rw, rb, n_proc=n_proc, batch=batch, seq=seq)
        n_blocks = n_proc // MOE_BLK + N_CLASSES
        pos, bcls, nval = _rank_call(cls.reshape(n_proc // 128, 128), n_blocks)
        pos3 = pos.reshape(n_proc // ROW_DMA, 1, ROW_DMA)
        bcls, nval = bcls[0, :n_blocks], nval[0, :n_blocks]
        sorted_rows = _scatter_call(h_ext, pos3, n_blocks * MOE_BLK)
        ys = _moe_call(sorted_rows, lo_tab[bcls], hi_tab[bcls], nval, w_gate[layer].astype(BF16),
                       w_up[layer].astype(BF16), w_down[layer].astype(BF16))
        ffn = (ys, pos)
        xs_all, mod_prev = x_mid, mod
    out = _final_call(xs_all, ffn[0], ffn[1], mod_prev, batch=batch, seq=seq)
    return out.reshape(batch, seq, d)
```

```python
import functools
import math

import numpy as np
import jax
import jax.numpy as jnp
from jax import lax
from jax.experimental import pallas as pl
from jax.experimental.pallas import tpu as pltpu

F32 = jnp.float32
BF16 = jnp.bfloat16
I32 = jnp.int32
U32 = jnp.uint32

D_MODEL = 1024
GRID_W = 64
HEAD_DIM = 64
HEADS = 4
DIFF_DIM = 32
WINDOW = 128
NA_ROWS = 8
NA_COLS = 16
ROPE_BASE = 10000.0
N_EXPERTS = 16
EXPERTS_PER_GROUP = 4
D_EXPERT = 512
ADA_CHUNKS = 6
EPS = 1e-6
NEG = -1e30
D_IN = 2560

COL_A_Q, COL_A_KV, COL_B_Q, COL_B_K, COL_B_V, COL_C_Q, COL_C_K, COL_C_V, COL_D_Q, COL_D_KV = range(10)

TOKEN_TILE = 512
ATT_TQ = 256
ATT_TQ_FULL = 512
NA_QROWS = 4
NA_KROWS = 12
KEY_TILE = 256
ATT_RB = 128
ATT_SLOTS = 4
WINDOW_TILES = 1
NA_TILES = 2
LOG2E = math.log2(math.e)
MOE_BLK = 256
ROW_DMA = 512
N_PAIRS = 6
N_CLASSES = (N_EXPERTS // EXPERTS_PER_GROUP) * N_PAIRS
PAIR_LO = (0, 0, 0, 1, 1, 2)
PAIR_HI = (1, 2, 3, 2, 3, 3)
EXT_W = D_MODEL // 2 + 128
VMEM_LIMIT = 56 * 1024 * 1024


def _cparams(*sem):
    return pltpu.CompilerParams(dimension_semantics=sem, vmem_limit_bytes=VMEM_LIMIT)


def _nt_dot(a, b):
    return lax.dot_general(a, b, (((1,), (1,)), ((), ())), preferred_element_type=F32)


def _dot(a, b):
    return jnp.dot(a, b, preferred_element_type=F32)


def _sigmoid(x):
    return 1.0 / (1.0 + jnp.exp(-x))


def _drop_arg(kern, pos):
    def wrapped(*refs):
        return kern(*refs[:pos], *refs[pos + 1:])
    return wrapped


def _ada_kernel(s_ref, w_ref, b_ref, o_ref):
    s = s_ref[...]
    act = s * _sigmoid(s)
    o_ref[0] = jnp.dot(act, w_ref[0], precision=lax.Precision.HIGHEST,
                       preferred_element_type=F32) + b_ref[0]


def _ada_call(s_all, ada_w, ada_b):
    depth, d, n = ada_w.shape
    rows = s_all.shape[0]
    tn = 1536
    return pl.pallas_call(
        _ada_kernel,
        out_shape=jax.ShapeDtypeStruct((depth, rows, n), F32),
        grid=(depth, n // tn),
        in_specs=[
            pl.BlockSpec((rows, d), lambda l, j: (0, 0)),
            pl.BlockSpec((1, d, tn), lambda l, j: (l, 0, j)),
            pl.BlockSpec((1, 1, tn), lambda l, j: (l, 0, j)),
        ],
        out_specs=pl.BlockSpec((1, rows, tn), lambda l, j: (l, 0, j)),
        compiler_params=_cparams("arbitrary", "arbitrary"),
        name="ada_mod",
    )(s_all, ada_w, ada_b.reshape(depth, 1, n))


def _start_row_gather(pos_ref, src_ref, dst_ref, sem):
    for r in range(dst_ref.shape[0]):
        pltpu.make_async_copy(src_ref.at[pl.ds(pos_ref[0, 0, r], 1)], dst_ref.at[pl.ds(r, 1)], sem).start()


def _wait_row_gather(src_ref, dst_ref, sem):
    pltpu.make_async_copy(src_ref.at[pl.ds(0, dst_ref.shape[0])], dst_ref, sem).wait()


def _prefetched_rows(pos_ref, pos_next_ref, src_ref, buf, sems):
    i = pl.program_id(0)
    slot = i % 2

    @pl.when(i == 0)
    def _():
        _start_row_gather(pos_ref, src_ref, buf.at[0], sems.at[0])

    @pl.when(i + 1 < pl.num_programs(0))
    def _():
        _start_row_gather(pos_next_ref, src_ref, buf.at[1 - slot], sems.at[1 - slot])

    _wait_row_gather(src_ref, buf.at[slot], sems.at[slot])
    return buf.at[slot]


def _inproj_kernel(*refs, has_ffn, tm, n_lat_tiles, tiles_per_batch, seq):
    if has_ffn:
        (x_ref, pos_ref, pos_next_ref, ys_ref, modp_ref, mod_ref, g_ref, w_ref, qkg_ref, bd64_ref, bd32_ref,
         c64_ref, s64_ref, c32_ref, s32_ref, xo_ref, u_ref, fbuf, fsems) = refs
        f_ref = _prefetched_rows(pos_ref, pos_next_ref, ys_ref, fbuf, fsems)
    else:
        (x_ref, mod_ref, g_ref, w_ref, qkg_ref, bd64_ref, bd32_ref,
         c64_ref, s64_ref, c32_ref, s32_ref, u_ref) = refs
    i = pl.program_id(0)
    hm = tm // 2

    def prologue(r0):
        x = x_ref[r0:r0 + hm, :]
        if has_ffn:
            x = x + modp_ref[0, 5:6, :] * f_ref[r0:r0 + hm, :]
            xo_ref[r0:r0 + hm, :] = x
        ms = jnp.mean(x * x, axis=-1, keepdims=True)
        hn = x * lax.rsqrt(ms + EPS) * g_ref[...]
        return (hn * (1.0 + mod_ref[0, 1:2, :]) + mod_ref[0, 0:1, :]).astype(BF16)

    p0 = jnp.where(i < n_lat_tiles, (i % tiles_per_batch) * tm, seq)
    p0 = pl.multiple_of(p0, tm)
    lane = lax.broadcasted_iota(I32, (hm, 128), 1)

    def norm_rope(a, r0, gidx, group, bd_ref, rope_tabs):
        w = a.shape[1]
        ssq = _dot((a * a).astype(BF16), bd_ref[0:w, 0:w])
        r = lax.rsqrt(ssq * (1.0 / group) + EPS)
        outs = []
        for s in range(w // 128):
            t = a[:, 128 * s:128 * s + 128] * r[:, 128 * s:128 * s + 128] * qkg_ref[gidx:gidx + 1, :]
            if rope_tabs is not None:
                c_ref, s_ref, half = rope_tabs
                fwd = pltpu.roll(t, 128 - half, axis=1)
                bwd = pltpu.roll(t, half, axis=1)
                sw = jnp.where((lane % (2 * half)) < half, fwd, bwd)
                t = t * c_ref[pl.ds(p0 + r0, hm), :] + sw * s_ref[pl.ds(p0 + r0, hm), :]
            outs.append(t)
        return outs[0] if len(outs) == 1 else jnp.concatenate(outs, axis=1)

    rope64 = (c64_ref, s64_ref, HEAD_DIM // 4)
    rope32 = (c32_ref, s32_ref, DIFF_DIM // 4)

    def epilogue(acc, r0, blk):
        if blk == COL_A_Q:
            out = norm_rope(acc, r0, 0, HEAD_DIM, bd64_ref, rope64)
        elif blk == COL_A_KV:
            out = jnp.concatenate([norm_rope(acc[:, :128], r0, 1, HEAD_DIM, bd64_ref, rope64), acc[:, 128:]], axis=1)
        elif blk == COL_B_Q:
            out = norm_rope(acc, r0, 2, HEAD_DIM, bd64_ref, None)
        elif blk == COL_B_K:
            out = norm_rope(acc, r0, 3, HEAD_DIM, bd64_ref, None)
        elif blk == COL_C_Q:
            out = norm_rope(acc, r0, 4, DIFF_DIM, bd32_ref, rope32)
        elif blk == COL_C_K:
            out = norm_rope(acc, r0, 5, DIFF_DIM, bd32_ref, rope32)
        elif blk == COL_D_Q:
            out = norm_rope(acc, r0, 6, HEAD_DIM, bd64_ref, rope64)
        elif blk == COL_D_KV:
            out = jnp.concatenate([norm_rope(acc[:, :128], r0, 7, HEAD_DIM, bd64_ref, rope64), acc[:, 128:]], axis=1)
        else:
            out = acc
        u_ref[r0:r0 + hm, 256 * blk:256 * blk + 256] = out.astype(BF16)

    n_blk = D_IN // 256
    units = [(r0, blk) for r0 in (0, hm) for blk in range(n_blk)]
    hbs = {0: prologue(0)}
    acc = _dot(hbs[0], w_ref[:, 0:256])
    hbs[hm] = prologue(hm)
    for t, (r0, blk) in enumerate(units):
        nxt = None
        if t + 1 < len(units):
            r1, b1 = units[t + 1]
            nxt = _dot(hbs[r1], w_ref[:, 256 * b1:256 * b1 + 256])
        epilogue(acc, r0, blk)
        acc = nxt


def _mod_spec(n_lat_tiles, tiles_per_batch, batch):
    return pl.BlockSpec((1, ADA_CHUNKS, D_MODEL),
                        lambda i: (jnp.where(i < n_lat_tiles, i // tiles_per_batch, batch), 0, 0))


def _const_spec(shape):
    return pl.BlockSpec(shape, lambda i: (0,) * len(shape))


def _pos_tile_specs(n_tiles, tm):
    return [pl.BlockSpec((1, 1, tm), lambda i: (i, 0, 0), memory_space=pltpu.SMEM),
            pl.BlockSpec((1, 1, tm), lambda i: (jnp.minimum(i + 1, n_tiles - 1), 0, 0), memory_space=pltpu.SMEM)]


def _inproj_call(x, ffn_src, mod_prev, mod, g, w_in, qkg, bd64, bd32, tabs, *, batch, seq):
    n_all = x.shape[0]
    tm = TOKEN_TILE
    n_lat_tiles = batch * seq // tm
    tpb = seq // tm
    has_ffn = ffn_src is not None
    tile = pl.BlockSpec((tm, D_MODEL), lambda i: (i, 0))
    modspec = _mod_spec(n_lat_tiles, tpb, batch)
    tab_rows = tabs[0].shape[0]
    in_specs = [tile]
    args = [x]
    scratch = []
    if has_ffn:
        ys, pos = ffn_src
        pos_tiles = pos.reshape(n_all // tm, 1, tm)
        in_specs += _pos_tile_specs(n_all // tm, tm) + [pl.BlockSpec(memory_space=pl.ANY), modspec]
        args += [pos_tiles, pos_tiles, ys, mod_prev]
        scratch = [pltpu.VMEM((2, tm, D_MODEL), F32), pltpu.SemaphoreType.DMA((2,))]
    in_specs += [modspec, _const_spec((1, D_MODEL)), _const_spec((D_MODEL, D_IN)), _const_spec((8, 128)),
                 _const_spec((256, 256)), _const_spec((256, 256))] + [_const_spec((tab_rows, 128))] * 4
    args += [mod, g, w_in, qkg, bd64, bd32] + list(tabs)
    u_shape = jax.ShapeDtypeStruct((n_all, D_IN), BF16)
    u_spec = pl.BlockSpec((tm, D_IN), lambda i: (i, 0))
    if has_ffn:
        out_shape = (jax.ShapeDtypeStruct((n_all, D_MODEL), F32), u_shape)
        out_specs = (tile, u_spec)
    else:
        out_shape, out_specs = u_shape, u_spec
    kern = functools.partial(_inproj_kernel, has_ffn=has_ffn, tm=tm, n_lat_tiles=n_lat_tiles,
                             tiles_per_batch=tpb, seq=seq)
    res = pl.pallas_call(
        kern, out_shape=out_shape, grid=(n_all // tm,), in_specs=in_specs, out_specs=out_specs,
        scratch_shapes=scratch, compiler_params=_cparams("arbitrary"), name="inproj",
    )(*args)
    return res if has_ffn else (x, res)


class _AttItem:
    def __init__(self, q, kt_view, v_view, ranges, bias=None, extra=None):
        self.q, self.kt_view, self.v_view, self.ranges, self.bias, self.extra = q, kt_view, v_view, ranges, bias, extra
        self.n_keys = sum(n for _, n in ranges)
        self.m = None

    def scores(self, s_scr, rows):
        q_blk = self.q()
        m, off, ti = None, 0, 0
        for start, n in self.ranges:
            for o in range(0, n, KEY_TILE):
                w = min(KEY_TILE, n - o)
                s = _dot(q_blk, self.kt_view[:, pl.ds(start + o, w)])
                b = self.bias(ti) if self.bias is not None else None
                if b is not None:
                    s = s + b
                ti += 1
                s_scr[rows, off + o:off + o + w] = s
                tile_max = s.max(axis=-1, keepdims=True)
                m = tile_max if m is None else jnp.maximum(m, tile_max)
            off += n
        self.m = m if self.extra is None else jnp.maximum(m, self.extra)

    def probs(self, s_scr, p_scr, rows):
        for o in range(0, self.n_keys, KEY_TILE):
            w = min(KEY_TILE, self.n_keys - o)
            p_scr[rows, o:o + w] = jnp.exp2(s_scr[rows, o:o + w] - self.m).astype(BF16)

    def values(self, p_scr, rows):
        pv, off = None, 0
        for start, n in self.ranges:
            c = _dot(p_scr[rows, off:off + n], self.v_view[pl.ds(start, n), :])
            pv = c if pv is None else pv + c
            off += n
        denom = pv[:, 64:65]
        if self.extra is not None:
            denom = denom + jnp.exp2(self.extra - self.m)
        return pv[:, 0:64] / denom


def _run_att_items(items, s_scr, p_scr, lag=1):
    assert ATT_SLOTS > 2 * lag

    def rows(t):
        r0 = (t % ATT_SLOTS) * ATT_RB
        return slice(r0, r0 + ATT_RB)

    outs = []
    for t in range(len(items) + 2 * lag):
        if t < len(items):
            items[t].scores(s_scr, rows(t))
        if 0 <= t - lag < len(items):
            items[t - lag].probs(s_scr, p_scr, rows(t - lag))
        if 0 <= t - 2 * lag < len(items):
            outs.append(items[t - 2 * lag].values(p_scr, rows(t - 2 * lag)))
    return outs


def _stage_heads(k_parts, v_parts, kt_scr, v_scr):
    eye = jnp.where(lax.broadcasted_iota(I32, (HEAD_DIM, HEAD_DIM), 0)
                    == lax.broadcasted_iota(I32, (HEAD_DIM, HEAD_DIM), 1), 1.0, 0.0).astype(BF16)
    for h in range(kt_scr.shape[0]):
        r0 = 0
        for (kr, kc0), (vr, vc0) in zip(k_parts, v_parts):
            n = kr.shape[0]
            kt_scr[h, :, r0:r0 + n] = _nt_dot(eye, kr[:, kc0 + 64 * h:kc0 + 64 * h + 64]).astype(BF16)
            v_scr[h, r0:r0 + n, 0:64] = vr[:, vc0 + 64 * h:vc0 + 64 * h + 64]
            r0 += n
        v_scr[h, :, 64:128] = jnp.ones((v_scr.shape[1], 64), BF16)


def _group_rms(y):
    return y * lax.rsqrt(jnp.mean(y * y, axis=-1, keepdims=True) + EPS)


def _stage_once_per_batch_row(k_parts, v_parts, k_scr, v_scr):
    @pl.when(pl.program_id(1) == 0)
    def _():
        _stage_heads(k_parts, v_parts, k_scr, v_scr)


def _row_blocks(n):
    return range(0, n, ATT_RB)


def _q_thunk(q_ref, r0, head):
    return lambda: q_ref[r0:r0 + ATT_RB, 64 * head:64 * head + 64]


def _heads_to_rows(outs, n_heads):
    per = len(outs) // n_heads
    return jnp.concatenate([jnp.concatenate(outs[h * per:(h + 1) * per], axis=0) for h in range(n_heads)], axis=1)


def _dense_kernel(q_ref, kv_ref, kvc_ref, o_ref, kt_scr, v_scr, s_scr, p_scr, *, tq):
    _stage_once_per_batch_row([(kv_ref, 0), (kvc_ref, 0)], [(kv_ref, 128), (kvc_ref, 128)], kt_scr, v_scr)
    nk = v_scr.shape[1]
    items = [_AttItem(_q_thunk(q_ref, r0, hq), kt_scr.at[hq // 2], v_scr.at[hq // 2], [(0, nk)])
             for hq in range(HEADS) for r0 in _row_blocks(tq)]
    y = _heads_to_rows(_run_att_items(items, s_scr, p_scr), HEADS)
    o_ref[...] = _group_rms(y).astype(BF16)


def _window_kernel(sink_ref, q_ref, kv_ref, kvc_ref, *refs, tq, seq, tiles):
    mask_refs, (o_ref, kt_scr, v_scr, s_scr, p_scr) = refs[:tiles], refs[tiles:]
    _stage_once_per_batch_row([(kv_ref, 0), (kvc_ref, 0)], [(kv_ref, 128), (kvc_ref, 128)], kt_scr, v_scr)
    kw = tq + 2 * WINDOW
    nc = v_scr.shape[1] - seq

    def mask_thunk(mask_ref, r0):
        return lambda ti: (mask_ref[0, r0:r0 + ATT_RB, KEY_TILE * ti:KEY_TILE * ti + KEY_TILE]
                           if KEY_TILE * ti < kw else None)

    items = []
    for t, mask_ref in enumerate(mask_refs):
        q0 = (tiles * pl.program_id(1) + t) * tq
        ks = pl.multiple_of(jnp.clip(q0 - WINDOW, 0, seq - kw), 128)
        items += [_AttItem(_q_thunk(q_ref, t * tq + r0, hq), kt_scr.at[hq // 2], v_scr.at[hq // 2],
                           [(ks, kw), (seq, nc)], bias=mask_thunk(mask_ref, r0), extra=sink_ref[hq] * LOG2E)
                  for hq in range(HEADS) for r0 in _row_blocks(tq)]
    outs = _run_att_items(items, s_scr, p_scr)
    per = len(outs) // tiles
    y = jnp.concatenate([_heads_to_rows(outs[t * per:(t + 1) * per], HEADS) for t in range(tiles)], axis=0)
    o_ref[...] = _group_rms(y).astype(BF16)


def _neigh_kernel(q_ref, k_ref, v_ref, kc_ref, vc_ref, *refs, rows, seq, tiles):
    bias_refs, (o_ref, kt_scr, v_scr, s_scr, p_scr) = refs[:tiles], refs[tiles:]
    _stage_once_per_batch_row([(k_ref, 0), (kc_ref, 0)], [(v_ref, 0), (vc_ref, 0)], kt_scr, v_scr)
    nk = NA_KROWS * GRID_W
    nc = v_scr.shape[1] - seq
    qrows = NA_QROWS * GRID_W

    def bias_thunk(bias_ref, h, r0):
        return lambda ti: (bias_ref[0, h, r0:r0 + ATT_RB, KEY_TILE * ti:KEY_TILE * ti + KEY_TILE]
                           if KEY_TILE * ti < nk else None)

    items = []
    for t, bias_ref in enumerate(bias_refs):
        rb = tiles * pl.program_id(1) + t
        rs = jnp.clip(NA_QROWS * rb - NA_ROWS // 2, 0, rows - NA_KROWS)
        k0 = pl.multiple_of(rs * GRID_W, KEY_TILE)
        items += [_AttItem(_q_thunk(q_ref, t * qrows + r0, h), kt_scr.at[h], v_scr.at[h], [(k0, nk), (seq, nc)],
                           bias=bias_thunk(bias_ref, h, r0))
                  for h in range(HEADS) for r0 in _row_blocks(qrows)]
    outs = _run_att_items(items, s_scr, p_scr)
    per = len(outs) // tiles
    y = jnp.concatenate([_heads_to_rows(outs[t * per:(t + 1) * per], HEADS) for t in range(tiles)], axis=0)
    o_ref[...] = _group_rms(y).astype(BF16)


def _diff_lambda(lam_ref, lam_init):
    lp = lam_ref[...]
    a = jnp.sum(lp[0:1] * lp[1:2], axis=-1, keepdims=True)
    b = jnp.sum(lp[2:3] * lp[3:4], axis=-1, keepdims=True)
    return jnp.exp(a) - jnp.exp(b) + lam_init


def _diff_kernel(lam_ref, q_ref, k_ref, v_ref, kc_ref, vc_ref, o_ref, kt_scr, v_scr, s_scr, p_scr, *, lam_init, tq):
    _stage_once_per_batch_row([(k_ref, 0), (kc_ref, 0)], [(v_ref, 0), (vc_ref, 0)], kt_scr, v_scr)
    lam = _diff_lambda(lam_ref, lam_init)
    nk = v_scr.shape[1]
    first = lax.broadcasted_iota(I32, (ATT_RB, HEAD_DIM), 1) < DIFF_DIM
    zero = jnp.zeros((ATT_RB, HEAD_DIM), BF16)

    def q_map(r0, h, half):
        def thunk():
            qh = q_ref[r0:r0 + ATT_RB, 64 * h:64 * h + 64]
            return jnp.where(first, qh, zero) if half == 0 else jnp.where(first, zero, qh)
        return thunk

    items = [_AttItem(q_map(r0, h, half), kt_scr.at[h], v_scr.at[h], [(0, nk)])
             for h in range(HEADS) for r0 in _row_blocks(tq) for half in range(2)]
    outs = _run_att_items(items, s_scr, p_scr)
    diffs = [outs[i] - lam * outs[i + 1] for i in range(0, len(outs), 2)]
    per = len(diffs) // HEADS
    heads = [_group_rms(jnp.concatenate(diffs[h * per:(h + 1) * per], axis=0)) * (1.0 - lam_init)
             for h in range(HEADS)]
    o_ref[...] = jnp.concatenate(heads, axis=1).astype(BF16)


def _softmax_parts(scores, extra=None):
    m = scores[0].max(axis=-1, keepdims=True)
    for s in scores[1:]:
        m = jnp.maximum(m, s.max(axis=-1, keepdims=True))
    if extra is not None:
        m = jnp.maximum(m, extra)
    ps = [jnp.exp2(s - m) for s in scores]
    l = ps[0].sum(axis=-1, keepdims=True)
    for p in ps[1:]:
        l = l + p.sum(axis=-1, keepdims=True)
    if extra is not None:
        l = l + jnp.exp2(extra - m)
    return ps, 1.0 / l


def _ctx_kernel(sink_ref, lam_ref, u_ref, o_ref, *, lam_init):
    L = u_ref.shape[0]
    top = lax.broadcasted_iota(I32, (2 * L, 1), 0) < L

    def cols(blk, lo, width=64):
        return u_ref[:, 256 * blk + lo:256 * blk + lo + width]

    for q_blk, kv_blk, out_col, use_sink in ((COL_A_Q, COL_A_KV, 0, True), (COL_D_Q, COL_D_KV, 768, False)):
        outs = [None] * HEADS
        for j in range(2):
            q2 = jnp.concatenate([cols(q_blk, 128 * j), cols(q_blk, 128 * j + 64)], axis=0)
            snk = jnp.where(top, sink_ref[2 * j], sink_ref[2 * j + 1]) * LOG2E if use_sink else None
            (p,), inv = _softmax_parts([_nt_dot(q2, cols(kv_blk, 64 * j))], extra=snk)
            o = _dot(p.astype(BF16), cols(kv_blk, 128 + 64 * j)) * inv
            outs[2 * j], outs[2 * j + 1] = o[:L], o[L:]
        o_ref[:, out_col:out_col + 256] = _group_rms(jnp.concatenate(outs, axis=1)).astype(BF16)
    outs = []
    for h in range(HEADS):
        (p,), inv = _softmax_parts([_nt_dot(cols(COL_B_Q, 64 * h), cols(COL_B_K, 64 * h))])
        outs.append(_dot(p.astype(BF16), cols(COL_B_V, 64 * h)) * inv)
    o_ref[:, 256:512] = _group_rms(jnp.concatenate(outs, axis=1)).astype(BF16)
    lam = _diff_lambda(lam_ref, lam_init)
    first = lax.broadcasted_iota(I32, (L, HEAD_DIM), 1) < DIFF_DIM
    zero = jnp.zeros((L, HEAD_DIM), BF16)
    outs = []
    for h in range(HEADS):
        qh, kh = cols(COL_C_Q, 64 * h), cols(COL_C_K, 64 * h)
        maps = []
        for half in range(2):
            qm = jnp.where(first, qh, zero) if half == 0 else jnp.where(first, zero, qh)
            (p,), inv = _softmax_parts([_nt_dot(qm, kh)])
            maps.append(p * inv)
        o = _dot((maps[0] - lam * maps[1]).astype(BF16), cols(COL_C_V, 64 * h))
        outs.append(_group_rms(o) * (1.0 - lam_init))
    o_ref[:, 512:768] = jnp.concatenate(outs, axis=1).astype(BF16)


def _attention_calls(u, sink, lam_p, bias, wmask, *, layer, batch, seq, ctx_len, need_ctx):
    lam_init = 0.8 - 0.6 * math.exp(-0.3 * layer)
    n_all = u.shape[0]
    tq = ATT_TQ
    nq = seq // tq
    nk_all = seq + ctx_len
    cb0 = batch * seq // ctx_len
    y_shape = jax.ShapeDtypeStruct((n_all, D_MODEL), BF16)
    smem = pl.BlockSpec(memory_space=pltpu.SMEM)
    any_spec = pl.BlockSpec(memory_space=pl.ANY)
    lam_spec2 = pl.BlockSpec((4, DIFF_DIM), lambda b, i: (0, 0))
    cp = _cparams("arbitrary", "arbitrary")

    def scratch(n_heads, n_keys):
        return [pltpu.VMEM((n_heads, HEAD_DIM, nk_all), BF16), pltpu.VMEM((n_heads, nk_all, 128), BF16),
                pltpu.VMEM((ATT_SLOTS * ATT_RB, n_keys), F32), pltpu.VMEM((ATT_SLOTS * ATT_RB, n_keys), BF16)]

    def qspec(col, t=tq):
        return pl.BlockSpec((t, 256), lambda b, i: (b * (seq // t) + i, col))

    tqf = ATT_TQ_FULL
    nqf = seq // tqf

    def latspec(col):
        return pl.BlockSpec((seq, 256), lambda b, i: (b, col))

    def ctxspec(col):
        return pl.BlockSpec((ctx_len, 256), lambda b, i: (cb0 + b, col))

    kw = tq + 2 * WINDOW

    def case_specs(shape, n_tiles, per_step):
        def spec(t):
            def index(b, i):
                j = per_step * i + t
                return (jnp.where(j == 0, 0, jnp.where(j == n_tiles - 1, 2, 1)),) + (0,) * (len(shape) - 1)
            return pl.BlockSpec(shape, index)
        return [spec(t) for t in range(per_step)]

    wt = WINDOW_TILES
    y = pl.pallas_call(
        functools.partial(_window_kernel, tq=tq, seq=seq, tiles=wt),
        out_shape=y_shape, grid=(batch, nq // wt),
        in_specs=[smem, qspec(COL_A_Q, wt * tq), latspec(COL_A_KV), ctxspec(COL_A_KV)]
        + case_specs((1, tq, kw), nq, wt),
        out_specs=qspec(0, wt * tq), scratch_shapes=scratch(2, kw + ctx_len),
        compiler_params=cp, name="mix_window",
    )(sink, u, u, u, *([wmask] * wt))

    rows = seq // GRID_W
    nrb = rows // NA_QROWS
    qrows = NA_QROWS * GRID_W
    nkb = NA_KROWS * GRID_W
    nt = NA_TILES
    y = pl.pallas_call(
        _drop_arg(functools.partial(_neigh_kernel, rows=rows, seq=seq, tiles=nt), 5 + nt),
        out_shape=y_shape, grid=(batch, nrb // nt),
        in_specs=[qspec(COL_B_Q, nt * qrows), latspec(COL_B_K), latspec(COL_B_V),
                  ctxspec(COL_B_K), ctxspec(COL_B_V)] + case_specs((1, HEADS, qrows, nkb), nrb, nt) + [any_spec],
        out_specs=qspec(1, nt * qrows), scratch_shapes=scratch(HEADS, nkb + ctx_len),
        input_output_aliases={5 + nt: 0}, compiler_params=cp, name="mix_neigh",
    )(u, u, u, u, u, *([bias] * nt), y)

    y = pl.pallas_call(
        _drop_arg(functools.partial(_diff_kernel, lam_init=lam_init, tq=tqf), 6),
        out_shape=y_shape, grid=(batch, nqf),
        in_specs=[lam_spec2, qspec(COL_C_Q, tqf), latspec(COL_C_K), latspec(COL_C_V),
                  ctxspec(COL_C_K), ctxspec(COL_C_V), any_spec],
        out_specs=qspec(2, tqf), scratch_shapes=scratch(HEADS, nk_all),
        input_output_aliases={6: 0}, compiler_params=cp, name="mix_diff",
    )(lam_p, u, u, u, u, u, y)

    y = pl.pallas_call(
        _drop_arg(functools.partial(_dense_kernel, tq=tqf), 3),
        out_shape=y_shape, grid=(batch, nqf),
        in_specs=[qspec(COL_D_Q, tqf), latspec(COL_D_KV), ctxspec(COL_D_KV), any_spec],
        out_specs=qspec(3, tqf), scratch_shapes=scratch(2, nk_all),
        input_output_aliases={3: 0}, compiler_params=cp, name="mix_dense",
    )(u, u, u, y)

    if need_ctx:
        y = pl.pallas_call(
            _drop_arg(functools.partial(_ctx_kernel, lam_init=lam_init), 3),
            out_shape=y_shape, grid=(batch,),
            in_specs=[smem, pl.BlockSpec((4, DIFF_DIM), lambda b: (0, 0)),
                      pl.BlockSpec((ctx_len, D_IN), lambda b: (cb0 + b, 0)), any_spec],
            out_specs=pl.BlockSpec((ctx_len, D_MODEL), lambda b: (cb0 + b, 0)),
            input_output_aliases={3: 0}, compiler_params=_cparams("arbitrary"), name="mix_ctx",
        )(sink, lam_p, u, y)
    return y


def _select4(idx, vals):
    return jnp.where(idx == 0, vals[0], jnp.where(idx == 1, vals[1], jnp.where(idx == 2, vals[2], vals[3])))


def _outproj_kernel(y_ref, x_ref, mod_ref, gain_ref, w_ref, g_ref, rw_ref, rb_ref, xo_ref, h_ref, cls_ref, *, tm):
    hm = tm // 2

    def project(r0):
        yg = (y_ref[r0:r0 + hm, :].astype(F32) * gain_ref[...]).astype(BF16)
        return _dot(yg, w_ref[...])

    def normalise(r0, proj):
        xm = x_ref[r0:r0 + hm, :] + mod_ref[0, 2:3, :] * proj
        xo_ref[r0:r0 + hm, :] = xm
        hn = xm * lax.rsqrt(jnp.mean(xm * xm, axis=-1, keepdims=True) + EPS) * g_ref[...]
        h2 = hn * (1.0 + mod_ref[0, 4:5, :]) + mod_ref[0, 3:4, :]
        hi = h2.astype(BF16)
        hi32 = hi.astype(F32)
        lo = (h2 - hi32).astype(BF16)
        bits = lax.bitcast_convert_type(hi32, U32)
        half = D_MODEL // 2
        h_ref[r0:r0 + hm, 0:half] = (bits[:, 0:half] >> 16) | (bits[:, half:D_MODEL] & jnp.uint32(0xFFFF0000))
        return (_dot(hi, rw_ref[...]) + _dot(lo, rw_ref[...])).T

    def route(r0, lt):
        score = _sigmoid(lt[0:N_EXPERTS] + lt[N_EXPERTS:2 * N_EXPERTS])
        sel = score + rb_ref[...]
        srow = [sel[e:e + 1] for e in range(N_EXPERTS)]
        crow = [score[e:e + 1] for e in range(N_EXPERTS)]

        gidx = jnp.zeros((1, hm), I32)
        gbest = None
        for g in range(N_EXPERTS // EXPERTS_PER_GROUP):
            v = srow[4 * g:4 * g + 4]
            best = None
            for a in range(4):
                for b in range(a + 1, 4):
                    t = v[a] + v[b]
                    best = t if best is None else jnp.maximum(best, t)
            if g == 0:
                gbest = best
            else:
                better = best > gbest
                gidx = jnp.where(better, g, gidx)
                gbest = jnp.where(better, best, gbest)

        iv = [_select4(gidx, [srow[4 * g + i] for g in range(4)]) for i in range(4)]
        sv = [_select4(gidx, [crow[4 * g + i] for g in range(4)]) for i in range(4)]
        chosen = []
        for i in range(4):
            rank = jnp.zeros((1, hm), I32)
            for j in range(4):
                if j != i:
                    beats = (iv[j] >= iv[i]) if j < i else (iv[j] > iv[i])
                    rank = rank + jnp.where(beats, 1, 0)
            chosen.append(rank < 2)
        lo_i = jnp.where(chosen[0], 0, jnp.where(chosen[1], 1, 2))
        hi_i = jnp.where(chosen[3], 3, jnp.where(chosen[2], 2, 1))
        pair = jnp.where(lo_i == 0, hi_i - 1, jnp.where(lo_i == 1, hi_i + 1, 5))
        cls_ref[0, :, r0:r0 + hm] = gidx * N_PAIRS + pair
        s_lo = jnp.where(lo_i == 0, sv[0], jnp.where(lo_i == 1, sv[1], sv[2]))
        s_hi = jnp.where(hi_i == 3, sv[3], jnp.where(hi_i == 2, sv[2], sv[1]))
        inv = 1.0 / (s_lo + s_hi)
        rowi = lax.broadcasted_iota(I32, (128, hm), 0)
        wts = jnp.where(rowi == 0, s_lo * inv, jnp.where(rowi == 1, s_hi * inv, 0.0))
        h_ref[r0:r0 + hm, D_MODEL // 2:EXT_W] = lax.bitcast_convert_type(wts.T, U32)

    proj0 = project(0)
    proj1 = project(hm)
    lt0 = normalise(0, proj0)
    lt1 = normalise(hm, proj1)
    route(0, lt0)
    route(hm, lt1)


def _outproj_call(y, x, mod, gain, w_out, g, rw, rb, *, n_proc, batch, seq):
    tm = TOKEN_TILE
    n_tiles = n_proc // tm
    tile = pl.BlockSpec((tm, D_MODEL), lambda i: (i, 0))
    return pl.pallas_call(
        functools.partial(_outproj_kernel, tm=tm),
        out_shape=(jax.ShapeDtypeStruct((n_proc, D_MODEL), F32),
                   jax.ShapeDtypeStruct((n_proc, EXT_W), U32),
                   jax.ShapeDtypeStruct((n_tiles, 1, tm), I32)),
        grid=(n_tiles,),
        in_specs=[tile, tile, _mod_spec(batch * seq // tm, seq // tm, batch), _const_spec((1, D_MODEL)),
                  _const_spec((D_MODEL, D_MODEL)), _const_spec((1, D_MODEL)), _const_spec((D_MODEL, 128)),
                  _const_spec((N_EXPERTS, 1))],
        out_specs=(tile, pl.BlockSpec((tm, EXT_W), lambda i: (i, 0)), pl.BlockSpec((1, 1, tm), lambda i: (i, 0, 0))),
        compiler_params=_cparams("arbitrary"), name="outproj_router",
    )(y, x, mod, gain, w_out, g, rw, rb)


def _rank_kernel(cls_ref, pos_ref, bcls_ref, nval_ref, *, blk):
    c = cls_ref[...]
    nr = c.shape[0]
    nbp = bcls_ref.shape[1]
    upper = (lax.broadcasted_iota(I32, (128, 128), 0) <= lax.broadcasted_iota(I32, (128, 128), 1))
    upper = jnp.where(upper, 1.0, 0.0).astype(BF16)
    lower = (lax.broadcasted_iota(I32, (nr, nr), 1) < lax.broadcasted_iota(I32, (nr, nr), 0))
    lower = jnp.where(lower, 1.0, 0.0).astype(BF16)
    bidx = lax.broadcasted_iota(I32, (1, nbp), 1).astype(F32)
    pos = jnp.zeros((nr, 128), F32)
    bcls = jnp.zeros((1, nbp), F32)
    nval = jnp.zeros((1, nbp), F32)
    start = jnp.zeros((1, 1), F32)
    for k in range(N_CLASSES):
        m = c == k
        incl = _dot(jnp.where(m, 1.0, 0.0).astype(BF16), upper)
        tot = incl[:, 127:128]
        above = _dot(lower, jnp.broadcast_to(tot, (nr, 128)).astype(BF16))
        cnt = above[nr - 1:nr, 0:1] + tot[nr - 1:nr, 0:1]
        nblk = jnp.floor((cnt + (blk - 1)) * (1.0 / blk))
        pos = jnp.where(m, start * blk + above + incl - 1.0, pos)
        end = start + nblk
        bcls = bcls + jnp.where(bidx >= end, 1.0, 0.0)
        nval = nval + jnp.where((bidx >= start) & (bidx < end), jnp.clip(cnt - (bidx - start) * blk, 0.0, blk), 0.0)
        start = end
    pos_ref[...] = pos.astype(I32)
    bcls_ref[...] = jnp.minimum(bcls, N_CLASSES - 1.0).astype(I32)
    nval_ref[...] = nval.astype(I32)


def _rank_call(cls2d, n_blocks):
    nr = cls2d.shape[0]
    nbp = -(-n_blocks // 128) * 128
    return pl.pallas_call(
        functools.partial(_rank_kernel, blk=MOE_BLK),
        out_shape=(jax.ShapeDtypeStruct((nr, 128), I32), jax.ShapeDtypeStruct((1, nbp), I32),
                   jax.ShapeDtypeStruct((1, nbp), I32)),
        grid=(1,),
        in_specs=[_const_spec((nr, 128))],
        out_specs=(_const_spec((nr, 128)), _const_spec((1, nbp)), _const_spec((1, nbp))),
        compiler_params=_cparams("arbitrary"), name="class_rank",
    )(cls2d)


def _scatter_kernel(pos_ref, h_ref, xs_ref, sem):
    n = h_ref.shape[0]
    for r in range(n):
        pltpu.make_async_copy(h_ref.at[pl.ds(r, 1)], xs_ref.at[pl.ds(pos_ref[0, 0, r], 1)], sem).start()
    pltpu.make_async_copy(h_ref, xs_ref.at[pl.ds(0, n)], sem).wait()


def _scatter_call(h_ext, pos3, n_slots):
    n, w = h_ext.shape
    r = ROW_DMA
    return pl.pallas_call(
        _scatter_kernel,
        out_shape=jax.ShapeDtypeStruct((n_slots, w), h_ext.dtype),
        grid=(n // r,),
        in_specs=[pl.BlockSpec((1, 1, r), lambda i: (i, 0, 0), memory_space=pltpu.SMEM),
                  pl.BlockSpec((r, w), lambda i: (i, 0))],
        out_specs=pl.BlockSpec(memory_space=pl.ANY),
        scratch_shapes=[pltpu.SemaphoreType.DMA(())],
        compiler_params=_cparams("arbitrary"), name="row_scatter",
    )(pos3, h_ext)


def _moe_kernel(elo_ref, ehi_ref, nval_ref, xs_ref, wg_lo, wu_lo, wd_lo, wg_hi, wu_hi, wd_hi, ys_ref):
    del elo_ref, ehi_ref
    nv = nval_ref[pl.program_id(0)]

    @pl.when(nv > 0)
    def _():
        live = lax.broadcasted_iota(I32, (MOE_BLK, 1), 0) < nv
        half = D_MODEL // 2
        words = jnp.where(live, xs_ref[:, 0:half], jnp.uint32(0))
        x = jnp.concatenate([lax.bitcast_convert_type(words << 16, F32).astype(BF16),
                             lax.bitcast_convert_type(words & jnp.uint32(0xFFFF0000), F32).astype(BF16)], axis=1)
        acts = []
        for wg, wu, col in ((wg_lo, wu_lo, half), (wg_hi, wu_hi, half + 1)):
            w = jnp.where(live, lax.bitcast_convert_type(xs_ref[:, col:col + 1], F32), 0.0)
            g = _dot(x, wg[0])
            acts.append((g * _sigmoid(g) * _dot(x, wu[0]) * w).astype(BF16))
        ys_ref[...] = _dot(acts[0], wd_lo[0]) + _dot(acts[1], wd_hi[0])

    @pl.when(nv == 0)
    def _():
        ys_ref[...] = jnp.zeros_like(ys_ref)


def _moe_call(xs, e_lo, e_hi, nval, w_gate, w_up, w_down):
    n_blocks = xs.shape[0] // MOE_BLK

    def wspec(shape, which):
        return pl.BlockSpec((1,) + shape, lambda b, elo, ehi, nv: ((elo, ehi)[which][b], 0, 0))

    gu, dn = (D_MODEL, D_EXPERT), (D_EXPERT, D_MODEL)
    grid_spec = pltpu.PrefetchScalarGridSpec(
        num_scalar_prefetch=3, grid=(n_blocks,),
        in_specs=[pl.BlockSpec((MOE_BLK, EXT_W), lambda b, elo, ehi, nv: (b, 0)),
                  wspec(gu, 0), wspec(gu, 0), wspec(dn, 0), wspec(gu, 1), wspec(gu, 1), wspec(dn, 1)],
        out_specs=pl.BlockSpec((MOE_BLK, D_MODEL), lambda b, elo, ehi, nv: (b, 0)),
    )
    return pl.pallas_call(
        _moe_kernel, out_shape=jax.ShapeDtypeStruct((xs.shape[0], D_MODEL), F32), grid_spec=grid_spec,
        compiler_params=_cparams("arbitrary"), name="moe_experts",
    )(e_lo, e_hi, nval, xs, w_gate, w_up, w_down, w_gate, w_up, w_down)


def _final_kernel(x_ref, pos_ref, pos_next_ref, ys_ref, mod_ref, o_ref, fbuf, fsems):
    f_ref = _prefetched_rows(pos_ref, pos_next_ref, ys_ref, fbuf, fsems)
    o_ref[...] = x_ref[...] + mod_ref[0, 5:6, :] * f_ref[...]


def _final_call(x, ys, pos, mod, *, batch, seq):
    tm = TOKEN_TILE
    n = batch * seq
    tile = pl.BlockSpec((tm, D_MODEL), lambda i: (i, 0))
    pos_tiles = pos.reshape(n // tm, 1, tm)
    return pl.pallas_call(
        _final_kernel, out_shape=jax.ShapeDtypeStruct((n, D_MODEL), F32), grid=(n // tm,),
        in_specs=[tile] + _pos_tile_specs(n // tm, tm) + [pl.BlockSpec(memory_space=pl.ANY),
                                                          _mod_spec(n // tm, seq // tm, batch)],
        out_specs=tile, scratch_shapes=[pltpu.VMEM((2, tm, D_MODEL), F32), pltpu.SemaphoreType.DMA((2,))],
        compiler_params=_cparams("arbitrary"), name="final_residual",
    )(x, pos_tiles, pos_tiles, ys, mod)


def _rope_tables(seq, ident_rows):
    pos = jnp.arange(seq, dtype=jnp.int32)
    row = (pos // GRID_W).astype(F32)[:, None]
    col = (pos % GRID_W).astype(F32)[:, None]

    def tab(half):
        inv = ROPE_BASE ** (-jnp.arange(half, dtype=F32) / half)
        ar, ac = row * inv, col * inv
        cos = jnp.concatenate([jnp.cos(ar), jnp.cos(ar), jnp.cos(ac), jnp.cos(ac)], axis=1)
        sin = jnp.concatenate([-jnp.sin(ar), jnp.sin(ar), -jnp.sin(ac), jnp.sin(ac)], axis=1)
        reps = 128 // (4 * half)
        cos = jnp.concatenate([jnp.tile(cos, (1, reps)), jnp.ones((ident_rows, 128), F32)], axis=0)
        sin = jnp.concatenate([jnp.tile(sin, (1, reps)), jnp.zeros((ident_rows, 128), F32)], axis=0)
        return cos, sin

    c64, s64 = tab(HEAD_DIM // 4)
    c32, s32 = tab(DIFF_DIM // 4)
    return c64, s64, c32, s32


def _block_diag_ones(group):
    idx = np.arange(256) // group
    return jnp.asarray((idx[:, None] == idx[None, :]).astype(np.float32), dtype=BF16)


def _qk_gains(g_win, g_na, g_diff, g_gqa):
    s64, s32 = HEAD_DIM ** -0.5 * LOG2E, DIFF_DIM ** -0.5 * LOG2E
    rows = [jnp.tile(g_win[0] * s64, 2), jnp.tile(g_win[1], 2), jnp.tile(g_na[0] * s64, 2), jnp.tile(g_na[1], 2),
            jnp.tile(g_diff[0] * s32, 4), jnp.tile(g_diff[1], 4), jnp.tile(g_gqa[0] * s64, 2), jnp.tile(g_gqa[1], 2)]
    return jnp.stack(rows).astype(F32)


def _na_bias_tiles(rpb, rows):
    w = GRID_W
    qc = np.arange(w)[:, None]
    kc = np.arange(w)[None, :]
    c_start = np.clip(qc - NA_COLS // 2, 0, w - NA_COLS)
    col_ok = (kc >= c_start) & (kc < c_start + NA_COLS)
    dc = np.clip(kc - qc + NA_COLS - 1, 0, 2 * NA_COLS - 2)
    t = jnp.where(jnp.asarray(col_ok), rpb.astype(F32)[:, :, dc] * LOG2E, NEG)
    neg = jnp.full((rpb.shape[0], w, w), NEG, F32)
    cases = []
    for r0, rs in ((0, 0), (NA_QROWS, 0), (rows - NA_QROWS, rows - NA_KROWS)):
        qtiles = []
        for i in range(NA_QROWS):
            qrow = r0 + i
            r_start = min(max(qrow - NA_ROWS // 2, 0), rows - NA_ROWS)
            blks = []
            for j in range(NA_KROWS):
                krow = rs + j
                blks.append(t[:, krow - qrow + NA_ROWS - 1] if r_start <= krow < r_start + NA_ROWS else neg)
            qtiles.append(jnp.concatenate(blks, axis=2))
        cases.append(jnp.concatenate(qtiles, axis=1))
    return jnp.stack(cases)


def _window_mask_tiles(tq):
    kw = tq + 2 * WINDOW
    r = lax.broadcasted_iota(I32, (3, tq, kw), 1)
    c = lax.broadcasted_iota(I32, (3, tq, kw), 2)
    shift = lax.broadcasted_iota(I32, (3, tq, kw), 0) * WINDOW
    return jnp.where(jnp.abs(c - r - shift) <= WINDOW, 0.0, NEG).astype(F32)


def _router_weights(router_w):
    hi = router_w.astype(BF16)
    lo = (router_w - hi.astype(F32)).astype(BF16)
    pad = jnp.zeros((router_w.shape[0], 128 - 2 * N_EXPERTS), BF16)
    return jnp.concatenate([hi, lo, pad], axis=1)


def kernel(x, c, ctx, c_ctx, ada_w, ada_b, norm_mix_g, norm_ffn_g, w_in, qk_g_win, qk_g_na, qk_g_diff, qk_g_gqa,
           sink_win, rpb_na, lambda_diff, out_gain, w_out, router_w, router_b, w_gate, w_up, w_down):
    batch, seq, d = x.shape
    ctx_len = ctx.shape[1]
    depth = w_in.shape[0]
    n_lat, n_ctx = batch * seq, batch * ctx_len
    n_all = n_lat + n_ctx
    rows = seq // GRID_W
    assert d == D_MODEL and seq % TOKEN_TILE == 0 and n_ctx % TOKEN_TILE == 0 and ctx_len % 128 == 0
    assert rows % NA_QROWS == 0 and rows >= NA_KROWS + NA_QROWS and seq >= ATT_TQ + 2 * WINDOW
    assert ctx_len % 128 == 0 and (NA_KROWS * GRID_W) % KEY_TILE == 0

    pad_rows = -(-(batch + 1) // 8) * 8
    s_all = jnp.concatenate([c, c_ctx[None, :], jnp.zeros((pad_rows - batch - 1, d), F32)], axis=0)
    mods = _ada_call(s_all, ada_w, ada_b).reshape(depth, pad_rows, ADA_CHUNKS, d)

    xs_all = jnp.concatenate([x.reshape(n_lat, d), ctx.reshape(n_ctx, d)], axis=0)
    tabs = _rope_tables(seq, TOKEN_TILE)
    wmask = _window_mask_tiles(ATT_TQ)
    bd64, bd32 = _block_diag_ones(HEAD_DIM), _block_diag_ones(DIFF_DIM)
    rw = _router_weights(router_w)
    rb = router_b.astype(F32).reshape(N_EXPERTS, 1)
    lo_tab = jnp.asarray([EXPERTS_PER_GROUP * (k // N_PAIRS) + PAIR_LO[k % N_PAIRS] for k in range(N_CLASSES)], I32)
    hi_tab = jnp.asarray([EXPERTS_PER_GROUP * (k // N_PAIRS) + PAIR_HI[k % N_PAIRS] for k in range(N_CLASSES)], I32)

    ffn, mod_prev = None, None
    for layer in range(depth):
        need_ctx = layer < depth - 1
        mod = mods[layer]
        qkg = _qk_gains(qk_g_win[layer], qk_g_na[layer], qk_g_diff[layer], qk_g_gqa[layer])
        xs_all, u = _inproj_call(xs_all, ffn, mod_prev, mod, norm_mix_g[layer].reshape(1, d),
                                 w_in[layer].astype(BF16), qkg, bd64, bd32, tabs, batch=batch, seq=seq)
        y = _attention_calls(u, sink_win[layer].astype(F32), lambda_diff[layer].astype(F32),
                             _na_bias_tiles(rpb_na[layer], rows), wmask, layer=layer, batch=batch, seq=seq,
                             ctx_len=ctx_len, need_ctx=need_ctx)
        n_proc = n_all if need_ctx else n_lat
        x_mid, h_ext, cls = _outproj_call(y, xs_all, mod, out_gain[layer].reshape(1, d), w_out[layer].astype(BF16),
                                          norm_ffn_g[layer].reshape(1, d), rw, rb, n_proc=n_proc, batch=batch, seq=seq)
        n_blocks = n_proc // MOE_BLK + N_CLASSES
        pos, bcls, nval = _rank_call(cls.reshape(n_proc // 128, 128), n_blocks)
        pos3 = pos.reshape(n_proc // ROW_DMA, 1, ROW_DMA)
        bcls, nval = bcls[0, :n_blocks], nval[0, :n_blocks]
        sorted_rows = _scatter_call(h_ext, pos3, n_blocks * MOE_BLK)
        ys = _moe_call(sorted_rows, lo_tab[bcls], hi_tab[bcls], nval, w_gate[layer].astype(BF16),
                       w_up[layer].astype(BF16), w_down[layer].astype(BF16))
        ffn = (ys, pos)
        xs_all, mod_prev = x_mid, mod
    out = _final_call(xs_all, ffn[0], ffn[1], mod_prev, batch=batch, seq=seq)
    return out.reshape(batch, seq, d)
```

```python
import functools
import math

import numpy as np
import jax
import jax.numpy as jnp
from jax import lax
from jax.experimental import pallas as pl
from jax.experimental.pallas import tpu as pltpu

F32 = jnp.float32
BF16 = jnp.bfloat16
I32 = jnp.int32
U32 = jnp.uint32

D_MODEL = 1024
GRID_W = 64
HEAD_DIM = 64
HEADS = 4
DIFF_DIM = 32
WINDOW = 128
NA_ROWS = 8
NA_COLS = 16
ROPE_BASE = 10000.0
N_EXPERTS = 16
EXPERTS_PER_GROUP = 4
D_EXPERT = 512
ADA_CHUNKS = 6
EPS = 1e-6
NEG = -1e30
D_IN = 2560

COL_A_Q, COL_A_KV, COL_B_Q, COL_B_K, COL_B_V, COL_C_Q, COL_C_K, COL_C_V, COL_D_Q, COL_D_KV = range(10)

TOKEN_TILE = 512
ATT_TQ = 256
ATT_TQ_FULL = 512
NA_QROWS = 4
NA_KROWS = 12
KEY_TILE = 256
ATT_RB = 128
WINDOW_TILES = 2
WINDOW_LAG = 2
NA_TILES = 2
CTX_ROWS_PER_STEP = 1
LOG2E = math.log2(math.e)
MOE_BLK = 512
ROW_DMA = 512
N_PAIRS = 6
N_CLASSES = (N_EXPERTS // EXPERTS_PER_GROUP) * N_PAIRS
PAIR_LO = (0, 0, 0, 1, 1, 2)
PAIR_HI = (1, 2, 3, 2, 3, 3)
EXT_W = D_MODEL // 2 + 128
VMEM_LIMIT = 56 * 1024 * 1024


def _cparams(*sem):
    return pltpu.CompilerParams(dimension_semantics=sem, vmem_limit_bytes=VMEM_LIMIT)


def _nt_dot(a, b):
    return lax.dot_general(a, b, (((1,), (1,)), ((), ())), preferred_element_type=F32)


def _dot(a, b):
    return jnp.dot(a, b, preferred_element_type=F32)


def _sigmoid(x):
    return 1.0 / (1.0 + jnp.exp(-x))


def _drop_arg(kern, pos):
    def wrapped(*refs):
        return kern(*refs[:pos], *refs[pos + 1:])
    return wrapped


def _ada_kernel(s_ref, w_ref, b_ref, o_ref):
    s = s_ref[...]
    act = s * _sigmoid(s)
    o_ref[0] = jnp.dot(act, w_ref[0], precision=lax.Precision.HIGHEST,
                       preferred_element_type=F32) + b_ref[0]


def _ada_call(s_all, ada_w, ada_b):
    depth, d, n = ada_w.shape
    rows = s_all.shape[0]
    tn = 1536
    return pl.pallas_call(
        _ada_kernel,
        out_shape=jax.ShapeDtypeStruct((depth, rows, n), F32),
        grid=(depth, n // tn),
        in_specs=[
            pl.BlockSpec((rows, d), lambda l, j: (0, 0)),
            pl.BlockSpec((1, d, tn), lambda l, j: (l, 0, j)),
            pl.BlockSpec((1, 1, tn), lambda l, j: (l, 0, j)),
        ],
        out_specs=pl.BlockSpec((1, rows, tn), lambda l, j: (l, 0, j)),
        compiler_params=_cparams("arbitrary", "arbitrary"),
        name="ada_mod",
    )(s_all, ada_w, ada_b.reshape(depth, 1, n))


def _start_row_gather(pos_ref, src_ref, dst_ref, sem):
    for r in range(dst_ref.shape[0]):
        pltpu.make_async_copy(src_ref.at[pl.ds(pos_ref[0, 0, r], 1)], dst_ref.at[pl.ds(r, 1)], sem).start()


def _wait_row_gather(src_ref, dst_ref, sem):
    pltpu.make_async_copy(src_ref.at[pl.ds(0, dst_ref.shape[0])], dst_ref, sem).wait()


def _prefetched_rows(pos_ref, pos_next_ref, src_ref, buf, sems):
    i = pl.program_id(0)
    slot = i % 2

    @pl.when(i == 0)
    def _():
        _start_row_gather(pos_ref, src_ref, buf.at[0], sems.at[0])

    @pl.when(i + 1 < pl.num_programs(0))
    def _():
        _start_row_gather(pos_next_ref, src_ref, buf.at[1 - slot], sems.at[1 - slot])

    _wait_row_gather(src_ref, buf.at[slot], sems.at[slot])
    return buf.at[slot]


def _inproj_kernel(*refs, has_ffn, tm, n_lat_tiles, tiles_per_batch, seq):
    if has_ffn:
        (x_ref, pos_ref, pos_next_ref, ys_ref, modp_ref, mod_ref, g_ref, w_ref, qkg_ref, bd64_ref, bd32_ref,
         c64_ref, s64_ref, c32_ref, s32_ref, xo_ref, u_ref, fbuf, fsems) = refs
        f_ref = _prefetched_rows(pos_ref, pos_next_ref, ys_ref, fbuf, fsems)
    else:
        (x_ref, mod_ref, g_ref, w_ref, qkg_ref, bd64_ref, bd32_ref,
         c64_ref, s64_ref, c32_ref, s32_ref, u_ref) = refs
    i = pl.program_id(0)
    hm = tm // 2

    def prologue(r0):
        x = x_ref[r0:r0 + hm, :]
        if has_ffn:
            x = x + modp_ref[0, 5:6, :] * f_ref[r0:r0 + hm, :]
            xo_ref[r0:r0 + hm, :] = x
        ms = jnp.mean(x * x, axis=-1, keepdims=True)
        hn = x * lax.rsqrt(ms + EPS) * g_ref[...]
        return (hn * (1.0 + mod_ref[0, 1:2, :]) + mod_ref[0, 0:1, :]).astype(BF16)

    p0 = jnp.where(i < n_lat_tiles, (i % tiles_per_batch) * tm, seq)
    p0 = pl.multiple_of(p0, tm)
    lane = lax.broadcasted_iota(I32, (hm, 128), 1)

    def norm_rope(a, r0, gidx, group, bd_ref, rope_tabs):
        w = a.shape[1]
        ssq = _dot((a * a).astype(BF16), bd_ref[0:w, 0:w])
        r = lax.rsqrt(ssq * (1.0 / group) + EPS)
        outs = []
        for s in range(w // 128):
            t = a[:, 128 * s:128 * s + 128] * r[:, 128 * s:128 * s + 128] * qkg_ref[gidx:gidx + 1, :]
            if rope_tabs is not None:
                c_ref, s_ref, half = rope_tabs
                fwd = pltpu.roll(t, 128 - half, axis=1)
                bwd = pltpu.roll(t, half, axis=1)
                sw = jnp.where((lane % (2 * half)) < half, fwd, bwd)
                t = t * c_ref[pl.ds(p0 + r0, hm), :] + sw * s_ref[pl.ds(p0 + r0, hm), :]
            outs.append(t)
        return outs[0] if len(outs) == 1 else jnp.concatenate(outs, axis=1)

    rope64 = (c64_ref, s64_ref, HEAD_DIM // 4)
    rope32 = (c32_ref, s32_ref, DIFF_DIM // 4)

    def epilogue(acc, r0, blk):
        if blk == COL_A_Q:
            out = norm_rope(acc, r0, 0, HEAD_DIM, bd64_ref, rope64)
        elif blk == COL_A_KV:
            out = jnp.concatenate([norm_rope(acc[:, :128], r0, 1, HEAD_DIM, bd64_ref, rope64), acc[:, 128:]], axis=1)
        elif blk == COL_B_Q:
            out = norm_rope(acc, r0, 2, HEAD_DIM, bd64_ref, None)
        elif blk == COL_B_K:
            out = norm_rope(acc, r0, 3, HEAD_DIM, bd64_ref, None)
        elif blk == COL_C_Q:
            out = norm_rope(acc, r0, 4, DIFF_DIM, bd32_ref, rope32)
        elif blk == COL_C_K:
            out = norm_rope(acc, r0, 5, DIFF_DIM, bd32_ref, rope32)
        elif blk == COL_D_Q:
            out = norm_rope(acc, r0, 6, HEAD_DIM, bd64_ref, rope64)
        elif blk == COL_D_KV:
            out = jnp.concatenate([norm_rope(acc[:, :128], r0, 7, HEAD_DIM, bd64_ref, rope64), acc[:, 128:]], axis=1)
        else:
            out = acc
        u_ref[r0:r0 + hm, 256 * blk:256 * blk + 256] = out.astype(BF16)

    n_blk = D_IN // 256
    units = [(r0, blk) for r0 in (0, hm) for blk in range(n_blk)]
    hbs = {0: prologue(0)}
    acc = _dot(hbs[0], w_ref[:, 0:256])
    hbs[hm] = prologue(hm)
    for t, (r0, blk) in enumerate(units):
        nxt = None
        if t + 1 < len(units):
            r1, b1 = units[t + 1]
            nxt = _dot(hbs[r1], w_ref[:, 256 * b1:256 * b1 + 256])
        epilogue(acc, r0, blk)
        acc = nxt


def _mod_spec(n_lat_tiles, tiles_per_batch, batch):
    return pl.BlockSpec((1, ADA_CHUNKS, D_MODEL),
                        lambda i: (jnp.where(i < n_lat_tiles, i // tiles_per_batch, batch), 0, 0))


def _const_spec(shape):
    return pl.BlockSpec(shape, lambda i: (0,) * len(shape))


def _pos_tile_specs(n_tiles, tm):
    return [pl.BlockSpec((1, 1, tm), lambda i: (i, 0, 0), memory_space=pltpu.SMEM),
            pl.BlockSpec((1, 1, tm), lambda i: (jnp.minimum(i + 1, n_tiles - 1), 0, 0), memory_space=pltpu.SMEM)]


def _inproj_call(x, ffn_src, mod_prev, mod, g, w_in, qkg, bd64, bd32, tabs, *, batch, seq):
    n_all = x.shape[0]
    tm = TOKEN_TILE
    n_lat_tiles = batch * seq // tm
    tpb = seq // tm
    has_ffn = ffn_src is not None
    tile = pl.BlockSpec((tm, D_MODEL), lambda i: (i, 0))
    modspec = _mod_spec(n_lat_tiles, tpb, batch)
    tab_rows = tabs[0].shape[0]
    in_specs = [tile]
    args = [x]
    scratch = []
    if has_ffn:
        ys, pos = ffn_src
        pos_tiles = pos.reshape(n_all // tm, 1, tm)
        in_specs += _pos_tile_specs(n_all // tm, tm) + [pl.BlockSpec(memory_space=pl.ANY), modspec]
        args += [pos_tiles, pos_tiles, ys, mod_prev]
        scratch = [pltpu.VMEM((2, tm, D_MODEL), F32), pltpu.SemaphoreType.DMA((2,))]
    in_specs += [modspec, _const_spec((1, D_MODEL)), _const_spec((D_MODEL, D_IN)), _const_spec((8, 128)),
                 _const_spec((256, 256)), _const_spec((256, 256))] + [_const_spec((tab_rows, 128))] * 4
    args += [mod, g, w_in, qkg, bd64, bd32] + list(tabs)
    u_shape = jax.ShapeDtypeStruct((n_all, D_IN), BF16)
    u_spec = pl.BlockSpec((tm, D_IN), lambda i: (i, 0))
    if has_ffn:
        out_shape = (jax.ShapeDtypeStruct((n_all, D_MODEL), F32), u_shape)
        out_specs = (tile, u_spec)
    else:
        out_shape, out_specs = u_shape, u_spec
    kern = functools.partial(_inproj_kernel, has_ffn=has_ffn, tm=tm, n_lat_tiles=n_lat_tiles,
                             tiles_per_batch=tpb, seq=seq)
    res = pl.pallas_call(
        kern, out_shape=out_shape, grid=(n_all // tm,), in_specs=in_specs, out_specs=out_specs,
        scratch_shapes=scratch, compiler_params=_cparams("arbitrary"), name="inproj",
    )(*args)
    return res if has_ffn else (x, res)


class _AttItem:
    def __init__(self, q, kt_view, v_view, ranges, bias=None, extra=None):
        self.q, self.kt_view, self.v_view, self.ranges, self.bias, self.extra = q, kt_view, v_view, ranges, bias, extra
        self.n_keys = sum(n for _, n in ranges)
        self.m = None

    def scores(self, s_scr, rows):
        q_blk = self.q()
        m, off, ti = None, 0, 0
        for start, n in self.ranges:
            for o in range(0, n, KEY_TILE):
                w = min(KEY_TILE, n - o)
                s = _dot(q_blk, self.kt_view[:, pl.ds(start + o, w)])
                b = self.bias(ti) if self.bias is not None else None
                if b is not None:
                    s = s + b
                ti += 1
                s_scr[rows, off + o:off + o + w] = s
                tile_max = s.max(axis=-1, keepdims=True)
                m = tile_max if m is None else jnp.maximum(m, tile_max)
            off += n
        self.m = m if self.extra is None else jnp.maximum(m, self.extra)

    def probs(self, s_scr, p_scr, rows):
        for o in range(0, self.n_keys, KEY_TILE):
            w = min(KEY_TILE, self.n_keys - o)
            p_scr[rows, o:o + w] = jnp.exp2(s_scr[rows, o:o + w] - self.m).astype(BF16)

    def values(self, p_scr, rows):
        pv, off = None, 0
        for start, n in self.ranges:
            c = _dot(p_scr[rows, off:off + n], self.v_view[pl.ds(start, n), :])
            pv = c if pv is None else pv + c
            off += n
        denom = pv[:, 64:65]
        if self.extra is not None:
            denom = denom + jnp.exp2(self.extra - self.m)
        return pv[:, 0:64] / denom


def _run_att_items(items, s_scr, p_scr, lag=1):
    slots = s_scr.shape[0] // ATT_RB
    assert slots > 2 * lag

    def rows(t):
        r0 = (t % slots) * ATT_RB
        return slice(r0, r0 + ATT_RB)

    outs = []
    for t in range(len(items) + 2 * lag):
        if t < len(items):
            items[t].scores(s_scr, rows(t))
        if 0 <= t - lag < len(items):
            items[t - lag].probs(s_scr, p_scr, rows(t - lag))
        if 0 <= t - 2 * lag < len(items):
            outs.append(items[t - 2 * lag].values(p_scr, rows(t - 2 * lag)))
    return outs


def _stage_heads(k_parts, v_parts, kt_scr, v_scr):
    eye = jnp.where(lax.broadcasted_iota(I32, (HEAD_DIM, HEAD_DIM), 0)
                    == lax.broadcasted_iota(I32, (HEAD_DIM, HEAD_DIM), 1), 1.0, 0.0).astype(BF16)
    for h in range(kt_scr.shape[0]):
        r0 = 0
        for (kr, kc0), (vr, vc0) in zip(k_parts, v_parts):
            n = kr.shape[0]
            kt_scr[h, :, r0:r0 + n] = _nt_dot(eye, kr[:, kc0 + 64 * h:kc0 + 64 * h + 64]).astype(BF16)
            v_scr[h, r0:r0 + n, 0:64] = vr[:, vc0 + 64 * h:vc0 + 64 * h + 64]
            r0 += n
        v_scr[h, :, 64:128] = jnp.ones((v_scr.shape[1], 64), BF16)


def _group_rms(y):
    return y * lax.rsqrt(jnp.mean(y * y, axis=-1, keepdims=True) + EPS)


def _stage_once_per_batch_row(k_parts, v_parts, k_scr, v_scr):
    @pl.when(pl.program_id(1) == 0)
    def _():
        _stage_heads(k_parts, v_parts, k_scr, v_scr)


def _row_blocks(n):
    return range(0, n, ATT_RB)


def _q_thunk(q_ref, r0, head):
    return lambda: q_ref[r0:r0 + ATT_RB, 64 * head:64 * head + 64]


def _heads_to_rows(outs, n_heads):
    per = len(outs) // n_heads
    return jnp.concatenate([jnp.concatenate(outs[h * per:(h + 1) * per], axis=0) for h in range(n_heads)], axis=1)


def _dense_kernel(q_ref, kv_ref, kvc_ref, o_ref, kt_scr, v_scr, s_scr, p_scr, *, tq):
    _stage_once_per_batch_row([(kv_ref, 0), (kvc_ref, 0)], [(kv_ref, 128), (kvc_ref, 128)], kt_scr, v_scr)
    nk = v_scr.shape[1]
    items = [_AttItem(_q_thunk(q_ref, r0, hq), kt_scr.at[hq // 2], v_scr.at[hq // 2], [(0, nk)])
             for hq in range(HEADS) for r0 in _row_blocks(tq)]
    y = _heads_to_rows(_run_att_items(items, s_scr, p_scr), HEADS)
    o_ref[...] = _group_rms(y).astype(BF16)


def _window_kernel(sink_ref, q_ref, kv_ref, kvc_ref, *refs, tq, seq, tiles):
    mask_refs, (o_ref, kt_scr, v_scr, s_scr, p_scr) = refs[:tiles], refs[tiles:]
    _stage_once_per_batch_row([(kv_ref, 0), (kvc_ref, 0)], [(kv_ref, 128), (kvc_ref, 128)], kt_scr, v_scr)
    kw = tq + 2 * WINDOW
    nc = v_scr.shape[1] - seq

    def mask_thunk(mask_ref, r0):
        return lambda ti: (mask_ref[0, r0:r0 + ATT_RB, KEY_TILE * ti:KEY_TILE * ti + KEY_TILE]
                           if KEY_TILE * ti < kw else None)

    items = []
    for t, mask_ref in enumerate(mask_refs):
        q0 = (tiles * pl.program_id(1) + t) * tq
        ks = pl.multiple_of(jnp.clip(q0 - WINDOW, 0, seq - kw), 128)
        items += [_AttItem(_q_thunk(q_ref, t * tq + r0, hq), kt_scr.at[hq // 2], v_scr.at[hq // 2],
                           [(ks, kw), (seq, nc)], bias=mask_thunk(mask_ref, r0), extra=sink_ref[hq] * LOG2E)
                  for hq in range(HEADS) for r0 in _row_blocks(tq)]
    outs = _run_att_items(items, s_scr, p_scr, lag=WINDOW_LAG)
    per = len(outs) // tiles
    y = jnp.concatenate([_heads_to_rows(outs[t * per:(t + 1) * per], HEADS) for t in range(tiles)], axis=0)
    o_ref[...] = _group_rms(y).astype(BF16)


def _neigh_kernel(q_ref, k_ref, v_ref, kc_ref, vc_ref, *refs, rows, seq, tiles):
    bias_refs, (o_ref, kt_scr, v_scr, s_scr, p_scr) = refs[:tiles], refs[tiles:]
    _stage_once_per_batch_row([(k_ref, 0), (kc_ref, 0)], [(v_ref, 0), (vc_ref, 0)], kt_scr, v_scr)
    nk = NA_KROWS * GRID_W
    nc = v_scr.shape[1] - seq
    qrows = NA_QROWS * GRID_W

    def bias_thunk(bias_ref, h, r0):
        return lambda ti: (bias_ref[0, h, r0:r0 + ATT_RB, KEY_TILE * ti:KEY_TILE * ti + KEY_TILE]
                           if KEY_TILE * ti < nk else None)

    items = []
    for t, bias_ref in enumerate(bias_refs):
        rb = tiles * pl.program_id(1) + t
        rs = jnp.clip(NA_QROWS * rb - NA_ROWS // 2, 0, rows - NA_KROWS)
        k0 = pl.multiple_of(rs * GRID_W, KEY_TILE)
        items += [_AttItem(_q_thunk(q_ref, t * qrows + r0, h), kt_scr.at[h], v_scr.at[h], [(k0, nk), (seq, nc)],
                           bias=bias_thunk(bias_ref, h, r0))
                  for h in range(HEADS) for r0 in _row_blocks(qrows)]
    outs = _run_att_items(items, s_scr, p_scr)
    per = len(outs) // tiles
    y = jnp.concatenate([_heads_to_rows(outs[t * per:(t + 1) * per], HEADS) for t in range(tiles)], axis=0)
    o_ref[...] = _group_rms(y).astype(BF16)


def _diff_lambda(lam_ref, lam_init):
    lp = lam_ref[...]
    a = jnp.sum(lp[0:1] * lp[1:2], axis=-1, keepdims=True)
    b = jnp.sum(lp[2:3] * lp[3:4], axis=-1, keepdims=True)
    return jnp.exp(a) - jnp.exp(b) + lam_init


def _diff_kernel(lam_ref, q_ref, k_ref, v_ref, kc_ref, vc_ref, o_ref, kt_scr, v_scr, s_scr, p_scr, *, lam_init, tq):
    _stage_once_per_batch_row([(k_ref, 0), (kc_ref, 0)], [(v_ref, 0), (vc_ref, 0)], kt_scr, v_scr)
    lam = _diff_lambda(lam_ref, lam_init)
    nk = v_scr.shape[1]
    first = lax.broadcasted_iota(I32, (ATT_RB, HEAD_DIM), 1) < DIFF_DIM
    zero = jnp.zeros((ATT_RB, HEAD_DIM), BF16)

    def q_map(r0, h, half):
        def thunk():
            qh = q_ref[r0:r0 + ATT_RB, 64 * h:64 * h + 64]
            return jnp.where(first, qh, zero) if half == 0 else jnp.where(first, zero, qh)
        return thunk

    items = [_AttItem(q_map(r0, h, half), kt_scr.at[h], v_scr.at[h], [(0, nk)])
             for h in range(HEADS) for r0 in _row_blocks(tq) for half in range(2)]
    outs = _run_att_items(items, s_scr, p_scr)
    diffs = [outs[i] - lam * outs[i + 1] for i in range(0, len(outs), 2)]
    per = len(diffs) // HEADS
    heads = [_group_rms(jnp.concatenate(diffs[h * per:(h + 1) * per], axis=0)) * (1.0 - lam_init)
             for h in range(HEADS)]
    o_ref[...] = jnp.concatenate(heads, axis=1).astype(BF16)


def _softmax_parts(scores, extra=None):
    m = scores[0].max(axis=-1, keepdims=True)
    for s in scores[1:]:
        m = jnp.maximum(m, s.max(axis=-1, keepdims=True))
    if extra is not None:
        m = jnp.maximum(m, extra)
    ps = [jnp.exp2(s - m) for s in scores]
    l = ps[0].sum(axis=-1, keepdims=True)
    for p in ps[1:]:
        l = l + p.sum(axis=-1, keepdims=True)
    if extra is not None:
        l = l + jnp.exp2(extra - m)
    return ps, 1.0 / l


def _ctx_kernel(sink_ref, lam_ref, u_ref, o_ref, *, lam_init, ctx_len):
    for g in range(u_ref.shape[0] // ctx_len):
        _ctx_rows(sink_ref, lam_ref, u_ref, o_ref, g * ctx_len, ctx_len, lam_init)


def _ctx_rows(sink_ref, lam_ref, u_ref, o_ref, r0, L, lam_init):
    top = lax.broadcasted_iota(I32, (2 * L, 1), 0) < L

    def cols(blk, lo, width=64):
        return u_ref[r0:r0 + L, 256 * blk + lo:256 * blk + lo + width]

    for q_blk, kv_blk, out_col, use_sink in ((COL_A_Q, COL_A_KV, 0, True), (COL_D_Q, COL_D_KV, 768, False)):
        outs = [None] * HEADS
        for j in range(2):
            q2 = jnp.concatenate([cols(q_blk, 128 * j), cols(q_blk, 128 * j + 64)], axis=0)
            snk = jnp.where(top, sink_ref[2 * j], sink_ref[2 * j + 1]) * LOG2E if use_sink else None
            (p,), inv = _softmax_parts([_nt_dot(q2, cols(kv_blk, 64 * j))], extra=snk)
            o = _dot(p.astype(BF16), cols(kv_blk, 128 + 64 * j)) * inv
            outs[2 * j], outs[2 * j + 1] = o[:L], o[L:]
        o_ref[r0:r0 + L, out_col:out_col + 256] =_group_rms(jnp.concatenate(outs, axis=1)).astype(BF16)
    outs = []
    for h in range(HEADS):
        (p,), inv = _softmax_parts([_nt_dot(cols(COL_B_Q, 64 * h), cols(COL_B_K, 64 * h))])
        outs.append(_dot(p.astype(BF16), cols(COL_B_V, 64 * h)) * inv)
    o_ref[r0:r0 + L, 256:512] =_group_rms(jnp.concatenate(outs, axis=1)).astype(BF16)
    lam = _diff_lambda(lam_ref, lam_init)
    first = lax.broadcasted_iota(I32, (L, HEAD_DIM), 1) < DIFF_DIM
    zero = jnp.zeros((L, HEAD_DIM), BF16)
    outs = []
    for h in range(HEADS):
        qh, kh = cols(COL_C_Q, 64 * h), cols(COL_C_K, 64 * h)
        maps = []
        for half in range(2):
            qm = jnp.where(first, qh, zero) if half == 0 else jnp.where(first, zero, qh)
            (p,), inv = _softmax_parts([_nt_dot(qm, kh)])
            maps.append(p * inv)
        o = _dot((maps[0] - lam * maps[1]).astype(BF16), cols(COL_C_V, 64 * h))
        outs.append(_group_rms(o) * (1.0 - lam_init))
    o_ref[r0:r0 + L, 512:768] = jnp.concatenate(outs, axis=1).astype(BF16)


def _attention_calls(u, sink, lam_p, bias, wmask, *, layer, batch, seq, ctx_len, need_ctx):
    lam_init = 0.8 - 0.6 * math.exp(-0.3 * layer)
    n_all = u.shape[0]
    tq = ATT_TQ
    nq = seq // tq
    nk_all = seq + ctx_len
    cb0 = batch * seq // ctx_len
    y_shape = jax.ShapeDtypeStruct((n_all, D_MODEL), BF16)
    smem = pl.BlockSpec(memory_space=pltpu.SMEM)
    any_spec = pl.BlockSpec(memory_space=pl.ANY)
    lam_spec2 = pl.BlockSpec((4, DIFF_DIM), lambda b, i: (0, 0))
    cp = _cparams("arbitrary", "arbitrary")

    def scratch(n_heads, n_keys, lag=1):
        rows_ = (2 * lag + 2) * ATT_RB
        return [pltpu.VMEM((n_heads, HEAD_DIM, nk_all), BF16), pltpu.VMEM((n_heads, nk_all, 128), BF16),
                pltpu.VMEM((rows_, n_keys), F32), pltpu.VMEM((rows_, n_keys), BF16)]

    def qspec(col, t=tq):
        return pl.BlockSpec((t, 256), lambda b, i: (b * (seq // t) + i, col))

    tqf = ATT_TQ_FULL
    nqf = seq // tqf

    def latspec(col):
        return pl.BlockSpec((seq, 256), lambda b, i: (b, col))

    def ctxspec(col):
        return pl.BlockSpec((ctx_len, 256), lambda b, i: (cb0 + b, col))

    kw = tq + 2 * WINDOW

    def case_specs(shape, n_tiles, per_step):
        def spec(t):
            def index(b, i):
                j = per_step * i + t
                return (jnp.where(j == 0, 0, jnp.where(j == n_tiles - 1, 2, 1)),) + (0,) * (len(shape) - 1)
            return pl.BlockSpec(shape, index)
        return [spec(t) for t in range(per_step)]

    wt = WINDOW_TILES
    y = pl.pallas_call(
        functools.partial(_window_kernel, tq=tq, seq=seq, tiles=wt),
        out_shape=y_shape, grid=(batch, nq // wt),
        in_specs=[smem, qspec(COL_A_Q, wt * tq), latspec(COL_A_KV), ctxspec(COL_A_KV)]
        + case_specs((1, tq, kw), nq, wt),
        out_specs=qspec(0, wt * tq), scratch_shapes=scratch(2, kw + ctx_len, WINDOW_LAG),
        compiler_params=cp, name="mix_window",
    )(sink, u, u, u, *([wmask] * wt))

    rows = seq // GRID_W
    nrb = rows // NA_QROWS
    qrows = NA_QROWS * GRID_W
    nkb = NA_KROWS * GRID_W
    nt = NA_TILES
    y = pl.pallas_call(
        _drop_arg(functools.partial(_neigh_kernel, rows=rows, seq=seq, tiles=nt), 5 + nt),
        out_shape=y_shape, grid=(batch, nrb // nt),
        in_specs=[qspec(COL_B_Q, nt * qrows), latspec(COL_B_K), latspec(COL_B_V),
                  ctxspec(COL_B_K), ctxspec(COL_B_V)] + case_specs((1, HEADS, qrows, nkb), nrb, nt) + [any_spec],
        out_specs=qspec(1, nt * qrows), scratch_shapes=scratch(HEADS, nkb + ctx_len),
        input_output_aliases={5 + nt: 0}, compiler_params=cp, name="mix_neigh",
    )(u, u, u, u, u, *([bias] * nt), y)

    y = pl.pallas_call(
        _drop_arg(functools.partial(_diff_kernel, lam_init=lam_init, tq=tqf), 6),
        out_shape=y_shape, grid=(batch, nqf),
        in_specs=[lam_spec2, qspec(COL_C_Q, tqf), latspec(COL_C_K), latspec(COL_C_V),
                  ctxspec(COL_C_K), ctxspec(COL_C_V), any_spec],
        out_specs=qspec(2, tqf), scratch_shapes=scratch(HEADS, nk_all),
        input_output_aliases={6: 0}, compiler_params=cp, name="mix_diff",
    )(lam_p, u, u, u, u, u, y)

    y = pl.pallas_call(
        _drop_arg(functools.partial(_dense_kernel, tq=tqf), 3),
        out_shape=y_shape, grid=(batch, nqf),
        in_specs=[qspec(COL_D_Q, tqf), latspec(COL_D_KV), ctxspec(COL_D_KV), any_spec],
        out_specs=qspec(3, tqf), scratch_shapes=scratch(2, nk_all),
        input_output_aliases={3: 0}, compiler_params=cp, name="mix_dense",
    )(u, u, u, y)

    if need_ctx:
        y = pl.pallas_call(
            _drop_arg(functools.partial(_ctx_kernel, lam_init=lam_init, ctx_len=ctx_len), 3),
            out_shape=y_shape, grid=(batch // CTX_ROWS_PER_STEP,),
            in_specs=[smem, pl.BlockSpec((4, DIFF_DIM), lambda b: (0, 0)),
                      pl.BlockSpec((CTX_ROWS_PER_STEP * ctx_len, D_IN), lambda b: (cb0 // CTX_ROWS_PER_STEP + b, 0)),
                      any_spec],
            out_specs=pl.BlockSpec((CTX_ROWS_PER_STEP * ctx_len, D_MODEL), lambda b: (cb0 // CTX_ROWS_PER_STEP + b, 0)),
            input_output_aliases={3: 0}, compiler_params=_cparams("arbitrary"), name="mix_ctx",
        )(sink, lam_p, u, y)
    return y


def _select4(idx, vals):
    return jnp.where(idx == 0, vals[0], jnp.where(idx == 1, vals[1], jnp.where(idx == 2, vals[2], vals[3])))


def _outproj_kernel(y_ref, x_ref, mod_ref, gain_ref, w_ref, g_ref, rw_ref, rb_ref, xo_ref, h_ref, cls_ref, *, tm):
    hm = tm // 2

    def project(r0):
        yg = (y_ref[r0:r0 + hm, :].astype(F32) * gain_ref[...]).astype(BF16)
        return _dot(yg, w_ref[...])

    def normalise(r0, proj):
        xm = x_ref[r0:r0 + hm, :] + mod_ref[0, 2:3, :] * proj
        xo_ref[r0:r0 + hm, :] = xm
        hn = xm * lax.rsqrt(jnp.mean(xm * xm, axis=-1, keepdims=True) + EPS) * g_ref[...]
        h2 = hn * (1.0 + mod_ref[0, 4:5, :]) + mod_ref[0, 3:4, :]
        hi = h2.astype(BF16)
        hi32 = hi.astype(F32)
        lo = (h2 - hi32).astype(BF16)
        bits = lax.bitcast_convert_type(hi32, U32)
        half = D_MODEL // 2
        h_ref[r0:r0 + hm, 0:half] = (bits[:, 0:half] >> 16) | (bits[:, half:D_MODEL] & jnp.uint32(0xFFFF0000))
        return (_dot(hi, rw_ref[...]) + _dot(lo, rw_ref[...])).T

    def route(r0, lt):
        score = _sigmoid(lt[0:N_EXPERTS] + lt[N_EXPERTS:2 * N_EXPERTS])
        sel = score + rb_ref[...]
        srow = [sel[e:e + 1] for e in range(N_EXPERTS)]
        crow = [score[e:e + 1] for e in range(N_EXPERTS)]

        gidx = jnp.zeros((1, hm), I32)
        gbest = None
        for g in range(N_EXPERTS // EXPERTS_PER_GROUP):
            v = srow[4 * g:4 * g + 4]
            best = None
            for a in range(4):
                for b in range(a + 1, 4):
                    t = v[a] + v[b]
                    best = t if best is None else jnp.maximum(best, t)
            if g == 0:
                gbest = best
            else:
                better = best > gbest
                gidx = jnp.where(better, g, gidx)
                gbest = jnp.where(better, best, gbest)

        iv = [_select4(gidx, [srow[4 * g + i] for g in range(4)]) for i in range(4)]
        sv = [_select4(gidx, [crow[4 * g + i] for g in range(4)]) for i in range(4)]
        chosen = []
        for i in range(4):
            rank = jnp.zeros((1, hm), I32)
            for j in range(4):
                if j != i:
                    beats = (iv[j] >= iv[i]) if j < i else (iv[j] > iv[i])
                    rank = rank + jnp.where(beats, 1, 0)
            chosen.append(rank < 2)
        lo_i = jnp.where(chosen[0], 0, jnp.where(chosen[1], 1, 2))
        hi_i = jnp.where(chosen[3], 3, jnp.where(chosen[2], 2, 1))
        pair = jnp.where(lo_i == 0, hi_i - 1, jnp.where(lo_i == 1, hi_i + 1, 5))
        cls_ref[0, :, r0:r0 + hm] = gidx * N_PAIRS + pair
        s_lo = jnp.where(lo_i == 0, sv[0], jnp.where(lo_i == 1, sv[1], sv[2]))
        s_hi = jnp.where(hi_i == 3, sv[3], jnp.where(hi_i == 2, sv[2], sv[1]))
        inv = 1.0 / (s_lo + s_hi)
        rowi = lax.broadcasted_iota(I32, (128, hm), 0)
        wts = jnp.where(rowi == 0, s_lo * inv, jnp.where(rowi == 1, s_hi * inv, 0.0))
        h_ref[r0:r0 + hm, D_MODEL // 2:EXT_W] = lax.bitcast_convert_type(wts.T, U32)

    proj0 = project(0)
    proj1 = project(hm)
    lt0 = normalise(0, proj0)
    lt1 = normalise(hm, proj1)
    route(0, lt0)
    route(hm, lt1)


def _outproj_call(y, x, mod, gain, w_out, g, rw, rb, *, n_proc, batch, seq):
    tm = TOKEN_TILE
    n_tiles = n_proc // tm
    tile = pl.BlockSpec((tm, D_MODEL), lambda i: (i, 0))
    return pl.pallas_call(
        functools.partial(_outproj_kernel, tm=tm),
        out_shape=(jax.ShapeDtypeStruct((n_proc, D_MODEL), F32),
                   jax.ShapeDtypeStruct((n_proc, EXT_W), U32),
                   jax.ShapeDtypeStruct((n_tiles, 1, tm), I32)),
        grid=(n_tiles,),
        in_specs=[tile, tile, _mod_spec(batch * seq // tm, seq // tm, batch), _const_spec((1, D_MODEL)),
                  _const_spec((D_MODEL, D_MODEL)), _const_spec((1, D_MODEL)), _const_spec((D_MODEL, 128)),
                  _const_spec((N_EXPERTS, 1))],
        out_specs=(tile, pl.BlockSpec((tm, EXT_W), lambda i: (i, 0)), pl.BlockSpec((1, 1, tm), lambda i: (i, 0, 0))),
        compiler_params=_cparams("arbitrary"), name="outproj_router",
    )(y, x, mod, gain, w_out, g, rw, rb)


def _rank_kernel(cls_ref, pos_ref, bcls_ref, nval_ref, *, blk):
    c = cls_ref[...]
    nr = c.shape[0]
    nbp = bcls_ref.shape[1]
    upper = (lax.broadcasted_iota(I32, (128, 128), 0) <= lax.broadcasted_iota(I32, (128, 128), 1))
    upper = jnp.where(upper, 1.0, 0.0).astype(BF16)
    lower = (lax.broadcasted_iota(I32, (nr, nr), 1) < lax.broadcasted_iota(I32, (nr, nr), 0))
    lower = jnp.where(lower, 1.0, 0.0).astype(BF16)
    bidx = lax.broadcasted_iota(I32, (1, nbp), 1).astype(F32)
    pos = jnp.zeros((nr, 128), F32)
    bcls = jnp.zeros((1, nbp), F32)
    nval = jnp.zeros((1, nbp), F32)
    start = jnp.zeros((1, 1), F32)
    for k in range(N_CLASSES):
        m = c == k
        incl = _dot(jnp.where(m, 1.0, 0.0).astype(BF16), upper)
        tot = incl[:, 127:128]
        above = _dot(lower, jnp.broadcast_to(tot, (nr, 128)).astype(BF16))
        cnt = above[nr - 1:nr, 0:1] + tot[nr - 1:nr, 0:1]
        nblk = jnp.floor((cnt + (blk - 1)) * (1.0 / blk))
        pos = jnp.where(m, start * blk + above + incl - 1.0, pos)
        end = start + nblk
        bcls = bcls + jnp.where(bidx >= end, 1.0, 0.0)
        nval = nval + jnp.where((bidx >= start) & (bidx < end), jnp.clip(cnt - (bidx - start) * blk, 0.0, blk), 0.0)
        start = end
    pos_ref[...] = pos.astype(I32)
    bcls_ref[...] = jnp.minimum(bcls, N_CLASSES - 1.0).astype(I32)
    nval_ref[...] = nval.astype(I32)


def _rank_call(cls2d, n_blocks):
    nr = cls2d.shape[0]
    nbp = -(-n_blocks // 128) * 128
    return pl.pallas_call(
        functools.partial(_rank_kernel, blk=MOE_BLK),
        out_shape=(jax.ShapeDtypeStruct((nr, 128), I32), jax.ShapeDtypeStruct((1, nbp), I32),
                   jax.ShapeDtypeStruct((1, nbp), I32)),
        grid=(1,),
        in_specs=[_const_spec((nr, 128))],
        out_specs=(_const_spec((nr, 128)), _const_spec((1, nbp)), _const_spec((1, nbp))),
        compiler_params=_cparams("arbitrary"), name="class_rank",
    )(cls2d)


def _scatter_kernel(pos_ref, h_ref, xs_ref, sem):
    n = h_ref.shape[0]
    for r in range(n):
        pltpu.make_async_copy(h_ref.at[pl.ds(r, 1)], xs_ref.at[pl.ds(pos_ref[0, 0, r], 1)], sem).start()
    pltpu.make_async_copy(h_ref, xs_ref.at[pl.ds(0, n)], sem).wait()


def _scatter_call(h_ext, pos3, n_slots):
    n, w = h_ext.shape
    r = ROW_DMA
    return pl.pallas_call(
        _scatter_kernel,
        out_shape=jax.ShapeDtypeStruct((n_slots, w), h_ext.dtype),
        grid=(n // r,),
        in_specs=[pl.BlockSpec((1, 1, r), lambda i: (i, 0, 0), memory_space=pltpu.SMEM),
                  pl.BlockSpec((r, w), lambda i: (i, 0))],
        out_specs=pl.BlockSpec(memory_space=pl.ANY),
        scratch_shapes=[pltpu.SemaphoreType.DMA(())],
        compiler_params=_cparams("arbitrary"), name="row_scatter",
    )(pos3, h_ext)


def _moe_kernel(elo_ref, ehi_ref, nval_ref, xs_ref, wg_lo, wu_lo, wd_lo, wg_hi, wu_hi, wd_hi, ys_ref):
    del elo_ref, ehi_ref
    nv = nval_ref[pl.program_id(0)]

    @pl.when(nv > 0)
    def _():
        live = lax.broadcasted_iota(I32, (MOE_BLK, 1), 0) < nv
        half = D_MODEL // 2
        words = jnp.where(live, xs_ref[:, 0:half], jnp.uint32(0))
        x = jnp.concatenate([lax.bitcast_convert_type(words << 16, F32).astype(BF16),
                             lax.bitcast_convert_type(words & jnp.uint32(0xFFFF0000), F32).astype(BF16)], axis=1)
        acts = []
        for wg, wu, col in ((wg_lo, wu_lo, half), (wg_hi, wu_hi, half + 1)):
            w = jnp.where(live, lax.bitcast_convert_type(xs_ref[:, col:col + 1], F32), 0.0)
            g = _dot(x, wg[0])
            acts.append((g * _sigmoid(g) * _dot(x, wu[0]) * w).astype(BF16))
        ys_ref[...] = _dot(acts[0], wd_lo[0]) + _dot(acts[1], wd_hi[0])

    @pl.when(nv == 0)
    def _():
        ys_ref[...] = jnp.zeros_like(ys_ref)


def _moe_call(xs, e_lo, e_hi, nval, w_gate, w_up, w_down):
    n_blocks = xs.shape[0] // MOE_BLK

    def wspec(shape, which):
        return pl.BlockSpec((1,) + shape, lambda b, elo, ehi, nv: ((elo, ehi)[which][b], 0, 0))

    gu, dn = (D_MODEL, D_EXPERT), (D_EXPERT, D_MODEL)
    grid_spec = pltpu.PrefetchScalarGridSpec(
        num_scalar_prefetch=3, grid=(n_blocks,),
        in_specs=[pl.BlockSpec((MOE_BLK, EXT_W), lambda b, elo, ehi, nv: (b, 0)),
                  wspec(gu, 0), wspec(gu, 0), wspec(dn, 0), wspec(gu, 1), wspec(gu, 1), wspec(dn, 1)],
        out_specs=pl.BlockSpec((MOE_BLK, D_MODEL), lambda b, elo, ehi, nv: (b, 0)),
    )
    return pl.pallas_call(
        _moe_kernel, out_shape=jax.ShapeDtypeStruct((xs.shape[0], D_MODEL), F32), grid_spec=grid_spec,
        compiler_params=_cparams("arbitrary"), name="moe_experts",
    )(e_lo, e_hi, nval, xs, w_gate, w_up, w_down, w_gate, w_up, w_down)


def _final_kernel(x_ref, pos_ref, pos_next_ref, ys_ref, mod_ref, o_ref, fbuf, fsems):
    f_ref = _prefetched_rows(pos_ref, pos_next_ref, ys_ref, fbuf, fsems)
    o_ref[...] = x_ref[...] + mod_ref[0, 5:6, :] * f_ref[...]


def _final_call(x, ys, pos, mod, *, batch, seq):
    tm = TOKEN_TILE
    n = batch * seq
    tile = pl.BlockSpec((tm, D_MODEL), lambda i: (i, 0))
    pos_tiles = pos.reshape(n // tm, 1, tm)
    return pl.pallas_call(
        _final_kernel, out_shape=jax.ShapeDtypeStruct((n, D_MODEL), F32), grid=(n // tm,),
        in_specs=[tile] + _pos_tile_specs(n // tm, tm) + [pl.BlockSpec(memory_space=pl.ANY),
                                                          _mod_spec(n // tm, seq // tm, batch)],
        out_specs=tile, scratch_shapes=[pltpu.VMEM((2, tm, D_MODEL), F32), pltpu.SemaphoreType.DMA((2,))],
        compiler_params=_cparams("arbitrary"), name="final_residual",
    )(x, pos_tiles, pos_tiles, ys, mod)


def _rope_tables(seq, ident_rows):
    pos = jnp.arange(seq, dtype=jnp.int32)
    row = (pos // GRID_W).astype(F32)[:, None]
    col = (pos % GRID_W).astype(F32)[:, None]

    def tab(half):
        inv = ROPE_BASE ** (-jnp.arange(half, dtype=F32) / half)
        ar, ac = row * inv, col * inv
        cos = jnp.concatenate([jnp.cos(ar), jnp.cos(ar), jnp.cos(ac), jnp.cos(ac)], axis=1)
        sin = jnp.concatenate([-jnp.sin(ar), jnp.sin(ar), -jnp.sin(ac), jnp.sin(ac)], axis=1)
        reps = 128 // (4 * half)
        cos = jnp.concatenate([jnp.tile(cos, (1, reps)), jnp.ones((ident_rows, 128), F32)], axis=0)
        sin = jnp.concatenate([jnp.tile(sin, (1, reps)), jnp.zeros((ident_rows, 128), F32)], axis=0)
        return cos, sin

    c64, s64 = tab(HEAD_DIM // 4)
    c32, s32 = tab(DIFF_DIM // 4)
    return c64, s64, c32, s32


def _block_diag_ones(group):
    idx = np.arange(256) // group
    return jnp.asarray((idx[:, None] == idx[None, :]).astype(np.float32), dtype=BF16)


def _qk_gains(g_win, g_na, g_diff, g_gqa):
    s64, s32 = HEAD_DIM ** -0.5 * LOG2E, DIFF_DIM ** -0.5 * LOG2E
    rows = [jnp.tile(g_win[0] * s64, 2), jnp.tile(g_win[1], 2), jnp.tile(g_na[0] * s64, 2), jnp.tile(g_na[1], 2),
            jnp.tile(g_diff[0] * s32, 4), jnp.tile(g_diff[1], 4), jnp.tile(g_gqa[0] * s64, 2), jnp.tile(g_gqa[1], 2)]
    return jnp.stack(rows).astype(F32)


def _na_bias_tiles(rpb, rows):
    w = GRID_W
    qc = np.arange(w)[:, None]
    kc = np.arange(w)[None, :]
    c_start = np.clip(qc - NA_COLS // 2, 0, w - NA_COLS)
    col_ok = (kc >= c_start) & (kc < c_start + NA_COLS)
    dc = np.clip(kc - qc + NA_COLS - 1, 0, 2 * NA_COLS - 2)
    t = jnp.where(jnp.asarray(col_ok), rpb.astype(F32)[:, :, dc] * LOG2E, NEG)
    neg = jnp.full((rpb.shape[0], w, w), NEG, F32)
    cases = []
    for r0, rs in ((0, 0), (NA_QROWS, 0), (rows - NA_QROWS, rows - NA_KROWS)):
        qtiles = []
        for i in range(NA_QROWS):
            qrow = r0 + i
            r_start = min(max(qrow - NA_ROWS // 2, 0), rows - NA_ROWS)
            blks = []
            for j in range(NA_KROWS):
                krow = rs + j
                blks.append(t[:, krow - qrow + NA_ROWS - 1] if r_start <= krow < r_start + NA_ROWS else neg)
            qtiles.append(jnp.concatenate(blks, axis=2))
        cases.append(jnp.concatenate(qtiles, axis=1))
    return jnp.stack(cases)


def _window_mask_tiles(tq):
    kw = tq + 2 * WINDOW
    r = lax.broadcasted_iota(I32, (3, tq, kw), 1)
    c = lax.broadcasted_iota(I32, (3, tq, kw), 2)
    shift = lax.broadcasted_iota(I32, (3, tq, kw), 0) * WINDOW
    return jnp.where(jnp.abs(c - r - shift) <= WINDOW, 0.0, NEG).astype(F32)


def _router_weights(router_w):
    hi = router_w.astype(BF16)
    lo = (router_w - hi.astype(F32)).astype(BF16)
    pad = jnp.zeros((router_w.shape[0], 128 - 2 * N_EXPERTS), BF16)
    return jnp.concatenate([hi, lo, pad], axis=1)


def kernel(x, c, ctx, c_ctx, ada_w, ada_b, norm_mix_g, norm_ffn_g, w_in, qk_g_win, qk_g_na, qk_g_diff, qk_g_gqa,
           sink_win, rpb_na, lambda_diff, out_gain, w_out, router_w, router_b, w_gate, w_up, w_down):
    batch, seq, d = x.shape
    ctx_len = ctx.shape[1]
    depth = w_in.shape[0]
    n_lat, n_ctx = batch * seq, batch * ctx_len
    n_all = n_lat + n_ctx
    rows = seq // GRID_W
    assert d == D_MODEL and seq % TOKEN_TILE == 0 and n_ctx % TOKEN_TILE == 0 and ctx_len % 128 == 0
    assert rows % NA_QROWS == 0 and rows >= NA_KROWS + NA_QROWS and seq >= ATT_TQ + 2 * WINDOW
    assert ctx_len % 128 == 0 and (NA_KROWS * GRID_W) % KEY_TILE == 0

    pad_rows = -(-(batch + 1) // 8) * 8
    s_all = jnp.concatenate([c, c_ctx[None, :], jnp.zeros((pad_rows - batch - 1, d), F32)], axis=0)
    mods = _ada_call(s_all, ada_w, ada_b).reshape(depth, pad_rows, ADA_CHUNKS, d)

    xs_all = jnp.concatenate([x.reshape(n_lat, d), ctx.reshape(n_ctx, d)], axis=0)
    tabs = _rope_tables(seq, TOKEN_TILE)
    wmask = _window_mask_tiles(ATT_TQ)
    bd64, bd32 = _block_diag_ones(HEAD_DIM), _block_diag_ones(DIFF_DIM)
    rw = _router_weights(router_w)
    rb = router_b.astype(F32).reshape(N_EXPERTS, 1)
    lo_tab = jnp.asarray([EXPERTS_PER_GROUP * (k // N_PAIRS) + PAIR_LO[k % N_PAIRS] for k in range(N_CLASSES)], I32)
    hi_tab = jnp.asarray([EXPERTS_PER_GROUP * (k // N_PAIRS) + PAIR_HI[k % N_PAIRS] for k in range(N_CLASSES)], I32)

    ffn, mod_prev = None, None
    for layer in range(depth):
        need_ctx = layer < depth - 1
        mod = mods[layer]
        qkg = _qk_gains(qk_g_win[layer], qk_g_na[layer], qk_g_diff[layer], qk_g_gqa[layer])
        xs_all, u = _inproj_call(xs_all, ffn, mod_prev, mod, norm_mix_g[layer].reshape(1, d),
                                 w_in[layer].astype(BF16), qkg, bd64, bd32, tabs, batch=batch, seq=seq)
        y = _attention_calls(u, sink_win[layer].astype(F32), lambda_diff[layer].astype(F32),
                             _na_bias_tiles(rpb_na[layer], rows), wmask, layer=layer, batch=batch, seq=seq,
                             ctx_len=ctx_len, need_ctx=need_ctx)
        n_proc = n_all if need_ctx else n_lat
        x_mid, h_ext, cls = _outproj_call(y, xs_all, mod, out_gain[layer].reshape(1, d), w_out[layer].astype(BF16),
                                          norm_ffn_g[layer].reshape(1, d), rw, rb, n_proc=n_proc, batch=batch, seq=seq)
        n_blocks = n_proc // MOE_BLK + N_CLASSES
        pos, bcls, nval = _rank_call(cls.reshape(n_proc // 128, 128), n_blocks)
        pos3 = pos.reshape(n_proc // ROW_DMA, 1, ROW_DMA)
        bcls, nval = bcls[0, :n_blocks], nval[0, :n_blocks]
        sorted_rows = _scatter_call(h_ext, pos3, n_blocks * MOE_BLK)
        ys = _moe_call(sorted_rows, lo_tab[bcls], hi_tab[bcls], nval, w_gate[layer].astype(BF16),
                       w_up[layer].astype(BF16), w_down[layer].astype(BF16))
        ffn = (ys, pos)
        xs_all, mod_prev = x_mid, mod
    out = _final_call(xs_all, ffn[0], ffn[1], mod_prev, batch=batch, seq=seq)
    return out.reshape(batch, seq, d)
```

```python
import functools
import math

import numpy as np
import jax
import jax.numpy as jnp
from jax import lax
from jax.experimental import pallas as pl
from jax.experimental.pallas import tpu as pltpu

F32 = jnp.float32
BF16 = jnp.bfloat16
I32 = jnp.int32
U32 = jnp.uint32

D_MODEL = 1024
GRID_W = 64
HEAD_DIM = 64
HEADS = 4
DIFF_DIM = 32
WINDOW = 128
NA_ROWS = 8
NA_COLS = 16
ROPE_BASE = 10000.0
N_EXPERTS = 16
EXPERTS_PER_GROUP = 4
D_EXPERT = 512
ADA_CHUNKS = 6
EPS = 1e-6
NEG = -1e30
D_IN = 2560

COL_A_Q, COL_A_KV, COL_B_Q, COL_B_K, COL_B_V, COL_C_Q, COL_C_K, COL_C_V, COL_D_Q, COL_D_KV = range(10)

TOKEN_TILE = 512
ATT_TQ = 256
ATT_TQ_FULL = 512
NA_QROWS = 4
NA_KROWS = 12
KEY_TILE = 256
ATT_RB = 128
WINDOW_TILES = 1
WINDOW_LAG = 1
NA_TILES = 2
CTX_ROWS_PER_STEP = 1
LOG2E = math.log2(math.e)
MOE_BLK = 512
ROW_DMA = 512
N_PAIRS = 6
N_CLASSES = (N_EXPERTS // EXPERTS_PER_GROUP) * N_PAIRS
PAIR_LO = (0, 0, 0, 1, 1, 2)
PAIR_HI = (1, 2, 3, 2, 3, 3)
EXT_W = D_MODEL // 2 + 128
VMEM_LIMIT = 56 * 1024 * 1024


def _cparams(*sem):
    return pltpu.CompilerParams(dimension_semantics=sem, vmem_limit_bytes=VMEM_LIMIT)


def _nt_dot(a, b):
    return lax.dot_general(a, b, (((1,), (1,)), ((), ())), preferred_element_type=F32)


def _dot(a, b):
    return jnp.dot(a, b, preferred_element_type=F32)


def _sigmoid(x):
    return 1.0 / (1.0 + jnp.exp(-x))


def _drop_arg(kern, pos):
    def wrapped(*refs):
        return kern(*refs[:pos], *refs[pos + 1:])
    return wrapped


def _ada_kernel(s_ref, w_ref, b_ref, o_ref):
    s = s_ref[...]
    act = s * _sigmoid(s)
    o_ref[0] = jnp.dot(act, w_ref[0], precision=lax.Precision.HIGHEST,
                       preferred_element_type=F32) + b_ref[0]


def _ada_call(s_all, ada_w, ada_b):
    depth, d, n = ada_w.shape
    rows = s_all.shape[0]
    tn = 1536
    return pl.pallas_call(
        _ada_kernel,
        out_shape=jax.ShapeDtypeStruct((depth, rows, n), F32),
        grid=(depth, n // tn),
        in_specs=[
            pl.BlockSpec((rows, d), lambda l, j: (0, 0)),
            pl.BlockSpec((1, d, tn), lambda l, j: (l, 0, j)),
            pl.BlockSpec((1, 1, tn), lambda l, j: (l, 0, j)),
        ],
        out_specs=pl.BlockSpec((1, rows, tn), lambda l, j: (l, 0, j)),
        compiler_params=_cparams("arbitrary", "arbitrary"),
        name="ada_mod",
    )(s_all, ada_w, ada_b.reshape(depth, 1, n))


def _start_row_gather(pos_ref, src_ref, dst_ref, sem):
    for r in range(dst_ref.shape[0]):
        pltpu.make_async_copy(src_ref.at[pl.ds(pos_ref[0, 0, r], 1)], dst_ref.at[pl.ds(r, 1)], sem).start()


def _wait_row_gather(src_ref, dst_ref, sem):
    pltpu.make_async_copy(src_ref.at[pl.ds(0, dst_ref.shape[0])], dst_ref, sem).wait()


def _prefetched_rows(pos_ref, pos_next_ref, src_ref, buf, sems):
    i = pl.program_id(0)
    slot = i % 2

    @pl.when(i == 0)
    def _():
        _start_row_gather(pos_ref, src_ref, buf.at[0], sems.at[0])

    @pl.when(i + 1 < pl.num_programs(0))
    def _():
        _start_row_gather(pos_next_ref, src_ref, buf.at[1 - slot], sems.at[1 - slot])

    _wait_row_gather(src_ref, buf.at[slot], sems.at[slot])
    return buf.at[slot]


def _inproj_kernel(*refs, has_ffn, tm, n_lat_tiles, tiles_per_batch, seq):
    if has_ffn:
        (x_ref, pos_ref, pos_next_ref, ys_ref, modp_ref, mod_ref, g_ref, w_ref, qkg_ref, bd64_ref, bd32_ref,
         c64_ref, s64_ref, c32_ref, s32_ref, xo_ref, u_ref, fbuf, fsems) = refs
        f_ref = _prefetched_rows(pos_ref, pos_next_ref, ys_ref, fbuf, fsems)
    else:
        (xl_ref, xc_ref, mod_ref, g_ref, w_ref, qkg_ref, bd64_ref, bd32_ref,
         c64_ref, s64_ref, c32_ref, s32_ref, u_ref) = refs
    i = pl.program_id(0)
    hm = tm // 2

    def prologue(r0):
        x = x_ref[r0:r0 + hm, :] if has_ffn else _two_source_rows(xl_ref, xc_ref, r0, hm, n_lat_tiles)
        if has_ffn:
            x = x + modp_ref[0, 5:6, :] * f_ref[r0:r0 + hm, :]
            xo_ref[r0:r0 + hm, :] = x
        ms = jnp.mean(x * x, axis=-1, keepdims=True)
        hn = x * lax.rsqrt(ms + EPS) * g_ref[...]
        return (hn * (1.0 + mod_ref[0, 1:2, :]) + mod_ref[0, 0:1, :]).astype(BF16)

    p0 = jnp.where(i < n_lat_tiles, (i % tiles_per_batch) * tm, seq)
    p0 = pl.multiple_of(p0, tm)
    lane = lax.broadcasted_iota(I32, (hm, 128), 1)

    def norm_rope(a, r0, gidx, group, bd_ref, rope_tabs):
        w = a.shape[1]
        ssq = _dot((a * a).astype(BF16), bd_ref[0:w, 0:w])
        r = lax.rsqrt(ssq * (1.0 / group) + EPS)
        outs = []
        for s in range(w // 128):
            t = a[:, 128 * s:128 * s + 128] * r[:, 128 * s:128 * s + 128] * qkg_ref[gidx:gidx + 1, :]
            if rope_tabs is not None:
                c_ref, s_ref, half = rope_tabs
                fwd = pltpu.roll(t, 128 - half, axis=1)
                bwd = pltpu.roll(t, half, axis=1)
                sw = jnp.where((lane % (2 * half)) < half, fwd, bwd)
                t = t * c_ref[pl.ds(p0 + r0, hm), :] + sw * s_ref[pl.ds(p0 + r0, hm), :]
            outs.append(t)
        return outs[0] if len(outs) == 1 else jnp.concatenate(outs, axis=1)

    rope64 = (c64_ref, s64_ref, HEAD_DIM // 4)
    rope32 = (c32_ref, s32_ref, DIFF_DIM // 4)

    def epilogue(acc, r0, blk):
        if blk == COL_A_Q:
            out = norm_rope(acc, r0, 0, HEAD_DIM, bd64_ref, rope64)
        elif blk == COL_A_KV:
            out = jnp.concatenate([norm_rope(acc[:, :128], r0, 1, HEAD_DIM, bd64_ref, rope64), acc[:, 128:]], axis=1)
        elif blk == COL_B_Q:
            out = norm_rope(acc, r0, 2, HEAD_DIM, bd64_ref, None)
        elif blk == COL_B_K:
            out = norm_rope(acc, r0, 3, HEAD_DIM, bd64_ref, None)
        elif blk == COL_C_Q:
            out = norm_rope(acc, r0, 4, DIFF_DIM, bd32_ref, rope32)
        elif blk == COL_C_K:
            out = norm_rope(acc, r0, 5, DIFF_DIM, bd32_ref, rope32)
        elif blk == COL_D_Q:
            out = norm_rope(acc, r0, 6, HEAD_DIM, bd64_ref, rope64)
        elif blk == COL_D_KV:
            out = jnp.concatenate([norm_rope(acc[:, :128], r0, 7, HEAD_DIM, bd64_ref, rope64), acc[:, 128:]], axis=1)
        else:
            out = acc
        u_ref[r0:r0 + hm, 256 * blk:256 * blk + 256] = out.astype(BF16)

    n_blk = D_IN // 256
    units = [(r0, blk) for r0 in (0, hm) for blk in range(n_blk)]
    hbs = {0: prologue(0)}
    acc = _dot(hbs[0], w_ref[:, 0:256])
    hbs[hm] = prologue(hm)
    for t, (r0, blk) in enumerate(units):
        nxt = None
        if t + 1 < len(units):
            r1, b1 = units[t + 1]
            nxt = _dot(hbs[r1], w_ref[:, 256 * b1:256 * b1 + 256])
        epilogue(acc, r0, blk)
        acc = nxt


def _two_source_specs(tm, n_lat_tiles):
    return [pl.BlockSpec((tm, D_MODEL), lambda i: (jnp.minimum(i, n_lat_tiles - 1), 0)),
            pl.BlockSpec((tm, D_MODEL), lambda i: (jnp.maximum(i - n_lat_tiles, 0), 0))]


def _two_source_rows(xl_ref, xc_ref, r0, n, n_lat_tiles):
    return jnp.where(pl.program_id(0) < n_lat_tiles, xl_ref[r0:r0 + n, :], xc_ref[r0:r0 + n, :])


def _mod_spec(n_lat_tiles, tiles_per_batch, batch):
    return pl.BlockSpec((1, ADA_CHUNKS, D_MODEL),
                        lambda i: (jnp.where(i < n_lat_tiles, i // tiles_per_batch, batch), 0, 0))


def _const_spec(shape):
    return pl.BlockSpec(shape, lambda i: (0,) * len(shape))


def _pos_tile_specs(n_tiles, tm):
    return [pl.BlockSpec((1, 1, tm), lambda i: (i, 0, 0), memory_space=pltpu.SMEM),
            pl.BlockSpec((1, 1, tm), lambda i: (jnp.minimum(i + 1, n_tiles - 1), 0, 0), memory_space=pltpu.SMEM)]


def _inproj_call(x, ffn_src, mod_prev, mod, g, w_in, qkg, bd64, bd32, tabs, *, batch, seq):
    tm = TOKEN_TILE
    n_lat_tiles = batch * seq // tm
    tpb = seq // tm
    has_ffn = ffn_src is not None
    tile = pl.BlockSpec((tm, D_MODEL), lambda i: (i, 0))
    modspec = _mod_spec(n_lat_tiles, tpb, batch)
    tab_rows = tabs[0].shape[0]
    if has_ffn:
        n_all = x.shape[0]
        in_specs, args = [tile], [x]
    else:
        n_all = x[0].shape[0] + x[1].shape[0]
        in_specs, args = _two_source_specs(tm, n_lat_tiles), list(x)
    scratch = []
    if has_ffn:
        ys, pos = ffn_src
        pos_tiles = pos.reshape(n_all // tm, 1, tm)
        in_specs += _pos_tile_specs(n_all // tm, tm) + [pl.BlockSpec(memory_space=pl.ANY), modspec]
        args += [pos_tiles, pos_tiles, ys, mod_prev]
        scratch = [pltpu.VMEM((2, tm, D_MODEL), F32), pltpu.SemaphoreType.DMA((2,))]
    in_specs += [modspec, _const_spec((1, D_MODEL)), _const_spec((D_MODEL, D_IN)), _const_spec((8, 128)),
                 _const_spec((256, 256)), _const_spec((256, 256))] + [_const_spec((tab_rows, 128))] * 4
    args += [mod, g, w_in, qkg, bd64, bd32] + list(tabs)
    u_shape = jax.ShapeDtypeStruct((n_all, D_IN), BF16)
    u_spec = pl.BlockSpec((tm, D_IN), lambda i: (i, 0))
    if has_ffn:
        out_shape = (jax.ShapeDtypeStruct((n_all, D_MODEL), F32), u_shape)
        out_specs = (tile, u_spec)
    else:
        out_shape, out_specs = u_shape, u_spec
    kern = functools.partial(_inproj_kernel, has_ffn=has_ffn, tm=tm, n_lat_tiles=n_lat_tiles,
                             tiles_per_batch=tpb, seq=seq)
    res = pl.pallas_call(
        kern, out_shape=out_shape, grid=(n_all // tm,), in_specs=in_specs, out_specs=out_specs,
        scratch_shapes=scratch, compiler_params=_cparams("arbitrary"), name="inproj",
    )(*args)
    return res if has_ffn else (x, res)


class _AttItem:
    def __init__(self, q, kt_view, v_view, ranges, bias=None, extra=None):
        self.q, self.kt_view, self.v_view, self.ranges, self.bias, self.extra = q, kt_view, v_view, ranges, bias, extra
        self.n_keys = sum(n for _, n in ranges)
        self.m = None

    def scores(self, s_scr, rows):
        q_blk = self.q()
        m, off, ti = None, 0, 0
        for start, n in self.ranges:
            for o in range(0, n, KEY_TILE):
                w = min(KEY_TILE, n - o)
                s = _dot(q_blk, self.kt_view[:, pl.ds(start + o, w)])
                b = self.bias(ti) if self.bias is not None else None
                if b is not None:
                    s = s + b
                ti += 1
                s_scr[rows, off + o:off + o + w] = s
                tile_max = s.max(axis=-1, keepdims=True)
                m = tile_max if m is None else jnp.maximum(m, tile_max)
            off += n
        self.m = m if self.extra is None else jnp.maximum(m, self.extra)

    def probs(self, s_scr, p_scr, rows):
        for o in range(0, self.n_keys, KEY_TILE):
            w = min(KEY_TILE, self.n_keys - o)
            p_scr[rows, o:o + w] = jnp.exp2(s_scr[rows, o:o + w] - self.m).astype(BF16)

    def values(self, p_scr, rows):
        pv, off = None, 0
        for start, n in self.ranges:
            c = _dot(p_scr[rows, off:off + n], self.v_view[pl.ds(start, n), :])
            pv = c if pv is None else pv + c
            off += n
        denom = pv[:, 64:65]
        if self.extra is not None:
            denom = denom + jnp.exp2(self.extra - self.m)
        return pv[:, 0:64] / denom


def _run_att_items(items, s_scr, p_scr, lag=1):
    slots = s_scr.shape[0] // ATT_RB
    assert slots > 2 * lag

    def rows(t):
        r0 = (t % slots) * ATT_RB
        return slice(r0, r0 + ATT_RB)

    outs = []
    for t in range(len(items) + 2 * lag):
        if t < len(items):
            items[t].scores(s_scr, rows(t))
        if 0 <= t - lag < len(items):
            items[t - lag].probs(s_scr, p_scr, rows(t - lag))
        if 0 <= t - 2 * lag < len(items):
            outs.append(items[t - 2 * lag].values(p_scr, rows(t - 2 * lag)))
    return outs


def _stage_heads(k_parts, v_parts, kt_scr, v_scr):
    eye = jnp.where(lax.broadcasted_iota(I32, (HEAD_DIM, HEAD_DIM), 0)
                    == lax.broadcasted_iota(I32, (HEAD_DIM, HEAD_DIM), 1), 1.0, 0.0).astype(BF16)
    for h in range(kt_scr.shape[0]):
        r0 = 0
        for (kr, kc0), (vr, vc0) in zip(k_parts, v_parts):
            n = kr.shape[0]
            kt_scr[h, :, r0:r0 + n] = _nt_dot(eye, kr[:, kc0 + 64 * h:kc0 + 64 * h + 64]).astype(BF16)
            v_scr[h, r0:r0 + n, 0:64] = vr[:, vc0 + 64 * h:vc0 + 64 * h + 64]
            r0 += n
        v_scr[h, :, 64:128] = jnp.ones((v_scr.shape[1], 64), BF16)


def _group_rms(y):
    return y * lax.rsqrt(jnp.mean(y * y, axis=-1, keepdims=True) + EPS)


def _stage_once_per_batch_row(k_parts, v_parts, k_scr, v_scr):
    @pl.when(pl.program_id(1) == 0)
    def _():
        _stage_heads(k_parts, v_parts, k_scr, v_scr)


def _row_blocks(n):
    return range(0, n, ATT_RB)


def _q_thunk(q_ref, r0, head):
    return lambda: q_ref[r0:r0 + ATT_RB, 64 * head:64 * head + 64]


def _heads_to_rows(outs, n_heads):
    per = len(outs) // n_heads
    return jnp.concatenate([jnp.concatenate(outs[h * per:(h + 1) * per], axis=0) for h in range(n_heads)], axis=1)


def _dense_kernel(q_ref, kv_ref, kvc_ref, o_ref, kt_scr, v_scr, s_scr, p_scr, *, tq):
    _stage_once_per_batch_row([(kv_ref, 0), (kvc_ref, 0)], [(kv_ref, 128), (kvc_ref, 128)], kt_scr, v_scr)
    nk = v_scr.shape[1]
    items = [_AttItem(_q_thunk(q_ref, r0, hq), kt_scr.at[hq // 2], v_scr.at[hq // 2], [(0, nk)])
             for hq in range(HEADS) for r0 in _row_blocks(tq)]
    y = _heads_to_rows(_run_att_items(items, s_scr, p_scr), HEADS)
    o_ref[...] = _group_rms(y).astype(BF16)


def _window_kernel(sink_ref, q_ref, kv_ref, kvc_ref, *refs, tq, seq, tiles):
    mask_refs, (o_ref, kt_scr, v_scr, s_scr, p_scr) = refs[:tiles], refs[tiles:]
    _stage_once_per_batch_row([(kv_ref, 0), (kvc_ref, 0)], [(kv_ref, 128), (kvc_ref, 128)], kt_scr, v_scr)
    kw = tq + 2 * WINDOW
    nc = v_scr.shape[1] - seq

    def mask_thunk(mask_ref, r0):
        return lambda ti: (mask_ref[0, r0:r0 + ATT_RB, KEY_TILE * ti:KEY_TILE * ti + KEY_TILE]
                           if KEY_TILE * ti < kw else None)

    items = []
    for t, mask_ref in enumerate(mask_refs):
        q0 = (tiles * pl.program_id(1) + t) * tq
        ks = pl.multiple_of(jnp.clip(q0 - WINDOW, 0, seq - kw), 128)
        items += [_AttItem(_q_thunk(q_ref, t * tq + r0, hq), kt_scr.at[hq // 2], v_scr.at[hq // 2],
                           [(ks, kw), (seq, nc)], bias=mask_thunk(mask_ref, r0), extra=sink_ref[hq] * LOG2E)
                  for hq in range(HEADS) for r0 in _row_blocks(tq)]
    outs = _run_att_items(items, s_scr, p_scr, lag=WINDOW_LAG)
    per = len(outs) // tiles
    y = jnp.concatenate([_heads_to_rows(outs[t * per:(t + 1) * per], HEADS) for t in range(tiles)], axis=0)
    o_ref[...] = _group_rms(y).astype(BF16)


def _neigh_kernel(q_ref, k_ref, v_ref, kc_ref, vc_ref, *refs, rows, seq, tiles):
    bias_refs, (o_ref, kt_scr, v_scr, s_scr, p_scr) = refs[:tiles], refs[tiles:]
    _stage_once_per_batch_row([(k_ref, 0), (kc_ref, 0)], [(v_ref, 0), (vc_ref, 0)], kt_scr, v_scr)
    nk = NA_KROWS * GRID_W
    nc = v_scr.shape[1] - seq
    qrows = NA_QROWS * GRID_W

    def bias_thunk(bias_ref, h, r0):
        return lambda ti: (bias_ref[0, h, r0:r0 + ATT_RB, KEY_TILE * ti:KEY_TILE * ti + KEY_TILE]
                           if KEY_TILE * ti < nk else None)

    items = []
    for t, bias_ref in enumerate(bias_refs):
        rb = tiles * pl.program_id(1) + t
        rs = jnp.clip(NA_QROWS * rb - NA_ROWS // 2, 0, rows - NA_KROWS)
        k0 = pl.multiple_of(rs * GRID_W, KEY_TILE)
        items += [_AttItem(_q_thunk(q_ref, t * qrows + r0, h), kt_scr.at[h], v_scr.at[h], [(k0, nk), (seq, nc)],
                           bias=bias_thunk(bias_ref, h, r0))
                  for h in range(HEADS) for r0 in _row_blocks(qrows)]
    outs = _run_att_items(items, s_scr, p_scr)
    per = len(outs) // tiles
    y = jnp.concatenate([_heads_to_rows(outs[t * per:(t + 1) * per], HEADS) for t in range(tiles)], axis=0)
    o_ref[...] = _group_rms(y).astype(BF16)


def _diff_lambda(lam_ref, lam_init):
    lp = lam_ref[...]
    a = jnp.sum(lp[0:1] * lp[1:2], axis=-1, keepdims=True)
    b = jnp.sum(lp[2:3] * lp[3:4], axis=-1, keepdims=True)
    return jnp.exp(a) - jnp.exp(b) + lam_init


def _diff_kernel(lam_ref, q_ref, k_ref, v_ref, kc_ref, vc_ref, o_ref, kt_scr, v_scr, s_scr, p_scr, *, lam_init, tq):
    _stage_once_per_batch_row([(k_ref, 0), (kc_ref, 0)], [(v_ref, 0), (vc_ref, 0)], kt_scr, v_scr)
    lam = _diff_lambda(lam_ref, lam_init)
    nk = v_scr.shape[1]
    first = lax.broadcasted_iota(I32, (ATT_RB, HEAD_DIM), 1) < DIFF_DIM
    zero = jnp.zeros((ATT_RB, HEAD_DIM), BF16)

    def q_map(r0, h, half):
        def thunk():
            qh = q_ref[r0:r0 + ATT_RB, 64 * h:64 * h + 64]
            return jnp.where(first, qh, zero) if half == 0 else jnp.where(first, zero, qh)
        return thunk

    items = [_AttItem(q_map(r0, h, half), kt_scr.at[h], v_scr.at[h], [(0, nk)])
             for h in range(HEADS) for r0 in _row_blocks(tq) for half in range(2)]
    outs = _run_att_items(items, s_scr, p_scr)
    diffs = [outs[i] - lam * outs[i + 1] for i in range(0, len(outs), 2)]
    per = len(diffs) // HEADS
    heads = [_group_rms(jnp.concatenate(diffs[h * per:(h + 1) * per], axis=0)) * (1.0 - lam_init)
             for h in range(HEADS)]
    o_ref[...] = jnp.concatenate(heads, axis=1).astype(BF16)


def _softmax_parts(scores, extra=None):
    m = scores[0].max(axis=-1, keepdims=True)
    for s in scores[1:]:
        m = jnp.maximum(m, s.max(axis=-1, keepdims=True))
    if extra is not None:
        m = jnp.maximum(m, extra)
    ps = [jnp.exp2(s - m) for s in scores]
    l = ps[0].sum(axis=-1, keepdims=True)
    for p in ps[1:]:
        l = l + p.sum(axis=-1, keepdims=True)
    if extra is not None:
        l = l + jnp.exp2(extra - m)
    return ps, 1.0 / l


def _ctx_kernel(sink_ref, lam_ref, u_ref, o_ref, *, lam_init, ctx_len):
    for g in range(u_ref.shape[0] // ctx_len):
        _ctx_rows(sink_ref, lam_ref, u_ref, o_ref, g * ctx_len, ctx_len, lam_init)


def _ctx_rows(sink_ref, lam_ref, u_ref, o_ref, r0, L, lam_init):
    top = lax.broadcasted_iota(I32, (2 * L, 1), 0) < L

    def cols(blk, lo, width=64):
        return u_ref[r0:r0 + L, 256 * blk + lo:256 * blk + lo + width]

    for q_blk, kv_blk, out_col, use_sink in ((COL_A_Q, COL_A_KV, 0, True), (COL_D_Q, COL_D_KV, 768, False)):
        outs = [None] * HEADS
        for j in range(2):
            q2 = jnp.concatenate([cols(q_blk, 128 * j), cols(q_blk, 128 * j + 64)], axis=0)
            snk = jnp.where(top, sink_ref[2 * j], sink_ref[2 * j + 1]) * LOG2E if use_sink else None
            (p,), inv = _softmax_parts([_nt_dot(q2, cols(kv_blk, 64 * j))], extra=snk)
            o = _dot(p.astype(BF16), cols(kv_blk, 128 + 64 * j)) * inv
            outs[2 * j], outs[2 * j + 1] = o[:L], o[L:]
        o_ref[r0:r0 + L, out_col:out_col + 256] =_group_rms(jnp.concatenate(outs, axis=1)).astype(BF16)
    outs = []
    for h in range(HEADS):
        (p,), inv = _softmax_parts([_nt_dot(cols(COL_B_Q, 64 * h), cols(COL_B_K, 64 * h))])
        outs.append(_dot(p.astype(BF16), cols(COL_B_V, 64 * h)) * inv)
    o_ref[r0:r0 + L, 256:512] =_group_rms(jnp.concatenate(outs, axis=1)).astype(BF16)
    lam = _diff_lambda(lam_ref, lam_init)
    first = lax.broadcasted_iota(I32, (L, HEAD_DIM), 1) < DIFF_DIM
    zero = jnp.zeros((L, HEAD_DIM), BF16)
    outs = []
    for h in range(HEADS):
        qh, kh = cols(COL_C_Q, 64 * h), cols(COL_C_K, 64 * h)
        maps = []
        for half in range(2):
            qm = jnp.where(first, qh, zero) if half == 0 else jnp.where(first, zero, qh)
            (p,), inv = _softmax_parts([_nt_dot(qm, kh)])
            maps.append(p * inv)
        o = _dot((maps[0] - lam * maps[1]).astype(BF16), cols(COL_C_V, 64 * h))
        outs.append(_group_rms(o) * (1.0 - lam_init))
    o_ref[r0:r0 + L, 512:768] = jnp.concatenate(outs, axis=1).astype(BF16)


def _attention_calls(u, sink, lam_p, bias, wmask, *, layer, batch, seq, ctx_len, need_ctx):
    lam_init = 0.8 - 0.6 * math.exp(-0.3 * layer)
    n_all = u.shape[0]
    tq = ATT_TQ
    nq = seq // tq
    nk_all = seq + ctx_len
    cb0 = batch * seq // ctx_len
    y_shape = jax.ShapeDtypeStruct((n_all, D_MODEL), BF16)
    smem = pl.BlockSpec(memory_space=pltpu.SMEM)
    any_spec = pl.BlockSpec(memory_space=pl.ANY)
    lam_spec2 = pl.BlockSpec((4, DIFF_DIM), lambda b, i: (0, 0))
    cp = _cparams("arbitrary", "arbitrary")

    def scratch(n_heads, n_keys, lag=1):
        rows_ = (2 * lag + 2) * ATT_RB
        return [pltpu.VMEM((n_heads, HEAD_DIM, nk_all), BF16), pltpu.VMEM((n_heads, nk_all, 128), BF16),
                pltpu.VMEM((rows_, n_keys), F32), pltpu.VMEM((rows_, n_keys), BF16)]

    def qspec(col, t=tq):
        return pl.BlockSpec((t, 256), lambda b, i: (b * (seq // t) + i, col))

    tqf = ATT_TQ_FULL
    nqf = seq // tqf

    def latspec(col):
        return pl.BlockSpec((seq, 256), lambda b, i: (b, col))

    def ctxspec(col):
        return pl.BlockSpec((ctx_len, 256), lambda b, i: (cb0 + b, col))

    kw = tq + 2 * WINDOW

    def case_specs(shape, n_tiles, per_step):
        def spec(t):
            def index(b, i):
                j = per_step * i + t
                return (jnp.where(j == 0, 0, jnp.where(j == n_tiles - 1, 2, 1)),) + (0,) * (len(shape) - 1)
            return pl.BlockSpec(shape, index)
        return [spec(t) for t in range(per_step)]

    wt = WINDOW_TILES
    y = pl.pallas_call(
        functools.partial(_window_kernel, tq=tq, seq=seq, tiles=wt),
        out_shape=y_shape, grid=(batch, nq // wt),
        in_specs=[smem, qspec(COL_A_Q, wt * tq), latspec(COL_A_KV), ctxspec(COL_A_KV)]
        + case_specs((1, tq, kw), nq, wt),
        out_specs=qspec(0, wt * tq), scratch_shapes=scratch(2, kw + ctx_len, WINDOW_LAG),
        compiler_params=cp, name="mix_window",
    )(sink, u, u, u, *([wmask] * wt))

    rows = seq // GRID_W
    nrb = rows // NA_QROWS
    qrows = NA_QROWS * GRID_W
    nkb = NA_KROWS * GRID_W
    nt = NA_TILES
    y = pl.pallas_call(
        _drop_arg(functools.partial(_neigh_kernel, rows=rows, seq=seq, tiles=nt), 5 + nt),
        out_shape=y_shape, grid=(batch, nrb // nt),
        in_specs=[qspec(COL_B_Q, nt * qrows), latspec(COL_B_K), latspec(COL_B_V),
                  ctxspec(COL_B_K), ctxspec(COL_B_V)] + case_specs((1, HEADS, qrows, nkb), nrb, nt) + [any_spec],
        out_specs=qspec(1, nt * qrows), scratch_shapes=scratch(HEADS, nkb + ctx_len),
        input_output_aliases={5 + nt: 0}, compiler_params=cp, name="mix_neigh",
    )(u, u, u, u, u, *([bias] * nt), y)

    y = pl.pallas_call(
        _drop_arg(functools.partial(_diff_kernel, lam_init=lam_init, tq=tqf), 6),
        out_shape=y_shape, grid=(batch, nqf),
        in_specs=[lam_spec2, qspec(COL_C_Q, tqf), latspec(COL_C_K), latspec(COL_C_V),
                  ctxspec(COL_C_K), ctxspec(COL_C_V), any_spec],
        out_specs=qspec(2, tqf), scratch_shapes=scratch(HEADS, nk_all),
        input_output_aliases={6: 0}, compiler_params=cp, name="mix_diff",
    )(lam_p, u, u, u, u, u, y)

    y = pl.pallas_call(
        _drop_arg(functools.partial(_dense_kernel, tq=tqf), 3),
        out_shape=y_shape, grid=(batch, nqf),
        in_specs=[qspec(COL_D_Q, tqf), latspec(COL_D_KV), ctxspec(COL_D_KV), any_spec],
        out_specs=qspec(3, tqf), scratch_shapes=scratch(2, nk_all),
        input_output_aliases={3: 0}, compiler_params=cp, name="mix_dense",
    )(u, u, u, y)

    if need_ctx:
        y = pl.pallas_call(
            _drop_arg(functools.partial(_ctx_kernel, lam_init=lam_init, ctx_len=ctx_len), 3),
            out_shape=y_shape, grid=(batch // CTX_ROWS_PER_STEP,),
            in_specs=[smem, pl.BlockSpec((4, DIFF_DIM), lambda b: (0, 0)),
                      pl.BlockSpec((CTX_ROWS_PER_STEP * ctx_len, D_IN), lambda b: (cb0 // CTX_ROWS_PER_STEP + b, 0)),
                      any_spec],
            out_specs=pl.BlockSpec((CTX_ROWS_PER_STEP * ctx_len, D_MODEL), lambda b: (cb0 // CTX_ROWS_PER_STEP + b, 0)),
            input_output_aliases={3: 0}, compiler_params=_cparams("arbitrary"), name="mix_ctx",
        )(sink, lam_p, u, y)
    return y


def _select4(idx, vals):
    return jnp.where(idx == 0, vals[0], jnp.where(idx == 1, vals[1], jnp.where(idx == 2, vals[2], vals[3])))


def _outproj_kernel(y_ref, *refs, tm, n_lat_tiles, two_source):
    n_x = 2 if two_source else 1
    x_refs, (mod_ref, gain_ref, w_ref, g_ref, rw_ref, rb_ref, xo_ref, h_ref, cls_ref) = refs[:n_x], refs[n_x:]
    hm = tm // 2

    def project(r0):
        yg = (y_ref[r0:r0 + hm, :].astype(F32) * gain_ref[...]).astype(BF16)
        return _dot(yg, w_ref[...])

    def normalise(r0, proj):
        x = (_two_source_rows(*x_refs, r0, hm, n_lat_tiles) if two_source else x_refs[0][r0:r0 + hm, :])
        xm = x + mod_ref[0, 2:3, :] * proj
        xo_ref[r0:r0 + hm, :] = xm
        hn = xm * lax.rsqrt(jnp.mean(xm * xm, axis=-1, keepdims=True) + EPS) * g_ref[...]
        h2 = hn * (1.0 + mod_ref[0, 4:5, :]) + mod_ref[0, 3:4, :]
        hi = h2.astype(BF16)
        hi32 = hi.astype(F32)
        lo = (h2 - hi32).astype(BF16)
        bits = lax.bitcast_convert_type(hi32, U32)
        half = D_MODEL // 2
        h_ref[r0:r0 + hm, 0:half] = (bits[:, 0:half] >> 16) | (bits[:, half:D_MODEL] & jnp.uint32(0xFFFF0000))
        return (_dot(hi, rw_ref[...]) + _dot(lo, rw_ref[...])).T

    def route(r0, lt):
        score = _sigmoid(lt[0:N_EXPERTS] + lt[N_EXPERTS:2 * N_EXPERTS])
        sel = score + rb_ref[...]
        srow = [sel[e:e + 1] for e in range(N_EXPERTS)]
        crow = [score[e:e + 1] for e in range(N_EXPERTS)]

        gidx = jnp.zeros((1, hm), I32)
        gbest = None
        for g in range(N_EXPERTS // EXPERTS_PER_GROUP):
            v = srow[4 * g:4 * g + 4]
            best = None
            for a in range(4):
                for b in range(a + 1, 4):
                    t = v[a] + v[b]
                    best = t if best is None else jnp.maximum(best, t)
            if g == 0:
                gbest = best
            else:
                better = best > gbest
                gidx = jnp.where(better, g, gidx)
                gbest = jnp.where(better, best, gbest)

        iv = [_select4(gidx, [srow[4 * g + i] for g in range(4)]) for i in range(4)]
        sv = [_select4(gidx, [crow[4 * g + i] for g in range(4)]) for i in range(4)]
        chosen = []
        for i in range(4):
            rank = jnp.zeros((1, hm), I32)
            for j in range(4):
                if j != i:
                    beats = (iv[j] >= iv[i]) if j < i else (iv[j] > iv[i])
                    rank = rank + jnp.where(beats, 1, 0)
            chosen.append(rank < 2)
        lo_i = jnp.where(chosen[0], 0, jnp.where(chosen[1], 1, 2))
        hi_i = jnp.where(chosen[3], 3, jnp.where(chosen[2], 2, 1))
        pair = jnp.where(lo_i == 0, hi_i - 1, jnp.where(lo_i == 1, hi_i + 1, 5))
        cls_ref[0, :, r0:r0 + hm] = gidx * N_PAIRS + pair
        s_lo = jnp.where(lo_i == 0, sv[0], jnp.where(lo_i == 1, sv[1], sv[2]))
        s_hi = jnp.where(hi_i == 3, sv[3], jnp.where(hi_i == 2, sv[2], sv[1]))
        inv = 1.0 / (s_lo + s_hi)
        rowi = lax.broadcasted_iota(I32, (128, hm), 0)
        wts = jnp.where(rowi == 0, s_lo * inv, jnp.where(rowi == 1, s_hi * inv, 0.0))
        h_ref[r0:r0 + hm, D_MODEL // 2:EXT_W] = lax.bitcast_convert_type(wts.T, U32)

    proj0 = project(0)
    proj1 = project(hm)
    lt0 = normalise(0, proj0)
    lt1 = normalise(hm, proj1)
    route(0, lt0)
    route(hm, lt1)


def _outproj_call(y, x, mod, gain, w_out, g, rw, rb, *, n_proc, batch, seq):
    tm = TOKEN_TILE
    n_tiles = n_proc // tm
    n_lat_tiles = batch * seq // tm
    tile = pl.BlockSpec((tm, D_MODEL), lambda i: (i, 0))
    two_source = isinstance(x, tuple)
    x_specs, x_args = (_two_source_specs(tm, n_lat_tiles), list(x)) if two_source else ([tile], [x])
    return pl.pallas_call(
        functools.partial(_outproj_kernel, tm=tm, n_lat_tiles=n_lat_tiles, two_source=two_source),
        out_shape=(jax.ShapeDtypeStruct((n_proc, D_MODEL), F32),
                   jax.ShapeDtypeStruct((n_proc, EXT_W), U32),
                   jax.ShapeDtypeStruct((n_tiles, 1, tm), I32)),
        grid=(n_tiles,),
        in_specs=[tile] + x_specs + [_mod_spec(n_lat_tiles, seq // tm, batch), _const_spec((1, D_MODEL)),
                  _const_spec((D_MODEL, D_MODEL)), _const_spec((1, D_MODEL)), _const_spec((D_MODEL, 128)),
                  _const_spec((N_EXPERTS, 1))],
        out_specs=(tile, pl.BlockSpec((tm, EXT_W), lambda i: (i, 0)), pl.BlockSpec((1, 1, tm), lambda i: (i, 0, 0))),
        compiler_params=_cparams("arbitrary"), name="outproj_router",
    )(y, *x_args, mod, gain, w_out, g, rw, rb)


def _rank_kernel(cls_ref, pos_ref, bcls_ref, nval_ref, *, blk):
    c = cls_ref[...]
    nr = c.shape[0]
    nbp = bcls_ref.shape[1]
    upper = (lax.broadcasted_iota(I32, (128, 128), 0) <= lax.broadcasted_iota(I32, (128, 128), 1))
    upper = jnp.where(upper, 1.0, 0.0).astype(BF16)
    lower = (lax.broadcasted_iota(I32, (nr, nr), 1) < lax.broadcasted_iota(I32, (nr, nr), 0))
    lower = jnp.where(lower, 1.0, 0.0).astype(BF16)
    bidx = lax.broadcasted_iota(I32, (1, nbp), 1).astype(F32)
    pos = jnp.zeros((nr, 128), F32)
    bcls = jnp.zeros((1, nbp), F32)
    nval = jnp.zeros((1, nbp), F32)
    start = jnp.zeros((1, 1), F32)
    for k in range(N_CLASSES):
        m = c == k
        incl = _dot(jnp.where(m, 1.0, 0.0).astype(BF16), upper)
        tot = incl[:, 127:128]
        above = _dot(lower, jnp.broadcast_to(tot, (nr, 128)).astype(BF16))
        cnt = above[nr - 1:nr, 0:1] + tot[nr - 1:nr, 0:1]
        nblk = jnp.floor((cnt + (blk - 1)) * (1.0 / blk))
        pos = jnp.where(m, start * blk + above + incl - 1.0, pos)
        end = start + nblk
        bcls = bcls + jnp.where(bidx >= end, 1.0, 0.0)
        nval = nval + jnp.where((bidx >= start) & (bidx < end), jnp.clip(cnt - (bidx - start) * blk, 0.0, blk), 0.0)
        start = end
    pos_ref[...] = pos.astype(I32)
    bcls_ref[...] = jnp.minimum(bcls, N_CLASSES - 1.0).astype(I32)
    nval_ref[...] = nval.astype(I32)


def _rank_call(cls2d, n_blocks):
    nr = cls2d.shape[0]
    nbp = -(-n_blocks // 128) * 128
    return pl.pallas_call(
        functools.partial(_rank_kernel, blk=MOE_BLK),
        out_shape=(jax.ShapeDtypeStruct((nr, 128), I32), jax.ShapeDtypeStruct((1, nbp), I32),
                   jax.ShapeDtypeStruct((1, nbp), I32)),
        grid=(1,),
        in_specs=[_const_spec((nr, 128))],
        out_specs=(_const_spec((nr, 128)), _const_spec((1, nbp)), _const_spec((1, nbp))),
        compiler_params=_cparams("arbitrary"), name="class_rank",
    )(cls2d)


def _scatter_kernel(pos_ref, h_ref, xs_ref, sem):
    n = h_ref.shape[0]
    for r in range(n):
        pltpu.make_async_copy(h_ref.at[pl.ds(r, 1)], xs_ref.at[pl.ds(pos_ref[0, 0, r], 1)], sem).start()
    pltpu.make_async_copy(h_ref, xs_ref.at[pl.ds(0, n)], sem).wait()


def _scatter_call(h_ext, pos3, n_slots):
    n, w = h_ext.shape
    r = ROW_DMA
    return pl.pallas_call(
        _scatter_kernel,
        out_shape=jax.ShapeDtypeStruct((n_slots, w), h_ext.dtype),
        grid=(n // r,),
        in_specs=[pl.BlockSpec((1, 1, r), lambda i: (i, 0, 0), memory_space=pltpu.SMEM),
                  pl.BlockSpec((r, w), lambda i: (i, 0))],
        out_specs=pl.BlockSpec(memory_space=pl.ANY),
        scratch_shapes=[pltpu.SemaphoreType.DMA(())],
        compiler_params=_cparams("arbitrary"), name="row_scatter",
    )(pos3, h_ext)


def _moe_kernel(elo_ref, ehi_ref, nval_ref, xs_ref, wg_lo, wu_lo, wd_lo, wg_hi, wu_hi, wd_hi, ys_ref):
    del elo_ref, ehi_ref
    nv = nval_ref[pl.program_id(0)]

    @pl.when(nv > 0)
    def _():
        live = lax.broadcasted_iota(I32, (MOE_BLK, 1), 0) < nv
        half = D_MODEL // 2
        words = jnp.where(live, xs_ref[:, 0:half], jnp.uint32(0))
        x = jnp.concatenate([lax.bitcast_convert_type(words << 16, F32).astype(BF16),
                             lax.bitcast_convert_type(words & jnp.uint32(0xFFFF0000), F32).astype(BF16)], axis=1)
        acts = []
        for wg, wu, col in ((wg_lo, wu_lo, half), (wg_hi, wu_hi, half + 1)):
            w = jnp.where(live, lax.bitcast_convert_type(xs_ref[:, col:col + 1], F32), 0.0)
            g = _dot(x, wg[0])
            acts.append((g * _sigmoid(g) * _dot(x, wu[0]) * w).astype(BF16))
        ys_ref[...] = _dot(acts[0], wd_lo[0]) + _dot(acts[1], wd_hi[0])

    @pl.when(nv == 0)
    def _():
        ys_ref[...] = jnp.zeros_like(ys_ref)


def _moe_call(xs, e_lo, e_hi, nval, w_gate, w_up, w_down):
    n_blocks = xs.shape[0] // MOE_BLK

    def wspec(shape, which):
        return pl.BlockSpec((1,) + shape, lambda b, elo, ehi, nv: ((elo, ehi)[which][b], 0, 0))

    gu, dn = (D_MODEL, D_EXPERT), (D_EXPERT, D_MODEL)
    grid_spec = pltpu.PrefetchScalarGridSpec(
        num_scalar_prefetch=3, grid=(n_blocks,),
        in_specs=[pl.BlockSpec((MOE_BLK, EXT_W), lambda b, elo, ehi, nv: (b, 0)),
                  wspec(gu, 0), wspec(gu, 0), wspec(dn, 0), wspec(gu, 1), wspec(gu, 1), wspec(dn, 1)],
        out_specs=pl.BlockSpec((MOE_BLK, D_MODEL), lambda b, elo, ehi, nv: (b, 0)),
    )
    return pl.pallas_call(
        _moe_kernel, out_shape=jax.ShapeDtypeStruct((xs.shape[0], D_MODEL), F32), grid_spec=grid_spec,
        compiler_params=_cparams("arbitrary"), name="moe_experts",
    )(e_lo, e_hi, nval, xs, w_gate, w_up, w_down, w_gate, w_up, w_down)


def _final_kernel(x_ref, pos_ref, pos_next_ref, ys_ref, mod_ref, o_ref, fbuf, fsems):
    f_ref = _prefetched_rows(pos_ref, pos_next_ref, ys_ref, fbuf, fsems)
    o_ref[...] = x_ref[...] + mod_ref[0, 5:6, :] * f_ref[...]


def _final_call(x, ys, pos, mod, *, batch, seq):
    tm = TOKEN_TILE
    n = batch * seq
    tile = pl.BlockSpec((tm, D_MODEL), lambda i: (i, 0))
    pos_tiles = pos.reshape(n // tm, 1, tm)
    return pl.pallas_call(
        _final_kernel, out_shape=jax.ShapeDtypeStruct((n, D_MODEL), F32), grid=(n // tm,),
        in_specs=[tile] + _pos_tile_specs(n // tm, tm) + [pl.BlockSpec(memory_space=pl.ANY),
                                                          _mod_spec(n // tm, seq // tm, batch)],
        out_specs=tile, scratch_shapes=[pltpu.VMEM((2, tm, D_MODEL), F32), pltpu.SemaphoreType.DMA((2,))],
        compiler_params=_cparams("arbitrary"), name="final_residual",
    )(x, pos_tiles, pos_tiles, ys, mod)


def _rope_tables(seq, ident_rows):
    pos = jnp.arange(seq, dtype=jnp.int32)
    row = (pos // GRID_W).astype(F32)[:, None]
    col = (pos % GRID_W).astype(F32)[:, None]

    def tab(half):
        inv = ROPE_BASE ** (-jnp.arange(half, dtype=F32) / half)
        ar, ac = row * inv, col * inv
        cos = jnp.concatenate([jnp.cos(ar), jnp.cos(ar), jnp.cos(ac), jnp.cos(ac)], axis=1)
        sin = jnp.concatenate([-jnp.sin(ar), jnp.sin(ar), -jnp.sin(ac), jnp.sin(ac)], axis=1)
        reps = 128 // (4 * half)
        cos = jnp.concatenate([jnp.tile(cos, (1, reps)), jnp.ones((ident_rows, 128), F32)], axis=0)
        sin = jnp.concatenate([jnp.tile(sin, (1, reps)), jnp.zeros((ident_rows, 128), F32)], axis=0)
        return cos, sin

    c64, s64 = tab(HEAD_DIM // 4)
    c32, s32 = tab(DIFF_DIM // 4)
    return c64, s64, c32, s32


def _block_diag_ones(group):
    idx = np.arange(256) // group
    return jnp.asarray((idx[:, None] == idx[None, :]).astype(np.float32), dtype=BF16)


def _qk_gains(g_win, g_na, g_diff, g_gqa):
    s64, s32 = HEAD_DIM ** -0.5 * LOG2E, DIFF_DIM ** -0.5 * LOG2E
    rows = [jnp.tile(g_win[0] * s64, 2), jnp.tile(g_win[1], 2), jnp.tile(g_na[0] * s64, 2), jnp.tile(g_na[1], 2),
            jnp.tile(g_diff[0] * s32, 4), jnp.tile(g_diff[1], 4), jnp.tile(g_gqa[0] * s64, 2), jnp.tile(g_gqa[1], 2)]
    return jnp.stack(rows).astype(F32)


def _na_bias_tiles(rpb, rows):
    w = GRID_W
    qc = np.arange(w)[:, None]
    kc = np.arange(w)[None, :]
    c_start = np.clip(qc - NA_COLS // 2, 0, w - NA_COLS)
    col_ok = (kc >= c_start) & (kc < c_start + NA_COLS)
    dc = np.clip(kc - qc + NA_COLS - 1, 0, 2 * NA_COLS - 2)
    pick = jnp.asarray((dc[None, :, :] == np.arange(2 * NA_COLS - 1)[:, None, None]).astype(np.float32))
    picked = jnp.einsum('hdc,cqk->hdqk', rpb.astype(F32), pick, precision=lax.Precision.HIGHEST)
    t = jnp.where(jnp.asarray(col_ok), picked * LOG2E, NEG)
    neg = jnp.full((rpb.shape[0], w, w), NEG, F32)
    cases = []
    for r0, rs in ((0, 0), (NA_QROWS, 0), (rows - NA_QROWS, rows - NA_KROWS)):
        qtiles = []
        for i in range(NA_QROWS):
            qrow = r0 + i
            r_start = min(max(qrow - NA_ROWS // 2, 0), rows - NA_ROWS)
            blks = []
            for j in range(NA_KROWS):
                krow = rs + j
                blks.append(t[:, krow - qrow + NA_ROWS - 1] if r_start <= krow < r_start + NA_ROWS else neg)
            qtiles.append(jnp.concatenate(blks, axis=2))
        cases.append(jnp.concatenate(qtiles, axis=1))
    return jnp.stack(cases)


def _window_mask_tiles(tq):
    kw = tq + 2 * WINDOW
    r = lax.broadcasted_iota(I32, (3, tq, kw), 1)
    c = lax.broadcasted_iota(I32, (3, tq, kw), 2)
    shift = lax.broadcasted_iota(I32, (3, tq, kw), 0) * WINDOW
    return jnp.where(jnp.abs(c - r - shift) <= WINDOW, 0.0, NEG).astype(F32)


def _router_weights(router_w):
    hi = router_w.astype(BF16)
    lo = (router_w - hi.astype(F32)).astype(BF16)
    pad = jnp.zeros((router_w.shape[0], 128 - 2 * N_EXPERTS), BF16)
    return jnp.concatenate([hi, lo, pad], axis=1)


def kernel(x, c, ctx, c_ctx, ada_w, ada_b, norm_mix_g, norm_ffn_g, w_in, qk_g_win, qk_g_na, qk_g_diff, qk_g_gqa,
           sink_win, rpb_na, lambda_diff, out_gain, w_out, router_w, router_b, w_gate, w_up, w_down):
    batch, seq, d = x.shape
    ctx_len = ctx.shape[1]
    depth = w_in.shape[0]
    n_lat, n_ctx = batch * seq, batch * ctx_len
    n_all = n_lat + n_ctx
    rows = seq // GRID_W
    assert d == D_MODEL and seq % TOKEN_TILE == 0 and n_ctx % TOKEN_TILE == 0 and ctx_len % 128 == 0
    assert rows % NA_QROWS == 0 and rows >= NA_KROWS + NA_QROWS and seq >= ATT_TQ + 2 * WINDOW
    assert ctx_len % 128 == 0 and (NA_KROWS * GRID_W) % KEY_TILE == 0

    pad_rows = -(-(batch + 1) // 8) * 8
    s_all = jnp.concatenate([c, c_ctx[None, :], jnp.zeros((pad_rows - batch - 1, d), F32)], axis=0)
    mods = _ada_call(s_all, ada_w, ada_b).reshape(depth, pad_rows, ADA_CHUNKS, d)

    xs_all = (x.reshape(n_lat, d), ctx.reshape(n_ctx, d))
    tabs = _rope_tables(seq, TOKEN_TILE)
    wmask = _window_mask_tiles(ATT_TQ)
    bd64, bd32 = _block_diag_ones(HEAD_DIM), _block_diag_ones(DIFF_DIM)
    rw = _router_weights(router_w)
    rb = router_b.astype(F32).reshape(N_EXPERTS, 1)
    lo_tab = jnp.asarray([EXPERTS_PER_GROUP * (k // N_PAIRS) + PAIR_LO[k % N_PAIRS] for k in range(N_CLASSES)], I32)
    hi_tab = jnp.asarray([EXPERTS_PER_GROUP * (k // N_PAIRS) + PAIR_HI[k % N_PAIRS] for k in range(N_CLASSES)], I32)

    ffn, mod_prev = None, None
    for layer in range(depth):
        need_ctx = layer < depth - 1
        mod = mods[layer]
        qkg = _qk_gains(qk_g_win[layer], qk_g_na[layer], qk_g_diff[layer], qk_g_gqa[layer])
        xs_all, u = _inproj_call(xs_all, ffn, mod_prev, mod, norm_mix_g[layer].reshape(1, d),
                                 w_in[layer].astype(BF16), qkg, bd64, bd32, tabs, batch=batch, seq=seq)
        y = _attention_calls(u, sink_win[layer].astype(F32), lambda_diff[layer].astype(F32),
                             _na_bias_tiles(rpb_na[layer], rows), wmask, layer=layer, batch=batch, seq=seq,
                             ctx_len=ctx_len, need_ctx=need_ctx)
        n_proc = n_all if need_ctx else n_lat
        x_mid, h_ext, cls = _outproj_call(y, xs_all, mod, out_gain[layer].reshape(1, d), w_out[layer].astype(BF16),
                                          norm_ffn_g[layer].reshape(1, d), rw, rb, n_proc=n_proc, batch=batch, seq=seq)
        n_blocks = n_proc // MOE_BLK + N_CLASSES
        pos, bcls, nval = _rank_call(cls.reshape(n_proc // 128, 128), n_blocks)
        pos3 = pos.reshape(n_proc // ROW_DMA, 1, ROW_DMA)
        bcls, nval = bcls[0, :n_blocks], nval[0, :n_blocks]
        sorted_rows = _scatter_call(h_ext, pos3, n_blocks * MOE_BLK)
        ys = _moe_call(sorted_rows, lo_tab[bcls], hi_tab[bcls], nval, w_gate[layer].astype(BF16),
                       w_up[layer].astype(BF16), w_down[layer].astype(BF16))
        ffn = (ys, pos)
        xs_all, mod_prev = x_mid, mod
    out = _final_call(xs_all, ffn[0], ffn[1], mod_prev, batch=batch, seq=seq)
    return out.reshape(batch, seq, d)
```

```python
import functools
import math

import numpy as np
import jax
import jax.numpy as jnp
from jax import lax
from jax.experimental import pallas as pl
from jax.experimental.pallas import tpu as pltpu

F32 = jnp.float32
BF16 = jnp.bfloat16
I32 = jnp.int32
U32 = jnp.uint32

D_MODEL = 1024
GRID_W = 64
HEAD_DIM = 64
HEADS = 4
DIFF_DIM = 32
WINDOW = 128
NA_ROWS = 8
NA_COLS = 16
ROPE_BASE = 10000.0
N_EXPERTS = 16
EXPERTS_PER_GROUP = 4
D_EXPERT = 512
ADA_CHUNKS = 6
EPS = 1e-6
NEG = -1e30
D_IN = 2560

COL_A_Q, COL_A_KV, COL_B_Q, COL_B_K, COL_B_V, COL_C_Q, COL_C_K, COL_C_V, COL_D_Q, COL_D_KV = range(10)

TOKEN_TILE = 512
ATT_TQ = 256
ATT_TQ_FULL = 512
NA_QROWS = 4
NA_KROWS = 12
KEY_TILE = 256
ATT_RB = 128
ATT_SLOTS = 4
WINDOW_TILES = 1
NA_TILES = 2
LOG2E = math.log2(math.e)
MOE_BLK = 512
ROW_DMA = 512
N_PAIRS = 6
N_CLASSES = (N_EXPERTS // EXPERTS_PER_GROUP) * N_PAIRS
PAIR_LO = (0, 0, 0, 1, 1, 2)
PAIR_HI = (1, 2, 3, 2, 3, 3)
EXT_W = D_MODEL // 2 + 128
VMEM_LIMIT = 56 * 1024 * 1024


def _cparams(*sem):
    return pltpu.CompilerParams(dimension_semantics=sem, vmem_limit_bytes=VMEM_LIMIT)


def _nt_dot(a, b):
    return lax.dot_general(a, b, (((1,), (1,)), ((), ())), preferred_element_type=F32)


def _dot(a, b):
    return jnp.dot(a, b, preferred_element_type=F32)


def _sigmoid(x):
    return 1.0 / (1.0 + jnp.exp(-x))


def _drop_arg(kern, pos):
    def wrapped(*refs):
        return kern(*refs[:pos], *refs[pos + 1:])
    return wrapped


def _ada_kernel(s_ref, w_ref, b_ref, o_ref):
    s = s_ref[...]
    act = s * _sigmoid(s)
    o_ref[0] = jnp.dot(act, w_ref[0], precision=lax.Precision.HIGHEST,
                       preferred_element_type=F32) + b_ref[0]


def _ada_call(s_all, ada_w, ada_b):
    depth, d, n = ada_w.shape
    rows = s_all.shape[0]
    tn = 1536
    return pl.pallas_call(
        _ada_kernel,
        out_shape=jax.ShapeDtypeStruct((depth, rows, n), F32),
        grid=(depth, n // tn),
        in_specs=[
            pl.BlockSpec((rows, d), lambda l, j: (0, 0)),
            pl.BlockSpec((1, d, tn), lambda l, j: (l, 0, j)),
            pl.BlockSpec((1, 1, tn), lambda l, j: (l, 0, j)),
        ],
        out_specs=pl.BlockSpec((1, rows, tn), lambda l, j: (l, 0, j)),
        compiler_params=_cparams("arbitrary", "arbitrary"),
        name="ada_mod",
    )(s_all, ada_w, ada_b.reshape(depth, 1, n))


def _start_row_gather(pos_ref, src_ref, dst_ref, sem):
    for r in range(dst_ref.shape[0]):
        pltpu.make_async_copy(src_ref.at[pl.ds(pos_ref[0, 0, r], 1)], dst_ref.at[pl.ds(r, 1)], sem).start()


def _wait_row_gather(src_ref, dst_ref, sem):
    pltpu.make_async_copy(src_ref.at[pl.ds(0, dst_ref.shape[0])], dst_ref, sem).wait()


def _prefetched_rows(pos_ref, pos_next_ref, src_ref, buf, sems):
    i = pl.program_id(0)
    slot = i % 2

    @pl.when(i == 0)
    def _():
        _start_row_gather(pos_ref, src_ref, buf.at[0], sems.at[0])

    @pl.when(i + 1 < pl.num_programs(0))
    def _():
        _start_row_gather(pos_next_ref, src_ref, buf.at[1 - slot], sems.at[1 - slot])

    _wait_row_gather(src_ref, buf.at[slot], sems.at[slot])
    return buf.at[slot]


def _inproj_kernel(*refs, has_ffn, tm, n_lat_tiles, tiles_per_batch, seq):
    if has_ffn:
        (x_ref, pos_ref, pos_next_ref, ys_ref, modp_ref, mod_ref, g_ref, w_ref, qkg_ref, bd64_ref, bd32_ref,
         c64_ref, s64_ref, c32_ref, s32_ref, xo_ref, u_ref, fbuf, fsems) = refs
        f_ref = _prefetched_rows(pos_ref, pos_next_ref, ys_ref, fbuf, fsems)
    else:
        (xl_ref, xc_ref, mod_ref, g_ref, w_ref, qkg_ref, bd64_ref, bd32_ref,
         c64_ref, s64_ref, c32_ref, s32_ref, u_ref) = refs
    i = pl.program_id(0)
    hm = tm // 2

    def prologue(r0):
        x = x_ref[r0:r0 + hm, :] if has_ffn else _two_source_rows(xl_ref, xc_ref, r0, hm, n_lat_tiles)
        if has_ffn:
            x = x + modp_ref[0, 5:6, :] * f_ref[r0:r0 + hm, :]
            xo_ref[r0:r0 + hm, :] = x
        ms = jnp.mean(x * x, axis=-1, keepdims=True)
        hn = x * lax.rsqrt(ms + EPS) * g_ref[...]
        return (hn * (1.0 + mod_ref[0, 1:2, :]) + mod_ref[0, 0:1, :]).astype(BF16)

    p0 = jnp.where(i < n_lat_tiles, (i % tiles_per_batch) * tm, seq)
    p0 = pl.multiple_of(p0, tm)
    lane = lax.broadcasted_iota(I32, (hm, 128), 1)

    def norm_rope(a, r0, gidx, group, bd_ref, rope_tabs):
        w = a.shape[1]
        ssq = _dot((a * a).astype(BF16), bd_ref[0:w, 0:w])
        r = lax.rsqrt(ssq * (1.0 / group) + EPS)
        outs = []
        for s in range(w // 128):
            t = a[:, 128 * s:128 * s + 128] * r[:, 128 * s:128 * s + 128] * qkg_ref[gidx:gidx + 1, :]
            if rope_tabs is not None:
                c_ref, s_ref, half = rope_tabs
                fwd = pltpu.roll(t, 128 - half, axis=1)
                bwd = pltpu.roll(t, half, axis=1)
                sw = jnp.where((lane % (2 * half)) < half, fwd, bwd)
                t = t * c_ref[pl.ds(p0 + r0, hm), :] + sw * s_ref[pl.ds(p0 + r0, hm), :]
            outs.append(t)
        return outs[0] if len(outs) == 1 else jnp.concatenate(outs, axis=1)

    rope64 = (c64_ref, s64_ref, HEAD_DIM // 4)
    rope32 = (c32_ref, s32_ref, DIFF_DIM // 4)

    def epilogue(acc, r0, blk):
        if blk == COL_A_Q:
            out = norm_rope(acc, r0, 0, HEAD_DIM, bd64_ref, rope64)
        elif blk == COL_A_KV:
            out = jnp.concatenate([norm_rope(acc[:, :128], r0, 1, HEAD_DIM, bd64_ref, rope64), acc[:, 128:]], axis=1)
        elif blk == COL_B_Q:
            out = norm_rope(acc, r0, 2, HEAD_DIM, bd64_ref, None)
        elif blk == COL_B_K:
            out = norm_rope(acc, r0, 3, HEAD_DIM, bd64_ref, None)
        elif blk == COL_C_Q:
            out = norm_rope(acc, r0, 4, DIFF_DIM, bd32_ref, rope32)
        elif blk == COL_C_K:
            out = norm_rope(acc, r0, 5, DIFF_DIM, bd32_ref, rope32)
        elif blk == COL_D_Q:
            out = norm_rope(acc, r0, 6, HEAD_DIM, bd64_ref, rope64)
        elif blk == COL_D_KV:
            out = jnp.concatenate([norm_rope(acc[:, :128], r0, 7, HEAD_DIM, bd64_ref, rope64), acc[:, 128:]], axis=1)
        else:
            out = acc
        u_ref[r0:r0 + hm, 256 * blk:256 * blk + 256] = out.astype(BF16)

    n_blk = D_IN // 256
    units = [(r0, blk) for r0 in (0, hm) for blk in range(n_blk)]
    hbs = {0: prologue(0)}
    acc = _dot(hbs[0], w_ref[:, 0:256])
    hbs[hm] = prologue(hm)
    for t, (r0, blk) in enumerate(units):
        nxt = None
        if t + 1 < len(units):
            r1, b1 = units[t + 1]
            nxt = _dot(hbs[r1], w_ref[:, 256 * b1:256 * b1 + 256])
        epilogue(acc, r0, blk)
        acc = nxt


def _two_source_specs(tm, n_lat_tiles):
    return [pl.BlockSpec((tm, D_MODEL), lambda i: (jnp.minimum(i, n_lat_tiles - 1), 0)),
            pl.BlockSpec((tm, D_MODEL), lambda i: (jnp.maximum(i - n_lat_tiles, 0), 0))]


def _two_source_rows(xl_ref, xc_ref, r0, n, n_lat_tiles):
    return jnp.where(pl.program_id(0) < n_lat_tiles, xl_ref[r0:r0 + n, :], xc_ref[r0:r0 + n, :])


def _mod_spec(n_lat_tiles, tiles_per_batch, batch):
    return pl.BlockSpec((1, ADA_CHUNKS, D_MODEL),
                        lambda i: (jnp.where(i < n_lat_tiles, i // tiles_per_batch, batch), 0, 0))


def _const_spec(shape):
    return pl.BlockSpec(shape, lambda i: (0,) * len(shape))


def _pos_tile_specs(n_tiles, tm):
    return [pl.BlockSpec((1, 1, tm), lambda i: (i, 0, 0), memory_space=pltpu.SMEM),
            pl.BlockSpec((1, 1, tm), lambda i: (jnp.minimum(i + 1, n_tiles - 1), 0, 0), memory_space=pltpu.SMEM)]


def _inproj_call(x, ffn_src, mod_prev, mod, g, w_in, qkg, bd64, bd32, tabs, *, batch, seq):
    tm = TOKEN_TILE
    n_lat_tiles = batch * seq // tm
    tpb = seq // tm
    has_ffn = ffn_src is not None
    tile = pl.BlockSpec((tm, D_MODEL), lambda i: (i, 0))
    modspec = _mod_spec(n_lat_tiles, tpb, batch)
    tab_rows = tabs[0].shape[0]
    if has_ffn:
        n_all = x.shape[0]
        in_specs, args = [tile], [x]
    else:
        n_all = x[0].shape[0] + x[1].shape[0]
        in_specs, args = _two_source_specs(tm, n_lat_tiles), list(x)
    scratch = []
    if has_ffn:
        ys, pos = ffn_src
        pos_tiles = pos.reshape(n_all // tm, 1, tm)
        in_specs += _pos_tile_specs(n_all // tm, tm) + [pl.BlockSpec(memory_space=pl.ANY), modspec]
        args += [pos_tiles, pos_tiles, ys, mod_prev]
        scratch = [pltpu.VMEM((2, tm, D_MODEL), F32), pltpu.SemaphoreType.DMA((2,))]
    in_specs += [modspec, _const_spec((1, D_MODEL)), _const_spec((D_MODEL, D_IN)), _const_spec((8, 128)),
                 _const_spec((256, 256)), _const_spec((256, 256))] + [_const_spec((tab_rows, 128))] * 4
    args += [mod, g, w_in, qkg, bd64, bd32] + list(tabs)
    u_shape = jax.ShapeDtypeStruct((n_all, D_IN), BF16)
    u_spec = pl.BlockSpec((tm, D_IN), lambda i: (i, 0))
    if has_ffn:
        out_shape = (jax.ShapeDtypeStruct((n_all, D_MODEL), F32), u_shape)
        out_specs = (tile, u_spec)
    else:
        out_shape, out_specs = u_shape, u_spec
    kern = functools.partial(_inproj_kernel, has_ffn=has_ffn, tm=tm, n_lat_tiles=n_lat_tiles,
                             tiles_per_batch=tpb, seq=seq)
    res = pl.pallas_call(
        kern, out_shape=out_shape, grid=(n_all // tm,), in_specs=in_specs, out_specs=out_specs,
        scratch_shapes=scratch, compiler_params=_cparams("arbitrary"), name="inproj",
    )(*args)
    return res if has_ffn else (x, res)


class _AttItem:
    def __init__(self, q, kt_view, v_view, ranges, bias=None, extra=None):
        self.q, self.kt_view, self.v_view, self.ranges, self.bias, self.extra = q, kt_view, v_view, ranges, bias, extra
        self.n_keys = sum(n for _, n in ranges)
        self.m = None

    def scores(self, s_scr, rows):
        q_blk = self.q()
        m, off, ti = None, 0, 0
        for start, n in self.ranges:
            for o in range(0, n, KEY_TILE):
                w = min(KEY_TILE, n - o)
                s = _dot(q_blk, self.kt_view[:, pl.ds(start + o, w)])
                b = self.bias(ti) if self.bias is not None else None
                if b is not None:
                    s = s + b
                ti += 1
                s_scr[rows, off + o:off + o + w] = s
                tile_max = s.max(axis=-1, keepdims=True)
                m = tile_max if m is None else jnp.maximum(m, tile_max)
            off += n
        self.m = m if self.extra is None else jnp.maximum(m, self.extra)

    def probs(self, s_scr, p_scr, rows):
        for o in range(0, self.n_keys, KEY_TILE):
            w = min(KEY_TILE, self.n_keys - o)
            p_scr[rows, o:o + w] = jnp.exp2(s_scr[rows, o:o + w] - self.m).astype(BF16)

    def values(self, p_scr, rows):
        pv, off = None, 0
        for start, n in self.ranges:
            c = _dot(p_scr[rows, off:off + n], self.v_view[pl.ds(start, n), :])
            pv = c if pv is None else pv + c
            off += n
        denom = pv[:, 64:65]
        if self.extra is not None:
            denom = denom + jnp.exp2(self.extra - self.m)
        return pv[:, 0:64] / denom


def _run_att_items(items, s_scr, p_scr):
    def rows(t):
        r0 = (t % ATT_SLOTS) * ATT_RB
        return slice(r0, r0 + ATT_RB)

    outs = []
    for t in range(len(items) + 2):
        if t < len(items):
            items[t].scores(s_scr, rows(t))
        if 0 <= t - 1 < len(items):
            items[t - 1].probs(s_scr, p_scr, rows(t - 1))
        if 0 <= t - 2 < len(items):
            outs.append(items[t - 2].values(p_scr, rows(t - 2)))
    return outs


def _stage_heads(k_parts, v_parts, kt_scr, v_scr):
    eye = jnp.where(lax.broadcasted_iota(I32, (HEAD_DIM, HEAD_DIM), 0)
                    == lax.broadcasted_iota(I32, (HEAD_DIM, HEAD_DIM), 1), 1.0, 0.0).astype(BF16)
    for h in range(kt_scr.shape[0]):
        r0 = 0
        for (kr, kc0), (vr, vc0) in zip(k_parts, v_parts):
            n = kr.shape[0]
            kt_scr[h, :, r0:r0 + n] = _nt_dot(eye, kr[:, kc0 + 64 * h:kc0 + 64 * h + 64]).astype(BF16)
            v_scr[h, r0:r0 + n, 0:64] = vr[:, vc0 + 64 * h:vc0 + 64 * h + 64]
            r0 += n
        v_scr[h, :, 64:128] = jnp.ones((v_scr.shape[1], 64), BF16)


def _group_rms(y):
    return y * lax.rsqrt(jnp.mean(y * y, axis=-1, keepdims=True) + EPS)


def _stage_once_per_batch_row(k_parts, v_parts, k_scr, v_scr):
    @pl.when(pl.program_id(1) == 0)
    def _():
        _stage_heads(k_parts, v_parts, k_scr, v_scr)


def _row_blocks(n):
    return range(0, n, ATT_RB)


def _q_thunk(q_ref, r0, head):
    return lambda: q_ref[r0:r0 + ATT_RB, 64 * head:64 * head + 64]


def _heads_to_rows(outs, n_heads):
    per = len(outs) // n_heads
    return jnp.concatenate([jnp.concatenate(outs[h * per:(h + 1) * per], axis=0) for h in range(n_heads)], axis=1)


def _dense_kernel(q_ref, kv_ref, kvc_ref, o_ref, kt_scr, v_scr, s_scr, p_scr, *, tq):
    _stage_once_per_batch_row([(kv_ref, 0), (kvc_ref, 0)], [(kv_ref, 128), (kvc_ref, 128)], kt_scr, v_scr)
    nk = v_scr.shape[1]
    items = [_AttItem(_q_thunk(q_ref, r0, hq), kt_scr.at[hq // 2], v_scr.at[hq // 2], [(0, nk)])
             for hq in range(HEADS) for r0 in _row_blocks(tq)]
    y = _heads_to_rows(_run_att_items(items, s_scr, p_scr), HEADS)
    o_ref[...] = _group_rms(y).astype(BF16)


def _window_kernel(sink_ref, q_ref, kv_ref, kvc_ref, *refs, tq, seq, tiles):
    mask_refs, (o_ref, kt_scr, v_scr, s_scr, p_scr) = refs[:tiles], refs[tiles:]
    _stage_once_per_batch_row([(kv_ref, 0), (kvc_ref, 0)], [(kv_ref, 128), (kvc_ref, 128)], kt_scr, v_scr)
    kw = tq + 2 * WINDOW
    nc = v_scr.shape[1] - seq

    def mask_thunk(mask_ref, r0):
        return lambda ti: (mask_ref[0, r0:r0 + ATT_RB, KEY_TILE * ti:KEY_TILE * ti + KEY_TILE]
                           if KEY_TILE * ti < kw else None)

    items = []
    for t, mask_ref in enumerate(mask_refs):
        q0 = (tiles * pl.program_id(1) + t) * tq
        ks = pl.multiple_of(jnp.clip(q0 - WINDOW, 0, seq - kw), 128)
        items += [_AttItem(_q_thunk(q_ref, t * tq + r0, hq), kt_scr.at[hq // 2], v_scr.at[hq // 2],
                           [(ks, kw), (seq, nc)], bias=mask_thunk(mask_ref, r0), extra=sink_ref[hq] * LOG2E)
                  for hq in range(HEADS) for r0 in _row_blocks(tq)]
    outs = _run_att_items(items, s_scr, p_scr)
    per = len(outs) // tiles
    y = jnp.concatenate([_heads_to_rows(outs[t * per:(t + 1) * per], HEADS) for t in range(tiles)], axis=0)
    o_ref[...] = _group_rms(y).astype(BF16)


def _neigh_kernel(q_ref, k_ref, v_ref, kc_ref, vc_ref, *refs, rows, seq, tiles):
    bias_refs, (o_ref, kt_scr, v_scr, s_scr, p_scr) = refs[:tiles], refs[tiles:]
    _stage_once_per_batch_row([(k_ref, 0), (kc_ref, 0)], [(v_ref, 0), (vc_ref, 0)], kt_scr, v_scr)
    nk = NA_KROWS * GRID_W
    nc = v_scr.shape[1] - seq
    qrows = NA_QROWS * GRID_W

    def bias_thunk(bias_ref, h, r0):
        return lambda ti: (bias_ref[0, h, r0:r0 + ATT_RB, KEY_TILE * ti:KEY_TILE * ti + KEY_TILE]
                           if KEY_TILE * ti < nk else None)

    items = []
    for t, bias_ref in enumerate(bias_refs):
        rb = tiles * pl.program_id(1) + t
        rs = jnp.clip(NA_QROWS * rb - NA_ROWS // 2, 0, rows - NA_KROWS)
        k0 = pl.multiple_of(rs * GRID_W, KEY_TILE)
        items += [_AttItem(_q_thunk(q_ref, t * qrows + r0, h), kt_scr.at[h], v_scr.at[h], [(k0, nk), (seq, nc)],
                           bias=bias_thunk(bias_ref, h, r0))
                  for h in range(HEADS) for r0 in _row_blocks(qrows)]
    outs = _run_att_items(items, s_scr, p_scr)
    per = len(outs) // tiles
    y = jnp.concatenate([_heads_to_rows(outs[t * per:(t + 1) * per], HEADS) for t in range(tiles)], axis=0)
    o_ref[...] = _group_rms(y).astype(BF16)


def _diff_lambda(lam_ref, lam_init):
    lp = lam_ref[...]
    a = jnp.sum(lp[0:1] * lp[1:2], axis=-1, keepdims=True)
    b = jnp.sum(lp[2:3] * lp[3:4], axis=-1, keepdims=True)
    return jnp.exp(a) - jnp.exp(b) + lam_init


def _diff_kernel(lam_ref, q_ref, k_ref, v_ref, kc_ref, vc_ref, o_ref, kt_scr, v_scr, s_scr, p_scr, *, lam_init, tq):
    _stage_once_per_batch_row([(k_ref, 0), (kc_ref, 0)], [(v_ref, 0), (vc_ref, 0)], kt_scr, v_scr)
    lam = _diff_lambda(lam_ref, lam_init)
    nk = v_scr.shape[1]
    first = lax.broadcasted_iota(I32, (ATT_RB, HEAD_DIM), 1) < DIFF_DIM
    zero = jnp.zeros((ATT_RB, HEAD_DIM), BF16)

    def q_map(r0, h, half):
        def thunk():
            qh = q_ref[r0:r0 + ATT_RB, 64 * h:64 * h + 64]
            return jnp.where(first, qh, zero) if half == 0 else jnp.where(first, zero, qh)
        return thunk

    items = [_AttItem(q_map(r0, h, half), kt_scr.at[h], v_scr.at[h], [(0, nk)])
             for h in range(HEADS) for r0 in _row_blocks(tq) for half in range(2)]
    outs = _run_att_items(items, s_scr, p_scr)
    diffs = [outs[i] - lam * outs[i + 1] for i in range(0, len(outs), 2)]
    per = len(diffs) // HEADS
    heads = [_group_rms(jnp.concatenate(diffs[h * per:(h + 1) * per], axis=0)) * (1.0 - lam_init)
             for h in range(HEADS)]
    o_ref[...] = jnp.concatenate(heads, axis=1).astype(BF16)


def _softmax_parts(scores, extra=None):
    m = scores[0].max(axis=-1, keepdims=True)
    for s in scores[1:]:
        m = jnp.maximum(m, s.max(axis=-1, keepdims=True))
    if extra is not None:
        m = jnp.maximum(m, extra)
    ps = [jnp.exp2(s - m) for s in scores]
    l = ps[0].sum(axis=-1, keepdims=True)
    for p in ps[1:]:
        l = l + p.sum(axis=-1, keepdims=True)
    if extra is not None:
        l = l + jnp.exp2(extra - m)
    return ps, 1.0 / l


def _ctx_kernel(sink_ref, lam_ref, u_ref, o_ref, *, lam_init):
    L = u_ref.shape[0]
    top = lax.broadcasted_iota(I32, (2 * L, 1), 0) < L

    def cols(blk, lo, width=64):
        return u_ref[:, 256 * blk + lo:256 * blk + lo + width]

    for q_blk, kv_blk, out_col, use_sink in ((COL_A_Q, COL_A_KV, 0, True), (COL_D_Q, COL_D_KV, 768, False)):
        outs = [None] * HEADS
        for j in range(2):
            q2 = jnp.concatenate([cols(q_blk, 128 * j), cols(q_blk, 128 * j + 64)], axis=0)
            snk = jnp.where(top, sink_ref[2 * j], sink_ref[2 * j + 1]) * LOG2E if use_sink else None
            (p,), inv = _softmax_parts([_nt_dot(q2, cols(kv_blk, 64 * j))], extra=snk)
            o = _dot(p.astype(BF16), cols(kv_blk, 128 + 64 * j)) * inv
            outs[2 * j], outs[2 * j + 1] = o[:L], o[L:]
        o_ref[:, out_col:out_col + 256] = _group_rms(jnp.concatenate(outs, axis=1)).astype(BF16)
    outs = []
    for h in range(HEADS):
        (p,), inv = _softmax_parts([_nt_dot(cols(COL_B_Q, 64 * h), cols(COL_B_K, 64 * h))])
        outs.append(_dot(p.astype(BF16), cols(COL_B_V, 64 * h)) * inv)
    o_ref[:, 256:512] = _group_rms(jnp.concatenate(outs, axis=1)).astype(BF16)
    lam = _diff_lambda(lam_ref, lam_init)
    first = lax.broadcasted_iota(I32, (L, HEAD_DIM), 1) < DIFF_DIM
    zero = jnp.zeros((L, HEAD_DIM), BF16)
    outs = []
    for h in range(HEADS):
        qh, kh = cols(COL_C_Q, 64 * h), cols(COL_C_K, 64 * h)
        maps = []
        for half in range(2):
            qm = jnp.where(first, qh, zero) if half == 0 else jnp.where(first, zero, qh)
            (p,), inv = _softmax_parts([_nt_dot(qm, kh)])
            maps.append(p * inv)
        o = _dot((maps[0] - lam * maps[1]).astype(BF16), cols(COL_C_V, 64 * h))
        outs.append(_group_rms(o) * (1.0 - lam_init))
    o_ref[:, 512:768] = jnp.concatenate(outs, axis=1).astype(BF16)


def _attention_calls(u, sink, lam_p, bias, wmask, *, layer, batch, seq, ctx_len, need_ctx):
    lam_init = 0.8 - 0.6 * math.exp(-0.3 * layer)
    n_all = u.shape[0]
    tq = ATT_TQ
    nq = seq // tq
    nk_all = seq + ctx_len
    cb0 = batch * seq // ctx_len
    y_shape = jax.ShapeDtypeStruct((n_all, D_MODEL), BF16)
    smem = pl.BlockSpec(memory_space=pltpu.SMEM)
    any_spec = pl.BlockSpec(memory_space=pl.ANY)
    lam_spec2 = pl.BlockSpec((4, DIFF_DIM), lambda b, i: (0, 0))
    cp = _cparams("arbitrary", "arbitrary")

    def scratch(n_heads, n_keys):
        return [pltpu.VMEM((n_heads, HEAD_DIM, nk_all), BF16), pltpu.VMEM((n_heads, nk_all, 128), BF16),
                pltpu.VMEM((ATT_SLOTS * ATT_RB, n_keys), F32), pltpu.VMEM((ATT_SLOTS * ATT_RB, n_keys), BF16)]

    def qspec(col, t=tq):
        return pl.BlockSpec((t, 256), lambda b, i: (b * (seq // t) + i, col))

    tqf = ATT_TQ_FULL
    nqf = seq // tqf

    def latspec(col):
        return pl.BlockSpec((seq, 256), lambda b, i: (b, col))

    def ctxspec(col):
        return pl.BlockSpec((ctx_len, 256), lambda b, i: (cb0 + b, col))

    kw = tq + 2 * WINDOW

    def case_specs(shape, n_tiles, per_step):
        def spec(t):
            def index(b, i):
                j = per_step * i + t
                return (jnp.where(j == 0, 0, jnp.where(j == n_tiles - 1, 2, 1)),) + (0,) * (len(shape) - 1)
            return pl.BlockSpec(shape, index)
        return [spec(t) for t in range(per_step)]

    wt = WINDOW_TILES
    y = pl.pallas_call(
        functools.partial(_window_kernel, tq=tq, seq=seq, tiles=wt),
        out_shape=y_shape, grid=(batch, nq // wt),
        in_specs=[smem, qspec(COL_A_Q, wt * tq), latspec(COL_A_KV), ctxspec(COL_A_KV)]
        + case_specs((1, tq, kw), nq, wt),
        out_specs=qspec(0, wt * tq), scratch_shapes=scratch(2, kw + ctx_len),
        compiler_params=cp, name="mix_window",
    )(sink, u, u, u, *([wmask] * wt))

    rows = seq // GRID_W
    nrb = rows // NA_QROWS
    qrows = NA_QROWS * GRID_W
    nkb = NA_KROWS * GRID_W
    nt = NA_TILES
    y = pl.pallas_call(
        _drop_arg(functools.partial(_neigh_kernel, rows=rows, seq=seq, tiles=nt), 5 + nt),
        out_shape=y_shape, grid=(batch, nrb // nt),
        in_specs=[qspec(COL_B_Q, nt * qrows), latspec(COL_B_K), latspec(COL_B_V),
                  ctxspec(COL_B_K), ctxspec(COL_B_V)] + case_specs((1, HEADS, qrows, nkb), nrb, nt) + [any_spec],
        out_specs=qspec(1, nt * qrows), scratch_shapes=scratch(HEADS, nkb + ctx_len),
        input_output_aliases={5 + nt: 0}, compiler_params=cp, name="mix_neigh",
    )(u, u, u, u, u, *([bias] * nt), y)

    y = pl.pallas_call(
        _drop_arg(functools.partial(_diff_kernel, lam_init=lam_init, tq=tqf), 6),
        out_shape=y_shape, grid=(batch, nqf),
        in_specs=[lam_spec2, qspec(COL_C_Q, tqf), latspec(COL_C_K), latspec(COL_C_V),
                  ctxspec(COL_C_K), ctxspec(COL_C_V), any_spec],
        out_specs=qspec(2, tqf), scratch_shapes=scratch(HEADS, nk_all),
        input_output_aliases={6: 0}, compiler_params=cp, name="mix_diff",
    )(lam_p, u, u, u, u, u, y)

    y = pl.pallas_call(
        _drop_arg(functools.partial(_dense_kernel, tq=tqf), 3),
        out_shape=y_shape, grid=(batch, nqf),
        in_specs=[qspec(COL_D_Q, tqf), latspec(COL_D_KV), ctxspec(COL_D_KV), any_spec],
        out_specs=qspec(3, tqf), scratch_shapes=scratch(2, nk_all),
        input_output_aliases={3: 0}, compiler_params=cp, name="mix_dense",
    )(u, u, u, y)

    if need_ctx:
        y = pl.pallas_call(
            _drop_arg(functools.partial(_ctx_kernel, lam_init=lam_init), 3),
            out_shape=y_shape, grid=(batch,),
            in_specs=[smem, pl.BlockSpec((4, DIFF_DIM), lambda b: (0, 0)),
                      pl.BlockSpec((ctx_len, D_IN), lambda b: (cb0 + b, 0)), any_spec],
            out_specs=pl.BlockSpec((ctx_len, D_MODEL), lambda b: (cb0 + b, 0)),
            input_output_aliases={3: 0}, compiler_params=_cparams("arbitrary"), name="mix_ctx",
        )(sink, lam_p, u, y)
    return y


def _select4(idx, vals):
    return jnp.where(idx == 0, vals[0], jnp.where(idx == 1, vals[1], jnp.where(idx == 2, vals[2], vals[3])))


def _outproj_kernel(y_ref, *refs, tm, n_lat_tiles, two_source):
    n_x = 2 if two_source else 1
    x_refs, (mod_ref, gain_ref, w_ref, g_ref, rw_ref, rb_ref, xo_ref, h_ref, cls_ref) = refs[:n_x], refs[n_x:]
    hm = tm // 2

    def project(r0):
        yg = (y_ref[r0:r0 + hm, :].astype(F32) * gain_ref[...]).astype(BF16)
        return _dot(yg, w_ref[...])

    def normalise(r0, proj):
        x = (_two_source_rows(*x_refs, r0, hm, n_lat_tiles) if two_source else x_refs[0][r0:r0 + hm, :])
        xm = x + mod_ref[0, 2:3, :] * proj
        xo_ref[r0:r0 + hm, :] = xm
        hn = xm * lax.rsqrt(jnp.mean(xm * xm, axis=-1, keepdims=True) + EPS) * g_ref[...]
        h2 = hn * (1.0 + mod_ref[0, 4:5, :]) + mod_ref[0, 3:4, :]
        hi = h2.astype(BF16)
        hi32 = hi.astype(F32)
        lo = (h2 - hi32).astype(BF16)
        bits = lax.bitcast_convert_type(hi32, U32)
        half = D_MODEL // 2
        h_ref[r0:r0 + hm, 0:half] = (bits[:, 0:half] >> 16) | (bits[:, half:D_MODEL] & jnp.uint32(0xFFFF0000))
        return (_dot(hi, rw_ref[...]) + _dot(lo, rw_ref[...])).T

    def route(r0, lt):
        score = _sigmoid(lt[0:N_EXPERTS] + lt[N_EXPERTS:2 * N_EXPERTS])
        sel = score + rb_ref[...]
        srow = [sel[e:e + 1] for e in range(N_EXPERTS)]
        crow = [score[e:e + 1] for e in range(N_EXPERTS)]

        gidx = jnp.zeros((1, hm), I32)
        gbest = None
        for g in range(N_EXPERTS // EXPERTS_PER_GROUP):
            v = srow[4 * g:4 * g + 4]
            best = None
            for a in range(4):
                for b in range(a + 1, 4):
                    t = v[a] + v[b]
                    best = t if best is None else jnp.maximum(best, t)
            if g == 0:
                gbest = best
            else:
                better = best > gbest
                gidx = jnp.where(better, g, gidx)
                gbest = jnp.where(better, best, gbest)

        iv = [_select4(gidx, [srow[4 * g + i] for g in range(4)]) for i in range(4)]
        sv = [_select4(gidx, [crow[4 * g + i] for g in range(4)]) for i in range(4)]
        chosen = []
        for i in range(4):
            rank = jnp.zeros((1, hm), I32)
            for j in range(4):
                if j != i:
                    beats = (iv[j] >= iv[i]) if j < i else (iv[j] > iv[i])
                    rank = rank + jnp.where(beats, 1, 0)
            chosen.append(rank < 2)
        lo_i = jnp.where(chosen[0], 0, jnp.where(chosen[1], 1, 2))
        hi_i = jnp.where(chosen[3], 3, jnp.where(chosen[2], 2, 1))
        pair = jnp.where(lo_i == 0, hi_i - 1, jnp.where(lo_i == 1, hi_i + 1, 5))
        cls_ref[0, :, r0:r0 + hm] = gidx * N_PAIRS + pair
        s_lo = jnp.where(lo_i == 0, sv[0], jnp.where(lo_i == 1, sv[1], sv[2]))
        s_hi = jnp.where(hi_i == 3, sv[3], jnp.where(hi_i == 2, sv[2], sv[1]))
        inv = 1.0 / (s_lo + s_hi)
        rowi = lax.broadcasted_iota(I32, (128, hm), 0)
        wts = jnp.where(rowi == 0, s_lo * inv, jnp.where(rowi == 1, s_hi * inv, 0.0))
        h_ref[r0:r0 + hm, D_MODEL // 2:EXT_W] = lax.bitcast_convert_type(wts.T, U32)

    proj0 = project(0)
    proj1 = project(hm)
    lt0 = normalise(0, proj0)
    lt1 = normalise(hm, proj1)
    route(0, lt0)
    route(hm, lt1)


def _outproj_call(y, x, mod, gain, w_out, g, rw, rb, *, n_proc, batch, seq):
    tm = TOKEN_TILE
    n_tiles = n_proc // tm
    n_lat_tiles = batch * seq // tm
    tile = pl.BlockSpec((tm, D_MODEL), lambda i: (i, 0))
    two_source = isinstance(x, tuple)
    x_specs, x_args = (_two_source_specs(tm, n_lat_tiles), list(x)) if two_source else ([tile], [x])
    return pl.pallas_call(
        functools.partial(_outproj_kernel, tm=tm, n_lat_tiles=n_lat_tiles, two_source=two_source),
        out_shape=(jax.ShapeDtypeStruct((n_proc, D_MODEL), F32),
                   jax.ShapeDtypeStruct((n_proc, EXT_W), U32),
                   jax.ShapeDtypeStruct((n_tiles, 1, tm), I32)),
        grid=(n_tiles,),
        in_specs=[tile] + x_specs + [_mod_spec(n_lat_tiles, seq // tm, batch), _const_spec((1, D_MODEL)),
                  _const_spec((D_MODEL, D_MODEL)), _const_spec((1, D_MODEL)), _const_spec((D_MODEL, 128)),
                  _const_spec((N_EXPERTS, 1))],
        out_specs=(tile, pl.BlockSpec((tm, EXT_W), lambda i: (i, 0)), pl.BlockSpec((1, 1, tm), lambda i: (i, 0, 0))),
        compiler_params=_cparams("arbitrary"), name="outproj_router",
    )(y, *x_args, mod, gain, w_out, g, rw, rb)


def _rank_kernel(cls_ref, pos_ref, bcls_ref, nval_ref, *, blk):
    c = cls_ref[...]
    nr = c.shape[0]
    nbp = bcls_ref.shape[1]
    upper = (lax.broadcasted_iota(I32, (128, 128), 0) <= lax.broadcasted_iota(I32, (128, 128), 1))
    upper = jnp.where(upper, 1.0, 0.0).astype(BF16)
    lower = (lax.broadcasted_iota(I32, (nr, nr), 1) < lax.broadcasted_iota(I32, (nr, nr), 0))
    lower = jnp.where(lower, 1.0, 0.0).astype(BF16)
    bidx = lax.broadcasted_iota(I32, (1, nbp), 1).astype(F32)
    pos = jnp.zeros((nr, 128), F32)
    bcls = jnp.zeros((1, nbp), F32)
    nval = jnp.zeros((1, nbp), F32)
    start = jnp.zeros((1, 1), F32)
    for k in range(N_CLASSES):
        m = c == k
        incl = _dot(jnp.where(m, 1.0, 0.0).astype(BF16), upper)
        tot = incl[:, 127:128]
        above = _dot(lower, jnp.broadcast_to(tot, (nr, 128)).astype(BF16))
        cnt = above[nr - 1:nr, 0:1] + tot[nr - 1:nr, 0:1]
        nblk = jnp.floor((cnt + (blk - 1)) * (1.0 / blk))
        pos = jnp.where(m, start * blk + above + incl - 1.0, pos)
        end = start + nblk
        bcls = bcls + jnp.where(bidx >= end, 1.0, 0.0)
        nval = nval + jnp.where((bidx >= start) & (bidx < end), jnp.clip(cnt - (bidx - start) * blk, 0.0, blk), 0.0)
        start = end
    pos_ref[...] = pos.astype(I32)
    bcls_ref[...] = jnp.minimum(bcls, N_CLASSES - 1.0).astype(I32)
    nval_ref[...] = nval.astype(I32)


def _rank_call(cls2d, n_blocks):
    nr = cls2d.shape[0]
    nbp = -(-n_blocks // 128) * 128
    return pl.pallas_call(
        functools.partial(_rank_kernel, blk=MOE_BLK),
        out_shape=(jax.ShapeDtypeStruct((nr, 128), I32), jax.ShapeDtypeStruct((1, nbp), I32),
                   jax.ShapeDtypeStruct((1, nbp), I32)),
        grid=(1,),
        in_specs=[_const_spec((nr, 128))],
        out_specs=(_const_spec((nr, 128)), _const_spec((1, nbp)), _const_spec((1, nbp))),
        compiler_params=_cparams("arbitrary"), name="class_rank",
    )(cls2d)


def _scatter_kernel(pos_ref, h_ref, xs_ref, sem):
    n = h_ref.shape[0]
    for r in range(n):
        pltpu.make_async_copy(h_ref.at[pl.ds(r, 1)], xs_ref.at[pl.ds(pos_ref[0, 0, r], 1)], sem).start()
    pltpu.make_async_copy(h_ref, xs_ref.at[pl.ds(0, n)], sem).wait()


def _scatter_call(h_ext, pos3, n_slots):
    n, w = h_ext.shape
    r = ROW_DMA
    return pl.pallas_call(
        _scatter_kernel,
        out_shape=jax.ShapeDtypeStruct((n_slots, w), h_ext.dtype),
        grid=(n // r,),
        in_specs=[pl.BlockSpec((1, 1, r), lambda i: (i, 0, 0), memory_space=pltpu.SMEM),
                  pl.BlockSpec((r, w), lambda i: (i, 0))],
        out_specs=pl.BlockSpec(memory_space=pl.ANY),
        scratch_shapes=[pltpu.SemaphoreType.DMA(())],
        compiler_params=_cparams("arbitrary"), name="row_scatter",
    )(pos3, h_ext)


def _moe_kernel(elo_ref, ehi_ref, nval_ref, xs_ref, wg_lo, wu_lo, wd_lo, wg_hi, wu_hi, wd_hi, ys_ref):
    del elo_ref, ehi_ref
    nv = nval_ref[pl.program_id(0)]

    @pl.when(nv > 0)
    def _():
        live = lax.broadcasted_iota(I32, (MOE_BLK, 1), 0) < nv
        half = D_MODEL // 2
        words = jnp.where(live, xs_ref[:, 0:half], jnp.uint32(0))
        x = jnp.concatenate([lax.bitcast_convert_type(words << 16, F32).astype(BF16),
                             lax.bitcast_convert_type(words & jnp.uint32(0xFFFF0000), F32).astype(BF16)], axis=1)
        acts = []
        for wg, wu, col in ((wg_lo, wu_lo, half), (wg_hi, wu_hi, half + 1)):
            w = jnp.where(live, lax.bitcast_convert_type(xs_ref[:, col:col + 1], F32), 0.0)
            g = _dot(x, wg[0])
            acts.append((g * _sigmoid(g) * _dot(x, wu[0]) * w).astype(BF16))
        ys_ref[...] = _dot(acts[0], wd_lo[0]) + _dot(acts[1], wd_hi[0])

    @pl.when(nv == 0)
    def _():
        ys_ref[...] = jnp.zeros_like(ys_ref)


def _moe_call(xs, e_lo, e_hi, nval, w_gate, w_up, w_down):
    n_blocks = xs.shape[0] // MOE_BLK

    def wspec(shape, which):
        return pl.BlockSpec((1,) + shape, lambda b, elo, ehi, nv: ((elo, ehi)[which][b], 0, 0))

    gu, dn = (D_MODEL, D_EXPERT), (D_EXPERT, D_MODEL)
    grid_spec = pltpu.PrefetchScalarGridSpec(
        num_scalar_prefetch=3, grid=(n_blocks,),
        in_specs=[pl.BlockSpec((MOE_BLK, EXT_W), lambda b, elo, ehi, nv: (b, 0)),
                  wspec(gu, 0), wspec(gu, 0), wspec(dn, 0), wspec(gu, 1), wspec(gu, 1), wspec(dn, 1)],
        out_specs=pl.BlockSpec((MOE_BLK, D_MODEL), lambda b, elo, ehi, nv: (b, 0)),
    )
    return pl.pallas_call(
        _moe_kernel, out_shape=jax.ShapeDtypeStruct((xs.shape[0], D_MODEL), F32), grid_spec=grid_spec,
        compiler_params=_cparams("arbitrary"), name="moe_experts",
    )(e_lo, e_hi, nval, xs, w_gate, w_up, w_down, w_gate, w_up, w_down)


def _final_kernel(x_ref, pos_ref, pos_next_ref, ys_ref, mod_ref, o_ref, fbuf, fsems):
    f_ref = _prefetched_rows(pos_ref, pos_next_ref, ys_ref, fbuf, fsems)
    o_ref[...] = x_ref[...] + mod_ref[0, 5:6, :] * f_ref[...]


def _final_call(x, ys, pos, mod, *, batch, seq):
    tm = TOKEN_TILE
    n = batch * seq
    tile = pl.BlockSpec((tm, D_MODEL), lambda i: (i, 0))
    pos_tiles = pos.reshape(n // tm, 1, tm)
    return pl.pallas_call(
        _final_kernel, out_shape=jax.ShapeDtypeStruct((n, D_MODEL), F32), grid=(n // tm,),
        in_specs=[tile] + _pos_tile_specs(n // tm, tm) + [pl.BlockSpec(memory_space=pl.ANY),
                                                          _mod_spec(n // tm, seq // tm, batch)],
        out_specs=tile, scratch_shapes=[pltpu.VMEM((2, tm, D_MODEL), F32), pltpu.SemaphoreType.DMA((2,))],
        compiler_params=_cparams("arbitrary"), name="final_residual",
    )(x, pos_tiles, pos_tiles, ys, mod)


def _rope_tables(seq, ident_rows):
    pos = jnp.arange(seq, dtype=jnp.int32)
    row = (pos // GRID_W).astype(F32)[:, None]
    col = (pos % GRID_W).astype(F32)[:, None]

    def tab(half):
        inv = ROPE_BASE ** (-jnp.arange(half, dtype=F32) / half)
        ar, ac = row * inv, col * inv
        cos = jnp.concatenate([jnp.cos(ar), jnp.cos(ar), jnp.cos(ac), jnp.cos(ac)], axis=1)
        sin = jnp.concatenate([-jnp.sin(ar), jnp.sin(ar), -jnp.sin(ac), jnp.sin(ac)], axis=1)
        reps = 128 // (4 * half)
        cos = jnp.concatenate([jnp.tile(cos, (1, reps)), jnp.ones((ident_rows, 128), F32)], axis=0)
        sin = jnp.concatenate([jnp.tile(sin, (1, reps)), jnp.zeros((ident_rows, 128), F32)], axis=0)
        return cos, sin

    c64, s64 = tab(HEAD_DIM // 4)
    c32, s32 = tab(DIFF_DIM // 4)
    return c64, s64, c32, s32


def _block_diag_ones(group):
    idx = np.arange(256) // group
    return jnp.asarray((idx[:, None] == idx[None, :]).astype(np.float32), dtype=BF16)


def _qk_gains(g_win, g_na, g_diff, g_gqa):
    s64, s32 = HEAD_DIM ** -0.5 * LOG2E, DIFF_DIM ** -0.5 * LOG2E
    rows = [jnp.tile(g_win[0] * s64, 2), jnp.tile(g_win[1], 2), jnp.tile(g_na[0] * s64, 2), jnp.tile(g_na[1], 2),
            jnp.tile(g_diff[0] * s32, 4), jnp.tile(g_diff[1], 4), jnp.tile(g_gqa[0] * s64, 2), jnp.tile(g_gqa[1], 2)]
    return jnp.stack(rows).astype(F32)


def _na_bias_tiles(rpb, rows):
    w = GRID_W
    qc = np.arange(w)[:, None]
    kc = np.arange(w)[None, :]
    c_start = np.clip(qc - NA_COLS // 2, 0, w - NA_COLS)
    col_ok = (kc >= c_start) & (kc < c_start + NA_COLS)
    dc = np.clip(kc - qc + NA_COLS - 1, 0, 2 * NA_COLS - 2)
    pick = jnp.asarray((dc[None, :, :] == np.arange(2 * NA_COLS - 1)[:, None, None]).astype(np.float32))
    picked = jnp.einsum('hdc,cqk->hdqk', rpb.astype(F32), pick, precision=lax.Precision.HIGHEST)
    t = jnp.where(jnp.asarray(col_ok), picked * LOG2E, NEG)
    neg = jnp.full((rpb.shape[0], w, w), NEG, F32)
    cases = []
    for r0, rs in ((0, 0), (NA_QROWS, 0), (rows - NA_QROWS, rows - NA_KROWS)):
        qtiles = []
        for i in range(NA_QROWS):
            qrow = r0 + i
            r_start = min(max(qrow - NA_ROWS // 2, 0), rows - NA_ROWS)
            blks = []
            for j in range(NA_KROWS):
                krow = rs + j
                blks.append(t[:, krow - qrow + NA_ROWS - 1] if r_start <= krow < r_start + NA_ROWS else neg)
            qtiles.append(jnp.concatenate(blks, axis=2))
        cases.append(jnp.concatenate(qtiles, axis=1))
    return jnp.stack(cases)


def _window_mask_tiles(tq):
    kw = tq + 2 * WINDOW
    r = lax.broadcasted_iota(I32, (3, tq, kw), 1)
    c = lax.broadcasted_iota(I32, (3, tq, kw), 2)
    shift = lax.broadcasted_iota(I32, (3, tq, kw), 0) * WINDOW
    return jnp.where(jnp.abs(c - r - shift) <= WINDOW, 0.0, NEG).astype(F32)


def _router_weights(router_w):
    hi = router_w.astype(BF16)
    lo = (router_w - hi.astype(F32)).astype(BF16)
    pad = jnp.zeros((router_w.shape[0], 128 - 2 * N_EXPERTS), BF16)
    return jnp.concatenate([hi, lo, pad], axis=1)


def kernel(x, c, ctx, c_ctx, ada_w, ada_b, norm_mix_g, norm_ffn_g, w_in, qk_g_win, qk_g_na, qk_g_diff, qk_g_gqa,
           sink_win, rpb_na, lambda_diff, out_gain, w_out, router_w, router_b, w_gate, w_up, w_down):
    batch, seq, d = x.shape
    ctx_len = ctx.shape[1]
    depth = w_in.shape[0]
    n_lat, n_ctx = batch * seq, batch * ctx_len
    n_all = n_lat + n_ctx
    rows = seq // GRID_W
    assert d == D_MODEL and seq % TOKEN_TILE == 0 and n_ctx % TOKEN_TILE == 0 and ctx_len % 128 == 0
    assert rows % NA_QROWS == 0 and rows >= NA_KROWS + NA_QROWS and seq >= ATT_TQ + 2 * WINDOW
    assert ctx_len % 128 == 0 and (NA_KROWS * GRID_W) % KEY_TILE == 0

    pad_rows = -(-(batch + 1) // 8) * 8
    s_all = jnp.concatenate([c, c_ctx[None, :], jnp.zeros((pad_rows - batch - 1, d), F32)], axis=0)
    mods = _ada_call(s_all, ada_w, ada_b).reshape(depth, pad_rows, ADA_CHUNKS, d)

    xs_all = (x.reshape(n_lat, d), ctx.reshape(n_ctx, d))
    tabs = _rope_tables(seq, TOKEN_TILE)
    wmask = _window_mask_tiles(ATT_TQ)
    bd64, bd32 = _block_diag_ones(HEAD_DIM), _block_diag_ones(DIFF_DIM)
    rw = _router_weights(router_w)
    rb = router_b.astype(F32).reshape(N_EXPERTS, 1)
    lo_tab = jnp.asarray([EXPERTS_PER_GROUP * (k // N_PAIRS) + PAIR_LO[k % N_PAIRS] for k in range(N_CLASSES)], I32)
    hi_tab = jnp.asarray([EXPERTS_PER_GROUP * (k // N_PAIRS) + PAIR_HI[k % N_PAIRS] for k in range(N_CLASSES)], I32)

    ffn, mod_prev = None, None
    for layer in range(depth):
        need_ctx = layer < depth - 1
        mod = mods[layer]
        qkg = _qk_gains(qk_g_win[layer], qk_g_na[layer], qk_g_diff[layer], qk_g_gqa[layer])
        xs_all, u = _inproj_call(xs_all, ffn, mod_prev, mod, norm_mix_g[layer].reshape(1, d),
                                 w_in[layer].astype(BF16), qkg, bd64, bd32, tabs, batch=batch, seq=seq)
        y = _attention_calls(u, sink_win[layer].astype(F32), lambda_diff[layer].astype(F32),
                             _na_bias_tiles(rpb_na[layer], rows), wmask, layer=layer, batch=batch, seq=seq,
                             ctx_len=ctx_len, need_ctx=need_ctx)
        n_proc = n_all if need_ctx else n_lat
        x_mid, h_ext, cls = _outproj_call(y, xs_all, mod, out_gain[layer].reshape(1, d), w_out[layer].astype(BF16),
                                          norm_ffn_g[layer].reshape(1, d), rw, rb, n_proc=n_proc, batch=batch, seq=seq)
        n_blocks = n_proc // MOE_BLK + N_CLASSES
        pos, bcls, nval = _rank_call(cls.reshape(n_proc // 128, 128), n_blocks)
        pos3 = pos.reshape(n_proc // ROW_DMA, 1, ROW_DMA)
        bcls, nval = bcls[0, :n_blocks], nval[0, :n_blocks]
        sorted_rows = _scatter_call(h_ext, pos3, n_blocks * MOE_BLK)
        ys = _moe_call(sorted_rows, lo_tab[bcls], hi_tab[bcls], nval, w_gate[layer].astype(BF16),
                       w_up[layer].astype(BF16), w_down[layer].astype(BF16))
        ffn = (ys, pos)
        xs_all, mod_prev = x_mid, mod
    out = _final_call(xs_all, ffn[0], ffn[1], mod_prev, batch=batch, seq=seq)
    return out.reshape(batch, seq, d)
```

```python
import functools
import math

import numpy as np
import jax
import jax.numpy as jnp
from jax import lax
from jax.experimental import pallas as pl
from jax.experimental.pallas import tpu as pltpu

F32 = jnp.float32
BF16 = jnp.bfloat16
I32 = jnp.int32
U32 = jnp.uint32

D_MODEL = 1024
GRID_W = 64
HEAD_DIM = 64
HEADS = 4
DIFF_DIM = 32
WINDOW = 128
NA_ROWS = 8
NA_COLS = 16
ROPE_BASE = 10000.0
N_EXPERTS = 16
EXPERTS_PER_GROUP = 4
D_EXPERT = 512
ADA_CHUNKS = 6
EPS = 1e-6
NEG = -1e30
D_IN = 2560

COL_A_Q, COL_A_KV, COL_B_Q, COL_B_K, COL_B_V, COL_C_Q, COL_C_K, COL_C_V, COL_D_Q, COL_D_KV = range(10)

TOKEN_TILE = 512
ATT_TQ = 256
ATT_TQ_FULL = 512
NA_QROWS = 4
NA_KROWS = 12
KEY_TILE = 256
ATT_RB = 128
ATT_SLOTS = 4
WINDOW_TILES = 1
NA_TILES = 4
LOG2E = math.log2(math.e)
MOE_BLK = 512
ROW_DMA = 1024
N_PAIRS = 6
N_CLASSES = (N_EXPERTS // EXPERTS_PER_GROUP) * N_PAIRS
PAIR_LO = (0, 0, 0, 1, 1, 2)
PAIR_HI = (1, 2, 3, 2, 3, 3)
EXT_W = D_MODEL // 2 + 128
VMEM_LIMIT = 56 * 1024 * 1024


def _cparams(*sem):
    return pltpu.CompilerParams(dimension_semantics=sem, vmem_limit_bytes=VMEM_LIMIT)


def _nt_dot(a, b):
    return lax.dot_general(a, b, (((1,), (1,)), ((), ())), preferred_element_type=F32)


def _dot(a, b):
    return jnp.dot(a, b, preferred_element_type=F32)


def _sigmoid(x):
    return 1.0 / (1.0 + jnp.exp(-x))


def _drop_arg(kern, pos):
    def wrapped(*refs):
        return kern(*refs[:pos], *refs[pos + 1:])
    return wrapped


def _ada_kernel(s_ref, w_ref, b_ref, o_ref):
    s = s_ref[...]
    act = s * _sigmoid(s)
    o_ref[0] = jnp.dot(act, w_ref[0], precision=lax.Precision.HIGHEST,
                       preferred_element_type=F32) + b_ref[0]


def _ada_call(s_all, ada_w, ada_b):
    depth, d, n = ada_w.shape
    rows = s_all.shape[0]
    tn = 1536
    return pl.pallas_call(
        _ada_kernel,
        out_shape=jax.ShapeDtypeStruct((depth, rows, n), F32),
        grid=(depth, n // tn),
        in_specs=[
            pl.BlockSpec((rows, d), lambda l, j: (0, 0)),
            pl.BlockSpec((1, d, tn), lambda l, j: (l, 0, j)),
            pl.BlockSpec((1, 1, tn), lambda l, j: (l, 0, j)),
        ],
        out_specs=pl.BlockSpec((1, rows, tn), lambda l, j: (l, 0, j)),
        compiler_params=_cparams("arbitrary", "arbitrary"),
        name="ada_mod",
    )(s_all, ada_w, ada_b.reshape(depth, 1, n))


def _start_row_gather(pos_ref, src_ref, dst_ref, sem):
    for r in range(dst_ref.shape[0]):
        pltpu.make_async_copy(src_ref.at[pl.ds(pos_ref[0, 0, r], 1)], dst_ref.at[pl.ds(r, 1)], sem).start()


def _wait_row_gather(src_ref, dst_ref, sem):
    pltpu.make_async_copy(src_ref.at[pl.ds(0, dst_ref.shape[0])], dst_ref, sem).wait()


def _prefetched_rows(pos_ref, pos_next_ref, src_ref, buf, sems):
    i = pl.program_id(0)
    slot = i % 2

    @pl.when(i == 0)
    def _():
        _start_row_gather(pos_ref, src_ref, buf.at[0], sems.at[0])

    @pl.when(i + 1 < pl.num_programs(0))
    def _():
        _start_row_gather(pos_next_ref, src_ref, buf.at[1 - slot], sems.at[1 - slot])

    _wait_row_gather(src_ref, buf.at[slot], sems.at[slot])
    return buf.at[slot]


def _inproj_kernel(*refs, has_ffn, tm, n_lat_tiles, tiles_per_batch, seq):
    if has_ffn:
        (x_ref, pos_ref, pos_next_ref, ys_ref, modp_ref, mod_ref, g_ref, w_ref, qkg_ref, bd64_ref, bd32_ref,
         c64_ref, s64_ref, c32_ref, s32_ref, xo_ref, u_ref, fbuf, fsems) = refs
        f_ref = _prefetched_rows(pos_ref, pos_next_ref, ys_ref, fbuf, fsems)
    else:
        (xl_ref, xc_ref, mod_ref, g_ref, w_ref, qkg_ref, bd64_ref, bd32_ref,
         c64_ref, s64_ref, c32_ref, s32_ref, u_ref) = refs
    i = pl.program_id(0)
    hm = tm // 2

    def prologue(r0):
        x = x_ref[r0:r0 + hm, :] if has_ffn else _two_source_rows(xl_ref, xc_ref, r0, hm, n_lat_tiles)
        if has_ffn:
            x = x + modp_ref[0, 5:6, :] * f_ref[r0:r0 + hm, :]
            xo_ref[r0:r0 + hm, :] = x
        ms = jnp.mean(x * x, axis=-1, keepdims=True)
        hn = x * lax.rsqrt(ms + EPS) * g_ref[...]
        return (hn * (1.0 + mod_ref[0, 1:2, :]) + mod_ref[0, 0:1, :]).astype(BF16)

    p0 = jnp.where(i < n_lat_tiles, (i % tiles_per_batch) * tm, seq)
    p0 = pl.multiple_of(p0, tm)
    lane = lax.broadcasted_iota(I32, (hm, 128), 1)

    def norm_rope(a, r0, gidx, group, bd_ref, rope_tabs):
        w = a.shape[1]
        ssq = _dot((a * a).astype(BF16), bd_ref[0:w, 0:w])
        r = lax.rsqrt(ssq * (1.0 / group) + EPS)
        outs = []
        for s in range(w // 128):
            t = a[:, 128 * s:128 * s + 128] * r[:, 128 * s:128 * s + 128] * qkg_ref[gidx:gidx + 1, :]
            if rope_tabs is not None:
                c_ref, s_ref, half = rope_tabs
                fwd = pltpu.roll(t, 128 - half, axis=1)
                bwd = pltpu.roll(t, half, axis=1)
                sw = jnp.where((lane % (2 * half)) < half, fwd, bwd)
                t = t * c_ref[pl.ds(p0 + r0, hm), :] + sw * s_ref[pl.ds(p0 + r0, hm), :]
            outs.append(t)
        return outs[0] if len(outs) == 1 else jnp.concatenate(outs, axis=1)

    rope64 = (c64_ref, s64_ref, HEAD_DIM // 4)
    rope32 = (c32_ref, s32_ref, DIFF_DIM // 4)

    def epilogue(acc, r0, blk):
        if blk == COL_A_Q:
            out = norm_rope(acc, r0, 0, HEAD_DIM, bd64_ref, rope64)
        elif blk == COL_A_KV:
            out = jnp.concatenate([norm_rope(acc[:, :128], r0, 1, HEAD_DIM, bd64_ref, rope64), acc[:, 128:]], axis=1)
        elif blk == COL_B_Q:
            out = norm_rope(acc, r0, 2, HEAD_DIM, bd64_ref, None)
        elif blk == COL_B_K:
            out = norm_rope(acc, r0, 3, HEAD_DIM, bd64_ref, None)
        elif blk == COL_C_Q:
            out = norm_rope(acc, r0, 4, DIFF_DIM, bd32_ref, rope32)
        elif blk == COL_C_K:
            out = norm_rope(acc, r0, 5, DIFF_DIM, bd32_ref, rope32)
        elif blk == COL_D_Q:
            out = norm_rope(acc, r0, 6, HEAD_DIM, bd64_ref, rope64)
        elif blk == COL_D_KV:
            out = jnp.concatenate([norm_rope(acc[:, :128], r0, 7, HEAD_DIM, bd64_ref, rope64), acc[:, 128:]], axis=1)
        else:
            out = acc
        u_ref[r0:r0 + hm, 256 * blk:256 * blk + 256] = out.astype(BF16)

    n_blk = D_IN // 256
    units = [(r0, blk) for r0 in (0, hm) for blk in range(n_blk)]
    hbs = {0: prologue(0)}
    acc = _dot(hbs[0], w_ref[:, 0:256])
    hbs[hm] = prologue(hm)
    for t, (r0, blk) in enumerate(units):
        nxt = None
        if t + 1 < len(units):
            r1, b1 = units[t + 1]
            nxt = _dot(hbs[r1], w_ref[:, 256 * b1:256 * b1 + 256])
        epilogue(acc, r0, blk)
        acc = nxt


def _two_source_specs(tm, n_lat_tiles):
    return [pl.BlockSpec((tm, D_MODEL), lambda i: (jnp.minimum(i, n_lat_tiles - 1), 0)),
            pl.BlockSpec((tm, D_MODEL), lambda i: (jnp.maximum(i - n_lat_tiles, 0), 0))]


def _two_source_rows(xl_ref, xc_ref, r0, n, n_lat_tiles):
    return jnp.where(pl.program_id(0) < n_lat_tiles, xl_ref[r0:r0 + n, :], xc_ref[r0:r0 + n, :])


def _mod_spec(n_lat_tiles, tiles_per_batch, batch):
    return pl.BlockSpec((1, ADA_CHUNKS, D_MODEL),
                        lambda i: (jnp.where(i < n_lat_tiles, i // tiles_per_batch, batch), 0, 0))


def _const_spec(shape):
    return pl.BlockSpec(shape, lambda i: (0,) * len(shape))


def _pos_tile_specs(n_tiles, tm):
    return [pl.BlockSpec((1, 1, tm), lambda i: (i, 0, 0), memory_space=pltpu.SMEM),
            pl.BlockSpec((1, 1, tm), lambda i: (jnp.minimum(i + 1, n_tiles - 1), 0, 0), memory_space=pltpu.SMEM)]


def _inproj_call(x, ffn_src, mod_prev, mod, g, w_in, qkg, bd64, bd32, tabs, *, batch, seq):
    tm = TOKEN_TILE
    n_lat_tiles = batch * seq // tm
    tpb = seq // tm
    has_ffn = ffn_src is not None
    tile = pl.BlockSpec((tm, D_MODEL), lambda i: (i, 0))
    modspec = _mod_spec(n_lat_tiles, tpb, batch)
    tab_rows = tabs[0].shape[0]
    if has_ffn:
        n_all = x.shape[0]
        in_specs, args = [tile], [x]
    else:
        n_all = x[0].shape[0] + x[1].shape[0]
        in_specs, args = _two_source_specs(tm, n_lat_tiles), list(x)
    scratch = []
    if has_ffn:
        ys, pos = ffn_src
        pos_tiles = pos.reshape(n_all // tm, 1, tm)
        in_specs += _pos_tile_specs(n_all // tm, tm) + [pl.BlockSpec(memory_space=pl.ANY), modspec]
        args += [pos_tiles, pos_tiles, ys, mod_prev]
        scratch = [pltpu.VMEM((2, tm, D_MODEL), F32), pltpu.SemaphoreType.DMA((2,))]
    in_specs += [modspec, _const_spec((1, D_MODEL)), _const_spec((D_MODEL, D_IN)), _const_spec((8, 128)),
                 _const_spec((256, 256)), _const_spec((256, 256))] + [_const_spec((tab_rows, 128))] * 4
    args += [mod, g, w_in, qkg, bd64, bd32] + list(tabs)
    u_shape = jax.ShapeDtypeStruct((n_all, D_IN), BF16)
    u_spec = pl.BlockSpec((tm, D_IN), lambda i: (i, 0))
    if has_ffn:
        out_shape = (jax.ShapeDtypeStruct((n_all, D_MODEL), F32), u_shape)
        out_specs = (tile, u_spec)
    else:
        out_shape, out_specs = u_shape, u_spec
    kern = functools.partial(_inproj_kernel, has_ffn=has_ffn, tm=tm, n_lat_tiles=n_lat_tiles,
                             tiles_per_batch=tpb, seq=seq)
    res = pl.pallas_call(
        kern, out_shape=out_shape, grid=(n_all // tm,), in_specs=in_specs, out_specs=out_specs,
        scratch_shapes=scratch, compiler_params=_cparams("arbitrary"), name="inproj",
    )(*args)
    return res if has_ffn else (x, res)


class _AttItem:
    def __init__(self, q, kt_view, v_view, ranges, bias=None, extra=None):
        self.q, self.kt_view, self.v_view, self.ranges, self.bias, self.extra = q, kt_view, v_view, ranges, bias, extra
        self.n_keys = sum(n for _, n in ranges)
        self.m = None

    def scores(self, s_scr, rows):
        q_blk = self.q()
        m, off, ti = None, 0, 0
        for start, n in self.ranges:
            for o in range(0, n, KEY_TILE):
                w = min(KEY_TILE, n - o)
                s = _dot(q_blk, self.kt_view[:, pl.ds(start + o, w)])
                b = self.bias(ti) if self.bias is not None else None
                if b is not None:
                    s = s + b
                ti += 1
                s_scr[rows, off + o:off + o + w] = s
                tile_max = s.max(axis=-1, keepdims=True)
                m = tile_max if m is None else jnp.maximum(m, tile_max)
            off += n
        self.m = m if self.extra is None else jnp.maximum(m, self.extra)

    def probs(self, s_scr, p_scr, rows):
        for o in range(0, self.n_keys, KEY_TILE):
            w = min(KEY_TILE, self.n_keys - o)
            p_scr[rows, o:o + w] = jnp.exp2(s_scr[rows, o:o + w] - self.m).astype(BF16)

    def values(self, p_scr, rows):
        pv, off = None, 0
        for start, n in self.ranges:
            c = _dot(p_scr[rows, off:off + n], self.v_view[pl.ds(start, n), :])
            pv = c if pv is None else pv + c
            off += n
        denom = pv[:, 64:65]
        if self.extra is not None:
            denom = denom + jnp.exp2(self.extra - self.m)
        return pv[:, 0:64] / denom


def _run_att_items(items, s_scr, p_scr):
    def rows(t):
        r0 = (t % ATT_SLOTS) * ATT_RB
        return slice(r0, r0 + ATT_RB)

    outs = []
    for t in range(len(items) + 2):
        if t < len(items):
            items[t].scores(s_scr, rows(t))
        if 0 <= t - 1 < len(items):
            items[t - 1].probs(s_scr, p_scr, rows(t - 1))
        if 0 <= t - 2 < len(items):
            outs.append(items[t - 2].values(p_scr, rows(t - 2)))
    return outs


def _stage_heads(k_parts, v_parts, kt_scr, v_scr):
    eye = jnp.where(lax.broadcasted_iota(I32, (HEAD_DIM, HEAD_DIM), 0)
                    == lax.broadcasted_iota(I32, (HEAD_DIM, HEAD_DIM), 1), 1.0, 0.0).astype(BF16)
    for h in range(kt_scr.shape[0]):
        r0 = 0
        for (kr, kc0), (vr, vc0) in zip(k_parts, v_parts):
            n = kr.shape[0]
            kt_scr[h, :, r0:r0 + n] = _nt_dot(eye, kr[:, kc0 + 64 * h:kc0 + 64 * h + 64]).astype(BF16)
            v_scr[h, r0:r0 + n, 0:64] = vr[:, vc0 + 64 * h:vc0 + 64 * h + 64]
            r0 += n
        v_scr[h, :, 64:128] = jnp.ones((v_scr.shape[1], 64), BF16)


def _group_rms(y):
    return y * lax.rsqrt(jnp.mean(y * y, axis=-1, keepdims=True) + EPS)


def _stage_once_per_batch_row(k_parts, v_parts, k_scr, v_scr):
    @pl.when(pl.program_id(1) == 0)
    def _():
        _stage_heads(k_parts, v_parts, k_scr, v_scr)


def _row_blocks(n):
    return range(0, n, ATT_RB)


def _q_thunk(q_ref, r0, head):
    return lambda: q_ref[r0:r0 + ATT_RB, 64 * head:64 * head + 64]


def _heads_to_rows(outs, n_heads):
    per = len(outs) // n_heads
    return jnp.concatenate([jnp.concatenate(outs[h * per:(h + 1) * per], axis=0) for h in range(n_heads)], axis=1)


def _dense_kernel(q_ref, kv_ref, kvc_ref, o_ref, kt_scr, v_scr, s_scr, p_scr, *, tq):
    _stage_once_per_batch_row([(kv_ref, 0), (kvc_ref, 0)], [(kv_ref, 128), (kvc_ref, 128)], kt_scr, v_scr)
    nk = v_scr.shape[1]
    items = [_AttItem(_q_thunk(q_ref, r0, hq), kt_scr.at[hq // 2], v_scr.at[hq // 2], [(0, nk)])
             for hq in range(HEADS) for r0 in _row_blocks(tq)]
    y = _heads_to_rows(_run_att_items(items, s_scr, p_scr), HEADS)
    o_ref[...] = _group_rms(y).astype(BF16)


def _window_kernel(sink_ref, q_ref, kv_ref, kvc_ref, *refs, tq, seq, tiles):
    mask_refs, (o_ref, kt_scr, v_scr, s_scr, p_scr) = refs[:tiles], refs[tiles:]
    _stage_once_per_batch_row([(kv_ref, 0), (kvc_ref, 0)], [(kv_ref, 128), (kvc_ref, 128)], kt_scr, v_scr)
    kw = tq + 2 * WINDOW
    nc = v_scr.shape[1] - seq

    def mask_thunk(mask_ref, r0):
        return lambda ti: (mask_ref[0, r0:r0 + ATT_RB, KEY_TILE * ti:KEY_TILE * ti + KEY_TILE]
                           if KEY_TILE * ti < kw else None)

    items = []
    for t, mask_ref in enumerate(mask_refs):
        q0 = (tiles * pl.program_id(1) + t) * tq
        ks = pl.multiple_of(jnp.clip(q0 - WINDOW, 0, seq - kw), 128)
        items += [_AttItem(_q_thunk(q_ref, t * tq + r0, hq), kt_scr.at[hq // 2], v_scr.at[hq // 2],
                           [(ks, kw), (seq, nc)], bias=mask_thunk(mask_ref, r0), extra=sink_ref[hq] * LOG2E)
                  for hq in range(HEADS) for r0 in _row_blocks(tq)]
    outs = _run_att_items(items, s_scr, p_scr)
    per = len(outs) // tiles
    y = jnp.concatenate([_heads_to_rows(outs[t * per:(t + 1) * per], HEADS) for t in range(tiles)], axis=0)
    o_ref[...] = _group_rms(y).astype(BF16)


def _neigh_kernel(q_ref, k_ref, v_ref, kc_ref, vc_ref, *refs, rows, seq, tiles):
    bias_refs, (o_ref, kt_scr, v_scr, s_scr, p_scr) = refs[:tiles], refs[tiles:]
    _stage_once_per_batch_row([(k_ref, 0), (kc_ref, 0)], [(v_ref, 0), (vc_ref, 0)], kt_scr, v_scr)
    nk = NA_KROWS * GRID_W
    nc = v_scr.shape[1] - seq
    qrows = NA_QROWS * GRID_W

    def bias_thunk(bias_ref, h, r0):
        return lambda ti: (bias_ref[0, h, r0:r0 + ATT_RB, KEY_TILE * ti:KEY_TILE * ti + KEY_TILE]
                           if KEY_TILE * ti < nk else None)

    items = []
    for t, bias_ref in enumerate(bias_refs):
        rb = tiles * pl.program_id(1) + t
        rs = jnp.clip(NA_QROWS * rb - NA_ROWS // 2, 0, rows - NA_KROWS)
        k0 = pl.multiple_of(rs * GRID_W, KEY_TILE)
        items += [_AttItem(_q_thunk(q_ref, t * qrows + r0, h), kt_scr.at[h], v_scr.at[h], [(k0, nk), (seq, nc)],
                           bias=bias_thunk(bias_ref, h, r0))
                  for h in range(HEADS) for r0 in _row_blocks(qrows)]
    outs = _run_att_items(items, s_scr, p_scr)
    per = len(outs) // tiles
    y = jnp.concatenate([_heads_to_rows(outs[t * per:(t + 1) * per], HEADS) for t in range(tiles)], axis=0)
    o_ref[...] = _group_rms(y).astype(BF16)


def _diff_lambda(lam_ref, lam_init):
    lp = lam_ref[...]
    a = jnp.sum(lp[0:1] * lp[1:2], axis=-1, keepdims=True)
    b = jnp.sum(lp[2:3] * lp[3:4], axis=-1, keepdims=True)
    return jnp.exp(a) - jnp.exp(b) + lam_init


def _diff_kernel(lam_ref, q_ref, k_ref, v_ref, kc_ref, vc_ref, o_ref, kt_scr, v_scr, s_scr, p_scr, *, lam_init, tq):
    _stage_once_per_batch_row([(k_ref, 0), (kc_ref, 0)], [(v_ref, 0), (vc_ref, 0)], kt_scr, v_scr)
    lam = _diff_lambda(lam_ref, lam_init)
    nk = v_scr.shape[1]
    first = lax.broadcasted_iota(I32, (ATT_RB, HEAD_DIM), 1) < DIFF_DIM
    zero = jnp.zeros((ATT_RB, HEAD_DIM), BF16)

    def q_map(r0, h, half):
        def thunk():
            qh = q_ref[r0:r0 + ATT_RB, 64 * h:64 * h + 64]
            return jnp.where(first, qh, zero) if half == 0 else jnp.where(first, zero, qh)
        return thunk

    items = [_AttItem(q_map(r0, h, half), kt_scr.at[h], v_scr.at[h], [(0, nk)])
             for h in range(HEADS) for r0 in _row_blocks(tq) for half in range(2)]
    outs = _run_att_items(items, s_scr, p_scr)
    diffs = [outs[i] - lam * outs[i + 1] for i in range(0, len(outs), 2)]
    per = len(diffs) // HEADS
    heads = [_group_rms(jnp.concatenate(diffs[h * per:(h + 1) * per], axis=0)) * (1.0 - lam_init)
             for h in range(HEADS)]
    o_ref[...] = jnp.concatenate(heads, axis=1).astype(BF16)


def _softmax_parts(scores, extra=None):
    m = scores[0].max(axis=-1, keepdims=True)
    for s in scores[1:]:
        m = jnp.maximum(m, s.max(axis=-1, keepdims=True))
    if extra is not None:
        m = jnp.maximum(m, extra)
    ps = [jnp.exp2(s - m) for s in scores]
    l = ps[0].sum(axis=-1, keepdims=True)
    for p in ps[1:]:
        l = l + p.sum(axis=-1, keepdims=True)
    if extra is not None:
        l = l + jnp.exp2(extra - m)
    return ps, 1.0 / l


def _ctx_kernel(sink_ref, lam_ref, u_ref, o_ref, *, lam_init):
    L = u_ref.shape[0]
    top = lax.broadcasted_iota(I32, (2 * L, 1), 0) < L

    def cols(blk, lo, width=64):
        return u_ref[:, 256 * blk + lo:256 * blk + lo + width]

    for q_blk, kv_blk, out_col, use_sink in ((COL_A_Q, COL_A_KV, 0, True), (COL_D_Q, COL_D_KV, 768, False)):
        outs = [None] * HEADS
        for j in range(2):
            q2 = jnp.concatenate([cols(q_blk, 128 * j), cols(q_blk, 128 * j + 64)], axis=0)
            snk = jnp.where(top, sink_ref[2 * j], sink_ref[2 * j + 1]) * LOG2E if use_sink else None
            (p,), inv = _softmax_parts([_nt_dot(q2, cols(kv_blk, 64 * j))], extra=snk)
            o = _dot(p.astype(BF16), cols(kv_blk, 128 + 64 * j)) * inv
            outs[2 * j], outs[2 * j + 1] = o[:L], o[L:]
        o_ref[:, out_col:out_col + 256] = _group_rms(jnp.concatenate(outs, axis=1)).astype(BF16)
    outs = []
    for h in range(HEADS):
        (p,), inv = _softmax_parts([_nt_dot(cols(COL_B_Q, 64 * h), cols(COL_B_K, 64 * h))])
        outs.append(_dot(p.astype(BF16), cols(COL_B_V, 64 * h)) * inv)
    o_ref[:, 256:512] = _group_rms(jnp.concatenate(outs, axis=1)).astype(BF16)
    lam = _diff_lambda(lam_ref, lam_init)
    first = lax.broadcasted_iota(I32, (L, HEAD_DIM), 1) < DIFF_DIM
    zero = jnp.zeros((L, HEAD_DIM), BF16)
    outs = []
    for h in range(HEADS):
        qh, kh = cols(COL_C_Q, 64 * h), cols(COL_C_K, 64 * h)
        maps = []
        for half in range(2):
            qm = jnp.where(first, qh, zero) if half == 0 else jnp.where(first, zero, qh)
            (p,), inv = _softmax_parts([_nt_dot(qm, kh)])
            maps.append(p * inv)
        o = _dot((maps[0] - lam * maps[1]).astype(BF16), cols(COL_C_V, 64 * h))
        outs.append(_group_rms(o) * (1.0 - lam_init))
    o_ref[:, 512:768] = jnp.concatenate(outs, axis=1).astype(BF16)


def _attention_calls(u, sink, lam_p, bias, wmask, *, layer, batch, seq, ctx_len, need_ctx):
    lam_init = 0.8 - 0.6 * math.exp(-0.3 * layer)
    n_all = u.shape[0]
    tq = ATT_TQ
    nq = seq // tq
    nk_all = seq + ctx_len
    cb0 = batch * seq // ctx_len
    y_shape = jax.ShapeDtypeStruct((n_all, D_MODEL), BF16)
    smem = pl.BlockSpec(memory_space=pltpu.SMEM)
    any_spec = pl.BlockSpec(memory_space=pl.ANY)
    lam_spec2 = pl.BlockSpec((4, DIFF_DIM), lambda b, i: (0, 0))
    cp = _cparams("arbitrary", "arbitrary")

    def scratch(n_heads, n_keys):
        return [pltpu.VMEM((n_heads, HEAD_DIM, nk_all), BF16), pltpu.VMEM((n_heads, nk_all, 128), BF16),
                pltpu.VMEM((ATT_SLOTS * ATT_RB, n_keys), F32), pltpu.VMEM((ATT_SLOTS * ATT_RB, n_keys), BF16)]

    def qspec(col, t=tq):
        return pl.BlockSpec((t, 256), lambda b, i: (b * (seq // t) + i, col))

    tqf = ATT_TQ_FULL
    nqf = seq // tqf

    def latspec(col):
        return pl.BlockSpec((seq, 256), lambda b, i: (b, col))

    def ctxspec(col):
        return pl.BlockSpec((ctx_len, 256), lambda b, i: (cb0 + b, col))

    kw = tq + 2 * WINDOW

    def case_specs(shape, n_tiles, per_step):
        def spec(t):
            def index(b, i):
                j = per_step * i + t
                return (jnp.where(j == 0, 0, jnp.where(j == n_tiles - 1, 2, 1)),) + (0,) * (len(shape) - 1)
            return pl.BlockSpec(shape, index)
        return [spec(t) for t in range(per_step)]

    wt = WINDOW_TILES
    y = pl.pallas_call(
        functools.partial(_window_kernel, tq=tq, seq=seq, tiles=wt),
        out_shape=y_shape, grid=(batch, nq // wt),
        in_specs=[smem, qspec(COL_A_Q, wt * tq), latspec(COL_A_KV), ctxspec(COL_A_KV)]
        + case_specs((1, tq, kw), nq, wt),
        out_specs=qspec(0, wt * tq), scratch_shapes=scratch(2, kw + ctx_len),
        compiler_params=cp, name="mix_window",
    )(sink, u, u, u, *([wmask] * wt))

    rows = seq // GRID_W
    nrb = rows // NA_QROWS
    qrows = NA_QROWS * GRID_W
    nkb = NA_KROWS * GRID_W
    nt = NA_TILES
    y = pl.pallas_call(
        _drop_arg(functools.partial(_neigh_kernel, rows=rows, seq=seq, tiles=nt), 5 + nt),
        out_shape=y_shape, grid=(batch, nrb // nt),
        in_specs=[qspec(COL_B_Q, nt * qrows), latspec(COL_B_K), latspec(COL_B_V),
                  ctxspec(COL_B_K), ctxspec(COL_B_V)] + case_specs((1, HEADS, qrows, nkb), nrb, nt) + [any_spec],
        out_specs=qspec(1, nt * qrows), scratch_shapes=scratch(HEADS, nkb + ctx_len),
        input_output_aliases={5 + nt: 0}, compiler_params=cp, name="mix_neigh",
    )(u, u, u, u, u, *([bias] * nt), y)

    y = pl.pallas_call(
        _drop_arg(functools.partial(_diff_kernel, lam_init=lam_init, tq=tqf), 6),
        out_shape=y_shape, grid=(batch, nqf),
        in_specs=[lam_spec2, qspec(COL_C_Q, tqf), latspec(COL_C_K), latspec(COL_C_V),
                  ctxspec(COL_C_K), ctxspec(COL_C_V), any_spec],
        out_specs=qspec(2, tqf), scratch_shapes=scratch(HEADS, nk_all),
        input_output_aliases={6: 0}, compiler_params=cp, name="mix_diff",
    )(lam_p, u, u, u, u, u, y)

    y = pl.pallas_call(
        _drop_arg(functools.partial(_dense_kernel, tq=tqf), 3),
        out_shape=y_shape, grid=(batch, nqf),
        in_specs=[qspec(COL_D_Q, tqf), latspec(COL_D_KV), ctxspec(COL_D_KV), any_spec],
        out_specs=qspec(3, tqf), scratch_shapes=scratch(2, nk_all),
        input_output_aliases={3: 0}, compiler_params=cp, name="mix_dense",
    )(u, u, u, y)

    if need_ctx:
        y = pl.pallas_call(
            _drop_arg(functools.partial(_ctx_kernel, lam_init=lam_init), 3),
            out_shape=y_shape, grid=(batch,),
            in_specs=[smem, pl.BlockSpec((4, DIFF_DIM), lambda b: (0, 0)),
                      pl.BlockSpec((ctx_len, D_IN), lambda b: (cb0 + b, 0)), any_spec],
            out_specs=pl.BlockSpec((ctx_len, D_MODEL), lambda b: (cb0 + b, 0)),
            input_output_aliases={3: 0}, compiler_params=_cparams("arbitrary"), name="mix_ctx",
        )(sink, lam_p, u, y)
    return y


def _select4(idx, vals):
    return jnp.where(idx == 0, vals[0], jnp.where(idx == 1, vals[1], jnp.where(idx == 2, vals[2], vals[3])))


def _outproj_kernel(y_ref, *refs, tm, n_lat_tiles, two_source):
    n_x = 2 if two_source else 1
    x_refs, (mod_ref, gain_ref, w_ref, g_ref, rw_ref, rb_ref, xo_ref, h_ref, cls_ref) = refs[:n_x], refs[n_x:]
    hm = tm // 2

    def project(r0):
        yg = (y_ref[r0:r0 + hm, :].astype(F32) * gain_ref[...]).astype(BF16)
        return _dot(yg, w_ref[...])

    def normalise(r0, proj):
        x = (_two_source_rows(*x_refs, r0, hm, n_lat_tiles) if two_source else x_refs[0][r0:r0 + hm, :])
        xm = x + mod_ref[0, 2:3, :] * proj
        xo_ref[r0:r0 + hm, :] = xm
        hn = xm * lax.rsqrt(jnp.mean(xm * xm, axis=-1, keepdims=True) + EPS) * g_ref[...]
        h2 = hn * (1.0 + mod_ref[0, 4:5, :]) + mod_ref[0, 3:4, :]
        hi = h2.astype(BF16)
        hi32 = hi.astype(F32)
        lo = (h2 - hi32).astype(BF16)
        bits = lax.bitcast_convert_type(hi32, U32)
        half = D_MODEL // 2
        h_ref[r0:r0 + hm, 0:half] = (bits[:, 0:half] >> 16) | (bits[:, half:D_MODEL] & jnp.uint32(0xFFFF0000))
        return (_dot(hi, rw_ref[...]) + _dot(lo, rw_ref[...])).T

    def route(r0, lt):
        score = _sigmoid(lt[0:N_EXPERTS] + lt[N_EXPERTS:2 * N_EXPERTS])
        sel = score + rb_ref[...]
        srow = [sel[e:e + 1] for e in range(N_EXPERTS)]
        crow = [score[e:e + 1] for e in range(N_EXPERTS)]

        gidx = jnp.zeros((1, hm), I32)
        gbest = None
        for g in range(N_EXPERTS // EXPERTS_PER_GROUP):
            v = srow[4 * g:4 * g + 4]
            best = None
            for a in range(4):
                for b in range(a + 1, 4):
                    t = v[a] + v[b]
                    best = t if best is None else jnp.maximum(best, t)
            if g == 0:
                gbest = best
            else:
                better = best > gbest
                gidx = jnp.where(better, g, gidx)
                gbest = jnp.where(better, best, gbest)

        iv = [_select4(gidx, [srow[4 * g + i] for g in range(4)]) for i in range(4)]
        sv = [_select4(gidx, [crow[4 * g + i] for g in range(4)]) for i in range(4)]
        chosen = []
        for i in range(4):
            rank = jnp.zeros((1, hm), I32)
            for j in range(4):
                if j != i:
                    beats = (iv[j] >= iv[i]) if j < i else (iv[j] > iv[i])
                    rank = rank + jnp.where(beats, 1, 0)
            chosen.append(rank < 2)
        lo_i = jnp.where(chosen[0], 0, jnp.where(chosen[1], 1, 2))
        hi_i = jnp.where(chosen[3], 3, jnp.where(chosen[2], 2, 1))
        pair = jnp.where(lo_i == 0, hi_i - 1, jnp.where(lo_i == 1, hi_i + 1, 5))
        cls_ref[0, :, r0:r0 + hm] = gidx * N_PAIRS + pair
        s_lo = jnp.where(lo_i == 0, sv[0], jnp.where(lo_i == 1, sv[1], sv[2]))
        s_hi = jnp.where(hi_i == 3, sv[3], jnp.where(hi_i == 2, sv[2], sv[1]))
        inv = 1.0 / (s_lo + s_hi)
        rowi = lax.broadcasted_iota(I32, (128, hm), 0)
        wts = jnp.where(rowi == 0, s_lo * inv, jnp.where(rowi == 1, s_hi * inv, 0.0))
        h_ref[r0:r0 + hm, D_MODEL // 2:EXT_W] = lax.bitcast_convert_type(wts.T, U32)

    proj0 = project(0)
    proj1 = project(hm)
    lt0 = normalise(0, proj0)
    lt1 = normalise(hm, proj1)
    route(0, lt0)
    route(hm, lt1)


def _outproj_call(y, x, mod, gain, w_out, g, rw, rb, *, n_proc, batch, seq):
    tm = TOKEN_TILE
    n_tiles = n_proc // tm
    n_lat_tiles = batch * seq // tm
    tile = pl.BlockSpec((tm, D_MODEL), lambda i: (i, 0))
    two_source = isinstance(x, tuple)
    x_specs, x_args = (_two_source_specs(tm, n_lat_tiles), list(x)) if two_source else ([tile], [x])
    return pl.pallas_call(
        functools.partial(_outproj_kernel, tm=tm, n_lat_tiles=n_lat_tiles, two_source=two_source),
        out_shape=(jax.ShapeDtypeStruct((n_proc, D_MODEL), F32),
                   jax.ShapeDtypeStruct((n_proc, EXT_W), U32),
                   jax.ShapeDtypeStruct((n_tiles, 1, tm), I32)),
        grid=(n_tiles,),
        in_specs=[tile] + x_specs + [_mod_spec(n_lat_tiles, seq // tm, batch), _const_spec((1, D_MODEL)),
                  _const_spec((D_MODEL, D_MODEL)), _const_spec((1, D_MODEL)), _const_spec((D_MODEL, 128)),
                  _const_spec((N_EXPERTS, 1))],
        out_specs=(tile, pl.BlockSpec((tm, EXT_W), lambda i: (i, 0)), pl.BlockSpec((1, 1, tm), lambda i: (i, 0, 0))),
        compiler_params=_cparams("arbitrary"), name="outproj_router",
    )(y, *x_args, mod, gain, w_out, g, rw, rb)


def _rank_kernel(cls_ref, pos_ref, bcls_ref, nval_ref, *, blk):
    c = cls_ref[...]
    nr = c.shape[0]
    nbp = bcls_ref.shape[1]
    upper = (lax.broadcasted_iota(I32, (128, 128), 0) <= lax.broadcasted_iota(I32, (128, 128), 1))
    upper = jnp.where(upper, 1.0, 0.0).astype(BF16)
    lower = (lax.broadcasted_iota(I32, (nr, nr), 1) < lax.broadcasted_iota(I32, (nr, nr), 0))
    lower = jnp.where(lower, 1.0, 0.0).astype(BF16)
    bidx = lax.broadcasted_iota(I32, (1, nbp), 1).astype(F32)
    pos = jnp.zeros((nr, 128), F32)
    bcls = jnp.zeros((1, nbp), F32)
    nval = jnp.zeros((1, nbp), F32)
    start = jnp.zeros((1, 1), F32)
    for k in range(N_CLASSES):
        m = c == k
        incl = _dot(jnp.where(m, 1.0, 0.0).astype(BF16), upper)
        tot = incl[:, 127:128]
        above = _dot(lower, jnp.broadcast_to(tot, (nr, 128)).astype(BF16))
        cnt = above[nr - 1:nr, 0:1] + tot[nr - 1:nr, 0:1]
        nblk = jnp.floor((cnt + (blk - 1)) * (1.0 / blk))
        pos = jnp.where(m, start * blk + above + incl - 1.0, pos)
        end = start + nblk
        bcls = bcls + jnp.where(bidx >= end, 1.0, 0.0)
        nval = nval + jnp.where((bidx >= start) & (bidx < end), jnp.clip(cnt - (bidx - start) * blk, 0.0, blk), 0.0)
        start = end
    pos_ref[...] = pos.astype(I32)
    bcls_ref[...] = jnp.minimum(bcls, N_CLASSES - 1.0).astype(I32)
    nval_ref[...] = nval.astype(I32)


def _rank_call(cls2d, n_blocks):
    nr = cls2d.shape[0]
    nbp = -(-n_blocks // 128) * 128
    return pl.pallas_call(
        functools.partial(_rank_kernel, blk=MOE_BLK),
        out_shape=(jax.ShapeDtypeStruct((nr, 128), I32), jax.ShapeDtypeStruct((1, nbp), I32),
                   jax.ShapeDtypeStruct((1, nbp), I32)),
        grid=(1,),
        in_specs=[_const_spec((nr, 128))],
        out_specs=(_const_spec((nr, 128)), _const_spec((1, nbp)), _const_spec((1, nbp))),
        compiler_params=_cparams("arbitrary"), name="class_rank",
    )(cls2d)


def _scatter_kernel(pos_ref, h_ref, xs_ref, sem):
    n = h_ref.shape[0]
    for r in range(n):
        pltpu.make_async_copy(h_ref.at[pl.ds(r, 1)], xs_ref.at[pl.ds(pos_ref[0, 0, r], 1)], sem).start()
    pltpu.make_async_copy(h_ref, xs_ref.at[pl.ds(0, n)], sem).wait()


def _scatter_call(h_ext, pos3, n_slots):
    n, w = h_ext.shape
    r = ROW_DMA
    return pl.pallas_call(
        _scatter_kernel,
        out_shape=jax.ShapeDtypeStruct((n_slots, w), h_ext.dtype),
        grid=(n // r,),
        in_specs=[pl.BlockSpec((1, 1, r), lambda i: (i, 0, 0), memory_space=pltpu.SMEM),
                  pl.BlockSpec((r, w), lambda i: (i, 0))],
        out_specs=pl.BlockSpec(memory_space=pl.ANY),
        scratch_shapes=[pltpu.SemaphoreType.DMA(())],
        compiler_params=_cparams("arbitrary"), name="row_scatter",
    )(pos3, h_ext)


def _moe_kernel(elo_ref, ehi_ref, nval_ref, xs_ref, wg_lo, wu_lo, wd_lo, wg_hi, wu_hi, wd_hi, ys_ref):
    del elo_ref, ehi_ref
    nv = nval_ref[pl.program_id(0)]

    @pl.when(nv > 0)
    def _():
        live = lax.broadcasted_iota(I32, (MOE_BLK, 1), 0) < nv
        half = D_MODEL // 2
        words = jnp.where(live, xs_ref[:, 0:half], jnp.uint32(0))
        x = jnp.concatenate([lax.bitcast_convert_type(words << 16, F32).astype(BF16),
                             lax.bitcast_convert_type(words & jnp.uint32(0xFFFF0000), F32).astype(BF16)], axis=1)
        acts = []
        for wg, wu, col in ((wg_lo, wu_lo, half), (wg_hi, wu_hi, half + 1)):
            w = jnp.where(live, lax.bitcast_convert_type(xs_ref[:, col:col + 1], F32), 0.0)
            g = _dot(x, wg[0])
            acts.append((g * _sigmoid(g) * _dot(x, wu[0]) * w).astype(BF16))
        ys_ref[...] = _dot(acts[0], wd_lo[0]) + _dot(acts[1], wd_hi[0])

    @pl.when(nv == 0)
    def _():
        ys_ref[...] = jnp.zeros_like(ys_ref)


def _moe_call(xs, e_lo, e_hi, nval, w_gate, w_up, w_down):
    n_blocks = xs.shape[0] // MOE_BLK

    def wspec(shape, which):
        return pl.BlockSpec((1,) + shape, lambda b, elo, ehi, nv: ((elo, ehi)[which][b], 0, 0))

    gu, dn = (D_MODEL, D_EXPERT), (D_EXPERT, D_MODEL)
    grid_spec = pltpu.PrefetchScalarGridSpec(
        num_scalar_prefetch=3, grid=(n_blocks,),
        in_specs=[pl.BlockSpec((MOE_BLK, EXT_W), lambda b, elo, ehi, nv: (b, 0)),
                  wspec(gu, 0), wspec(gu, 0), wspec(dn, 0), wspec(gu, 1), wspec(gu, 1), wspec(dn, 1)],
        out_specs=pl.BlockSpec((MOE_BLK, D_MODEL), lambda b, elo, ehi, nv: (b, 0)),
    )
    return pl.pallas_call(
        _moe_kernel, out_shape=jax.ShapeDtypeStruct((xs.shape[0], D_MODEL), F32), grid_spec=grid_spec,
        compiler_params=_cparams("arbitrary"), name="moe_experts",
    )(e_lo, e_hi, nval, xs, w_gate, w_up, w_down, w_gate, w_up, w_down)


def _final_kernel(x_ref, pos_ref, pos_next_ref, ys_ref, mod_ref, o_ref, fbuf, fsems):
    f_ref = _prefetched_rows(pos_ref, pos_next_ref, ys_ref, fbuf, fsems)
    o_ref[...] = x_ref[...] + mod_ref[0, 5:6, :] * f_ref[...]


def _final_call(x, ys, pos, mod, *, batch, seq):
    tm = TOKEN_TILE
    n = batch * seq
    tile = pl.BlockSpec((tm, D_MODEL), lambda i: (i, 0))
    pos_tiles = pos.reshape(n // tm, 1, tm)
    return pl.pallas_call(
        _final_kernel, out_shape=jax.ShapeDtypeStruct((n, D_MODEL), F32), grid=(n // tm,),
        in_specs=[tile] + _pos_tile_specs(n // tm, tm) + [pl.BlockSpec(memory_space=pl.ANY),
                                                          _mod_spec(n // tm, seq // tm, batch)],
        out_specs=tile, scratch_shapes=[pltpu.VMEM((2, tm, D_MODEL), F32), pltpu.SemaphoreType.DMA((2,))],
        compiler_params=_cparams("arbitrary"), name="final_residual",
    )(x, pos_tiles, pos_tiles, ys, mod)


def _rope_tables(seq, ident_rows):
    pos = jnp.arange(seq, dtype=jnp.int32)
    row = (pos // GRID_W).astype(F32)[:, None]
    col = (pos % GRID_W).astype(F32)[:, None]

    def tab(half):
        inv = ROPE_BASE ** (-jnp.arange(half, dtype=F32) / half)
        ar, ac = row * inv, col * inv
        cos = jnp.concatenate([jnp.cos(ar), jnp.cos(ar), jnp.cos(ac), jnp.cos(ac)], axis=1)
        sin = jnp.concatenate([-jnp.sin(ar), jnp.sin(ar), -jnp.sin(ac), jnp.sin(ac)], axis=1)
        reps = 128 // (4 * half)
        cos = jnp.concatenate([jnp.tile(cos, (1, reps)), jnp.ones((ident_rows, 128), F32)], axis=0)
        sin = jnp.concatenate([jnp.tile(sin, (1, reps)), jnp.zeros((ident_rows, 128), F32)], axis=0)
        return cos, sin

    c64, s64 = tab(HEAD_DIM // 4)
    c32, s32 = tab(DIFF_DIM // 4)
    return c64, s64, c32, s32


def _block_diag_ones(group):
    idx = np.arange(256) // group
    return jnp.asarray((idx[:, None] == idx[None, :]).astype(np.float32), dtype=BF16)


def _qk_gains(g_win, g_na, g_diff, g_gqa):
    s64, s32 = HEAD_DIM ** -0.5 * LOG2E, DIFF_DIM ** -0.5 * LOG2E
    rows = [jnp.tile(g_win[0] * s64, 2), jnp.tile(g_win[1], 2), jnp.tile(g_na[0] * s64, 2), jnp.tile(g_na[1], 2),
            jnp.tile(g_diff[0] * s32, 4), jnp.tile(g_diff[1], 4), jnp.tile(g_gqa[0] * s64, 2), jnp.tile(g_gqa[1], 2)]
    return jnp.stack(rows).astype(F32)


def _na_bias_tiles(rpb, rows):
    w = GRID_W
    qc = np.arange(w)[:, None]
    kc = np.arange(w)[None, :]
    c_start = np.clip(qc - NA_COLS // 2, 0, w - NA_COLS)
    col_ok = (kc >= c_start) & (kc < c_start + NA_COLS)
    dc = np.clip(kc - qc + NA_COLS - 1, 0, 2 * NA_COLS - 2)
    pick = jnp.asarray((dc[None, :, :] == np.arange(2 * NA_COLS - 1)[:, None, None]).astype(np.float32))
    picked = jnp.einsum('hdc,cqk->hdqk', rpb.astype(F32), pick, precision=lax.Precision.HIGHEST)
    t = jnp.where(jnp.asarray(col_ok), picked * LOG2E, NEG)
    neg = jnp.full((rpb.shape[0], w, w), NEG, F32)
    cases = []
    for r0, rs in ((0, 0), (NA_QROWS, 0), (rows - NA_QROWS, rows - NA_KROWS)):
        qtiles = []
        for i in range(NA_QROWS):
            qrow = r0 + i
            r_start = min(max(qrow - NA_ROWS // 2, 0), rows - NA_ROWS)
            blks = []
            for j in range(NA_KROWS):
                krow = rs + j
                blks.append(t[:, krow - qrow + NA_ROWS - 1] if r_start <= krow < r_start + NA_ROWS else neg)
            qtiles.append(jnp.concatenate(blks, axis=2))
        cases.append(jnp.concatenate(qtiles, axis=1))
    return jnp.stack(cases)


def _window_mask_tiles(tq):
    kw = tq + 2 * WINDOW
    r = lax.broadcasted_iota(I32, (3, tq, kw), 1)
    c = lax.broadcasted_iota(I32, (3, tq, kw), 2)
    shift = lax.broadcasted_iota(I32, (3, tq, kw), 0) * WINDOW
    return jnp.where(jnp.abs(c - r - shift) <= WINDOW, 0.0, NEG).astype(F32)


def _router_weights(router_w):
    hi = router_w.astype(BF16)
    lo = (router_w - hi.astype(F32)).astype(BF16)
    pad = jnp.zeros((router_w.shape[0], 128 - 2 * N_EXPERTS), BF16)
    return jnp.concatenate([hi, lo, pad], axis=1)


def kernel(x, c, ctx, c_ctx, ada_w, ada_b, norm_mix_g, norm_ffn_g, w_in, qk_g_win, qk_g_na, qk_g_diff, qk_g_gqa,
           sink_win, rpb_na, lambda_diff, out_gain, w_out, router_w, router_b, w_gate, w_up, w_down):
    batch, seq, d = x.shape
    ctx_len = ctx.shape[1]
    depth = w_in.shape[0]
    n_lat, n_ctx = batch * seq, batch * ctx_len
    n_all = n_lat + n_ctx
    rows = seq // GRID_W
    assert d == D_MODEL and seq % TOKEN_TILE == 0 and n_ctx % TOKEN_TILE == 0 and ctx_len % 128 == 0
    assert rows % NA_QROWS == 0 and rows >= NA_KROWS + NA_QROWS and seq >= ATT_TQ + 2 * WINDOW
    assert ctx_len % 128 == 0 and (NA_KROWS * GRID_W) % KEY_TILE == 0

    pad_rows = -(-(batch + 1) // 8) * 8
    s_all = jnp.concatenate([c, c_ctx[None, :], jnp.zeros((pad_rows - batch - 1, d), F32)], axis=0)
    mods = _ada_call(s_all, ada_w, ada_b).reshape(depth, pad_rows, ADA_CHUNKS, d)

    xs_all = (x.reshape(n_lat, d), ctx.reshape(n_ctx, d))
    tabs = _rope_tables(seq, TOKEN_TILE)
    wmask = _window_mask_tiles(ATT_TQ)
    bd64, bd32 = _block_diag_ones(HEAD_DIM), _block_diag_ones(DIFF_DIM)
    rw = _router_weights(router_w)
    rb = router_b.astype(F32).reshape(N_EXPERTS, 1)
    lo_tab = jnp.asarray([EXPERTS_PER_GROUP * (k // N_PAIRS) + PAIR_LO[k % N_PAIRS] for k in range(N_CLASSES)], I32)
    hi_tab = jnp.asarray([EXPERTS_PER_GROUP * (k // N_PAIRS) + PAIR_HI[k % N_PAIRS] for k in range(N_CLASSES)], I32)

    ffn, mod_prev = None, None
    for layer in range(depth):
        need_ctx = layer < depth - 1
        mod = mods[layer]
        qkg = _qk_gains(qk_g_win[layer], qk_g_na[layer], qk_g_diff[layer], qk_g_gqa[layer])
        xs_all, u = _inproj_call(xs_all, ffn, mod_prev, mod, norm_mix_g[layer].reshape(1, d),
                                 w_in[layer].astype(BF16), qkg, bd64, bd32, tabs, batch=batch, seq=seq)
        y = _attention_calls(u, sink_win[layer].astype(F32), lambda_diff[layer].astype(F32),
                             _na_bias_tiles(rpb_na[layer], rows), wmask, layer=layer, batch=batch, seq=seq,
                             ctx_len=ctx_len, need_ctx=need_ctx)
        n_proc = n_all if need_ctx else n_lat
        x_mid, h_ext, cls = _outproj_call(y, xs_all, mod, out_gain[layer].reshape(1, d), w_out[layer].astype(BF16),
                                          norm_ffn_g[layer].reshape(1, d), rw, rb, n_proc=n_proc, batch=batch, seq=seq)
        n_blocks = n_proc // MOE_BLK + N_CLASSES
        pos, bcls, nval = _rank_call(cls.reshape(n_proc // 128, 128), n_blocks)
        pos3 = pos.reshape(n_proc // ROW_DMA, 1, ROW_DMA)
        bcls, nval = bcls[0, :n_blocks], nval[0, :n_blocks]
        sorted_rows = _scatter_call(h_ext, pos3, n_blocks * MOE_BLK)
        ys = _moe_call(sorted_rows, lo_tab[bcls], hi_tab[bcls], nval, w_gate[layer].astype(BF16),
                       w_up[layer].astype(BF16), w_down[layer].astype(BF16))
        ffn = (ys, pos)
        xs_all, mod_prev = x_mid, mod
    out = _final_call(xs_all, ffn[0], ffn[1], mod_prev, batch=batch, seq=seq)
    return out.reshape(batch, seq, d)
```

```python
import functools
import math

import numpy as np
import jax
import jax.numpy as jnp
from jax import lax
from jax.experimental import pallas as pl
from jax.experimental.pallas import tpu as pltpu

F32 = jnp.float32
BF16 = jnp.bfloat16
I32 = jnp.int32
U32 = jnp.uint32

D_MODEL = 1024
GRID_W = 64
HEAD_DIM = 64
HEADS = 4
DIFF_DIM = 32
WINDOW = 128
NA_ROWS = 8
NA_COLS = 16
ROPE_BASE = 10000.0
N_EXPERTS = 16
EXPERTS_PER_GROUP = 4
D_EXPERT = 512
ADA_CHUNKS = 6
EPS = 1e-6
NEG = -1e30
D_IN = 2560

COL_A_Q, COL_A_KV, COL_B_Q, COL_B_K, COL_B_V, COL_C_Q, COL_C_K, COL_C_V, COL_D_Q, COL_D_KV = range(10)

TOKEN_TILE = 512
ATT_TQ = 256
ATT_TQ_FULL = 512
NA_QROWS = 4
NA_KROWS = 12
KEY_TILE = 256
ATT_RB = 128
ATT_SLOTS = 4
WINDOW_TILES = 1
NA_TILES = 2
LOG2E = math.log2(math.e)
MOE_BLK = 512
ROW_DMA = 1024
N_PAIRS = 6
N_CLASSES = (N_EXPERTS // EXPERTS_PER_GROUP) * N_PAIRS
PAIR_LO = (0, 0, 0, 1, 1, 2)
PAIR_HI = (1, 2, 3, 2, 3, 3)
EXT_W = D_MODEL // 2 + 128
VMEM_LIMIT = 56 * 1024 * 1024


def _cparams(*sem):
    return pltpu.CompilerParams(dimension_semantics=sem, vmem_limit_bytes=VMEM_LIMIT)


def _nt_dot(a, b):
    return lax.dot_general(a, b, (((1,), (1,)), ((), ())), preferred_element_type=F32)


def _dot(a, b):
    return jnp.dot(a, b, preferred_element_type=F32)


def _sigmoid(x):
    return 1.0 / (1.0 + jnp.exp(-x))


def _drop_arg(kern, pos):
    def wrapped(*refs):
        return kern(*refs[:pos], *refs[pos + 1:])
    return wrapped


def _ada_kernel(s_ref, w_ref, b_ref, o_ref):
    s = s_ref[...]
    act = s * _sigmoid(s)
    o_ref[0] = jnp.dot(act, w_ref[0], precision=lax.Precision.HIGHEST,
                       preferred_element_type=F32) + b_ref[0]


def _ada_call(s_all, ada_w, ada_b):
    depth, d, n = ada_w.shape
    rows = s_all.shape[0]
    tn = 1536
    return pl.pallas_call(
        _ada_kernel,
        out_shape=jax.ShapeDtypeStruct((depth, rows, n), F32),
        grid=(depth, n // tn),
        in_specs=[
            pl.BlockSpec((rows, d), lambda l, j: (0, 0)),
            pl.BlockSpec((1, d, tn), lambda l, j: (l, 0, j)),
            pl.BlockSpec((1, 1, tn), lambda l, j: (l, 0, j)),
        ],
        out_specs=pl.BlockSpec((1, rows, tn), lambda l, j: (l, 0, j)),
        compiler_params=_cparams("arbitrary", "arbitrary"),
        name="ada_mod",
    )(s_all, ada_w, ada_b.reshape(depth, 1, n))


def _start_row_gather(pos_ref, src_ref, dst_ref, sem):
    for r in range(dst_ref.shape[0]):
        pltpu.make_async_copy(src_ref.at[pl.ds(pos_ref[0, 0, r], 1)], dst_ref.at[pl.ds(r, 1)], sem).start()


def _wait_row_gather(src_ref, dst_ref, sem):
    pltpu.make_async_copy(src_ref.at[pl.ds(0, dst_ref.shape[0])], dst_ref, sem).wait()


def _prefetched_rows(pos_ref, pos_next_ref, src_ref, buf, sems):
    i = pl.program_id(0)
    slot = i % 2

    @pl.when(i == 0)
    def _():
        _start_row_gather(pos_ref, src_ref, buf.at[0], sems.at[0])

    @pl.when(i + 1 < pl.num_programs(0))
    def _():
        _start_row_gather(pos_next_ref, src_ref, buf.at[1 - slot], sems.at[1 - slot])

    _wait_row_gather(src_ref, buf.at[slot], sems.at[slot])
    return buf.at[slot]


def _inproj_kernel(*refs, has_ffn, tm, n_lat_tiles, tiles_per_batch, seq):
    if has_ffn:
        (x_ref, pos_ref, pos_next_ref, ys_ref, modp_ref, mod_ref, g_ref, w_ref, qkg_ref, bd64_ref, bd32_ref,
         c64_ref, s64_ref, c32_ref, s32_ref, xo_ref, u_ref, fbuf, fsems) = refs
        f_ref = _prefetched_rows(pos_ref, pos_next_ref, ys_ref, fbuf, fsems)
    else:
        (xl_ref, xc_ref, mod_ref, g_ref, w_ref, qkg_ref, bd64_ref, bd32_ref,
         c64_ref, s64_ref, c32_ref, s32_ref, u_ref) = refs
    i = pl.program_id(0)
    hm = tm // 2

    def prologue(r0):
        x = x_ref[r0:r0 + hm, :] if has_ffn else _two_source_rows(xl_ref, xc_ref, r0, hm, n_lat_tiles)
        if has_ffn:
            x = x + modp_ref[0, 5:6, :] * f_ref[r0:r0 + hm, :]
            xo_ref[r0:r0 + hm, :] = x
        ms = jnp.mean(x * x, axis=-1, keepdims=True)
        hn = x * lax.rsqrt(ms + EPS) * g_ref[...]
        return (hn * (1.0 + mod_ref[0, 1:2, :]) + mod_ref[0, 0:1, :]).astype(BF16)

    p0 = jnp.where(i < n_lat_tiles, (i % tiles_per_batch) * tm, seq)
    p0 = pl.multiple_of(p0, tm)
    lane = lax.broadcasted_iota(I32, (hm, 128), 1)

    def norm_rope(a, r0, gidx, group, bd_ref, rope_tabs):
        w = a.shape[1]
        ssq = _dot((a * a).astype(BF16), bd_ref[0:w, 0:w])
        r = lax.rsqrt(ssq * (1.0 / group) + EPS)
        outs = []
        for s in range(w // 128):
            t = a[:, 128 * s:128 * s + 128] * r[:, 128 * s:128 * s + 128] * qkg_ref[gidx:gidx + 1, :]
            if rope_tabs is not None:
                c_ref, s_ref, half = rope_tabs
                fwd = pltpu.roll(t, 128 - half, axis=1)
                bwd = pltpu.roll(t, half, axis=1)
                sw = jnp.where((lane % (2 * half)) < half, fwd, bwd)
                t = t * c_ref[pl.ds(p0 + r0, hm), :] + sw * s_ref[pl.ds(p0 + r0, hm), :]
            outs.append(t)
        return outs[0] if len(outs) == 1 else jnp.concatenate(outs, axis=1)

    rope64 = (c64_ref, s64_ref, HEAD_DIM // 4)
    rope32 = (c32_ref, s32_ref, DIFF_DIM // 4)

    def epilogue(acc, r0, blk):
        if blk == COL_A_Q:
            out = norm_rope(acc, r0, 0, HEAD_DIM, bd64_ref, rope64)
        elif blk == COL_A_KV:
            out = jnp.concatenate([norm_rope(acc[:, :128], r0, 1, HEAD_DIM, bd64_ref, rope64), acc[:, 128:]], axis=1)
        elif blk == COL_B_Q:
            out = norm_rope(acc, r0, 2, HEAD_DIM, bd64_ref, None)
        elif blk == COL_B_K:
            out = norm_rope(acc, r0, 3, HEAD_DIM, bd64_ref, None)
        elif blk == COL_C_Q:
            out = norm_rope(acc, r0, 4, DIFF_DIM, bd32_ref, rope32)
        elif blk == COL_C_K:
            out = norm_rope(acc, r0, 5, DIFF_DIM, bd32_ref, rope32)
        elif blk == COL_D_Q:
            out = norm_rope(acc, r0, 6, HEAD_DIM, bd64_ref, rope64)
        elif blk == COL_D_KV:
            out = jnp.concatenate([norm_rope(acc[:, :128], r0, 7, HEAD_DIM, bd64_ref, rope64), acc[:, 128:]], axis=1)
        else:
            out = acc
        u_ref[r0:r0 + hm, 256 * blk:256 * blk + 256] = out.astype(BF16)

    n_blk = D_IN // 256
    units = [(r0, blk) for r0 in (0, hm) for blk in range(n_blk)]
    hbs = {0: prologue(0)}
    acc = _dot(hbs[0], w_ref[:, 0:256])
    hbs[hm] = prologue(hm)
    for t, (r0, blk) in enumerate(units):
        nxt = None
        if t + 1 < len(units):
            r1, b1 = units[t + 1]
            nxt = _dot(hbs[r1], w_ref[:, 256 * b1:256 * b1 + 256])
        epilogue(acc, r0, blk)
        acc = nxt


def _two_source_specs(tm, n_lat_tiles):
    return [pl.BlockSpec((tm, D_MODEL), lambda i: (jnp.minimum(i, n_lat_tiles - 1), 0)),
            pl.BlockSpec((tm, D_MODEL), lambda i: (jnp.maximum(i - n_lat_tiles, 0), 0))]


def _two_source_rows(xl_ref, xc_ref, r0, n, n_lat_tiles):
    return jnp.where(pl.program_id(0) < n_lat_tiles, xl_ref[r0:r0 + n, :], xc_ref[r0:r0 + n, :])


def _mod_spec(n_lat_tiles, tiles_per_batch, batch):
    return pl.BlockSpec((1, ADA_CHUNKS, D_MODEL),
                        lambda i: (jnp.where(i < n_lat_tiles, i // tiles_per_batch, batch), 0, 0))


def _const_spec(shape):
    return pl.BlockSpec(shape, lambda i: (0,) * len(shape))


def _pos_tile_specs(n_tiles, tm):
    return [pl.BlockSpec((1, 1, tm), lambda i: (i, 0, 0), memory_space=pltpu.SMEM),
            pl.BlockSpec((1, 1, tm), lambda i: (jnp.minimum(i + 1, n_tiles - 1), 0, 0), memory_space=pltpu.SMEM)]


def _inproj_call(x, ffn_src, mod_prev, mod, g, w_in, qkg, bd64, bd32, tabs, *, batch, seq):
    tm = TOKEN_TILE
    n_lat_tiles = batch * seq // tm
    tpb = seq // tm
    has_ffn = ffn_src is not None
    tile = pl.BlockSpec((tm, D_MODEL), lambda i: (i, 0))
    modspec = _mod_spec(n_lat_tiles, tpb, batch)
    tab_rows = tabs[0].shape[0]
    if has_ffn:
        n_all = x.shape[0]
        in_specs, args = [tile], [x]
    else:
        n_all = x[0].shape[0] + x[1].shape[0]
        in_specs, args = _two_source_specs(tm, n_lat_tiles), list(x)
    scratch = []
    if has_ffn:
        ys, pos = ffn_src
        pos_tiles = pos.reshape(n_all // tm, 1, tm)
        in_specs += _pos_tile_specs(n_all // tm, tm) + [pl.BlockSpec(memory_space=pl.ANY), modspec]
        args += [pos_tiles, pos_tiles, ys, mod_prev]
        scratch = [pltpu.VMEM((2, tm, D_MODEL), F32), pltpu.SemaphoreType.DMA((2,))]
    in_specs += [modspec, _const_spec((1, D_MODEL)), _const_spec((D_MODEL, D_IN)), _const_spec((8, 128)),
                 _const_spec((256, 256)), _const_spec((256, 256))] + [_const_spec((tab_rows, 128))] * 4
    args += [mod, g, w_in, qkg, bd64, bd32] + list(tabs)
    u_shape = jax.ShapeDtypeStruct((n_all, D_IN), BF16)
    u_spec = pl.BlockSpec((tm, D_IN), lambda i: (i, 0))
    if has_ffn:
        out_shape = (jax.ShapeDtypeStruct((n_all, D_MODEL), F32), u_shape)
        out_specs = (tile, u_spec)
    else:
        out_shape, out_specs = u_shape, u_spec
    kern = functools.partial(_inproj_kernel, has_ffn=has_ffn, tm=tm, n_lat_tiles=n_lat_tiles,
                             tiles_per_batch=tpb, seq=seq)
    res = pl.pallas_call(
        kern, out_shape=out_shape, grid=(n_all // tm,), in_specs=in_specs, out_specs=out_specs,
        scratch_shapes=scratch, compiler_params=_cparams("arbitrary"), name="inproj",
    )(*args)
    return res if has_ffn else (x, res)


class _AttItem:
    def __init__(self, q, kt_view, v_view, ranges, bias=None, extra=None):
        self.q, self.kt_view, self.v_view, self.ranges, self.bias, self.extra = q, kt_view, v_view, ranges, bias, extra
        self.n_keys = sum(n for _, n in ranges)
        self.m = None

    def scores(self, s_scr, rows):
        q_blk = self.q()
        m, off, ti = None, 0, 0
        for start, n in self.ranges:
            for o in range(0, n, KEY_TILE):
                w = min(KEY_TILE, n - o)
                s = _dot(q_blk, self.kt_view[:, pl.ds(start + o, w)])
                b = self.bias(ti) if self.bias is not None else None
                if b is not None:
                    s = s + b
                ti += 1
                s_scr[rows, off + o:off + o + w] = s
                tile_max = s.max(axis=-1, keepdims=True)
                m = tile_max if m is None else jnp.maximum(m, tile_max)
            off += n
        self.m = m if self.extra is None else jnp.maximum(m, self.extra)

    def probs(self, s_scr, p_scr, rows):
        for o in range(0, self.n_keys, KEY_TILE):
            w = min(KEY_TILE, self.n_keys - o)
            p_scr[rows, o:o + w] = jnp.exp2(s_scr[rows, o:o + w] - self.m).astype(BF16)

    def values(self, p_scr, rows):
        pv, off = None, 0
        for start, n in self.ranges:
            c = _dot(p_scr[rows, off:off + n], self.v_view[pl.ds(start, n), :])
            pv = c if pv is None else pv + c
            off += n
        denom = pv[:, 64:65]
        if self.extra is not None:
            denom = denom + jnp.exp2(self.extra - self.m)
        return pv[:, 0:64] / denom


def _run_att_items(items, s_scr, p_scr):
    def rows(t):
        r0 = (t % ATT_SLOTS) * ATT_RB
        return slice(r0, r0 + ATT_RB)

    outs = []
    for t in range(len(items) + 2):
        if t < len(items):
            items[t].scores(s_scr, rows(t))
        if 0 <= t - 1 < len(items):
            items[t - 1].probs(s_scr, p_scr, rows(t - 1))
        if 0 <= t - 2 < len(items):
            outs.append(items[t - 2].values(p_scr, rows(t - 2)))
    return outs


def _stage_heads(k_parts, v_parts, kt_scr, v_scr):
    eye = jnp.where(lax.broadcasted_iota(I32, (HEAD_DIM, HEAD_DIM), 0)
                    == lax.broadcasted_iota(I32, (HEAD_DIM, HEAD_DIM), 1), 1.0, 0.0).astype(BF16)
    for h in range(kt_scr.shape[0]):
        r0 = 0
        for (kr, kc0), (vr, vc0) in zip(k_parts, v_parts):
            n = kr.shape[0]
            kt_scr[h, :, r0:r0 + n] = _nt_dot(eye, kr[:, kc0 + 64 * h:kc0 + 64 * h + 64]).astype(BF16)
            v_scr[h, r0:r0 + n, 0:64] = vr[:, vc0 + 64 * h:vc0 + 64 * h + 64]
            r0 += n
        v_scr[h, :, 64:128] = jnp.ones((v_scr.shape[1], 64), BF16)


def _group_rms(y):
    return y * lax.rsqrt(jnp.mean(y * y, axis=-1, keepdims=True) + EPS)


def _stage_once_per_batch_row(k_parts, v_parts, k_scr, v_scr):
    @pl.when(pl.program_id(1) == 0)
    def _():
        _stage_heads(k_parts, v_parts, k_scr, v_scr)


def _row_blocks(n):
    return range(0, n, ATT_RB)


def _q_thunk(q_ref, r0, head):
    return lambda: q_ref[r0:r0 + ATT_RB, 64 * head:64 * head + 64]


def _heads_to_rows(outs, n_heads):
    per = len(outs) // n_heads
    return jnp.concatenate([jnp.concatenate(outs[h * per:(h + 1) * per], axis=0) for h in range(n_heads)], axis=1)


def _dense_kernel(q_ref, kv_ref, kvc_ref, o_ref, kt_scr, v_scr, s_scr, p_scr, *, tq):
    _stage_once_per_batch_row([(kv_ref, 0), (kvc_ref, 0)], [(kv_ref, 128), (kvc_ref, 128)], kt_scr, v_scr)
    nk = v_scr.shape[1]
    items = [_AttItem(_q_thunk(q_ref, r0, hq), kt_scr.at[hq // 2], v_scr.at[hq // 2], [(0, nk)])
             for hq in range(HEADS) for r0 in _row_blocks(tq)]
    y = _heads_to_rows(_run_att_items(items, s_scr, p_scr), HEADS)
    o_ref[...] = _group_rms(y).astype(BF16)


def _window_kernel(sink_ref, q_ref, kv_ref, kvc_ref, *refs, tq, seq, tiles):
    mask_refs, (o_ref, kt_scr, v_scr, s_scr, p_scr) = refs[:tiles], refs[tiles:]
    _stage_once_per_batch_row([(kv_ref, 0), (kvc_ref, 0)], [(kv_ref, 128), (kvc_ref, 128)], kt_scr, v_scr)
    kw = tq + 2 * WINDOW
    nc = v_scr.shape[1] - seq

    def mask_thunk(mask_ref, r0):
        return lambda ti: (mask_ref[0, r0:r0 + ATT_RB, KEY_TILE * ti:KEY_TILE * ti + KEY_TILE]
                           if KEY_TILE * ti < kw else None)

    items = []
    for t, mask_ref in enumerate(mask_refs):
        q0 = (tiles * pl.program_id(1) + t) * tq
        ks = pl.multiple_of(jnp.clip(q0 - WINDOW, 0, seq - kw), 128)
        items += [_AttItem(_q_thunk(q_ref, t * tq + r0, hq), kt_scr.at[hq // 2], v_scr.at[hq // 2],
                           [(ks, kw), (seq, nc)], bias=mask_thunk(mask_ref, r0), extra=sink_ref[hq] * LOG2E)
                  for hq in range(HEADS) for r0 in _row_blocks(tq)]
    outs = _run_att_items(items, s_scr, p_scr)
    per = len(outs) // tiles
    y = jnp.concatenate([_heads_to_rows(outs[t * per:(t + 1) * per], HEADS) for t in range(tiles)], axis=0)
    o_ref[...] = _group_rms(y).astype(BF16)


def _neigh_kernel(q_ref, k_ref, v_ref, kc_ref, vc_ref, *refs, rows, seq, tiles):
    bias_refs, (o_ref, kt_scr, v_scr, s_scr, p_scr) = refs[:tiles], refs[tiles:]
    _stage_once_per_batch_row([(k_ref, 0), (kc_ref, 0)], [(v_ref, 0), (vc_ref, 0)], kt_scr, v_scr)
    nk = NA_KROWS * GRID_W
    nc = v_scr.shape[1] - seq
    qrows = NA_QROWS * GRID_W

    def bias_thunk(bias_ref, h, r0):
        return lambda ti: (bias_ref[0, h, r0:r0 + ATT_RB, KEY_TILE * ti:KEY_TILE * ti + KEY_TILE]
                           if KEY_TILE * ti < nk else None)

    items = []
    for t, bias_ref in enumerate(bias_refs):
        rb = tiles * pl.program_id(1) + t
        rs = jnp.clip(NA_QROWS * rb - NA_ROWS // 2, 0, rows - NA_KROWS)
        k0 = pl.multiple_of(rs * GRID_W, KEY_TILE)
        items += [_AttItem(_q_thunk(q_ref, t * qrows + r0, h), kt_scr.at[h], v_scr.at[h], [(k0, nk), (seq, nc)],
                           bias=bias_thunk(bias_ref, h, r0))
                  for h in range(HEADS) for r0 in _row_blocks(qrows)]
    outs = _run_att_items(items, s_scr, p_scr)
    per = len(outs) // tiles
    y = jnp.concatenate([_heads_to_rows(outs[t * per:(t + 1) * per], HEADS) for t in range(tiles)], axis=0)
    o_ref[...] = _group_rms(y).astype(BF16)


def _diff_lambda(lam_ref, lam_init):
    lp = lam_ref[...]
    a = jnp.sum(lp[0:1] * lp[1:2], axis=-1, keepdims=True)
    b = jnp.sum(lp[2:3] * lp[3:4], axis=-1, keepdims=True)
    return jnp.exp(a) - jnp.exp(b) + lam_init


def _diff_kernel(lam_ref, q_ref, k_ref, v_ref, kc_ref, vc_ref, o_ref, kt_scr, v_scr, s_scr, p_scr, *, lam_init, tq):
    _stage_once_per_batch_row([(k_ref, 0), (kc_ref, 0)], [(v_ref, 0), (vc_ref, 0)], kt_scr, v_scr)
    lam = _diff_lambda(lam_ref, lam_init)
    nk = v_scr.shape[1]
    first = lax.broadcasted_iota(I32, (ATT_RB, HEAD_DIM), 1) < DIFF_DIM
    zero = jnp.zeros((ATT_RB, HEAD_DIM), BF16)

    def q_map(r0, h, half):
        def thunk():
            qh = q_ref[r0:r0 + ATT_RB, 64 * h:64 * h + 64]
            return jnp.where(first, qh, zero) if half == 0 else jnp.where(first, zero, qh)
        return thunk

    items = [_AttItem(q_map(r0, h, half), kt_scr.at[h], v_scr.at[h], [(0, nk)])
             for h in range(HEADS) for r0 in _row_blocks(tq) for half in range(2)]
    outs = _run_att_items(items, s_scr, p_scr)
    diffs = [outs[i] - lam * outs[i + 1] for i in range(0, len(outs), 2)]
    per = len(diffs) // HEADS
    heads = [_group_rms(jnp.concatenate(diffs[h * per:(h + 1) * per], axis=0)) * (1.0 - lam_init)
             for h in range(HEADS)]
    o_ref[...] = jnp.concatenate(heads, axis=1).astype(BF16)


def _softmax_parts(scores, extra=None):
    m = scores[0].max(axis=-1, keepdims=True)
    for s in scores[1:]:
        m = jnp.maximum(m, s.max(axis=-1, keepdims=True))
    if extra is not None:
        m = jnp.maximum(m, extra)
    ps = [jnp.exp2(s - m) for s in scores]
    l = ps[0].sum(axis=-1, keepdims=True)
    for p in ps[1:]:
        l = l + p.sum(axis=-1, keepdims=True)
    if extra is not None:
        l = l + jnp.exp2(extra - m)
    return ps, 1.0 / l


def _ctx_kernel(sink_ref, lam_ref, u_ref, o_ref, *, lam_init):
    L = u_ref.shape[0]
    top = lax.broadcasted_iota(I32, (2 * L, 1), 0) < L

    def cols(blk, lo, width=64):
        return u_ref[:, 256 * blk + lo:256 * blk + lo + width]

    for q_blk, kv_blk, out_col, use_sink in ((COL_A_Q, COL_A_KV, 0, True), (COL_D_Q, COL_D_KV, 768, False)):
        outs = [None] * HEADS
        for j in range(2):
            q2 = jnp.concatenate([cols(q_blk, 128 * j), cols(q_blk, 128 * j + 64)], axis=0)
            snk = jnp.where(top, sink_ref[2 * j], sink_ref[2 * j + 1]) * LOG2E if use_sink else None
            (p,), inv = _softmax_parts([_nt_dot(q2, cols(kv_blk, 64 * j))], extra=snk)
            o = _dot(p.astype(BF16), cols(kv_blk, 128 + 64 * j)) * inv
            outs[2 * j], outs[2 * j + 1] = o[:L], o[L:]
        o_ref[:, out_col:out_col + 256] = _group_rms(jnp.concatenate(outs, axis=1)).astype(BF16)
    outs = []
    for h in range(HEADS):
        (p,), inv = _softmax_parts([_nt_dot(cols(COL_B_Q, 64 * h), cols(COL_B_K, 64 * h))])
        outs.append(_dot(p.astype(BF16), cols(COL_B_V, 64 * h)) * inv)
    o_ref[:, 256:512] = _group_rms(jnp.concatenate(outs, axis=1)).astype(BF16)
    lam = _diff_lambda(lam_ref, lam_init)
    first = lax.broadcasted_iota(I32, (L, HEAD_DIM), 1) < DIFF_DIM
    zero = jnp.zeros((L, HEAD_DIM), BF16)
    outs = []
    for h in range(HEADS):
        qh, kh = cols(COL_C_Q, 64 * h), cols(COL_C_K, 64 * h)
        maps = []
        for half in range(2):
            qm = jnp.where(first, qh, zero) if half == 0 else jnp.where(first, zero, qh)
            (p,), inv = _softmax_parts([_nt_dot(qm, kh)])
            maps.append(p * inv)
        o = _dot((maps[0] - lam * maps[1]).astype(BF16), cols(COL_C_V, 64 * h))
        outs.append(_group_rms(o) * (1.0 - lam_init))
    o_ref[:, 512:768] = jnp.concatenate(outs, axis=1).astype(BF16)


def _attention_calls(u, sink, lam_p, bias, wmask, *, layer, batch, seq, ctx_len, need_ctx):
    lam_init = 0.8 - 0.6 * math.exp(-0.3 * layer)
    n_all = u.shape[0]
    tq = ATT_TQ
    nq = seq // tq
    nk_all = seq + ctx_len
    cb0 = batch * seq // ctx_len
    y_shape = jax.ShapeDtypeStruct((n_all, D_MODEL), BF16)
    smem = pl.BlockSpec(memory_space=pltpu.SMEM)
    any_spec = pl.BlockSpec(memory_space=pl.ANY)
    lam_spec2 = pl.BlockSpec((4, DIFF_DIM), lambda b, i: (0, 0))
    cp = _cparams("arbitrary", "arbitrary")

    def scratch(n_heads, n_keys):
        return [pltpu.VMEM((n_heads, HEAD_DIM, nk_all), BF16), pltpu.VMEM((n_heads, nk_all, 128), BF16),
                pltpu.VMEM((ATT_SLOTS * ATT_RB, n_keys), F32), pltpu.VMEM((ATT_SLOTS * ATT_RB, n_keys), BF16)]

    def qspec(col, t=tq):
        return pl.BlockSpec((t, 256), lambda b, i: (b * (seq // t) + i, col))

    tqf = ATT_TQ_FULL
    nqf = seq // tqf

    def latspec(col):
        return pl.BlockSpec((seq, 256), lambda b, i: (b, col))

    def ctxspec(col):
        return pl.BlockSpec((ctx_len, 256), lambda b, i: (cb0 + b, col))

    kw = tq + 2 * WINDOW

    def case_specs(shape, n_tiles, per_step):
        def spec(t):
            def index(b, i):
                j = per_step * i + t
                return (jnp.where(j == 0, 0, jnp.where(j == n_tiles - 1, 2, 1)),) + (0,) * (len(shape) - 1)
            return pl.BlockSpec(shape, index)
        return [spec(t) for t in range(per_step)]

    wt = WINDOW_TILES
    y = pl.pallas_call(
        functools.partial(_window_kernel, tq=tq, seq=seq, tiles=wt),
        out_shape=y_shape, grid=(batch, nq // wt),
        in_specs=[smem, qspec(COL_A_Q, wt * tq), latspec(COL_A_KV), ctxspec(COL_A_KV)]
        + case_specs((1, tq, kw), nq, wt),
        out_specs=qspec(0, wt * tq), scratch_shapes=scratch(2, kw + ctx_len),
        compiler_params=cp, name="mix_window",
    )(sink, u, u, u, *([wmask] * wt))

    rows = seq // GRID_W
    nrb = rows // NA_QROWS
    qrows = NA_QROWS * GRID_W
    nkb = NA_KROWS * GRID_W
    nt = NA_TILES
    y = pl.pallas_call(
        _drop_arg(functools.partial(_neigh_kernel, rows=rows, seq=seq, tiles=nt), 5 + nt),
        out_shape=y_shape, grid=(batch, nrb // nt),
        in_specs=[qspec(COL_B_Q, nt * qrows), latspec(COL_B_K), latspec(COL_B_V),
                  ctxspec(COL_B_K), ctxspec(COL_B_V)] + case_specs((1, HEADS, qrows, nkb), nrb, nt) + [any_spec],
        out_specs=qspec(1, nt * qrows), scratch_shapes=scratch(HEADS, nkb + ctx_len),
        input_output_aliases={5 + nt: 0}, compiler_params=cp, name="mix_neigh",
    )(u, u, u, u, u, *([bias] * nt), y)

    y = pl.pallas_call(
        _drop_arg(functools.partial(_diff_kernel, lam_init=lam_init, tq=tqf), 6),
        out_shape=y_shape, grid=(batch, nqf),
        in_specs=[lam_spec2, qspec(COL_C_Q, tqf), latspec(COL_C_K), latspec(COL_C_V),
                  ctxspec(COL_C_K), ctxspec(COL_C_V), any_spec],
        out_specs=qspec(2, tqf), scratch_shapes=scratch(HEADS, nk_all),
        input_output_aliases={6: 0}, compiler_params=cp, name="mix_diff",
    )(lam_p, u, u, u, u, u, y)

    y = pl.pallas_call(
        _drop_arg(functools.partial(_dense_kernel, tq=tqf), 3),
        out_shape=y_shape, grid=(batch, nqf),
        in_specs=[qspec(COL_D_Q, tqf), latspec(COL_D_KV), ctxspec(COL_D_KV), any_spec],
        out_specs=qspec(3, tqf), scratch_shapes=scratch(2, nk_all),
        input_output_aliases={3: 0}, compiler_params=cp, name="mix_dense",
    )(u, u, u, y)

    if need_ctx:
        y = pl.pallas_call(
            _drop_arg(functools.partial(_ctx_kernel, lam_init=lam_init), 3),
            out_shape=y_shape, grid=(batch,),
            in_specs=[smem, pl.BlockSpec((4, DIFF_DIM), lambda b: (0, 0)),
                      pl.BlockSpec((ctx_len, D_IN), lambda b: (cb0 + b, 0)), any_spec],
            out_specs=pl.BlockSpec((ctx_len, D_MODEL), lambda b: (cb0 + b, 0)),
            input_output_aliases={3: 0}, compiler_params=_cparams("arbitrary"), name="mix_ctx",
        )(sink, lam_p, u, y)
    return y


def _select4(idx, vals):
    return jnp.where(idx == 0, vals[0], jnp.where(idx == 1, vals[1], jnp.where(idx == 2, vals[2], vals[3])))


def _outproj_kernel(y_ref, *refs, tm, n_lat_tiles, two_source):
    n_x = 2 if two_source else 1
    x_refs, (mod_ref, gain_ref, w_ref, g_ref, rw_ref, rb_ref, xo_ref, h_ref, cls_ref) = refs[:n_x], refs[n_x:]
    hm = tm // 2

    def project(r0):
        yg = (y_ref[r0:r0 + hm, :].astype(F32) * gain_ref[...]).astype(BF16)
        return _dot(yg, w_ref[...])

    def normalise(r0, proj):
        x = (_two_source_rows(*x_refs, r0, hm, n_lat_tiles) if two_source else x_refs[0][r0:r0 + hm, :])
        xm = x + mod_ref[0, 2:3, :] * proj
        xo_ref[r0:r0 + hm, :] = xm
        hn = xm * lax.rsqrt(jnp.mean(xm * xm, axis=-1, keepdims=True) + EPS) * g_ref[...]
        h2 = hn * (1.0 + mod_ref[0, 4:5, :]) + mod_ref[0, 3:4, :]
        hi = h2.astype(BF16)
        hi32 = hi.astype(F32)
        lo = (h2 - hi32).astype(BF16)
        bits = lax.bitcast_convert_type(hi32, U32)
        half = D_MODEL // 2
        h_ref[r0:r0 + hm, 0:half] = (bits[:, 0:half] >> 16) | (bits[:, half:D_MODEL] & jnp.uint32(0xFFFF0000))
        return (_dot(hi, rw_ref[...]) + _dot(lo, rw_ref[...])).T

    def route(r0, lt):
        score = _sigmoid(lt[0:N_EXPERTS] + lt[N_EXPERTS:2 * N_EXPERTS])
        sel = score + rb_ref[...]
        srow = [sel[e:e + 1] for e in range(N_EXPERTS)]
        crow = [score[e:e + 1] for e in range(N_EXPERTS)]

        gidx = jnp.zeros((1, hm), I32)
        gbest = None
        for g in range(N_EXPERTS // EXPERTS_PER_GROUP):
            v = srow[4 * g:4 * g + 4]
            best = None
            for a in range(4):
                for b in range(a + 1, 4):
                    t = v[a] + v[b]
                    best = t if best is None else jnp.maximum(best, t)
            if g == 0:
                gbest = best
            else:
                better = best > gbest
                gidx = jnp.where(better, g, gidx)
                gbest = jnp.where(better, best, gbest)

        iv = [_select4(gidx, [srow[4 * g + i] for g in range(4)]) for i in range(4)]
        sv = [_select4(gidx, [crow[4 * g + i] for g in range(4)]) for i in range(4)]
        chosen = []
        for i in range(4):
            rank = jnp.zeros((1, hm), I32)
            for j in range(4):
                if j != i:
                    beats = (iv[j] >= iv[i]) if j < i else (iv[j] > iv[i])
                    rank = rank + jnp.where(beats, 1, 0)
            chosen.append(rank < 2)
        lo_i = jnp.where(chosen[0], 0, jnp.where(chosen[1], 1, 2))
        hi_i = jnp.where(chosen[3], 3, jnp.where(chosen[2], 2, 1))
        pair = jnp.where(lo_i == 0, hi_i - 1, jnp.where(lo_i == 1, hi_i + 1, 5))
        cls_ref[0, :, r0:r0 + hm] = gidx * N_PAIRS + pair
        s_lo = jnp.where(lo_i == 0, sv[0], jnp.where(lo_i == 1, sv[1], sv[2]))
        s_hi = jnp.where(hi_i == 3, sv[3], jnp.where(hi_i == 2, sv[2], sv[1]))
        inv = 1.0 / (s_lo + s_hi)
        rowi = lax.broadcasted_iota(I32, (128, hm), 0)
        wts = jnp.where(rowi == 0, s_lo * inv, jnp.where(rowi == 1, s_hi * inv, 0.0))
        h_ref[r0:r0 + hm, D_MODEL // 2:EXT_W] = lax.bitcast_convert_type(wts.T, U32)

    proj0 = project(0)
    proj1 = project(hm)
    lt0 = normalise(0, proj0)
    lt1 = normalise(hm, proj1)
    route(0, lt0)
    route(hm, lt1)


def _outproj_call(y, x, mod, gain, w_out, g, rw, rb, *, n_proc, batch, seq):
    tm = TOKEN_TILE
    n_tiles = n_proc // tm
    n_lat_tiles = batch * seq // tm
    tile = pl.BlockSpec((tm, D_MODEL), lambda i: (i, 0))
    two_source = isinstance(x, tuple)
    x_specs, x_args = (_two_source_specs(tm, n_lat_tiles), list(x)) if two_source else ([tile], [x])
    return pl.pallas_call(
        functools.partial(_outproj_kernel, tm=tm, n_lat_tiles=n_lat_tiles, two_source=two_source),
        out_shape=(jax.ShapeDtypeStruct((n_proc, D_MODEL), F32),
                   jax.ShapeDtypeStruct((n_proc, EXT_W), U32),
                   jax.ShapeDtypeStruct((n_tiles, 1, tm), I32)),
        grid=(n_tiles,),
        in_specs=[tile] + x_specs + [_mod_spec(n_lat_tiles, seq // tm, batch), _const_spec((1, D_MODEL)),
                  _const_spec((D_MODEL, D_MODEL)), _const_spec((1, D_MODEL)), _const_spec((D_MODEL, 128)),
                  _const_spec((N_EXPERTS, 1))],
        out_specs=(tile, pl.BlockSpec((tm, EXT_W), lambda i: (i, 0)), pl.BlockSpec((1, 1, tm), lambda i: (i, 0, 0))),
        compiler_params=_cparams("arbitrary"), name="outproj_router",
    )(y, *x_args, mod, gain, w_out, g, rw, rb)


def _rank_kernel(cls_ref, pos_ref, bcls_ref, nval_ref, *, blk):
    c = cls_ref[...]
    nr = c.shape[0]
    nbp = bcls_ref.shape[1]
    upper = (lax.broadcasted_iota(I32, (128, 128), 0) <= lax.broadcasted_iota(I32, (128, 128), 1))
    upper = jnp.where(upper, 1.0, 0.0).astype(BF16)
    lower = (lax.broadcasted_iota(I32, (nr, nr), 1) < lax.broadcasted_iota(I32, (nr, nr), 0))
    lower = jnp.where(lower, 1.0, 0.0).astype(BF16)
    bidx = lax.broadcasted_iota(I32, (1, nbp), 1).astype(F32)
    pos = jnp.zeros((nr, 128), F32)
    bcls = jnp.zeros((1, nbp), F32)
    nval = jnp.zeros((1, nbp), F32)
    start = jnp.zeros((1, 1), F32)
    for k in range(N_CLASSES):
        m = c == k
        incl = _dot(jnp.where(m, 1.0, 0.0).astype(BF16), upper)
        tot = incl[:, 127:128]
        above = _dot(lower, jnp.broadcast_to(tot, (nr, 128)).astype(BF16))
        cnt = above[nr - 1:nr, 0:1] + tot[nr - 1:nr, 0:1]
        nblk = jnp.floor((cnt + (blk - 1)) * (1.0 / blk))
        pos = jnp.where(m, start * blk + above + incl - 1.0, pos)
        end = start + nblk
        bcls = bcls + jnp.where(bidx >= end, 1.0, 0.0)
        nval = nval + jnp.where((bidx >= start) & (bidx < end), jnp.clip(cnt - (bidx - start) * blk, 0.0, blk), 0.0)
        start = end
    pos_ref[...] = pos.astype(I32)
    bcls_ref[...] = jnp.minimum(bcls, N_CLASSES - 1.0).astype(I32)
    nval_ref[...] = nval.astype(I32)


def _rank_call(cls2d, n_blocks):
    nr = cls2d.shape[0]
    nbp = -(-n_blocks // 128) * 128
    return pl.pallas_call(
        functools.partial(_rank_kernel, blk=MOE_BLK),
        out_shape=(jax.ShapeDtypeStruct((nr, 128), I32), jax.ShapeDtypeStruct((1, nbp), I32),
                   jax.ShapeDtypeStruct((1, nbp), I32)),
        grid=(1,),
        in_specs=[_const_spec((nr, 128))],
        out_specs=(_const_spec((nr, 128)), _const_spec((1, nbp)), _const_spec((1, nbp))),
        compiler_params=_cparams("arbitrary"), name="class_rank",
    )(cls2d)


def _scatter_kernel(pos_ref, h_ref, xs_ref, sem):
    n = h_ref.shape[0]
    for r in range(n):
        pltpu.make_async_copy(h_ref.at[pl.ds(r, 1)], xs_ref.at[pl.ds(pos_ref[0, 0, r], 1)], sem).start()
    pltpu.make_async_copy(h_ref, xs_ref.at[pl.ds(0, n)], sem).wait()


def _scatter_call(h_ext, pos3, n_slots):
    n, w = h_ext.shape
    r = ROW_DMA
    return pl.pallas_call(
        _scatter_kernel,
        out_shape=jax.ShapeDtypeStruct((n_slots, w), h_ext.dtype),
        grid=(n // r,),
        in_specs=[pl.BlockSpec((1, 1, r), lambda i: (i, 0, 0), memory_space=pltpu.SMEM),
                  pl.BlockSpec((r, w), lambda i: (i, 0))],
        out_specs=pl.BlockSpec(memory_space=pl.ANY),
        scratch_shapes=[pltpu.SemaphoreType.DMA(())],
        compiler_params=_cparams("arbitrary"), name="row_scatter",
    )(pos3, h_ext)


def _moe_kernel(elo_ref, ehi_ref, nval_ref, xs_ref, wg_lo, wu_lo, wd_lo, wg_hi, wu_hi, wd_hi, ys_ref):
    del elo_ref, ehi_ref
    nv = nval_ref[pl.program_id(0)]

    @pl.when(nv > 0)
    def _():
        live = lax.broadcasted_iota(I32, (MOE_BLK, 1), 0) < nv
        half = D_MODEL // 2
        words = jnp.where(live, xs_ref[:, 0:half], jnp.uint32(0))
        x = jnp.concatenate([lax.bitcast_convert_type(words << 16, F32).astype(BF16),
                             lax.bitcast_convert_type(words & jnp.uint32(0xFFFF0000), F32).astype(BF16)], axis=1)
        acts = []
        for wg, wu, col in ((wg_lo, wu_lo, half), (wg_hi, wu_hi, half + 1)):
            w = jnp.where(live, lax.bitcast_convert_type(xs_ref[:, col:col + 1], F32), 0.0)
            g = _dot(x, wg[0])
            acts.append((g * _sigmoid(g) * _dot(x, wu[0]) * w).astype(BF16))
        ys_ref[...] = _dot(acts[0], wd_lo[0]) + _dot(acts[1], wd_hi[0])

    @pl.when(nv == 0)
    def _():
        ys_ref[...] = jnp.zeros_like(ys_ref)


def _moe_call(xs, e_lo, e_hi, nval, w_gate, w_up, w_down):
    n_blocks = xs.shape[0] // MOE_BLK

    def wspec(shape, which):
        return pl.BlockSpec((1,) + shape, lambda b, elo, ehi, nv: ((elo, ehi)[which][b], 0, 0))

    gu, dn = (D_MODEL, D_EXPERT), (D_EXPERT, D_MODEL)
    grid_spec = pltpu.PrefetchScalarGridSpec(
        num_scalar_prefetch=3, grid=(n_blocks,),
        in_specs=[pl.BlockSpec((MOE_BLK, EXT_W), lambda b, elo, ehi, nv: (b, 0)),
                  wspec(gu, 0), wspec(gu, 0), wspec(dn, 0), wspec(gu, 1), wspec(gu, 1), wspec(dn, 1)],
        out_specs=pl.BlockSpec((MOE_BLK, D_MODEL), lambda b, elo, ehi, nv: (b, 0)),
    )
    return pl.pallas_call(
        _moe_kernel, out_shape=jax.ShapeDtypeStruct((xs.shape[0], D_MODEL), F32), grid_spec=grid_spec,
        compiler_params=_cparams("arbitrary"), name="moe_experts",
    )(e_lo, e_hi, nval, xs, w_gate, w_up, w_down, w_gate, w_up, w_down)


def _final_kernel(x_ref, pos_ref, pos_next_ref, ys_ref, mod_ref, o_ref, fbuf, fsems):
    f_ref = _prefetched_rows(pos_ref, pos_next_ref, ys_ref, fbuf, fsems)
    o_ref[...] = x_ref[...] + mod_ref[0, 5:6, :] * f_ref[...]


def _final_call(x, ys, pos, mod, *, batch, seq):
    tm = TOKEN_TILE
    n = batch * seq
    tile = pl.BlockSpec((tm, D_MODEL), lambda i: (i, 0))
    pos_tiles = pos.reshape(n // tm, 1, tm)
    return pl.pallas_call(
        _final_kernel, out_shape=jax.ShapeDtypeStruct((n, D_MODEL), F32), grid=(n // tm,),
        in_specs=[tile] + _pos_tile_specs(n // tm, tm) + [pl.BlockSpec(memory_space=pl.ANY),
                                                          _mod_spec(n // tm, seq // tm, batch)],
        out_specs=tile, scratch_shapes=[pltpu.VMEM((2, tm, D_MODEL), F32), pltpu.SemaphoreType.DMA((2,))],
        compiler_params=_cparams("arbitrary"), name="final_residual",
    )(x, pos_tiles, pos_tiles, ys, mod)


def _rope_tables(seq, ident_rows):
    pos = jnp.arange(seq, dtype=jnp.int32)
    row = (pos // GRID_W).astype(F32)[:, None]
    col = (pos % GRID_W).astype(F32)[:, None]

    def tab(half):
        inv = ROPE_BASE ** (-jnp.arange(half, dtype=F32) / half)
        ar, ac = row * inv, col * inv
        cos = jnp.concatenate([jnp.cos(ar), jnp.cos(ar), jnp.cos(ac), jnp.cos(ac)], axis=1)
        sin = jnp.concatenate([-jnp.sin(ar), jnp.sin(ar), -jnp.sin(ac), jnp.sin(ac)], axis=1)
        reps = 128 // (4 * half)
        cos = jnp.concatenate([jnp.tile(cos, (1, reps)), jnp.ones((ident_rows, 128), F32)], axis=0)
        sin = jnp.concatenate([jnp.tile(sin, (1, reps)), jnp.zeros((ident_rows, 128), F32)], axis=0)
        return cos, sin

    c64, s64 = tab(HEAD_DIM // 4)
    c32, s32 = tab(DIFF_DIM // 4)
    return c64, s64, c32, s32


def _block_diag_ones(group):
    idx = np.arange(256) // group
    return jnp.asarray((idx[:, None] == idx[None, :]).astype(np.float32), dtype=BF16)


def _qk_gains(g_win, g_na, g_diff, g_gqa):
    s64, s32 = HEAD_DIM ** -0.5 * LOG2E, DIFF_DIM ** -0.5 * LOG2E
    rows = [jnp.tile(g_win[0] * s64, 2), jnp.tile(g_win[1], 2), jnp.tile(g_na[0] * s64, 2), jnp.tile(g_na[1], 2),
            jnp.tile(g_diff[0] * s32, 4), jnp.tile(g_diff[1], 4), jnp.tile(g_gqa[0] * s64, 2), jnp.tile(g_gqa[1], 2)]
    return jnp.stack(rows).astype(F32)


def _na_bias_tiles(rpb, rows):
    w = GRID_W
    qc = np.arange(w)[:, None]
    kc = np.arange(w)[None, :]
    c_start = np.clip(qc - NA_COLS // 2, 0, w - NA_COLS)
    col_ok = (kc >= c_start) & (kc < c_start + NA_COLS)
    dc = np.clip(kc - qc + NA_COLS - 1, 0, 2 * NA_COLS - 2)
    pick = jnp.asarray((dc[None, :, :] == np.arange(2 * NA_COLS - 1)[:, None, None]).astype(np.float32))
    picked = jnp.einsum('hdc,cqk->hdqk', rpb.astype(F32), pick, precision=lax.Precision.HIGHEST)
    t = jnp.where(jnp.asarray(col_ok), picked * LOG2E, NEG)
    neg = jnp.full((rpb.shape[0], w, w), NEG, F32)
    cases = []
    for r0, rs in ((0, 0), (NA_QROWS, 0), (rows - NA_QROWS, rows - NA_KROWS)):
        qtiles = []
        for i in range(NA_QROWS):
            qrow = r0 + i
            r_start = min(max(qrow - NA_ROWS // 2, 0), rows - NA_ROWS)
            blks = []
            for j in range(NA_KROWS):
                krow = rs + j
                blks.append(t[:, krow - qrow + NA_ROWS - 1] if r_start <= krow < r_start + NA_ROWS else neg)
            qtiles.append(jnp.concatenate(blks, axis=2))
        cases.append(jnp.concatenate(qtiles, axis=1))
    return jnp.stack(cases)


def _window_mask_tiles(tq):
    kw = tq + 2 * WINDOW
    r = lax.broadcasted_iota(I32, (3, tq, kw), 1)
    c = lax.broadcasted_iota(I32, (3, tq, kw), 2)
    shift = lax.broadcasted_iota(I32, (3, tq, kw), 0) * WINDOW
    return jnp.where(jnp.abs(c - r - shift) <= WINDOW, 0.0, NEG).astype(F32)


def _router_weights(router_w):
    hi = router_w.astype(BF16)
    lo = (router_w - hi.astype(F32)).astype(BF16)
    pad = jnp.zeros((router_w.shape[0], 128 - 2 * N_EXPERTS), BF16)
    return jnp.concatenate([hi, lo, pad], axis=1)


def kernel(x, c, ctx, c_ctx, ada_w, ada_b, norm_mix_g, norm_ffn_g, w_in, qk_g_win, qk_g_na, qk_g_diff, qk_g_gqa,
           sink_win, rpb_na, lambda_diff, out_gain, w_out, router_w, router_b, w_gate, w_up, w_down):
    batch, seq, d = x.shape
    ctx_len = ctx.shape[1]
    depth = w_in.shape[0]
    n_lat, n_ctx = batch * seq, batch * ctx_len
    n_all = n_lat + n_ctx
    rows = seq // GRID_W
    assert d == D_MODEL and seq % TOKEN_TILE == 0 and n_ctx % TOKEN_TILE == 0 and ctx_len % 128 == 0
    assert rows % NA_QROWS == 0 and rows >= NA_KROWS + NA_QROWS and seq >= ATT_TQ + 2 * WINDOW
    assert ctx_len % 128 == 0 and (NA_KROWS * GRID_W) % KEY_TILE == 0

    pad_rows = -(-(batch + 1) // 8) * 8
    s_all = jnp.concatenate([c, c_ctx[None, :], jnp.zeros((pad_rows - batch - 1, d), F32)], axis=0)
    mods = _ada_call(s_all, ada_w, ada_b).reshape(depth, pad_rows, ADA_CHUNKS, d)

    xs_all = (x.reshape(n_lat, d), ctx.reshape(n_ctx, d))
    tabs = _rope_tables(seq, TOKEN_TILE)
    wmask = _window_mask_tiles(ATT_TQ)
    bd64, bd32 = _block_diag_ones(HEAD_DIM), _block_diag_ones(DIFF_DIM)
    rw = _router_weights(router_w)
    rb = router_b.astype(F32).reshape(N_EXPERTS, 1)
    lo_tab = jnp.asarray([EXPERTS_PER_GROUP * (k // N_PAIRS) + PAIR_LO[k % N_PAIRS] for k in range(N_CLASSES)], I32)
    hi_tab = jnp.asarray([EXPERTS_PER_GROUP * (k // N_PAIRS) + PAIR_HI[k % N_PAIRS] for k in range(N_CLASSES)], I32)

    ffn, mod_prev = None, None
    for layer in range(depth):
        need_ctx = layer < depth - 1
        mod = mods[layer]
        qkg = _qk_gains(qk_g_win[layer], qk_g_na[layer], qk_g_diff[layer], qk_g_gqa[layer])
        xs_all, u = _inproj_call(xs_all, ffn, mod_prev, mod, norm_mix_g[layer].reshape(1, d),
                                 w_in[layer].astype(BF16), qkg, bd64, bd32, tabs, batch=batch, seq=seq)
        y = _attention_calls(u, sink_win[layer].astype(F32), lambda_diff[layer].astype(F32),
                             _na_bias_tiles(rpb_na[layer], rows), wmask, layer=layer, batch=batch, seq=seq,
                             ctx_len=ctx_len, need_ctx=need_ctx)
        n_proc = n_all if need_ctx else n_lat
        x_mid, h_ext, cls = _outproj_call(y, xs_all, mod, out_gain[layer].reshape(1, d), w_out[layer].astype(BF16),
                                          norm_ffn_g[layer].reshape(1, d), rw, rb, n_proc=n_proc, batch=batch, seq=seq)
        n_blocks = n_proc // MOE_BLK + N_CLASSES
        pos, bcls, nval = _rank_call(cls.reshape(n_proc // 128, 128), n_blocks)
        pos3 = pos.reshape(n_proc // ROW_DMA, 1, ROW_DMA)
        bcls, nval = bcls[0, :n_blocks], nval[0, :n_blocks]
        sorted_rows = _scatter_call(h_ext, pos3, n_blocks * MOE_BLK)
        ys = _moe_call(sorted_rows, lo_tab[bcls], hi_tab[bcls], nval, w_gate[layer].astype(BF16),
                       w_up[layer].astype(BF16), w_down[layer].astype(BF16))
        ffn = (ys, pos)
        xs_all, mod_prev = x_mid, mod
    out = _final_call(xs_all, ffn[0], ffn[1], mod_prev, batch=batch, seq=seq)
    return out.reshape(batch, seq, d)
```
